```python
import math
import jax
import jax.numpy as jnp
from jax import lax
import numpy as np

D_MODEL = 1024
BATCH = 16
SEQ = 256
DEPTH = 2
DEC_BATCH = 8
DEC_SEQ = 1024
PAST_LEN = 512

GRID_W = 64
HEAD_DIM = 64
ROPE_BASE = 10000.0
Q_BLOCK = 128
NEG_INF = -1e30
MOD_CHUNKS = 6
GQA_HEADS = 4
GQA_KV_HEADS = 2
GQA_GROUP = GQA_HEADS // GQA_KV_HEADS
NA_HEADS = 4
NA_KH = 8
NA_KW = 16
DN_HEADS = 4
DN_DK = 64
DN_DV = 64
DN_CONV = 4
DN_CHUNK = 64
DN_QKV = DN_HEADS * (2 * DN_DK + DN_DV)
MLA_HEADS = 4
MLA_Q_LORA = 256
MLA_KV_LORA = 128
MLA_NOPE = 64
MLA_ROPE = 32
MLA_V = 64
MLA_SCALE = (MLA_NOPE + MLA_ROPE) ** -0.5

MIX_W = GQA_HEADS * HEAD_DIM + NA_HEADS * HEAD_DIM + DN_HEADS * DN_DV + MLA_HEADS * MLA_V
D_FF = -(-8 * D_MODEL // (3 * 256)) * 256
IN_SIZES = (GQA_HEADS * HEAD_DIM, GQA_KV_HEADS * HEAD_DIM, GQA_KV_HEADS * HEAD_DIM,
            NA_HEADS * HEAD_DIM, NA_HEADS * HEAD_DIM, NA_HEADS * HEAD_DIM,
            DN_QKV, DN_HEADS * DN_DV, 2 * DN_HEADS, 2 * DN_HEADS,
            MLA_Q_LORA, MLA_KV_LORA, MLA_ROPE)
IN_COLS = sum(IN_SIZES)

kernel_name = 'hybrid_flow_prefix_step'


def rms_norm(x, g, eps=1e-6):
    xf = x.astype(jnp.float32)
    y = xf * lax.rsqrt(jnp.mean(xf * xf, axis=-1, keepdims=True) + eps)
    return (y * g.astype(jnp.float32)).astype(x.dtype)


def l2_norm(x, eps=1e-6):
    xf = x.astype(jnp.float32)
    return (xf * lax.rsqrt(jnp.sum(xf * xf, axis=-1, keepdims=True) + eps)).astype(x.dtype)


def rope_axis(x, pos):
    half = x.shape[-1] // 2
    inv = ROPE_BASE ** (-jnp.arange(half, dtype=jnp.float32) / half)
    ang = pos.astype(jnp.float32)[:, None] * inv[None, :]
    cos = jnp.cos(ang)[:, None, :]
    sin = jnp.sin(ang)[:, None, :]
    xf = x.astype(jnp.float32)
    x1, x2 = xf[..., :half], xf[..., half:]
    return jnp.concatenate([x1 * cos - x2 * sin, x2 * cos + x1 * sin], axis=-1).astype(x.dtype)


def rope_2d(x):
    t = jnp.arange(x.shape[1])
    half = x.shape[-1] // 2
    return jnp.concatenate([rope_axis(x[..., :half], t // GRID_W),
                            rope_axis(x[..., half:], t % GRID_W)], axis=-1)


def block_attention(q, k, v, scale):
    b, lq = q.shape[0], q.shape[1]
    nb = lq // Q_BLOCK
    qb = jnp.swapaxes(q.reshape(b, nb, Q_BLOCK, *q.shape[2:]), 0, 1)

    def one_block(qi):
        s = jnp.einsum('bqhgd,bkhd->bhgqk', qi, k, preferred_element_type=jnp.float32) * scale
        pr = jax.nn.softmax(s, axis=-1).astype(v.dtype)
        return jnp.einsum('bhgqk,bkhd->bqhgd', pr, v)

    o = lax.map(one_block, qb)
    return jnp.swapaxes(o, 0, 1).reshape(b, lq, *o.shape[3:])


def natten_latent(q, k, v, k_ctx, v_ctx, bias):
    b, n, h, d = q.shape
    rows = n // GRID_W
    kh = min(NA_KH, rows)
    kw = NA_KW
    r = jnp.arange(rows)
    col = jnp.arange(GRID_W)
    r0 = jnp.clip(r - kh // 2, 0, rows - kh)
    band = r0[:, None] + jnp.arange(kh)[None, :]
    c0 = jnp.clip(col - kw // 2, 0, GRID_W - kw)
    allowed = (col[None, :] >= c0[:, None]) & (col[None, :] < c0[:, None] + kw)
    ri = band - r[:, None] + (NA_KH - 1)
    ci = jnp.clip(col[None, :] - col[:, None] + (kw - 1), 0, 2 * kw - 2)
    rel = bias[:, ri[:, None, :, None], ci[None, :, None, :]].astype(jnp.float32)
    rel = jnp.where(allowed[None, None, :, None, :], rel, NEG_INF)
    rel = rel.transpose(1, 0, 2, 3, 4).reshape(rows, h, GRID_W, kh * GRID_W)
    qg = q.reshape(b, rows, GRID_W, h, d)
    kb = k.reshape(b, rows, GRID_W, h, d)[:, band].reshape(b, rows, kh * GRID_W, h, d)
    vb = v.reshape(b, rows, GRID_W, h, d)[:, band].reshape(b, rows, kh * GRID_W, h, d)
    scale = d ** -0.5
    s_loc = jnp.einsum('brchd,brkhd->brhck', qg, kb, preferred_element_type=jnp.float32) * scale + rel[None]
    s_ctx = jnp.einsum('brchd,blhd->brhcl', qg, k_ctx, preferred_element_type=jnp.float32) * scale
    pr = jax.nn.softmax(jnp.concatenate([s_loc, s_ctx], axis=-1), axis=-1).astype(v.dtype)
    nloc = kh * GRID_W
    o = (jnp.einsum('brhck,brkhd->brchd', pr[..., :nloc], vb)
         + jnp.einsum('brhcl,blhd->brchd', pr[..., nloc:], v_ctx))
    return o.reshape(b, n, h * d)


def short_conv(x, w):
    return lax.conv_general_dilated(
        x, w[:, None, :].astype(x.dtype), window_strides=(1,),
        padding=[((DN_CONV - 1) // 2, DN_CONV // 2)],
        dimension_numbers=('NWC', 'WIO', 'NWC'), feature_group_count=x.shape[-1])


def gated_delta_chunked(q, k, v, beta, logd, s0):
    b, l, h, dk = q.shape
    dv = v.shape[-1]
    c = DN_CHUNK
    n = l // c

    def chunks(t):
        t = t.astype(jnp.float32).reshape(b, n, c, h, *t.shape[3:])
        return jnp.moveaxis(t, (1, 3), (0, 2))

    qc, kc, vc = chunks(q), chunks(k), chunks(v)
    bc, gc = chunks(beta), chunks(logd)
    gcum = jnp.cumsum(gc, axis=-1)
    tril = jnp.tril(jnp.ones((c, c), dtype=bool))
    strict = jnp.tril(jnp.ones((c, c), dtype=bool), -1)
    dmask = jnp.where(tril, jnp.exp(jnp.where(tril, gcum[..., :, None] - gcum[..., None, :], 0.0)), 0.0)
    kb = kc * bc[..., None]
    a = jnp.where(strict, jnp.einsum('nbhid,nbhjd->nbhij', kb, kc) * dmask, 0.0)
    rhs = jnp.concatenate([vc * bc[..., None], kb * jnp.exp(gcum)[..., None]], axis=-1)
    sol = lax.linalg.triangular_solve(a + jnp.eye(c, dtype=jnp.float32), rhs,
                                      left_side=True, lower=True, unit_diagonal=True)
    u, w = sol[..., :dv], sol[..., dv:]
    qk = jnp.where(tril, jnp.einsum('nbhid,nbhjd->nbhij', qc, kc) * dmask, 0.0)
    q_dec = qc * jnp.exp(gcum)[..., None]
    k_dec = kc * jnp.exp(gcum[..., -1:] - gcum)[..., None]
    g_last = jnp.exp(gcum[..., -1])

    def step(s, xs):
        u_i, w_i, qk_i, qd_i, kd_i, gl_i = xs
        v_new = u_i - jnp.einsum('bhcd,bhde->bhce', w_i, s)
        o_i = jnp.einsum('bhcd,bhde->bhce', qd_i, s) + jnp.einsum('bhij,bhje->bhie', qk_i, v_new)
        s = s * gl_i[..., None, None] + jnp.einsum('bhcd,bhce->bhde', kd_i, v_new)
        return s, o_i

    s_fin, o = lax.scan(step, s0.astype(jnp.float32), (u, w, qk, q_dec, k_dec, g_last))
    o = jnp.moveaxis(o, (0, 2), (1, 3)).reshape(b, l, h, dv)
    return o.astype(v.dtype), s_fin.astype(v.dtype)


def flip_seq(t):
    return jnp.flip(t, axis=1)


def deltanet_mixer(zqkv, zg, za, zb, s0, p):
    b, l, _ = zqkv.shape
    qkv = jax.nn.silu(short_conv(zqkv, p['dn_conv_w']))
    q, k, v = jnp.split(qkv, [DN_HEADS * DN_DK, 2 * DN_HEADS * DN_DK], axis=-1)
    q = l2_norm(q.reshape(b, l, DN_HEADS, DN_DK)) * (DN_DK ** -0.5)
    k = l2_norm(k.reshape(b, l, DN_HEADS, DN_DK))
    v = v.reshape(b, l, DN_HEADS, DN_DV)
    a = za.reshape(b, l, 2, DN_HEADS).astype(jnp.float32)
    beta = jax.nn.sigmoid(zb.reshape(b, l, 2, DN_HEADS).astype(jnp.float32))
    logd = -jnp.exp(p['dn_a_log'].astype(jnp.float32)) * jax.nn.softplus(a + p['dn_dt_bias'].astype(jnp.float32))
    o_f, s_f = gated_delta_chunked(q, k, v, beta[:, :, 0], logd[:, :, 0], s0[:, 0])
    o_b, s_b = gated_delta_chunked(flip_seq(q), flip_seq(k), flip_seq(v),
                                   flip_seq(beta[:, :, 1]), flip_seq(logd[:, :, 1]), s0[:, 1])
    o = o_f + flip_seq(o_b)
    o = rms_norm(o, p['dn_out_g']) * jax.nn.silu(zg.reshape(b, l, DN_HEADS, DN_DV))
    return o.reshape(b, l, DN_HEADS * DN_DV), jnp.stack([s_f, s_b], axis=1)


def gqa_qkv(zq, zk, zv, p):
    b, l, _ = zq.shape
    q = rms_norm(zq.reshape(b, l, GQA_HEADS, HEAD_DIM), p['gqa_qn_g'])
    k = rms_norm(zk.reshape(b, l, GQA_KV_HEADS, HEAD_DIM), p['gqa_kn_g'])
    v = zv.reshape(b, l, GQA_KV_HEADS, HEAD_DIM)
    return q, k, v


def gqa_attend(q, k, v):
    b, lq = q.shape[0], q.shape[1]
    qg = q.reshape(b, lq, GQA_KV_HEADS, GQA_GROUP, HEAD_DIM)
    return block_attention(qg, k, v, HEAD_DIM ** -0.5).reshape(b, lq, GQA_HEADS * HEAD_DIM)


def mla_query(zcq, p, positioned):
    b, l, _ = zcq.shape
    q = (rms_norm(zcq, p['mla_qn_g']) @ p['mla_wq_up']).reshape(b, l, MLA_HEADS, MLA_NOPE + MLA_ROPE)
    if positioned:
        q = jnp.concatenate([q[..., :MLA_NOPE], rope_2d(q[..., MLA_NOPE:])], axis=-1)
    return q


def mla_keys_values(ckv, krope, p):
    b, l, _ = ckv.shape
    kv = (ckv @ p['mla_wkv_up']).reshape(b, l, MLA_HEADS, MLA_NOPE + MLA_V)
    k_rope = jnp.broadcast_to(krope[:, :, None, :], (b, l, MLA_HEADS, MLA_ROPE))
    k = jnp.concatenate([kv[..., :MLA_NOPE], k_rope], axis=-1)
    return k, kv[..., MLA_NOPE:]


def project_in(h, p):
    z = jnp.einsum('bld,de->ble', h, p['w_in'])
    offsets = np.cumsum(IN_SIZES)[:-1].tolist()
    return jnp.split(z, offsets, axis=-1)


def mix_context(h, p):
    b, l, _ = h.shape
    (a_q, a_k, a_v, b_q, b_k, b_v, c_qkv, c_g, c_a, c_b, d_cq, d_ckv, d_kr) = project_in(h, p)
    q, k, v = gqa_qkv(a_q, a_k, a_v, p)
    o_a = gqa_attend(q, k, v)
    nq = b_q.reshape(b, l, NA_HEADS, HEAD_DIM)
    nk = b_k.reshape(b, l, NA_HEADS, HEAD_DIM)
    nv = b_v.reshape(b, l, NA_HEADS, HEAD_DIM)
    o_b = block_attention(nq[:, :, :, None], nk, nv, HEAD_DIM ** -0.5).reshape(b, l, NA_HEADS * HEAD_DIM)
    s0 = jnp.zeros((b, 2, DN_HEADS, DN_DK, DN_DV), h.dtype)
    o_c, s_dn = deltanet_mixer(c_qkv, c_g, c_a, c_b, s0, p)
    q_m = mla_query(d_cq, p, False)
    ckv = rms_norm(d_ckv, p['mla_kvn_g'])
    k_m, v_m = mla_keys_values(ckv, d_kr, p)
    o_d = block_attention(q_m[:, :, :, None], k_m, v_m, MLA_SCALE).reshape(b, l, MLA_HEADS * MLA_V)
    out = jnp.concatenate([o_a, o_b, o_c, o_d], axis=-1) @ p['w_out']
    return out, (k, v, nk, nv, s_dn, ckv, d_kr)


def mix_latent(h, ck_a, cv_a, ck_b, cv_b, s_dn, c_ckv, c_kr, p):
    b, n, _ = h.shape
    (a_q, a_k, a_v, b_q, b_k, b_v, c_qkv, c_g, c_a, c_b, d_cq, d_ckv, d_kr) = project_in(h, p)
    q, k, v = gqa_qkv(a_q, a_k, a_v, p)
    o_a = gqa_attend(rope_2d(q), jnp.concatenate([ck_a, rope_2d(k)], axis=1), jnp.concatenate([cv_a, v], axis=1))
    nq = b_q.reshape(b, n, NA_HEADS, HEAD_DIM)
    nk = b_k.reshape(b, n, NA_HEADS, HEAD_DIM)
    nv = b_v.reshape(b, n, NA_HEADS, HEAD_DIM)
    o_b = natten_latent(nq, nk, nv, ck_b, cv_b, p['na_bias'])
    o_c, _ = deltanet_mixer(c_qkv, c_g, c_a, c_b, s_dn, p)
    q_m = mla_query(d_cq, p, True)
    ckv = rms_norm(d_ckv, p['mla_kvn_g'])
    kr = rope_2d(d_kr[:, :, None, :])[:, :, 0, :]
    k_lat, v_lat = mla_keys_values(ckv, kr, p)
    k_ctx, v_ctx = mla_keys_values(c_ckv, c_kr, p)
    o_d = block_attention(q_m[:, :, :, None], jnp.concatenate([k_ctx, k_lat], axis=1),
                          jnp.concatenate([v_ctx, v_lat], axis=1), MLA_SCALE).reshape(b, n, MLA_HEADS * MLA_V)
    return jnp.concatenate([o_a, o_b, o_c, o_d], axis=-1) @ p['w_out']


def modulation(cond, p):
    m = jax.nn.silu(cond) @ p['w_mod'] + p['b_mod']
    return jnp.split(m[:, None, :], MOD_CHUNKS, axis=-1)


def modulated_norm(x, g, shift, scale):
    return rms_norm(x, g) * (1 + scale) + shift


def ffn_sublayer(x, mods, p):
    h = modulated_norm(x, p['norm2_g'], mods[3], mods[4])
    y = (jax.nn.silu(h @ p['ffn_w_gate']) * (h @ p['ffn_w_up'])) @ p['ffn_w_down']
    return x + mods[5] * y


def setup_inputs(seed: int = 0) -> dict:
    key = jax.random.key(seed)
    ks = jax.random.split(key, 32)

    def nrm(i, shape, s=1.0):
        return s * jax.random.normal(ks[i], shape, dtype=jnp.float32)

    def gain(i, shape):
        return 1.0 + 0.05 * jax.random.normal(ks[i], shape, dtype=jnp.float32)

    dt = jnp.exp(jax.random.uniform(ks[20], (DEPTH, 2, DN_HEADS), minval=math.log(1e-3), maxval=math.log(1e-1)))
    return {
        'x_prompt': nrm(0, (BATCH, SEQ, D_MODEL)),
        'x_sample': nrm(1, (DEC_BATCH, DEC_SEQ, D_MODEL)),
        'cache_gqa_k': nrm(2, (DEC_BATCH, DEPTH, PAST_LEN, GQA_KV_HEADS, HEAD_DIM)),
        'cache_gqa_v': nrm(3, (DEC_BATCH, DEPTH, PAST_LEN, GQA_KV_HEADS, HEAD_DIM)),
        'cache_na_k': nrm(4, (DEC_BATCH, DEPTH, PAST_LEN, NA_HEADS, HEAD_DIM)),
        'cache_na_v': nrm(5, (DEC_BATCH, DEPTH, PAST_LEN, NA_HEADS, HEAD_DIM)),
        'state_dn': nrm(6, (DEC_BATCH, DEPTH, 2, DN_HEADS, DN_DK, DN_DV), 0.1),
        'cache_mla_ckv': nrm(7, (DEC_BATCH, DEPTH, PAST_LEN, MLA_KV_LORA)),
        'cache_mla_krope': nrm(8, (DEC_BATCH, DEPTH, PAST_LEN, MLA_ROPE)),
        'c': nrm(9, (DEC_BATCH, D_MODEL)),
        'c_ctx': nrm(10, (D_MODEL,)),
        'norm1_g': gain(11, (DEPTH, D_MODEL)),
        'norm2_g': gain(12, (DEPTH, D_MODEL)),
        'w_mod': nrm(13, (DEPTH, D_MODEL, MOD_CHUNKS * D_MODEL), 0.5 * D_MODEL ** -0.5),
        'b_mod': nrm(14, (DEPTH, MOD_CHUNKS * D_MODEL), 0.01),
        'w_in': nrm(15, (DEPTH, D_MODEL, IN_COLS), D_MODEL ** -0.5),
        'w_out': nrm(16, (DEPTH, MIX_W, D_MODEL), MIX_W ** -0.5),
        'gqa_qn_g': gain(17, (DEPTH, HEAD_DIM)),
        'gqa_kn_g': gain(18, (DEPTH, HEAD_DIM)),
        'na_bias': nrm(19, (DEPTH, NA_HEADS, 2 * NA_KH - 1, 2 * NA_KW - 1), 0.1),
        'dn_conv_w': nrm(21, (DEPTH, DN_CONV, DN_QKV), DN_CONV ** -0.5),
        'dn_a_log': jnp.log(jax.random.uniform(ks[22], (DEPTH, 2, DN_HEADS), minval=1.0, maxval=16.0)),
        'dn_dt_bias': dt + jnp.log(-jnp.expm1(-dt)),
        'dn_out_g': gain(23, (DEPTH, DN_DV)),
        'mla_qn_g': gain(24, (DEPTH, MLA_Q_LORA)),
        'mla_wq_up': nrm(25, (DEPTH, MLA_Q_LORA, MLA_HEADS * (MLA_NOPE + MLA_ROPE)), MLA_Q_LORA ** -0.5),
        'mla_kvn_g': gain(26, (DEPTH, MLA_KV_LORA)),
        'mla_wkv_up': nrm(27, (DEPTH, MLA_KV_LORA, MLA_HEADS * (MLA_NOPE + MLA_V)), MLA_KV_LORA ** -0.5),
        'ffn_w_gate': nrm(28, (DEPTH, D_MODEL, D_FF), D_MODEL ** -0.5),
        'ffn_w_up': nrm(29, (DEPTH, D_MODEL, D_FF), D_MODEL ** -0.5),
        'ffn_w_down': nrm(30, (DEPTH, D_FF, D_MODEL), D_FF ** -0.5),
        'final_g': gain(31, (D_MODEL,)),
    }


def reference(x_prompt, x_sample, cache_gqa_k, cache_gqa_v, cache_na_k, cache_na_v, state_dn,
              cache_mla_ckv, cache_mla_krope, c, c_ctx, norm1_g, norm2_g, w_mod, b_mod, w_in, w_out,
              gqa_qn_g, gqa_kn_g, na_bias, dn_conv_w, dn_a_log, dn_dt_bias, dn_out_g, mla_qn_g,
              mla_wq_up, mla_kvn_g, mla_wkv_up, ffn_w_gate, ffn_w_up, ffn_w_down, final_g):
    stacked = {
        'norm1_g': norm1_g, 'norm2_g': norm2_g, 'w_mod': w_mod, 'b_mod': b_mod, 'w_in': w_in,
        'w_out': w_out, 'gqa_qn_g': gqa_qn_g, 'gqa_kn_g': gqa_kn_g, 'na_bias': na_bias,
        'dn_conv_w': dn_conv_w, 'dn_a_log': dn_a_log, 'dn_dt_bias': dn_dt_bias, 'dn_out_g': dn_out_g,
        'mla_qn_g': mla_qn_g, 'mla_wq_up': mla_wq_up, 'mla_kvn_g': mla_kvn_g, 'mla_wkv_up': mla_wkv_up,
        'ffn_w_gate': ffn_w_gate, 'ffn_w_up': ffn_w_up, 'ffn_w_down': ffn_w_down,
    }

    x = x_prompt
    ctx_cond = c_ctx[None, :]
    per_layer = []
    for l in range(DEPTH):
        p = {name: arr[l] for name, arr in stacked.items()}
        mods = modulation(ctx_cond, p)
        mix, st = mix_context(modulated_norm(x, p['norm1_g'], mods[0], mods[1]), p)
        x = ffn_sublayer(x + mods[2] * mix, mods, p)
        per_layer.append(st)
    y_prompt = rms_norm(x, final_g)
    new_gqa_k = jnp.stack([s[0] for s in per_layer], axis=1)
    new_gqa_v = jnp.stack([s[1] for s in per_layer], axis=1)
    new_na_k = jnp.stack([s[2] for s in per_layer], axis=1)
    new_na_v = jnp.stack([s[3] for s in per_layer], axis=1)
    new_dn_state = jnp.stack([s[4] for s in per_layer], axis=1)
    new_mla_ckv = jnp.stack([s[5] for s in per_layer], axis=1)
    new_mla_krope = jnp.stack([s[6] for s in per_layer], axis=1)

    x = x_sample
    for l in range(DEPTH):
        p = {name: arr[l] for name, arr in stacked.items()}
        mods = modulation(c, p)
        mix = mix_latent(modulated_norm(x, p['norm1_g'], mods[0], mods[1]),
                         cache_gqa_k[:, l], cache_gqa_v[:, l], cache_na_k[:, l], cache_na_v[:, l],
                         state_dn[:, l], cache_mla_ckv[:, l], cache_mla_krope[:, l], p)
        x = ffn_sublayer(x + mods[2] * mix, mods, p)
    y_sample = rms_norm(x, final_g)

    return (y_prompt, y_sample, new_gqa_k, new_gqa_v, new_na_k, new_na_v, new_dn_state, new_mla_ckv, new_mla_krope)
```

```python
import functools

import numpy as np
import jax
import jax.numpy as jnp
from jax import lax
from jax.experimental import pallas as pl
from jax.experimental.pallas import tpu as pltpu

F32 = jnp.float32
BF16 = jnp.bfloat16

D_MODEL = 1024
DEPTH = 2
GRID_W = 64
HEAD_DIM = 64
ROPE_BASE = 10000.0
NEG_INF = -1e30
MOD_CHUNKS = 6
GQA_HEADS, GQA_KV_HEADS = 4, 2
NA_HEADS, NA_KH, NA_KW = 4, 8, 16
DN_HEADS, DN_DK, DN_DV, DN_CONV, DN_CHUNK = 4, 64, 64, 4, 64
DN_QKV = DN_HEADS * (2 * DN_DK + DN_DV)
MLA_HEADS, MLA_Q_LORA, MLA_KV_LORA, MLA_NOPE, MLA_ROPE, MLA_V = 4, 256, 128, 64, 32, 64
MLA_SCALE = (MLA_NOPE + MLA_ROPE) ** -0.5
D_FF = -(-8 * D_MODEL // (3 * 256)) * 256
EPS = 1e-6

LANES = 128
ROW_TILE = 256
Q_TILE = 256
NA_BAND = 768
IN_PAD_COLS = 2816
VMEM_LIMIT = 56 * 1024 * 1024

_NT = (((1,), (1,)), ((), ()))


def _cparams(n_axes):
    return pltpu.CompilerParams(dimension_semantics=("arbitrary",) * n_axes,
                                vmem_limit_bytes=VMEM_LIMIT)


def _lane(shape):
    return lax.broadcasted_iota(jnp.int32, shape, len(shape) - 1)


def _silu(x):
    return x / (1.0 + jnp.exp(-x))


def _rms_full(x, g):
    return x * lax.rsqrt(jnp.mean(x * x, axis=-1, keepdims=True) + EPS) * g


def _seg64_sum(x):
    lo = _lane(x.shape) < HEAD_DIM
    s_lo = jnp.sum(jnp.where(lo, x, 0.0), axis=-1, keepdims=True)
    s_hi = jnp.sum(jnp.where(lo, 0.0, x), axis=-1, keepdims=True)
    return jnp.where(lo, s_lo, s_hi)


def _head_rms(x, g):
    parts = []
    for p in range(x.shape[-1] // LANES):
        xp = x[:, p * LANES:(p + 1) * LANES]
        ms = _seg64_sum(xp * xp) * (1.0 / HEAD_DIM)
        parts.append(xp * lax.rsqrt(ms + EPS))
    y = parts[0] if len(parts) == 1 else jnp.concatenate(parts, axis=-1)
    return y * g


def _head_l2(x):
    parts = []
    for p in range(x.shape[-1] // LANES):
        xp = x[:, p * LANES:(p + 1) * LANES]
        parts.append(xp * lax.rsqrt(_seg64_sum(xp * xp) + EPS))
    return parts[0] if len(parts) == 1 else jnp.concatenate(parts, axis=-1)


def _rope(x, cos, sin, half):
    first = (_lane(x.shape) & (2 * half - 1)) < half
    rot = jnp.where(first, pltpu.roll(x, LANES - half, 1), pltpu.roll(x, half, 1))
    return x * cos + rot * sin


def _softmax_parts(scores):
    m = jnp.max(scores[0], axis=-1, keepdims=True)
    for s in scores[1:]:
        m = jnp.maximum(m, jnp.max(s, axis=-1, keepdims=True))
    es = [jnp.exp(s - m) for s in scores]
    l = jnp.sum(es[0], axis=-1, keepdims=True)
    for e in es[1:]:
        l = l + jnp.sum(e, axis=-1, keepdims=True)
    return es, 1.0 / l


def _bdot(a, b):
    return jnp.dot(a, b, preferred_element_type=F32)


def _mod_kernel(c_ref, w_ref, b_ref, o_ref):
    s = _silu(c_ref[...]).astype(BF16)
    o_ref[...] = _bdot(s, w_ref[...].astype(BF16)) + b_ref[...]


def _modulation(cond, w_mod, b_mod):
    n = MOD_CHUNKS * D_MODEL
    tn = 1536
    return pl.pallas_call(
        _mod_kernel,
        grid=(DEPTH, n // tn),
        in_specs=[pl.BlockSpec((16, D_MODEL), lambda l, j: (0, 0)),
                  pl.BlockSpec((None, D_MODEL, tn), lambda l, j: (l, 0, j)),
                  pl.BlockSpec((None, 1, tn), lambda l, j: (l, 0, j))],
        out_specs=pl.BlockSpec((None, 16, tn), lambda l, j: (l, 0, j)),
        out_shape=jax.ShapeDtypeStruct((DEPTH, 16, n), F32),
        compiler_params=_cparams(2),
        name="modulation",
    )(cond, w_mod, b_mod.reshape(DEPTH, 1, n))


_IN_OUT_WIDTHS = (256, 128, 128, 256, 256, 256, 1024, 512, 128, 128, 128)


def _inproj_kernel(positioned, *refs):
    (x_ref, mod_ref, g1_ref, w_ref, qng_ref, kng_ref, mqg_ref, wq_ref, mkg_ref) = refs[:9]
    n_in = 9
    if positioned:
        cos64_ref, sin64_ref, cosm_ref, sinm_ref, coskr_ref, sinkr_ref = refs[9:15]
        n_in = 15
    (qa_ref, ka_ref, va_ref, qb_ref, kb_ref, vb_ref, zc_ref, qd_ref, ckv_ref, zm_ref,
     krr_ref) = refs[n_in:]

    m = mod_ref[...]
    h = _rms_full(x_ref[...], g1_ref[...]) * (1.0 + m[1:2]) + m[0:1]
    hb = h.astype(BF16)

    za = _bdot(hb, w_ref[:, 0:512])
    q = _head_rms(za[:, 0:256], qng_ref[...])
    k = _head_rms(za[:, 256:384], kng_ref[...])
    if positioned:
        cos, sin = cos64_ref[...], sin64_ref[...]
        q = jnp.concatenate([_rope(q[:, 0:128], cos, sin, 16), _rope(q[:, 128:256], cos, sin, 16)],
                            axis=-1)
        k = _rope(k, cos, sin, 16)
    qa_ref[...] = q * (HEAD_DIM ** -0.5)
    ka_ref[...] = k
    va_ref[...] = za[:, 384:512]

    zb = _bdot(hb, w_ref[:, 512:1280])
    qb_ref[...] = zb[:, 0:256] * (HEAD_DIM ** -0.5)
    kb_ref[...] = zb[:, 256:512]
    vb_ref[...] = zb[:, 512:768]

    zc_ref[...] = _bdot(hb, w_ref[:, 1280:2304])

    zd = _bdot(hb, w_ref[:, 2304:2816])
    cq = _rms_full(zd[:, 0:256], mqg_ref[...])
    qm = _bdot(cq.astype(BF16), wq_ref[...])
    if positioned:
        cm, sm = cosm_ref[...], sinm_ref[...]
        qm = jnp.concatenate([_rope(qm[:, i * LANES:(i + 1) * LANES], cm, sm, 8)
                              for i in range(MLA_HEADS)], axis=-1)
    qd_ref[...] = qm * MLA_SCALE
    ckv_ref[...] = _rms_full(zd[:, 256:384], mkg_ref[...])
    zmisc = zd[:, 384:512]
    zm_ref[...] = zmisc
    kr = jnp.where(_lane(zmisc.shape) < MLA_ROPE, zmisc, 0.0)
    if positioned:
        kr = _rope(kr, coskr_ref[...], sinkr_ref[...], 8)
    krr_ref[...] = kr


def _inproj(x, mods, per_batch_mods, lw, ropes):
    b, l, _ = x.shape
    tm = ROW_TILE
    positioned = ropes is not None
    row = lambda w: pl.BlockSpec((None, tm, w), lambda bi, i: (bi, i, 0))
    const = lambda shape: pl.BlockSpec(shape, lambda bi, i: (0,) * len(shape))
    mod_spec = pl.BlockSpec((None, MOD_CHUNKS, D_MODEL),
                            (lambda bi, i: (bi, 0, 0)) if per_batch_mods else (lambda bi, i: (0, 0, 0)))
    in_specs = [row(D_MODEL), mod_spec, const((1, D_MODEL)), const((D_MODEL, IN_PAD_COLS)),
                const((1, 256)), const((1, 128)), const((1, MLA_Q_LORA)),
                const((MLA_Q_LORA, 4 * LANES)), const((1, MLA_KV_LORA))]
    args = [x, mods, lw["norm1_g"], lw["w_in"], lw["qn_g"], lw["kn_g"], lw["mla_qn_g"], lw["wq"],
            lw["mla_kvn_g"]]
    if positioned:
        in_specs += [pl.BlockSpec((tm, LANES), lambda bi, i: (i, 0))] * 6
        args += list(ropes)
    return pl.pallas_call(
        functools.partial(_inproj_kernel, positioned),
        grid=(b, l // tm),
        in_specs=in_specs,
        out_specs=[row(w) for w in _IN_OUT_WIDTHS],
        out_shape=[jax.ShapeDtypeStruct((b, l, w), F32) for w in _IN_OUT_WIDTHS],
        compiler_params=_cparams(2),
        name="inproj_lat" if positioned else "inproj_ctx",
    )(*args)


def _attn_pair_kernel(has_cache, qmap, *refs):
    if has_cache:
        q_ref, kc_ref, vc_ref, k_ref, v_ref, o_ref, kbuf, vbuf = refs
    else:
        q_ref, k_ref, v_ref, o_ref, kbuf, vbuf = refs

    @pl.when(pl.program_id(1) == 0)
    def _():
        off = 0
        if has_cache:
            off = kc_ref.shape[0]
            kbuf[0:off, :] = kc_ref[...].astype(BF16)
            vbuf[0:off, :] = vc_ref[...].astype(BF16)
        kbuf[off:, :] = k_ref[...].astype(BF16)
        vbuf[off:, :] = v_ref[...].astype(BF16)

    tq = q_ref.shape[0]
    lo = _lane((tq, LANES)) < HEAD_DIM
    for p, kv in enumerate(qmap):
        qp = q_ref[:, p * LANES:(p + 1) * LANES]
        kblk = kbuf[:, kv * LANES:(kv + 1) * LANES]
        vblk = vbuf[:, kv * LANES:(kv + 1) * LANES]
        outs = []
        for half in range(2):
            qm = jnp.where(lo if half == 0 else jnp.logical_not(lo), qp, 0.0).astype(BF16)
            s = lax.dot_general(qm, kblk, _NT, preferred_element_type=F32)
            (e,), rl = _softmax_parts([s])
            outs.append(_bdot(e.astype(BF16), vblk) * rl)
        o_ref[:, p * LANES:(p + 1) * LANES] = jnp.where(lo, outs[0], outs[1])


def _attn_pair(q, k, v, qmap, cache=None, name="attn_pair"):
    b, lq, wq = q.shape
    ls, wk = k.shape[1], k.shape[2]
    tq = Q_TILE
    lc = 0 if cache is None else cache[0].shape[2]
    in_specs = [pl.BlockSpec((None, tq, wq), lambda bi, i: (bi, i, 0))]
    args = [q]
    if cache is not None:
        kc, vc, layer = cache
        cspec = pl.BlockSpec((None, None, lc, wk), lambda bi, i: (bi, layer, 0, 0))
        in_specs += [cspec, cspec]
        args += [kc, vc]
    sspec = pl.BlockSpec((None, ls, wk), lambda bi, i: (bi, 0, 0))
    in_specs += [sspec, sspec]
    args += [k, v]
    return pl.pallas_call(
        functools.partial(_attn_pair_kernel, cache is not None, qmap),
        grid=(b, lq // tq),
        in_specs=in_specs,
        out_specs=pl.BlockSpec((None, tq, wq), lambda bi, i: (bi, i, 0)),
        out_shape=jax.ShapeDtypeStruct((b, lq, wq), F32),
        scratch_shapes=[pltpu.VMEM((lc + ls, wk), BF16), pltpu.VMEM((lc + ls, wk), BF16)],
        compiler_params=_cparams(2),
        name=name,
    )(*args)


def _mla_kernel(has_cache, *refs):
    if has_cache:
        q_ref, ckvc_ref, krc_ref, ckv_ref, kr_ref, wk_ref, wv_ref, o_ref, kbuf, vbuf = refs
    else:
        q_ref, ckv_ref, kr_ref, wk_ref, wv_ref, o_ref, kbuf, vbuf = refs

    @pl.when(pl.program_id(1) == 0)
    def _():
        def expand(c_ref, r_ref, r0, r1):
            c = c_ref[...].astype(BF16)
            ckr = jnp.concatenate([c, r_ref[...].astype(BF16)], axis=-1)
            kbuf[r0:r1, :] = _bdot(ckr, wk_ref[...]).astype(BF16)
            vbuf[r0:r1, :] = _bdot(c, wv_ref[...]).astype(BF16)
        off = 0
        if has_cache:
            off = ckvc_ref.shape[0]
            expand(ckvc_ref, krc_ref, 0, off)
        expand(ckv_ref, kr_ref, off, kbuf.shape[0])

    tq = q_ref.shape[0]
    lo = _lane((tq, LANES)) < MLA_V
    for p in range(MLA_HEADS // 2):
        vblk = vbuf[:, p * LANES:(p + 1) * LANES]
        outs = []
        for half in range(2):
            hd = 2 * p + half
            qh = q_ref[:, hd * LANES:(hd + 1) * LANES].astype(BF16)
            s = lax.dot_general(qh, kbuf[:, hd * LANES:(hd + 1) * LANES], _NT,
                                preferred_element_type=F32)
            (e,), rl = _softmax_parts([s])
            outs.append(_bdot(e.astype(BF16), vblk) * rl)
        o_ref[:, p * LANES:(p + 1) * LANES] = jnp.where(lo, outs[0], outs[1])


def _mla(q, ckv, kr, wk, wv, cache=None, name="mla"):
    b, lq, wq = q.shape
    ls = ckv.shape[1]
    tq = Q_TILE
    lc = 0 if cache is None else cache[0].shape[2]
    in_specs = [pl.BlockSpec((None, tq, wq), lambda bi, i: (bi, i, 0))]
    args = [q]
    if cache is not None:
        ckvc, krc, layer = cache
        cspec = pl.BlockSpec((None, None, lc, LANES), lambda bi, i: (bi, layer, 0, 0))
        in_specs += [cspec, cspec]
        args += [ckvc, krc]
    sspec = pl.BlockSpec((None, ls, LANES), lambda bi, i: (bi, 0, 0))
    in_specs += [sspec, sspec,
                 pl.BlockSpec((2 * LANES, 4 * LANES), lambda bi, i: (0, 0)),
                 pl.BlockSpec((LANES, 2 * LANES), lambda bi, i: (0, 0))]
    args += [ckv, kr, wk, wv]
    return pl.pallas_call(
        functools.partial(_mla_kernel, cache is not None),
        grid=(b, lq // tq),
        in_specs=in_specs,
        out_specs=pl.BlockSpec((None, tq, 2 * LANES), lambda bi, i: (bi, i, 0)),
        out_shape=jax.ShapeDtypeStruct((b, lq, 2 * LANES), F32),
        scratch_shapes=[pltpu.VMEM((lc + ls, 4 * LANES), BF16), pltpu.VMEM((lc + ls, 2 * LANES), BF16)],
        compiler_params=_cparams(2),
        name=name,
    )(*args)


def _na_kernel(q_ref, k_ref, v_ref, kc_ref, vc_ref, bias_ref, o_ref):
    j = pl.program_id(0)
    start = pl.multiple_of((j >> 1) * 256, 256)
    kband = k_ref[pl.ds(start, NA_BAND), :].astype(BF16)
    vband = v_ref[pl.ds(start, NA_BAND), :].astype(BF16)
    kc = kc_ref[...].astype(BF16)
    vc = vc_ref[...].astype(BF16)
    tq = q_ref.shape[0]
    lo = _lane((tq, LANES)) < HEAD_DIM
    for p in range(NA_HEADS // 2):
        sl = slice(p * LANES, (p + 1) * LANES)
        qp = q_ref[:, sl]
        outs = []
        for half in range(2):
            qm = jnp.where(lo if half == 0 else jnp.logical_not(lo), qp, 0.0).astype(BF16)
            s_loc = lax.dot_general(qm, kband[:, sl], _NT, preferred_element_type=F32)
            s_loc = s_loc + bias_ref[2 * p + half]
            s_ctx = lax.dot_general(qm, kc[:, sl], _NT, preferred_element_type=F32)
            (e_loc, e_ctx), rl = _softmax_parts([s_loc, s_ctx])
            o = _bdot(e_loc.astype(BF16), vband[:, sl]) + _bdot(e_ctx.astype(BF16), vc[:, sl])
            outs.append(o * rl)
        o_ref[:, sl] = jnp.where(lo, outs[0], outs[1])


def _na_latent(q, k, v, kc, vc, layer, bias_blocks):
    b, n, w = q.shape
    lc = kc.shape[2]
    nq = n // Q_TILE
    full = pl.BlockSpec((None, n, w), lambda j, bi: (bi, 0, 0))
    cspec = pl.BlockSpec((None, None, lc, w), lambda j, bi: (bi, layer, 0, 0))
    return pl.pallas_call(
        _na_kernel,
        grid=(nq, b),
        in_specs=[pl.BlockSpec((None, Q_TILE, w), lambda j, bi: (bi, j, 0)), full, full, cspec, cspec,
                  pl.BlockSpec((None, NA_HEADS, Q_TILE, NA_BAND), lambda j, bi: (j, 0, 0, 0))],
        out_specs=pl.BlockSpec((None, Q_TILE, w), lambda j, bi: (bi, j, 0)),
        out_shape=jax.ShapeDtypeStruct((b, n, w), F32),
        compiler_params=_cparams(2),
        name="na_latent",
    )(q, k, v, kc, vc, bias_blocks)


def _na_bias_blocks(bias):
    rows = 16
    qtok = np.arange(rows * GRID_W).reshape(4, Q_TILE)
    r, c = qtok // GRID_W, qtok % GRID_W
    band_row0 = np.array([0, 0, 4, 4])
    kk = np.arange(NA_BAND)
    kr = band_row0[:, None] + kk[None, :] // GRID_W
    kc = kk % GRID_W
    r0 = np.clip(r - NA_KH // 2, 0, rows - NA_KH)
    c0 = np.clip(c - NA_KW // 2, 0, GRID_W - NA_KW)
    ok = ((kr[:, None, :] >= r0[:, :, None]) & (kr[:, None, :] < r0[:, :, None] + NA_KH)
          & (kc[None, None, :] >= c0[:, :, None]) & (kc[None, None, :] < c0[:, :, None] + NA_KW))
    ri = np.clip(kr[:, None, :] - r[:, :, None] + (NA_KH - 1), 0, 2 * NA_KH - 2)
    ci = np.clip(kc[None, None, :] - c[:, :, None] + (NA_KW - 1), 0, 2 * NA_KW - 2)
    rel = bias.astype(F32)[:, ri, ci]
    rel = jnp.where(ok[None], rel, NEG_INF)
    return rel.transpose(1, 0, 2, 3)


def _block_diag_rows(z):
    blk = _lane(z.shape) >> 6
    return jnp.concatenate([jnp.where(blk == hd, z, 0.0).astype(BF16) for hd in range(DN_HEADS)], axis=0)


def _widen(cols, n):
    blk = _lane((n, DN_HEADS * DN_DV)) >> 6
    return jnp.where(blk == 0, cols[0], jnp.where(blk == 1, cols[1], jnp.where(blk == 2, cols[2], cols[3])))


def _deltanet_kernel(seq, has_state, *refs):
    if has_state:
        (zc_ref, zm_ref, s0_ref, cw_ref, alog_ref, dtb_ref, og_ref, o_ref,
         q_s, k_s, v_s, b_s, g_s, o_s) = refs
    else:
        (zc_ref, zm_ref, cw_ref, alog_ref, dtb_ref, og_ref, o_ref, sfin_ref,
         q_s, k_s, v_s, b_s, g_s, o_s) = refs
    n_chunks = seq // DN_CHUNK
    wide = DN_HEADS * DN_DV

    row = lax.broadcasted_iota(jnp.int32, (seq, wide), 0)
    for part, dst in enumerate((q_s, k_s, v_s)):
        cs = slice(part * wide, (part + 1) * wide)
        x = zc_ref[:, cs]
        w = cw_ref[:, cs]
        y = (w[0:1] * jnp.where(row >= 1, pltpu.roll(x, 1, 0), 0.0) + w[1:2] * x
             + w[2:3] * jnp.where(row < seq - 1, pltpu.roll(x, seq - 1, 0), 0.0)
             + w[3:4] * jnp.where(row < seq - 2, pltpu.roll(x, seq - 2, 0), 0.0))
        y = _silu(y)
        if part == 0:
            y = _head_l2(y) * (DN_DK ** -0.5)
        elif part == 1:
            y = _head_l2(y)
        dst[...] = y

    zm = zm_ref[...]
    xa = zm + dtb_ref[...]
    logd = -jnp.exp(alog_ref[...]) * (jnp.maximum(xa, 0.0) + jnp.log1p(jnp.exp(-jnp.abs(xa))))
    beta = 1.0 / (1.0 + jnp.exp(-zm))
    a_off, b_off = MLA_ROPE, MLA_ROPE + 2 * DN_HEADS
    ri = lax.broadcasted_iota(jnp.int32, (256, 256), 0)
    ci = lax.broadcasted_iota(jnp.int32, (256, 256), 1)
    same = (ri >> 6) == (ci >> 6)
    for d in range(2):
        b_s[d] = _widen([beta[:, b_off + 4 * d + hd:b_off + 4 * d + hd + 1] for hd in range(DN_HEADS)], seq)
        lw = _widen([logd[:, a_off + 4 * d + hd:a_off + 4 * d + hd + 1] for hd in range(DN_HEADS)], seq)
        tri = jnp.where(same & ((ci <= ri) if d == 0 else (ci >= ri)), 1.0, 0.0).astype(F32)
        for rb in range(seq // 256):
            rs = slice(rb * 256, (rb + 1) * 256)
            g_s[d, rs, :] = jnp.dot(tri, lw[rs], preferred_element_type=F32,
                                    precision=lax.Precision.HIGHEST)

    ii = lax.broadcasted_iota(jnp.int32, (DN_CHUNK, wide), 0)
    jj = _lane((DN_CHUNK, wide)) & (DN_CHUNK - 1)
    blk = _lane((DN_CHUNK, wide)) >> 6
    diag = ii == jj

    for d in range(2):
        incl = (jj <= ii) if d == 0 else (jj >= ii)
        strict = (jj < ii) if d == 0 else (jj > ii)
        pair = [((ii >> (lvl + 1)) == (jj >> (lvl + 1)))
                & (((ii >> lvl) & 1) == (1 - d)) & (((jj >> lvl) & 1) == d) for lvl in range(6)]

        def chunk(ci_, s, d=d, incl=incl, strict=strict, pair=pair):
            c = ci_ if d == 0 else n_chunks - 1 - ci_
            r0 = pl.multiple_of(c * DN_CHUNK, DN_CHUNK)
            rows = pl.ds(r0, DN_CHUNK)
            q, k, v = q_s[rows, :], k_s[rows, :], v_s[rows, :]
            beta_c, g = b_s[d, rows, :], g_s[d, rows, :]
            kb = k * beta_c
            r = lax.dot_general(jnp.concatenate([kb, q], axis=0).astype(BF16), _block_diag_rows(k), _NT,
                                preferred_element_type=F32)
            g_row = jnp.sum(jnp.where(diag, g, 0.0), axis=0, keepdims=True)
            dm = jnp.where(incl, jnp.exp(jnp.where(incl, g - g_row, 0.0)), 0.0)
            a = jnp.where(strict, r[0:DN_CHUNK] * dm, 0.0)
            qk = r[DN_CHUNK:] * dm
            eg = jnp.exp(g)
            t = jnp.where(diag, 1.0, 0.0) - jnp.where(pair[0], a, 0.0)
            for lvl in range(1, 6):
                te = _bdot(t.astype(BF16), _block_diag_rows(jnp.where(pair[lvl], a, 0.0)))
                t = t - _bdot(te.astype(BF16), _block_diag_rows(t))
            nb = jnp.where(diag, 0.0, t).astype(BF16)
            rhs_u, rhs_w = v * beta_c, kb * eg
            u = rhs_u + _bdot(nb, _block_diag_rows(rhs_u))
            w = rhs_w + _bdot(nb, _block_diag_rows(rhs_w))
            ws = _bdot(jnp.concatenate([w, q * eg], axis=0).astype(BF16), _block_diag_rows(s))
            v_new = u - ws[0:DN_CHUNK]
            o = ws[DN_CHUNK:] + _bdot(qk.astype(BF16), _block_diag_rows(v_new))
            g_last = g[DN_CHUNK - 1:DN_CHUNK] if d == 0 else g[0:1]
            kd = k * jnp.exp(g_last - g)
            gram = lax.dot_general(kd.astype(BF16), v_new.astype(BF16), (((0,), (0,)), ((), ())),
                                   preferred_element_type=F32)
            s_add = jnp.where(blk == 0, gram[0:64], 0.0)
            for hd in range(1, DN_HEADS):
                s_add = s_add + jnp.where(blk == hd, gram[hd * 64:(hd + 1) * 64], 0.0)
            if d == 0:
                o_s[rows, :] = o
            else:
                o_s[rows, :] = o_s[rows, :] + o
            return s * jnp.exp(g_last) + s_add

        s0 = s0_ref[d] if has_state else jnp.zeros((DN_DK, wide), F32)
        s_fin = lax.fori_loop(0, n_chunks, chunk, s0)
        if not has_state:
            sfin_ref[d] = s_fin

    o_ref[...] = _head_rms(o_s[...], og_ref[...]) * _silu(zc_ref[:, 3 * wide:4 * wide])


def _deltanet(zc, zm, lw, state=None):
    b, seq, _ = zc.shape
    wide = DN_HEADS * DN_DV
    has_state = state is not None
    per_b = lambda w: pl.BlockSpec((None, seq, w), lambda bi: (bi, 0, 0))
    const = lambda shape: pl.BlockSpec(shape, lambda bi: (0,) * len(shape))
    st_spec = pl.BlockSpec((None, 2, DN_DK, wide), lambda bi: (bi, 0, 0, 0))
    in_specs = [per_b(4 * wide), per_b(LANES)]
    args = [zc, zm]
    if has_state:
        in_specs.append(st_spec)
        args.append(state)
    in_specs += [const((DN_CONV, DN_QKV)), const((1, LANES)), const((1, LANES)), const((1, wide))]
    args += [lw["dn_conv_w"], lw["dn_alog_row"], lw["dn_dtb_row"], lw["dn_out_g"]]
    out_specs = [per_b(wide)]
    out_shape = [jax.ShapeDtypeStruct((b, seq, wide), F32)]
    if not has_state:
        out_specs.append(st_spec)
        out_shape.append(jax.ShapeDtypeStruct((b, 2, DN_DK, wide), F32))
    res = pl.pallas_call(
        functools.partial(_deltanet_kernel, seq, has_state),
        grid=(b,),
        in_specs=in_specs,
        out_specs=out_specs,
        out_shape=out_shape,
        scratch_shapes=[pltpu.VMEM((seq, wide), F32), pltpu.VMEM((seq, wide), F32),
                        pltpu.VMEM((seq, wide), F32), pltpu.VMEM((2, seq, wide), F32),
                        pltpu.VMEM((2, seq, wide), F32), pltpu.VMEM((seq, wide), F32)],
        compiler_params=_cparams(1),
        name="deltanet_lat" if has_state else "deltanet_ctx",
    )(*args)
    return (res[0], None) if has_state else (res[0], res[1])


def _outffn_kernel(final, oa_ref, ob_ref, oc_ref, od_ref, x_ref, mod_ref, g2_ref, wo_ref, wg_ref,
                   wu_ref, wd_ref, fg_ref, y_ref):
    m = mod_ref[...]
    o = jnp.concatenate([oa_ref[...], ob_ref[...], oc_ref[...], od_ref[...]], axis=-1).astype(BF16)
    x1 = x_ref[...] + m[2:3] * _bdot(o, wo_ref[...])
    h = (_rms_full(x1, g2_ref[...]) * (1.0 + m[4:5]) + m[3:4]).astype(BF16)
    act = (_silu(_bdot(h, wg_ref[...])) * _bdot(h, wu_ref[...])).astype(BF16)
    x2 = x1 + m[5:6] * _bdot(act, wd_ref[...])
    y_ref[...] = _rms_full(x2, fg_ref[...]) if final else x2


def _outffn(outs, x, mods, per_batch_mods, lw, final_g, final):
    b, l, _ = x.shape
    tm = ROW_TILE
    row = lambda w: pl.BlockSpec((None, tm, w), lambda bi, i: (bi, i, 0))
    const = lambda shape: pl.BlockSpec(shape, lambda bi, i: (0,) * len(shape),
                                       pipeline_mode=pl.Buffered(1))
    mod_spec = pl.BlockSpec((None, MOD_CHUNKS, D_MODEL),
                            (lambda bi, i: (bi, 0, 0)) if per_batch_mods else (lambda bi, i: (0, 0, 0)))
    return pl.pallas_call(
        functools.partial(_outffn_kernel, final),
        grid=(b, l // tm),
        in_specs=[row(256), row(256), row(256), row(256), row(D_MODEL), mod_spec, const((1, D_MODEL)),
                  const((D_MODEL, D_MODEL)), const((D_MODEL, D_FF)), const((D_MODEL, D_FF)),
                  const((D_FF, D_MODEL)), const((1, D_MODEL))],
        out_specs=row(D_MODEL),
        out_shape=jax.ShapeDtypeStruct((b, l, D_MODEL), F32),
        compiler_params=_cparams(2),
        name="outffn",
    )(*outs, x, mods, lw["norm2_g"], lw["w_out"], lw["w_gate"], lw["w_up"], lw["w_down"], final_g)


def _rope_tables(n):
    t = jnp.arange(n)

    def axis(pos, half):
        inv = ROPE_BASE ** (-jnp.arange(half, dtype=F32) / half)
        ang = pos.astype(F32)[:, None] * inv[None, :]
        c, s = jnp.cos(ang), jnp.sin(ang)
        return jnp.concatenate([c, c], -1), jnp.concatenate([-s, s], -1)

    cr, sr = axis(t // GRID_W, 16)
    cc, sc = axis(t % GRID_W, 16)
    cos64 = jnp.tile(jnp.concatenate([cr, cc], -1), (1, 2))
    sin64 = jnp.tile(jnp.concatenate([sr, sc], -1), (1, 2))
    cr, sr = axis(t // GRID_W, 8)
    cc, sc = axis(t % GRID_W, 8)
    cos32, sin32 = jnp.concatenate([cr, cc], -1), jnp.concatenate([sr, sc], -1)
    one, zero = jnp.ones((n, 1), F32), jnp.zeros((n, 1), F32)
    cosm = jnp.concatenate([jnp.tile(one, (1, 64)), cos32, jnp.tile(one, (1, 32))], -1)
    sinm = jnp.concatenate([jnp.tile(zero, (1, 64)), sin32, jnp.tile(zero, (1, 32))], -1)
    coskr = jnp.concatenate([cos32, jnp.tile(one, (1, 96))], -1)
    sinkr = jnp.concatenate([sin32, jnp.tile(zero, (1, 96))], -1)
    return cos64, sin64, cosm, sinm, coskr, sinkr


_QA_PERM = np.concatenate([np.arange(0, 64), np.arange(128, 192), np.arange(64, 128), np.arange(192, 256)])


def _layer_weights(p, l):
    w_in = p["w_in"][l]
    cols = np.concatenate([_QA_PERM, np.arange(256, 2304), np.arange(2320, 2736), np.arange(2304, 2320)])
    w_in = jnp.pad(w_in[:, cols], ((0, 0), (0, IN_PAD_COLS - cols.size))).astype(BF16)
    w_out = p["w_out"][l]
    w_out = jnp.concatenate([w_out[_QA_PERM], w_out[256:]], axis=0).astype(BF16)
    wq = p["mla_wq_up"][l].reshape(MLA_Q_LORA, MLA_HEADS, MLA_NOPE + MLA_ROPE)
    wq = jnp.pad(wq, ((0, 0), (0, 0), (0, LANES - MLA_NOPE - MLA_ROPE))).reshape(MLA_Q_LORA, 4 * LANES)
    wkv = p["mla_wkv_up"][l].reshape(MLA_KV_LORA, MLA_HEADS, MLA_NOPE + MLA_V)
    wk_top = jnp.pad(wkv[:, :, :MLA_NOPE], ((0, 0), (0, 0), (0, LANES - MLA_NOPE)))
    place = jnp.pad(jnp.eye(MLA_ROPE, dtype=F32), ((0, LANES - MLA_ROPE), (MLA_NOPE, LANES - MLA_NOPE - MLA_ROPE)))
    wk_bot = jnp.broadcast_to(place[:, None, :], (LANES, MLA_HEADS, LANES))
    wk = jnp.concatenate([wk_top, wk_bot], axis=0).reshape(2 * LANES, 4 * LANES)
    wv = wkv[:, :, MLA_NOPE:].reshape(MLA_KV_LORA, MLA_HEADS * MLA_V)
    gate_row = lambda v: jnp.pad(v.reshape(1, 2 * DN_HEADS), ((0, 0), (MLA_ROPE, LANES - MLA_ROPE - 2 * DN_HEADS)))
    return {
        "norm1_g": p["norm1_g"][l][None], "norm2_g": p["norm2_g"][l][None],
        "w_in": w_in, "w_out": w_out,
        "qn_g": jnp.tile(p["gqa_qn_g"][l], 4)[None], "kn_g": jnp.tile(p["gqa_kn_g"][l], 2)[None],
        "mla_qn_g": p["mla_qn_g"][l][None], "mla_kvn_g": p["mla_kvn_g"][l][None],
        "wq": wq.astype(BF16), "wk": wk.astype(BF16), "wv": wv.astype(BF16),
        "dn_conv_w": p["dn_conv_w"][l], "dn_alog_row": gate_row(p["dn_a_log"][l]),
        "dn_dtb_row": gate_row(p["dn_dt_bias"][l]), "dn_out_g": jnp.tile(p["dn_out_g"][l], DN_HEADS)[None],
        "w_gate": p["ffn_w_gate"][l].astype(BF16), "w_up": p["ffn_w_up"][l].astype(BF16),
        "w_down": p["ffn_w_down"][l].astype(BF16),
    }


def _state_to_wide(s):
    b = s.shape[0]
    return s.transpose(0, 1, 3, 2, 4).reshape(b, 2, DN_DK, DN_HEADS * DN_DV)


def _state_from_wide(s):
    b = s.shape[0]
    return s.reshape(b, 2, DN_DK, DN_HEADS, DN_DV).transpose(0, 1, 3, 2, 4)


def kernel(x_prompt, x_sample, cache_gqa_k, cache_gqa_v, cache_na_k, cache_na_v, state_dn,
           cache_mla_ckv, cache_mla_krope, c, c_ctx, norm1_g, norm2_g, w_mod, b_mod, w_in, w_out,
           gqa_qn_g, gqa_kn_g, na_bias, dn_conv_w, dn_a_log, dn_dt_bias, dn_out_g, mla_qn_g,
           mla_wq_up, mla_kvn_g, mla_wkv_up, ffn_w_gate, ffn_w_up, ffn_w_down, final_g):
    p = {"norm1_g": norm1_g, "norm2_g": norm2_g, "w_in": w_in, "w_out": w_out, "gqa_qn_g": gqa_qn_g,
         "gqa_kn_g": gqa_kn_g, "dn_conv_w": dn_conv_w, "dn_a_log": dn_a_log, "dn_dt_bias": dn_dt_bias,
         "dn_out_g": dn_out_g, "mla_qn_g": mla_qn_g, "mla_wq_up": mla_wq_up, "mla_kvn_g": mla_kvn_g,
         "mla_wkv_up": mla_wkv_up, "ffn_w_gate": ffn_w_gate, "ffn_w_up": ffn_w_up, "ffn_w_down": ffn_w_down}
    nb_ctx, seq_ctx, _ = x_prompt.shape
    nb_lat, seq_lat, _ = x_sample.shape
    past = cache_gqa_k.shape[2]
    fg = final_g[None]

    cond = jnp.concatenate([c_ctx[None], c, jnp.zeros((16 - 1 - nb_lat, D_MODEL), F32)], axis=0)
    mods = _modulation(cond, w_mod, b_mod).reshape(DEPTH, 16, MOD_CHUNKS, D_MODEL)
    weights = [_layer_weights(p, l) for l in range(DEPTH)]

    x = x_prompt
    ctx_out = []
    for l in range(DEPTH):
        lw = weights[l]
        m = mods[l, 0:1]
        qa, ka, va, qb, kb, vb, zc, qd, ckv, zm, krr = _inproj(x, m, False, lw, None)
        o_a = _attn_pair(qa, ka, va, (0, 0), name="gqa_ctx")
        o_b = _attn_pair(qb, kb, vb, (0, 1), name="na_ctx")
        o_c, s_dn = _deltanet(zc, zm, lw)
        o_d = _mla(qd, ckv, krr, lw["wk"], lw["wv"], name="mla_ctx")
        x = _outffn((o_a, o_b, o_c, o_d), x, m, False, lw, fg, l == DEPTH - 1)
        ctx_out.append((ka.reshape(nb_ctx, seq_ctx, GQA_KV_HEADS, HEAD_DIM),
                        va.reshape(nb_ctx, seq_ctx, GQA_KV_HEADS, HEAD_DIM),
                        kb.reshape(nb_ctx, seq_ctx, NA_HEADS, HEAD_DIM),
                        vb.reshape(nb_ctx, seq_ctx, NA_HEADS, HEAD_DIM),
                        _state_from_wide(s_dn), ckv, zm[:, :, :MLA_ROPE]))
    y_prompt = x
    new = [jnp.stack([s[i] for s in ctx_out], axis=1) for i in range(7)]

    ropes = _rope_tables(seq_lat)
    ck_a = cache_gqa_k.reshape(nb_lat, DEPTH, past, GQA_KV_HEADS * HEAD_DIM)
    cv_a = cache_gqa_v.reshape(nb_lat, DEPTH, past, GQA_KV_HEADS * HEAD_DIM)
    ck_b = cache_na_k.reshape(nb_lat, DEPTH, past, NA_HEADS * HEAD_DIM)
    cv_b = cache_na_v.reshape(nb_lat, DEPTH, past, NA_HEADS * HEAD_DIM)
    c_kr = jnp.pad(cache_mla_krope, ((0, 0), (0, 0), (0, 0), (0, LANES - MLA_ROPE)))
    x = x_sample
    for l in range(DEPTH):
        lw = weights[l]
        m = mods[l, 1:1 + nb_lat]
        qa, ka, va, qb, kb, vb, zc, qd, ckv, zm, krr = _inproj(x, m, True, lw, ropes)
        o_a = _attn_pair(qa, ka, va, (0, 0), cache=(ck_a, cv_a, l), name="gqa_lat")
        o_b = _na_latent(qb, kb, vb, ck_b, cv_b, l, _na_bias_blocks(na_bias[l]))
        o_c, _ = _deltanet(zc, zm, lw, state=_state_to_wide(state_dn[:, l]))
        o_d = _mla(qd, ckv, krr, lw["wk"], lw["wv"], cache=(cache_mla_ckv, c_kr, l), name="mla_lat")
        x = _outffn((o_a, o_b, o_c, o_d), x, m, True, lw, fg, l == DEPTH - 1)
    y_sample = x

    return (y_prompt, y_sample, *new)
```

```python
import functools

import numpy as np
import jax
import jax.numpy as jnp
from jax import lax
from jax.experimental import pallas as pl
from jax.experimental.pallas import tpu as pltpu

F32 = jnp.float32
BF16 = jnp.bfloat16

D_MODEL = 1024
DEPTH = 2
GRID_W = 64
HEAD_DIM = 64
ROPE_BASE = 10000.0
NEG_INF = -1e30
MOD_CHUNKS = 6
GQA_HEADS, GQA_KV_HEADS = 4, 2
NA_HEADS, NA_KH, NA_KW = 4, 8, 16
DN_HEADS, DN_DK, DN_DV, DN_CONV, DN_CHUNK = 4, 64, 64, 4, 64
DN_QKV = DN_HEADS * (2 * DN_DK + DN_DV)
MLA_HEADS, MLA_Q_LORA, MLA_KV_LORA, MLA_NOPE, MLA_ROPE, MLA_V = 4, 256, 128, 64, 32, 64
MLA_SCALE = (MLA_NOPE + MLA_ROPE) ** -0.5
D_FF = -(-8 * D_MODEL // (3 * 256)) * 256
EPS = 1e-6

LANES = 128
ROW_TILE = 256
Q_TILE = 256
NA_BAND = 768
IN_PAD_COLS = 2816
VMEM_LIMIT = 56 * 1024 * 1024

_NT = (((1,), (1,)), ((), ()))


def _cparams(n_axes):
    return pltpu.CompilerParams(dimension_semantics=("arbitrary",) * n_axes,
                                vmem_limit_bytes=VMEM_LIMIT)


def _lane(shape):
    return lax.broadcasted_iota(jnp.int32, shape, len(shape) - 1)


def _silu(x):
    return x / (1.0 + jnp.exp(-x))


def _rms_full(x, g):
    return x * lax.rsqrt(jnp.mean(x * x, axis=-1, keepdims=True) + EPS) * g


def _seg64_sum(x):
    lo = _lane(x.shape) < HEAD_DIM
    s_lo = jnp.sum(jnp.where(lo, x, 0.0), axis=-1, keepdims=True)
    s_hi = jnp.sum(jnp.where(lo, 0.0, x), axis=-1, keepdims=True)
    return jnp.where(lo, s_lo, s_hi)


def _head_rms(x, g):
    parts = []
    for p in range(x.shape[-1] // LANES):
        xp = x[:, p * LANES:(p + 1) * LANES]
        ms = _seg64_sum(xp * xp) * (1.0 / HEAD_DIM)
        parts.append(xp * lax.rsqrt(ms + EPS))
    y = parts[0] if len(parts) == 1 else jnp.concatenate(parts, axis=-1)
    return y * g


def _head_l2(x):
    parts = []
    for p in range(x.shape[-1] // LANES):
        xp = x[:, p * LANES:(p + 1) * LANES]
        parts.append(xp * lax.rsqrt(_seg64_sum(xp * xp) + EPS))
    return parts[0] if len(parts) == 1 else jnp.concatenate(parts, axis=-1)


def _rope(x, cos, sin, half):
    first = (_lane(x.shape) & (2 * half - 1)) < half
    rot = jnp.where(first, pltpu.roll(x, LANES - half, 1), pltpu.roll(x, half, 1))
    return x * cos + rot * sin


def _softmax_parts(scores):
    m = jnp.max(scores[0], axis=-1, keepdims=True)
    for s in scores[1:]:
        m = jnp.maximum(m, jnp.max(s, axis=-1, keepdims=True))
    es = [jnp.exp(s - m) for s in scores]
    l = jnp.sum(es[0], axis=-1, keepdims=True)
    for e in es[1:]:
        l = l + jnp.sum(e, axis=-1, keepdims=True)
    return es, 1.0 / l


def _bdot(a, b):
    return jnp.dot(a, b, preferred_element_type=F32)


def _mod_kernel(c_ref, w_ref, b_ref, o_ref):
    s = _silu(c_ref[...]).astype(BF16)
    o_ref[...] = _bdot(s, w_ref[...].astype(BF16)) + b_ref[...]


def _modulation(cond, w_mod, b_mod):
    n = MOD_CHUNKS * D_MODEL
    tn = 1536
    return pl.pallas_call(
        _mod_kernel,
        grid=(DEPTH, n // tn),
        in_specs=[pl.BlockSpec((16, D_MODEL), lambda l, j: (0, 0)),
                  pl.BlockSpec((None, D_MODEL, tn), lambda l, j: (l, 0, j)),
                  pl.BlockSpec((None, 1, tn), lambda l, j: (l, 0, j))],
        out_specs=pl.BlockSpec((None, 16, tn), lambda l, j: (l, 0, j)),
        out_shape=jax.ShapeDtypeStruct((DEPTH, 16, n), F32),
        compiler_params=_cparams(2),
        name="modulation",
    )(cond, w_mod, b_mod.reshape(DEPTH, 1, n))


_IN_OUT_WIDTHS = (256, 128, 128, 256, 256, 256, 1024, 512, 128, 128, 128)


def _inproj_kernel(positioned, *refs):
    (x_ref, mod_ref, g1_ref, w_ref, qng_ref, kng_ref, mqg_ref, wq_ref, mkg_ref) = refs[:9]
    n_in = 9
    if positioned:
        cos64_ref, sin64_ref, cosm_ref, sinm_ref, coskr_ref, sinkr_ref = refs[9:15]
        n_in = 15
    (qa_ref, ka_ref, va_ref, qb_ref, kb_ref, vb_ref, zc_ref, qd_ref, ckv_ref, zm_ref,
     krr_ref) = refs[n_in:]

    m = mod_ref[...]
    h = _rms_full(x_ref[...], g1_ref[...]) * (1.0 + m[1:2]) + m[0:1]
    hb = h.astype(BF16)

    za = _bdot(hb, w_ref[:, 0:512])
    q = _head_rms(za[:, 0:256], qng_ref[...])
    k = _head_rms(za[:, 256:384], kng_ref[...])
    if positioned:
        cos, sin = cos64_ref[...], sin64_ref[...]
        q = jnp.concatenate([_rope(q[:, 0:128], cos, sin, 16), _rope(q[:, 128:256], cos, sin, 16)],
                            axis=-1)
        k = _rope(k, cos, sin, 16)
    qa_ref[...] = q * (HEAD_DIM ** -0.5)
    ka_ref[...] = k
    va_ref[...] = za[:, 384:512]

    zb = _bdot(hb, w_ref[:, 512:1280])
    qb_ref[...] = zb[:, 0:256] * (HEAD_DIM ** -0.5)
    kb_ref[...] = zb[:, 256:512]
    vb_ref[...] = zb[:, 512:768]

    zc_ref[...] = _bdot(hb, w_ref[:, 1280:2304])

    zd = _bdot(hb, w_ref[:, 2304:2816])
    cq = _rms_full(zd[:, 0:256], mqg_ref[...])
    qm = _bdot(cq.astype(BF16), wq_ref[...])
    if positioned:
        cm, sm = cosm_ref[...], sinm_ref[...]
        qm = jnp.concatenate([_rope(qm[:, i * LANES:(i + 1) * LANES], cm, sm, 8)
                              for i in range(MLA_HEADS)], axis=-1)
    qd_ref[...] = qm * MLA_SCALE
    ckv_ref[...] = _rms_full(zd[:, 256:384], mkg_ref[...])
    zmisc = zd[:, 384:512]
    zm_ref[...] = zmisc
    kr = jnp.where(_lane(zmisc.shape) < MLA_ROPE, zmisc, 0.0)
    if positioned:
        kr = _rope(kr, coskr_ref[...], sinkr_ref[...], 8)
    krr_ref[...] = kr


def _inproj(x, mods, per_batch_mods, lw, ropes):
    b, l, _ = x.shape
    tm = ROW_TILE
    positioned = ropes is not None
    row = lambda w: pl.BlockSpec((None, tm, w), lambda bi, i: (bi, i, 0))
    const = lambda shape: pl.BlockSpec(shape, lambda bi, i: (0,) * len(shape))
    mod_spec = pl.BlockSpec((None, MOD_CHUNKS, D_MODEL),
                            (lambda bi, i: (bi, 0, 0)) if per_batch_mods else (lambda bi, i: (0, 0, 0)))
    in_specs = [row(D_MODEL), mod_spec, const((1, D_MODEL)), const((D_MODEL, IN_PAD_COLS)),
                const((1, 256)), const((1, 128)), const((1, MLA_Q_LORA)),
                const((MLA_Q_LORA, 4 * LANES)), const((1, MLA_KV_LORA))]
    args = [x, mods, lw["norm1_g"], lw["w_in"], lw["qn_g"], lw["kn_g"], lw["mla_qn_g"], lw["wq"],
            lw["mla_kvn_g"]]
    if positioned:
        in_specs += [pl.BlockSpec((tm, LANES), lambda bi, i: (i, 0))] * 6
        args += list(ropes)
    return pl.pallas_call(
        functools.partial(_inproj_kernel, positioned),
        grid=(b, l // tm),
        in_specs=in_specs,
        out_specs=[row(w) for w in _IN_OUT_WIDTHS],
        out_shape=[jax.ShapeDtypeStruct((b, l, w), F32) for w in _IN_OUT_WIDTHS],
        compiler_params=_cparams(2),
        name="inproj_lat" if positioned else "inproj_ctx",
    )(*args)


def _attn_pair_kernel(has_cache, qmap, *refs):
    if has_cache:
        q_ref, kc_ref, vc_ref, k_ref, v_ref, o_ref, kbuf, vbuf = refs
    else:
        q_ref, k_ref, v_ref, o_ref, kbuf, vbuf = refs

    @pl.when(pl.program_id(1) == 0)
    def _():
        off = 0
        if has_cache:
            off = kc_ref.shape[0]
            kbuf[0:off, :] = kc_ref[...].astype(BF16)
            vbuf[0:off, :] = vc_ref[...].astype(BF16)
        kbuf[off:, :] = k_ref[...].astype(BF16)
        vbuf[off:, :] = v_ref[...].astype(BF16)

    tq = q_ref.shape[0]
    lo = _lane((tq, LANES)) < HEAD_DIM
    for p, kv in enumerate(qmap):
        qp = q_ref[:, p * LANES:(p + 1) * LANES]
        kblk = kbuf[:, kv * LANES:(kv + 1) * LANES]
        vblk = vbuf[:, kv * LANES:(kv + 1) * LANES]
        outs = []
        for half in range(2):
            qm = jnp.where(lo if half == 0 else jnp.logical_not(lo), qp, 0.0).astype(BF16)
            s = lax.dot_general(qm, kblk, _NT, preferred_element_type=F32)
            (e,), rl = _softmax_parts([s])
            outs.append(_bdot(e.astype(BF16), vblk) * rl)
        o_ref[:, p * LANES:(p + 1) * LANES] = jnp.where(lo, outs[0], outs[1])


def _attn_pair(q, k, v, qmap, cache=None, name="attn_pair"):
    b, lq, wq = q.shape
    ls, wk = k.shape[1], k.shape[2]
    tq = Q_TILE
    lc = 0 if cache is None else cache[0].shape[2]
    in_specs = [pl.BlockSpec((None, tq, wq), lambda bi, i: (bi, i, 0))]
    args = [q]
    if cache is not None:
        kc, vc, layer = cache
        cspec = pl.BlockSpec((None, None, lc, wk), lambda bi, i: (bi, layer, 0, 0))
        in_specs += [cspec, cspec]
        args += [kc, vc]
    sspec = pl.BlockSpec((None, ls, wk), lambda bi, i: (bi, 0, 0))
    in_specs += [sspec, sspec]
    args += [k, v]
    return pl.pallas_call(
        functools.partial(_attn_pair_kernel, cache is not None, qmap),
        grid=(b, lq // tq),
        in_specs=in_specs,
        out_specs=pl.BlockSpec((None, tq, wq), lambda bi, i: (bi, i, 0)),
        out_shape=jax.ShapeDtypeStruct((b, lq, wq), F32),
        scratch_shapes=[pltpu.VMEM((lc + ls, wk), BF16), pltpu.VMEM((lc + ls, wk), BF16)],
        compiler_params=_cparams(2),
        name=name,
    )(*args)


def _mla_kernel(has_cache, *refs):
    if has_cache:
        q_ref, ckvc_ref, krc_ref, ckv_ref, kr_ref, wk_ref, wv_ref, o_ref, kbuf, vbuf = refs
    else:
        q_ref, ckv_ref, kr_ref, wk_ref, wv_ref, o_ref, kbuf, vbuf = refs

    @pl.when(pl.program_id(1) == 0)
    def _():
        def expand(c_ref, r_ref, r0, r1):
            c = c_ref[...].astype(BF16)
            ckr = jnp.concatenate([c, r_ref[...].astype(BF16)], axis=-1)
            kbuf[r0:r1, :] = _bdot(ckr, wk_ref[...]).astype(BF16)
            vbuf[r0:r1, :] = _bdot(c, wv_ref[...]).astype(BF16)
        off = 0
        if has_cache:
            off = ckvc_ref.shape[0]
            expand(ckvc_ref, krc_ref, 0, off)
        expand(ckv_ref, kr_ref, off, kbuf.shape[0])

    tq = q_ref.shape[0]
    lo = _lane((tq, LANES)) < MLA_V
    for p in range(MLA_HEADS // 2):
        vblk = vbuf[:, p * LANES:(p + 1) * LANES]
        outs = []
        for half in range(2):
            hd = 2 * p + half
            qh = q_ref[:, hd * LANES:(hd + 1) * LANES].astype(BF16)
            s = lax.dot_general(qh, kbuf[:, hd * LANES:(hd + 1) * LANES], _NT,
                                preferred_element_type=F32)
            (e,), rl = _softmax_parts([s])
            outs.append(_bdot(e.astype(BF16), vblk) * rl)
        o_ref[:, p * LANES:(p + 1) * LANES] = jnp.where(lo, outs[0], outs[1])


def _mla(q, ckv, kr, wk, wv, cache=None, name="mla"):
    b, lq, wq = q.shape
    ls = ckv.shape[1]
    tq = Q_TILE
    lc = 0 if cache is None else cache[0].shape[2]
    in_specs = [pl.BlockSpec((None, tq, wq), lambda bi, i: (bi, i, 0))]
    args = [q]
    if cache is not None:
        ckvc, krc, layer = cache
        cspec = pl.BlockSpec((None, None, lc, LANES), lambda bi, i: (bi, layer, 0, 0))
        in_specs += [cspec, cspec]
        args += [ckvc, krc]
    sspec = pl.BlockSpec((None, ls, LANES), lambda bi, i: (bi, 0, 0))
    in_specs += [sspec, sspec,
                 pl.BlockSpec((2 * LANES, 4 * LANES), lambda bi, i: (0, 0)),
                 pl.BlockSpec((LANES, 2 * LANES), lambda bi, i: (0, 0))]
    args += [ckv, kr, wk, wv]
    return pl.pallas_call(
        functools.partial(_mla_kernel, cache is not None),
        grid=(b, lq // tq),
        in_specs=in_specs,
        out_specs=pl.BlockSpec((None, tq, 2 * LANES), lambda bi, i: (bi, i, 0)),
        out_shape=jax.ShapeDtypeStruct((b, lq, 2 * LANES), F32),
        scratch_shapes=[pltpu.VMEM((lc + ls, 4 * LANES), BF16), pltpu.VMEM((lc + ls, 2 * LANES), BF16)],
        compiler_params=_cparams(2),
        name=name,
    )(*args)


def _na_kernel(q_ref, k_ref, v_ref, kc_ref, vc_ref, bias_ref, o_ref):
    j = pl.program_id(0)
    start = pl.multiple_of((j >> 1) * 256, 256)
    kband = k_ref[pl.ds(start, NA_BAND), :].astype(BF16)
    vband = v_ref[pl.ds(start, NA_BAND), :].astype(BF16)
    kc = kc_ref[...].astype(BF16)
    vc = vc_ref[...].astype(BF16)
    tq = q_ref.shape[0]
    lo = _lane((tq, LANES)) < HEAD_DIM
    for p in range(NA_HEADS // 2):
        sl = slice(p * LANES, (p + 1) * LANES)
        qp = q_ref[:, sl]
        outs = []
        for half in range(2):
            qm = jnp.where(lo if half == 0 else jnp.logical_not(lo), qp, 0.0).astype(BF16)
            s_loc = lax.dot_general(qm, kband[:, sl], _NT, preferred_element_type=F32)
            s_loc = s_loc + bias_ref[2 * p + half]
            s_ctx = lax.dot_general(qm, kc[:, sl], _NT, preferred_element_type=F32)
            (e_loc, e_ctx), rl = _softmax_parts([s_loc, s_ctx])
            o = _bdot(e_loc.astype(BF16), vband[:, sl]) + _bdot(e_ctx.astype(BF16), vc[:, sl])
            outs.append(o * rl)
        o_ref[:, sl] = jnp.where(lo, outs[0], outs[1])


def _na_latent(q, k, v, kc, vc, layer, bias_blocks):
    b, n, w = q.shape
    lc = kc.shape[2]
    nq = n // Q_TILE
    full = pl.BlockSpec((None, n, w), lambda j, bi: (bi, 0, 0))
    cspec = pl.BlockSpec((None, None, lc, w), lambda j, bi: (bi, layer, 0, 0))
    return pl.pallas_call(
        _na_kernel,
        grid=(nq, b),
        in_specs=[pl.BlockSpec((None, Q_TILE, w), lambda j, bi: (bi, j, 0)), full, full, cspec, cspec,
                  pl.BlockSpec((None, NA_HEADS, Q_TILE, NA_BAND), lambda j, bi: (j, 0, 0, 0))],
        out_specs=pl.BlockSpec((None, Q_TILE, w), lambda j, bi: (bi, j, 0)),
        out_shape=jax.ShapeDtypeStruct((b, n, w), F32),
        compiler_params=_cparams(2),
        name="na_latent",
    )(q, k, v, kc, vc, bias_blocks)


NA_GRID_ROWS = 16
NA_BAND_ROW0 = (0, 0, 4, 4)


def _na_bias_kernel(b_ref, o_ref, tp_s):
    hd = pl.program_id(0)
    n_dr, n_dc = 2 * NA_KH - 1, 2 * NA_KW - 1
    shape = (GRID_W, LANES)
    c = lax.broadcasted_iota(jnp.int32, shape, 0)
    lane = _lane(shape)
    kc = lane & (GRID_W - 1)
    lo = lane < GRID_W
    diff = kc - c + (NA_KW - 1)
    c0 = jnp.clip(c - NA_KW // 2, 0, GRID_W - NA_KW)
    col_ok = (kc >= c0) & (kc < c0 + NA_KW)
    neg = jnp.full(shape, NEG_INF, F32)
    for dr0 in range(-1, n_dr):
        acc = neg
        for d in range(n_dc):
            v_lo = b_ref[hd * n_dr + dr0, d] if dr0 >= 0 else 0.0
            v_hi = b_ref[hd * n_dr + dr0 + 1, d] if dr0 + 1 < n_dr else 0.0
            acc = jnp.where(diff == d, jnp.where(lo, v_lo, v_hi), acc)
        tp_s[dr0 + 1] = jnp.where(col_ok, acc, NEG_INF)
    for j in range(NA_GRID_ROWS // 4):
        for ri in range(4):
            r = 4 * j + ri
            r0 = min(max(r - NA_KH // 2, 0), NA_GRID_ROWS - NA_KH)
            for kp in range(NA_BAND // LANES):
                kr = NA_BAND_ROW0[j] + 2 * kp
                ok_lo, ok_hi = r0 <= kr < r0 + NA_KH, r0 <= kr + 1 < r0 + NA_KH
                dr0 = kr - r + (NA_KH - 1)
                if ok_lo and ok_hi:
                    t = tp_s[dr0 + 1]
                elif ok_lo:
                    t = jnp.where(lo, tp_s[dr0 + 1], NEG_INF)
                elif ok_hi:
                    t = jnp.where(lo, NEG_INF, tp_s[dr0 + 1])
                else:
                    t = neg
                o_ref[j, ri * GRID_W:(ri + 1) * GRID_W, kp * LANES:(kp + 1) * LANES] = t


def _na_bias_blocks(bias):
    nq = NA_GRID_ROWS // 4
    return pl.pallas_call(
        _na_bias_kernel,
        grid=(NA_HEADS,),
        in_specs=[pl.BlockSpec(memory_space=pltpu.SMEM)],
        out_specs=pl.BlockSpec((nq, None, Q_TILE, NA_BAND), lambda h: (0, h, 0, 0)),
        out_shape=jax.ShapeDtypeStruct((nq, NA_HEADS, Q_TILE, NA_BAND), F32),
        scratch_shapes=[pltpu.VMEM((2 * NA_KH, GRID_W, LANES), F32)],
        compiler_params=_cparams(1),
        name="na_bias",
    )(bias.reshape(NA_HEADS * (2 * NA_KH - 1), 2 * NA_KW - 1))


def _block_diag_rows(z):
    blk = _lane(z.shape) >> 6
    return jnp.concatenate([jnp.where(blk == hd, z, 0.0).astype(BF16) for hd in range(DN_HEADS)], axis=0)


def _widen(cols, n):
    blk = _lane((n, DN_HEADS * DN_DV)) >> 6
    return jnp.where(blk == 0, cols[0], jnp.where(blk == 1, cols[1], jnp.where(blk == 2, cols[2], cols[3])))


def _deltanet_kernel(seq, has_state, *refs):
    if has_state:
        (zc_ref, zm_ref, s0_ref, cw_ref, alog_ref, dtb_ref, og_ref, o_ref,
         q_s, k_s, v_s, b_s, g_s, o_s) = refs
    else:
        (zc_ref, zm_ref, cw_ref, alog_ref, dtb_ref, og_ref, o_ref, sfin_ref,
         q_s, k_s, v_s, b_s, g_s, o_s) = refs
    n_chunks = seq // DN_CHUNK
    wide = DN_HEADS * DN_DV

    row = lax.broadcasted_iota(jnp.int32, (seq, wide), 0)
    for part, dst in enumerate((q_s, k_s, v_s)):
        cs = slice(part * wide, (part + 1) * wide)
        x = zc_ref[:, cs]
        w = cw_ref[:, cs]
        y = (w[0:1] * jnp.where(row >= 1, pltpu.roll(x, 1, 0), 0.0) + w[1:2] * x
             + w[2:3] * jnp.where(row < seq - 1, pltpu.roll(x, seq - 1, 0), 0.0)
             + w[3:4] * jnp.where(row < seq - 2, pltpu.roll(x, seq - 2, 0), 0.0))
        y = _silu(y)
        if part == 0:
            y = _head_l2(y) * (DN_DK ** -0.5)
        elif part == 1:
            y = _head_l2(y)
        dst[...] = y

    zm = zm_ref[...]
    xa = zm + dtb_ref[...]
    logd = -jnp.exp(alog_ref[...]) * (jnp.maximum(xa, 0.0) + jnp.log1p(jnp.exp(-jnp.abs(xa))))
    beta = 1.0 / (1.0 + jnp.exp(-zm))
    a_off, b_off = MLA_ROPE, MLA_ROPE + 2 * DN_HEADS
    ri = lax.broadcasted_iota(jnp.int32, (256, 256), 0)
    ci = lax.broadcasted_iota(jnp.int32, (256, 256), 1)
    same = (ri >> 6) == (ci >> 6)
    for d in range(2):
        b_s[d] = _widen([beta[:, b_off + 4 * d + hd:b_off + 4 * d + hd + 1] for hd in range(DN_HEADS)], seq)
        lw = _widen([logd[:, a_off + 4 * d + hd:a_off + 4 * d + hd + 1] for hd in range(DN_HEADS)], seq)
        tri = jnp.where(same & ((ci <= ri) if d == 0 else (ci >= ri)), 1.0, 0.0).astype(F32)
        for rb in range(seq // 256):
            rs = slice(rb * 256, (rb + 1) * 256)
            g_s[d, rs, :] = jnp.dot(tri, lw[rs], preferred_element_type=F32,
                                    precision=lax.Precision.HIGHEST)

    ii = lax.broadcasted_iota(jnp.int32, (DN_CHUNK, wide), 0)
    jj = _lane((DN_CHUNK, wide)) & (DN_CHUNK - 1)
    blk = _lane((DN_CHUNK, wide)) >> 6
    diag = ii == jj

    for d in range(2):
        incl = (jj <= ii) if d == 0 else (jj >= ii)
        strict = (jj < ii) if d == 0 else (jj > ii)
        pair = [((ii >> (lvl + 1)) == (jj >> (lvl + 1)))
                & (((ii >> lvl) & 1) == (1 - d)) & (((jj >> lvl) & 1) == d) for lvl in range(6)]

        def chunk(ci_, s, d=d, incl=incl, strict=strict, pair=pair):
            c = ci_ if d == 0 else n_chunks - 1 - ci_
            r0 = pl.multiple_of(c * DN_CHUNK, DN_CHUNK)
            rows = pl.ds(r0, DN_CHUNK)
            q, k, v = q_s[rows, :], k_s[rows, :], v_s[rows, :]
            beta_c, g = b_s[d, rows, :], g_s[d, rows, :]
            kb = k * beta_c
            r = lax.dot_general(jnp.concatenate([kb, q], axis=0).astype(BF16), _block_diag_rows(k), _NT,
                                preferred_element_type=F32)
            g_row = jnp.sum(jnp.where(diag, g, 0.0), axis=0, keepdims=True)
            dm = jnp.where(incl, jnp.exp(jnp.where(incl, g - g_row, 0.0)), 0.0)
            a = jnp.where(strict, r[0:DN_CHUNK] * dm, 0.0)
            qk = r[DN_CHUNK:] * dm
            eg = jnp.exp(g)
            t = jnp.where(diag, 1.0, 0.0) - jnp.where(pair[0], a, 0.0)
            for lvl in range(1, 6):
                te = _bdot(t.astype(BF16), _block_diag_rows(jnp.where(pair[lvl], a, 0.0)))
                t = t - _bdot(te.astype(BF16), _block_diag_rows(t))
            nb = jnp.where(diag, 0.0, t).astype(BF16)
            rhs_u, rhs_w = v * beta_c, kb * eg
            u = rhs_u + _bdot(nb, _block_diag_rows(rhs_u))
            w = rhs_w + _bdot(nb, _block_diag_rows(rhs_w))
            ws = _bdot(jnp.concatenate([w, q * eg], axis=0).astype(BF16), _block_diag_rows(s))
            v_new = u - ws[0:DN_CHUNK]
            o = ws[DN_CHUNK:] + _bdot(qk.astype(BF16), _block_diag_rows(v_new))
            g_last = g[DN_CHUNK - 1:DN_CHUNK] if d == 0 else g[0:1]
            kd = k * jnp.exp(g_last - g)
            gram = lax.dot_general(kd.astype(BF16), v_new.astype(BF16), (((0,), (0,)), ((), ())),
                                   preferred_element_type=F32)
            s_add = jnp.where(blk == 0, gram[0:64], 0.0)
            for hd in range(1, DN_HEADS):
                s_add = s_add + jnp.where(blk == hd, gram[hd * 64:(hd + 1) * 64], 0.0)
            if d == 0:
                o_s[rows, :] = o
            else:
                o_s[rows, :] = o_s[rows, :] + o
            return s * jnp.exp(g_last) + s_add

        s0 = s0_ref[d] if has_state else jnp.zeros((DN_DK, wide), F32)
        s_fin = lax.fori_loop(0, n_chunks, chunk, s0)
        if not has_state:
            sfin_ref[d] = s_fin

    o_ref[...] = _head_rms(o_s[...], og_ref[...]) * _silu(zc_ref[:, 3 * wide:4 * wide])


def _deltanet(zc, zm, lw, state=None):
    b, seq, _ = zc.shape
    wide = DN_HEADS * DN_DV
    has_state = state is not None
    per_b = lambda w: pl.BlockSpec((None, seq, w), lambda bi: (bi, 0, 0))
    const = lambda shape: pl.BlockSpec(shape, lambda bi: (0,) * len(shape))
    st_spec = pl.BlockSpec((None, 2, DN_DK, wide), lambda bi: (bi, 0, 0, 0))
    in_specs = [per_b(4 * wide), per_b(LANES)]
    args = [zc, zm]
    if has_state:
        in_specs.append(st_spec)
        args.append(state)
    in_specs += [const((DN_CONV, DN_QKV)), const((1, LANES)), const((1, LANES)), const((1, wide))]
    args += [lw["dn_conv_w"], lw["dn_alog_row"], lw["dn_dtb_row"], lw["dn_out_g"]]
    out_specs = [per_b(wide)]
    out_shape = [jax.ShapeDtypeStruct((b, seq, wide), F32)]
    if not has_state:
        out_specs.append(st_spec)
        out_shape.append(jax.ShapeDtypeStruct((b, 2, DN_DK, wide), F32))
    res = pl.pallas_call(
        functools.partial(_deltanet_kernel, seq, has_state),
        grid=(b,),
        in_specs=in_specs,
        out_specs=out_specs,
        out_shape=out_shape,
        scratch_shapes=[pltpu.VMEM((seq, wide), F32), pltpu.VMEM((seq, wide), F32),
                        pltpu.VMEM((seq, wide), F32), pltpu.VMEM((2, seq, wide), F32),
                        pltpu.VMEM((2, seq, wide), F32), pltpu.VMEM((seq, wide), F32)],
        compiler_params=_cparams(1),
        name="deltanet_lat" if has_state else "deltanet_ctx",
    )(*args)
    return (res[0], None) if has_state else (res[0], res[1])


def _outffn_kernel(final, oa_ref, ob_ref, oc_ref, od_ref, x_ref, mod_ref, g2_ref, wo_ref, wg_ref,
                   wu_ref, wd_ref, fg_ref, y_ref):
    m = mod_ref[...]
    o = jnp.concatenate([oa_ref[...], ob_ref[...], oc_ref[...], od_ref[...]], axis=-1).astype(BF16)
    x1 = x_ref[...] + m[2:3] * _bdot(o, wo_ref[...])
    h = (_rms_full(x1, g2_ref[...]) * (1.0 + m[4:5]) + m[3:4]).astype(BF16)
    act = (_silu(_bdot(h, wg_ref[...])) * _bdot(h, wu_ref[...])).astype(BF16)
    x2 = x1 + m[5:6] * _bdot(act, wd_ref[...])
    y_ref[...] = _rms_full(x2, fg_ref[...]) if final else x2


def _outffn(outs, x, mods, per_batch_mods, lw, final_g, final):
    b, l, _ = x.shape
    tm = ROW_TILE
    row = lambda w: pl.BlockSpec((None, tm, w), lambda bi, i: (bi, i, 0))
    const = lambda shape: pl.BlockSpec(shape, lambda bi, i: (0,) * len(shape),
                                       pipeline_mode=pl.Buffered(1))
    mod_spec = pl.BlockSpec((None, MOD_CHUNKS, D_MODEL),
                            (lambda bi, i: (bi, 0, 0)) if per_batch_mods else (lambda bi, i: (0, 0, 0)))
    return pl.pallas_call(
        functools.partial(_outffn_kernel, final),
        grid=(b, l // tm),
        in_specs=[row(256), row(256), row(256), row(256), row(D_MODEL), mod_spec, const((1, D_MODEL)),
                  const((D_MODEL, D_MODEL)), const((D_MODEL, D_FF)), const((D_MODEL, D_FF)),
                  const((D_FF, D_MODEL)), const((1, D_MODEL))],
        out_specs=row(D_MODEL),
        out_shape=jax.ShapeDtypeStruct((b, l, D_MODEL), F32),
        compiler_params=_cparams(2),
        name="outffn",
    )(*outs, x, mods, lw["norm2_g"], lw["w_out"], lw["w_gate"], lw["w_up"], lw["w_down"], final_g)


def _rope_tables(n):
    t = jnp.arange(n)

    def axis(pos, half):
        inv = ROPE_BASE ** (-jnp.arange(half, dtype=F32) / half)
        ang = pos.astype(F32)[:, None] * inv[None, :]
        c, s = jnp.cos(ang), jnp.sin(ang)
        return jnp.concatenate([c, c], -1), jnp.concatenate([-s, s], -1)

    cr, sr = axis(t // GRID_W, 16)
    cc, sc = axis(t % GRID_W, 16)
    cos64 = jnp.tile(jnp.concatenate([cr, cc], -1), (1, 2))
    sin64 = jnp.tile(jnp.concatenate([sr, sc], -1), (1, 2))
    cr, sr = axis(t // GRID_W, 8)
    cc, sc = axis(t % GRID_W, 8)
    cos32, sin32 = jnp.concatenate([cr, cc], -1), jnp.concatenate([sr, sc], -1)
    one, zero = jnp.ones((n, 1), F32), jnp.zeros((n, 1), F32)
    cosm = jnp.concatenate([jnp.tile(one, (1, 64)), cos32, jnp.tile(one, (1, 32))], -1)
    sinm = jnp.concatenate([jnp.tile(zero, (1, 64)), sin32, jnp.tile(zero, (1, 32))], -1)
    coskr = jnp.concatenate([cos32, jnp.tile(one, (1, 96))], -1)
    sinkr = jnp.concatenate([sin32, jnp.tile(zero, (1, 96))], -1)
    return cos64, sin64, cosm, sinm, coskr, sinkr


_QA_PERM = np.concatenate([np.arange(0, 64), np.arange(128, 192), np.arange(64, 128), np.arange(192, 256)])


def _layer_weights(p, l):
    w_in = p["w_in"][l]
    cols = np.concatenate([_QA_PERM, np.arange(256, 2304), np.arange(2320, 2736), np.arange(2304, 2320)])
    w_in = jnp.pad(w_in[:, cols], ((0, 0), (0, IN_PAD_COLS - cols.size))).astype(BF16)
    w_out = p["w_out"][l]
    w_out = jnp.concatenate([w_out[_QA_PERM], w_out[256:]], axis=0).astype(BF16)
    wq = p["mla_wq_up"][l].reshape(MLA_Q_LORA, MLA_HEADS, MLA_NOPE + MLA_ROPE)
    wq = jnp.pad(wq, ((0, 0), (0, 0), (0, LANES - MLA_NOPE - MLA_ROPE))).reshape(MLA_Q_LORA, 4 * LANES)
    wkv = p["mla_wkv_up"][l].reshape(MLA_KV_LORA, MLA_HEADS, MLA_NOPE + MLA_V)
    wk_top = jnp.pad(wkv[:, :, :MLA_NOPE], ((0, 0), (0, 0), (0, LANES - MLA_NOPE)))
    place = jnp.pad(jnp.eye(MLA_ROPE, dtype=F32), ((0, LANES - MLA_ROPE), (MLA_NOPE, LANES - MLA_NOPE - MLA_ROPE)))
    wk_bot = jnp.broadcast_to(place[:, None, :], (LANES, MLA_HEADS, LANES))
    wk = jnp.concatenate([wk_top, wk_bot], axis=0).reshape(2 * LANES, 4 * LANES)
    wv = wkv[:, :, MLA_NOPE:].reshape(MLA_KV_LORA, MLA_HEADS * MLA_V)
    gate_row = lambda v: jnp.pad(v.reshape(1, 2 * DN_HEADS), ((0, 0), (MLA_ROPE, LANES - MLA_ROPE - 2 * DN_HEADS)))
    return {
        "norm1_g": p["norm1_g"][l][None], "norm2_g": p["norm2_g"][l][None],
        "w_in": w_in, "w_out": w_out,
        "qn_g": jnp.tile(p["gqa_qn_g"][l], 4)[None], "kn_g": jnp.tile(p["gqa_kn_g"][l], 2)[None],
        "mla_qn_g": p["mla_qn_g"][l][None], "mla_kvn_g": p["mla_kvn_g"][l][None],
        "wq": wq.astype(BF16), "wk": wk.astype(BF16), "wv": wv.astype(BF16),
        "dn_conv_w": p["dn_conv_w"][l], "dn_alog_row": gate_row(p["dn_a_log"][l]),
        "dn_dtb_row": gate_row(p["dn_dt_bias"][l]), "dn_out_g": jnp.tile(p["dn_out_g"][l], DN_HEADS)[None],
        "w_gate": p["ffn_w_gate"][l].astype(BF16), "w_up": p["ffn_w_up"][l].astype(BF16),
        "w_down": p["ffn_w_down"][l].astype(BF16),
    }


def _state_to_wide(s):
    b = s.shape[0]
    return s.transpose(0, 1, 3, 2, 4).reshape(b, 2, DN_DK, DN_HEADS * DN_DV)


def _state_from_wide(s):
    b = s.shape[0]
    return s.reshape(b, 2, DN_DK, DN_HEADS, DN_DV).transpose(0, 1, 3, 2, 4)


def kernel(x_prompt, x_sample, cache_gqa_k, cache_gqa_v, cache_na_k, cache_na_v, state_dn,
           cache_mla_ckv, cache_mla_krope, c, c_ctx, norm1_g, norm2_g, w_mod, b_mod, w_in, w_out,
           gqa_qn_g, gqa_kn_g, na_bias, dn_conv_w, dn_a_log, dn_dt_bias, dn_out_g, mla_qn_g,
           mla_wq_up, mla_kvn_g, mla_wkv_up, ffn_w_gate, ffn_w_up, ffn_w_down, final_g):
    p = {"norm1_g": norm1_g, "norm2_g": norm2_g, "w_in": w_in, "w_out": w_out, "gqa_qn_g": gqa_qn_g,
         "gqa_kn_g": gqa_kn_g, "dn_conv_w": dn_conv_w, "dn_a_log": dn_a_log, "dn_dt_bias": dn_dt_bias,
         "dn_out_g": dn_out_g, "mla_qn_g": mla_qn_g, "mla_wq_up": mla_wq_up, "mla_kvn_g": mla_kvn_g,
         "mla_wkv_up": mla_wkv_up, "ffn_w_gate": ffn_w_gate, "ffn_w_up": ffn_w_up, "ffn_w_down": ffn_w_down}
    nb_ctx, seq_ctx, _ = x_prompt.shape
    nb_lat, seq_lat, _ = x_sample.shape
    past = cache_gqa_k.shape[2]
    fg = final_g[None]

    cond = jnp.concatenate([c_ctx[None], c, jnp.zeros((16 - 1 - nb_lat, D_MODEL), F32)], axis=0)
    mods = _modulation(cond, w_mod, b_mod).reshape(DEPTH, 16, MOD_CHUNKS, D_MODEL)
    weights = [_layer_weights(p, l) for l in range(DEPTH)]

    x = x_prompt
    ctx_out = []
    for l in range(DEPTH):
        lw = weights[l]
        m = mods[l, 0:1]
        qa, ka, va, qb, kb, vb, zc, qd, ckv, zm, krr = _inproj(x, m, False, lw, None)
        o_a = _attn_pair(qa, ka, va, (0, 0), name="gqa_ctx")
        o_b = _attn_pair(qb, kb, vb, (0, 1), name="na_ctx")
        o_c, s_dn = _deltanet(zc, zm, lw)
        o_d = _mla(qd, ckv, krr, lw["wk"], lw["wv"], name="mla_ctx")
        x = _outffn((o_a, o_b, o_c, o_d), x, m, False, lw, fg, l == DEPTH - 1)
        ctx_out.append((ka.reshape(nb_ctx, seq_ctx, GQA_KV_HEADS, HEAD_DIM),
                        va.reshape(nb_ctx, seq_ctx, GQA_KV_HEADS, HEAD_DIM),
                        kb.reshape(nb_ctx, seq_ctx, NA_HEADS, HEAD_DIM),
                        vb.reshape(nb_ctx, seq_ctx, NA_HEADS, HEAD_DIM),
                        _state_from_wide(s_dn), ckv, zm[:, :, :MLA_ROPE]))
    y_prompt = x
    new = [jnp.stack([s[i] for s in ctx_out], axis=1) for i in range(7)]

    ropes = _rope_tables(seq_lat)
    ck_a = cache_gqa_k.reshape(nb_lat, DEPTH, past, GQA_KV_HEADS * HEAD_DIM)
    cv_a = cache_gqa_v.reshape(nb_lat, DEPTH, past, GQA_KV_HEADS * HEAD_DIM)
    ck_b = cache_na_k.reshape(nb_lat, DEPTH, past, NA_HEADS * HEAD_DIM)
    cv_b = cache_na_v.reshape(nb_lat, DEPTH, past, NA_HEADS * HEAD_DIM)
    c_kr = jnp.pad(cache_mla_krope, ((0, 0), (0, 0), (0, 0), (0, LANES - MLA_ROPE)))
    x = x_sample
    for l in range(DEPTH):
        lw = weights[l]
        m = mods[l, 1:1 + nb_lat]
        qa, ka, va, qb, kb, vb, zc, qd, ckv, zm, krr = _inproj(x, m, True, lw, ropes)
        o_a = _attn_pair(qa, ka, va, (0, 0), cache=(ck_a, cv_a, l), name="gqa_lat")
        o_b = _na_latent(qb, kb, vb, ck_b, cv_b, l, _na_bias_blocks(na_bias[l]))
        o_c, _ = _deltanet(zc, zm, lw, state=_state_to_wide(state_dn[:, l]))
        o_d = _mla(qd, ckv, krr, lw["wk"], lw["wv"], cache=(cache_mla_ckv, c_kr, l), name="mla_lat")
        x = _outffn((o_a, o_b, o_c, o_d), x, m, True, lw, fg, l == DEPTH - 1)
    y_sample = x

    return (y_prompt, y_sample, *new)
```

```python
import functools

import numpy as np
import jax
import jax.numpy as jnp
from jax import lax
from jax.experimental import pallas as pl
from jax.experimental.pallas import tpu as pltpu

F32 = jnp.float32
BF16 = jnp.bfloat16

D_MODEL = 1024
DEPTH = 2
GRID_W = 64
HEAD_DIM = 64
ROPE_BASE = 10000.0
NEG_INF = -1e30
MOD_CHUNKS = 6
GQA_HEADS, GQA_KV_HEADS = 4, 2
NA_HEADS, NA_KH, NA_KW = 4, 8, 16
DN_HEADS, DN_DK, DN_DV, DN_CONV, DN_CHUNK = 4, 64, 64, 4, 64
DN_QKV = DN_HEADS * (2 * DN_DK + DN_DV)
MLA_HEADS, MLA_Q_LORA, MLA_KV_LORA, MLA_NOPE, MLA_ROPE, MLA_V = 4, 256, 128, 64, 32, 64
MLA_SCALE = (MLA_NOPE + MLA_ROPE) ** -0.5
D_FF = -(-8 * D_MODEL // (3 * 256)) * 256
EPS = 1e-6

LANES = 128
ROW_TILE = 256
Q_TILE = 256
NA_BAND = 768
PREP_CHUNKS = 4
IN_PAD_COLS = 2816
VMEM_LIMIT = 56 * 1024 * 1024

_NT = (((1,), (1,)), ((), ()))
_TN = (((0,), (0,)), ((), ()))


def _cparams(n_axes):
    return pltpu.CompilerParams(dimension_semantics=("arbitrary",) * n_axes,
                                vmem_limit_bytes=VMEM_LIMIT)


def _lane(shape):
    return lax.broadcasted_iota(jnp.int32, shape, len(shape) - 1)


def _silu(x):
    return x / (1.0 + jnp.exp(-x))


def _rms_full(x, g):
    return x * lax.rsqrt(jnp.mean(x * x, axis=-1, keepdims=True) + EPS) * g


def _seg64_sum(x):
    lo = _lane(x.shape) < HEAD_DIM
    s_lo = jnp.sum(jnp.where(lo, x, 0.0), axis=-1, keepdims=True)
    s_hi = jnp.sum(jnp.where(lo, 0.0, x), axis=-1, keepdims=True)
    return jnp.where(lo, s_lo, s_hi)


def _head_rms(x, g):
    parts = []
    for p in range(x.shape[-1] // LANES):
        xp = x[:, p * LANES:(p + 1) * LANES]
        ms = _seg64_sum(xp * xp) * (1.0 / HEAD_DIM)
        parts.append(xp * lax.rsqrt(ms + EPS))
    y = parts[0] if len(parts) == 1 else jnp.concatenate(parts, axis=-1)
    return y * g


def _head_l2(x):
    parts = []
    for p in range(x.shape[-1] // LANES):
        xp = x[:, p * LANES:(p + 1) * LANES]
        parts.append(xp * lax.rsqrt(_seg64_sum(xp * xp) + EPS))
    return parts[0] if len(parts) == 1 else jnp.concatenate(parts, axis=-1)


def _rope(x, cos, sin, half):
    first = (_lane(x.shape) & (2 * half - 1)) < half
    rot = jnp.where(first, pltpu.roll(x, LANES - half, 1), pltpu.roll(x, half, 1))
    return x * cos + rot * sin


def _softmax_parts(scores):
    m = jnp.max(scores[0], axis=-1, keepdims=True)
    for s in scores[1:]:
        m = jnp.maximum(m, jnp.max(s, axis=-1, keepdims=True))
    es = [jnp.exp(s - m) for s in scores]
    l = jnp.sum(es[0], axis=-1, keepdims=True)
    for e in es[1:]:
        l = l + jnp.sum(e, axis=-1, keepdims=True)
    return es, 1.0 / l


def _bdot(a, b):
    return jnp.dot(a, b, preferred_element_type=F32)


def _mod_kernel(c_ref, w_ref, b_ref, o_ref):
    s = _silu(c_ref[...]).astype(BF16)
    o_ref[...] = _bdot(s, w_ref[...].astype(BF16)) + b_ref[...]


def _modulation(cond, w_mod, b_mod):
    n = MOD_CHUNKS * D_MODEL
    tn = 1536
    return pl.pallas_call(
        _mod_kernel,
        grid=(DEPTH, n // tn),
        in_specs=[pl.BlockSpec((16, D_MODEL), lambda l, j: (0, 0)),
                  pl.BlockSpec((None, D_MODEL, tn), lambda l, j: (l, 0, j)),
                  pl.BlockSpec((None, 1, tn), lambda l, j: (l, 0, j))],
        out_specs=pl.BlockSpec((None, 16, tn), lambda l, j: (l, 0, j)),
        out_shape=jax.ShapeDtypeStruct((DEPTH, 16, n), F32),
        compiler_params=_cparams(2),
        name="modulation",
    )(cond, w_mod, b_mod.reshape(DEPTH, 1, n))


_IN_OUT_WIDTHS = (256, 128, 128, 256, 256, 256, 1024, 512, 128, 128, 128)


def _inproj_kernel(positioned, *refs):
    (x_ref, mod_ref, g1_ref, w_ref, qng_ref, kng_ref, mqg_ref, wq_ref, mkg_ref) = refs[:9]
    n_in = 9
    if positioned:
        cos64_ref, sin64_ref, cosm_ref, sinm_ref, coskr_ref, sinkr_ref = refs[9:15]
        n_in = 15
    (qa_ref, ka_ref, va_ref, qb_ref, kb_ref, vb_ref, zc_ref, qd_ref, ckv_ref, zm_ref,
     krr_ref) = refs[n_in:]

    m = mod_ref[...]
    h = _rms_full(x_ref[...], g1_ref[...]) * (1.0 + m[1:2]) + m[0:1]
    hb = h.astype(BF16)

    za = _bdot(hb, w_ref[:, 0:512])
    q = _head_rms(za[:, 0:256], qng_ref[...])
    k = _head_rms(za[:, 256:384], kng_ref[...])
    if positioned:
        cos, sin = cos64_ref[...], sin64_ref[...]
        q = jnp.concatenate([_rope(q[:, 0:128], cos, sin, 16), _rope(q[:, 128:256], cos, sin, 16)],
                            axis=-1)
        k = _rope(k, cos, sin, 16)
    qa_ref[...] = q * (HEAD_DIM ** -0.5)
    ka_ref[...] = k
    va_ref[...] = za[:, 384:512]

    zb = _bdot(hb, w_ref[:, 512:1280])
    qb_ref[...] = zb[:, 0:256] * (HEAD_DIM ** -0.5)
    kb_ref[...] = zb[:, 256:512]
    vb_ref[...] = zb[:, 512:768]

    zc_ref[...] = _bdot(hb, w_ref[:, 1280:2304])

    zd = _bdot(hb, w_ref[:, 2304:2816])
    cq = _rms_full(zd[:, 0:256], mqg_ref[...])
    qm = _bdot(cq.astype(BF16), wq_ref[...])
    if positioned:
        cm, sm = cosm_ref[...], sinm_ref[...]
        qm = jnp.concatenate([_rope(qm[:, i * LANES:(i + 1) * LANES], cm, sm, 8)
                              for i in range(MLA_HEADS)], axis=-1)
    qd_ref[...] = qm * MLA_SCALE
    ckv_ref[...] = _rms_full(zd[:, 256:384], mkg_ref[...])
    zmisc = zd[:, 384:512]
    zm_ref[...] = zmisc
    kr = jnp.where(_lane(zmisc.shape) < MLA_ROPE, zmisc, 0.0)
    if positioned:
        kr = _rope(kr, coskr_ref[...], sinkr_ref[...], 8)
    krr_ref[...] = kr


def _inproj(x, mods, per_batch_mods, lw, ropes):
    b, l, _ = x.shape
    tm = ROW_TILE
    positioned = ropes is not None
    row = lambda w: pl.BlockSpec((None, tm, w), lambda bi, i: (bi, i, 0))
    const = lambda shape: pl.BlockSpec(shape, lambda bi, i: (0,) * len(shape))
    mod_spec = pl.BlockSpec((None, MOD_CHUNKS, D_MODEL),
                            (lambda bi, i: (bi, 0, 0)) if per_batch_mods else (lambda bi, i: (0, 0, 0)))
    in_specs = [row(D_MODEL), mod_spec, const((1, D_MODEL)), const((D_MODEL, IN_PAD_COLS)),
                const((1, 256)), const((1, 128)), const((1, MLA_Q_LORA)),
                const((MLA_Q_LORA, 4 * LANES)), const((1, MLA_KV_LORA))]
    args = [x, mods, lw["norm1_g"], lw["w_in"], lw["qn_g"], lw["kn_g"], lw["mla_qn_g"], lw["wq"],
            lw["mla_kvn_g"]]
    if positioned:
        in_specs += [pl.BlockSpec((tm, LANES), lambda bi, i: (i, 0))] * 6
        args += list(ropes)
    return pl.pallas_call(
        functools.partial(_inproj_kernel, positioned),
        grid=(b, l // tm),
        in_specs=in_specs,
        out_specs=[row(w) for w in _IN_OUT_WIDTHS],
        out_shape=[jax.ShapeDtypeStruct((b, l, w), F32) for w in _IN_OUT_WIDTHS],
        compiler_params=_cparams(2),
        name="inproj_lat" if positioned else "inproj_ctx",
    )(*args)


def _attn_pair_kernel(has_cache, qmap, *refs):
    if has_cache:
        q_ref, kc_ref, vc_ref, k_ref, v_ref, o_ref, kbuf, vbuf = refs
    else:
        q_ref, k_ref, v_ref, o_ref, kbuf, vbuf = refs

    @pl.when(pl.program_id(1) == 0)
    def _():
        off = 0
        if has_cache:
            off = kc_ref.shape[0]
            kbuf[0:off, :] = kc_ref[...].astype(BF16)
            vbuf[0:off, :] = vc_ref[...].astype(BF16)
        kbuf[off:, :] = k_ref[...].astype(BF16)
        vbuf[off:, :] = v_ref[...].astype(BF16)

    tq = q_ref.shape[0]
    lo = _lane((tq, LANES)) < HEAD_DIM
    for p, kv in enumerate(qmap):
        qp = q_ref[:, p * LANES:(p + 1) * LANES]
        kblk = kbuf[:, kv * LANES:(kv + 1) * LANES]
        vblk = vbuf[:, kv * LANES:(kv + 1) * LANES]
        outs = []
        for half in range(2):
            qm = jnp.where(lo if half == 0 else jnp.logical_not(lo), qp, 0.0).astype(BF16)
            s = lax.dot_general(qm, kblk, _NT, preferred_element_type=F32)
            (e,), rl = _softmax_parts([s])
            outs.append(_bdot(e.astype(BF16), vblk) * rl)
        o_ref[:, p * LANES:(p + 1) * LANES] = jnp.where(lo, outs[0], outs[1])


def _attn_pair(q, k, v, qmap, cache=None, name="attn_pair"):
    b, lq, wq = q.shape
    ls, wk = k.shape[1], k.shape[2]
    tq = Q_TILE
    lc = 0 if cache is None else cache[0].shape[2]
    in_specs = [pl.BlockSpec((None, tq, wq), lambda bi, i: (bi, i, 0))]
    args = [q]
    if cache is not None:
        kc, vc, layer = cache
        cspec = pl.BlockSpec((None, None, lc, wk), lambda bi, i: (bi, layer, 0, 0))
        in_specs += [cspec, cspec]
        args += [kc, vc]
    sspec = pl.BlockSpec((None, ls, wk), lambda bi, i: (bi, 0, 0))
    in_specs += [sspec, sspec]
    args += [k, v]
    return pl.pallas_call(
        functools.partial(_attn_pair_kernel, cache is not None, qmap),
        grid=(b, lq // tq),
        in_specs=in_specs,
        out_specs=pl.BlockSpec((None, tq, wq), lambda bi, i: (bi, i, 0)),
        out_shape=jax.ShapeDtypeStruct((b, lq, wq), F32),
        scratch_shapes=[pltpu.VMEM((lc + ls, wk), BF16), pltpu.VMEM((lc + ls, wk), BF16)],
        compiler_params=_cparams(2),
        name=name,
    )(*args)


def _mla_kernel(has_cache, *refs):
    if has_cache:
        q_ref, ckvc_ref, krc_ref, ckv_ref, kr_ref, wk_ref, wv_ref, o_ref, kbuf, vbuf = refs
    else:
        q_ref, ckv_ref, kr_ref, wk_ref, wv_ref, o_ref, kbuf, vbuf = refs

    @pl.when(pl.program_id(1) == 0)
    def _():
        def expand(c_ref, r_ref, r0, r1):
            c = c_ref[...].astype(BF16)
            ckr = jnp.concatenate([c, r_ref[...].astype(BF16)], axis=-1)
            kbuf[r0:r1, :] = _bdot(ckr, wk_ref[...]).astype(BF16)
            vbuf[r0:r1, :] = _bdot(c, wv_ref[...]).astype(BF16)
        off = 0
        if has_cache:
            off = ckvc_ref.shape[0]
            expand(ckvc_ref, krc_ref, 0, off)
        expand(ckv_ref, kr_ref, off, kbuf.shape[0])

    tq = q_ref.shape[0]
    lo = _lane((tq, LANES)) < MLA_V
    for p in range(MLA_HEADS // 2):
        vblk = vbuf[:, p * LANES:(p + 1) * LANES]
        outs = []
        for half in range(2):
            hd = 2 * p + half
            qh = q_ref[:, hd * LANES:(hd + 1) * LANES].astype(BF16)
            s = lax.dot_general(qh, kbuf[:, hd * LANES:(hd + 1) * LANES], _NT,
                                preferred_element_type=F32)
            (e,), rl = _softmax_parts([s])
            outs.append(_bdot(e.astype(BF16), vblk) * rl)
        o_ref[:, p * LANES:(p + 1) * LANES] = jnp.where(lo, outs[0], outs[1])


def _mla(q, ckv, kr, wk, wv, cache=None, name="mla"):
    b, lq, wq = q.shape
    ls = ckv.shape[1]
    tq = Q_TILE
    lc = 0 if cache is None else cache[0].shape[2]
    in_specs = [pl.BlockSpec((None, tq, wq), lambda bi, i: (bi, i, 0))]
    args = [q]
    if cache is not None:
        ckvc, krc, layer = cache
        cspec = pl.BlockSpec((None, None, lc, LANES), lambda bi, i: (bi, layer, 0, 0))
        in_specs += [cspec, cspec]
        args += [ckvc, krc]
    sspec = pl.BlockSpec((None, ls, LANES), lambda bi, i: (bi, 0, 0))
    in_specs += [sspec, sspec,
                 pl.BlockSpec((2 * LANES, 4 * LANES), lambda bi, i: (0, 0)),
                 pl.BlockSpec((LANES, 2 * LANES), lambda bi, i: (0, 0))]
    args += [ckv, kr, wk, wv]
    return pl.pallas_call(
        functools.partial(_mla_kernel, cache is not None),
        grid=(b, lq // tq),
        in_specs=in_specs,
        out_specs=pl.BlockSpec((None, tq, 2 * LANES), lambda bi, i: (bi, i, 0)),
        out_shape=jax.ShapeDtypeStruct((b, lq, 2 * LANES), F32),
        scratch_shapes=[pltpu.VMEM((lc + ls, 4 * LANES), BF16), pltpu.VMEM((lc + ls, 2 * LANES), BF16)],
        compiler_params=_cparams(2),
        name=name,
    )(*args)


def _na_kernel(q_ref, k_ref, v_ref, kc_ref, vc_ref, bias_ref, o_ref):
    j = pl.program_id(0)
    start = pl.multiple_of((j >> 1) * 256, 256)
    kband = k_ref[pl.ds(start, NA_BAND), :].astype(BF16)
    vband = v_ref[pl.ds(start, NA_BAND), :].astype(BF16)
    kc = kc_ref[...].astype(BF16)
    vc = vc_ref[...].astype(BF16)
    tq = q_ref.shape[0]
    lo = _lane((tq, LANES)) < HEAD_DIM
    for p in range(NA_HEADS // 2):
        sl = slice(p * LANES, (p + 1) * LANES)
        qp = q_ref[:, sl]
        outs = []
        for half in range(2):
            qm = jnp.where(lo if half == 0 else jnp.logical_not(lo), qp, 0.0).astype(BF16)
            s_loc = lax.dot_general(qm, kband[:, sl], _NT, preferred_element_type=F32)
            s_loc = s_loc + bias_ref[2 * p + half]
            s_ctx = lax.dot_general(qm, kc[:, sl], _NT, preferred_element_type=F32)
            (e_loc, e_ctx), rl = _softmax_parts([s_loc, s_ctx])
            o = _bdot(e_loc.astype(BF16), vband[:, sl]) + _bdot(e_ctx.astype(BF16), vc[:, sl])
            outs.append(o * rl)
        o_ref[:, sl] = jnp.where(lo, outs[0], outs[1])


def _na_latent(q, k, v, kc, vc, layer, bias_blocks):
    b, n, w = q.shape
    lc = kc.shape[2]
    nq = n // Q_TILE
    full = pl.BlockSpec((None, n, w), lambda j, bi: (bi, 0, 0))
    cspec = pl.BlockSpec((None, None, lc, w), lambda j, bi: (bi, layer, 0, 0))
    return pl.pallas_call(
        _na_kernel,
        grid=(nq, b),
        in_specs=[pl.BlockSpec((None, Q_TILE, w), lambda j, bi: (bi, j, 0)), full, full, cspec, cspec,
                  pl.BlockSpec((None, NA_HEADS, Q_TILE, NA_BAND), lambda j, bi: (j, 0, 0, 0))],
        out_specs=pl.BlockSpec((None, Q_TILE, w), lambda j, bi: (bi, j, 0)),
        out_shape=jax.ShapeDtypeStruct((b, n, w), F32),
        compiler_params=_cparams(2),
        name="na_latent",
    )(q, k, v, kc, vc, bias_blocks)


NA_GRID_ROWS = 16
NA_BAND_ROW0 = (0, 0, 4, 4)


def _na_bias_kernel(b_ref, o_ref, tp_s):
    hd = pl.program_id(0)
    n_dr, n_dc = 2 * NA_KH - 1, 2 * NA_KW - 1
    shape = (GRID_W, LANES)
    c = lax.broadcasted_iota(jnp.int32, shape, 0)
    lane = _lane(shape)
    kc = lane & (GRID_W - 1)
    lo = lane < GRID_W
    diff = kc - c + (NA_KW - 1)
    c0 = jnp.clip(c - NA_KW // 2, 0, GRID_W - NA_KW)
    col_ok = (kc >= c0) & (kc < c0 + NA_KW)
    neg = jnp.full(shape, NEG_INF, F32)
    for dr0 in range(-1, n_dr):
        acc = neg
        for d in range(n_dc):
            v_lo = b_ref[hd * n_dr + dr0, d] if dr0 >= 0 else 0.0
            v_hi = b_ref[hd * n_dr + dr0 + 1, d] if dr0 + 1 < n_dr else 0.0
            acc = jnp.where(diff == d, jnp.where(lo, v_lo, v_hi), acc)
        tp_s[dr0 + 1] = jnp.where(col_ok, acc, NEG_INF)
    for j in range(NA_GRID_ROWS // 4):
        for ri in range(4):
            r = 4 * j + ri
            r0 = min(max(r - NA_KH // 2, 0), NA_GRID_ROWS - NA_KH)
            for kp in range(NA_BAND // LANES):
                kr = NA_BAND_ROW0[j] + 2 * kp
                ok_lo, ok_hi = r0 <= kr < r0 + NA_KH, r0 <= kr + 1 < r0 + NA_KH
                dr0 = kr - r + (NA_KH - 1)
                if ok_lo and ok_hi:
                    t = tp_s[dr0 + 1]
                elif ok_lo:
                    t = jnp.where(lo, tp_s[dr0 + 1], NEG_INF)
                elif ok_hi:
                    t = jnp.where(lo, NEG_INF, tp_s[dr0 + 1])
                else:
                    t = neg
                o_ref[j, ri * GRID_W:(ri + 1) * GRID_W, kp * LANES:(kp + 1) * LANES] = t


def _na_bias_blocks(bias):
    nq = NA_GRID_ROWS // 4
    return pl.pallas_call(
        _na_bias_kernel,
        grid=(NA_HEADS,),
        in_specs=[pl.BlockSpec(memory_space=pltpu.SMEM)],
        out_specs=pl.BlockSpec((nq, None, Q_TILE, NA_BAND), lambda h: (0, h, 0, 0)),
        out_shape=jax.ShapeDtypeStruct((nq, NA_HEADS, Q_TILE, NA_BAND), F32),
        scratch_shapes=[pltpu.VMEM((2 * NA_KH, GRID_W, LANES), F32)],
        compiler_params=_cparams(1),
        name="na_bias",
    )(bias.reshape(NA_HEADS * (2 * NA_KH - 1), 2 * NA_KW - 1))


def _widen(cols, n):
    blk = _lane((n, DN_HEADS * DN_DV)) >> 6
    return jnp.where(blk == 0, cols[0], jnp.where(blk == 1, cols[1], jnp.where(blk == 2, cols[2], cols[3])))


def _deltanet_kernel(seq, has_state, *refs):
    if has_state:
        (zc_ref, zm_ref, s0_ref, cw_ref, alog_ref, dtb_ref, og_ref, o_ref,
         q_s, k_s, v_s, b_s, g_s, o_s, c_s, mp_s) = refs
    else:
        (zc_ref, zm_ref, cw_ref, alog_ref, dtb_ref, og_ref, o_ref, sfin_ref,
         q_s, k_s, v_s, b_s, g_s, o_s, c_s, mp_s) = refs
    n_chunks = seq // DN_CHUNK
    wide = DN_HEADS * DN_DV

    row = lax.broadcasted_iota(jnp.int32, (seq, wide), 0)
    for part, dst in enumerate((q_s, k_s, v_s)):
        cs = slice(part * wide, (part + 1) * wide)
        x = zc_ref[:, cs]
        w = cw_ref[:, cs]
        y = (w[0:1] * jnp.where(row >= 1, pltpu.roll(x, 1, 0), 0.0) + w[1:2] * x
             + w[2:3] * jnp.where(row < seq - 1, pltpu.roll(x, seq - 1, 0), 0.0)
             + w[3:4] * jnp.where(row < seq - 2, pltpu.roll(x, seq - 2, 0), 0.0))
        y = _silu(y)
        if part == 0:
            y = _head_l2(y) * (DN_DK ** -0.5)
        elif part == 1:
            y = _head_l2(y)
        dst[...] = y

    zm = zm_ref[...]
    xa = zm + dtb_ref[...]
    logd = -jnp.exp(alog_ref[...]) * (jnp.maximum(xa, 0.0) + jnp.log1p(jnp.exp(-jnp.abs(xa))))
    beta = 1.0 / (1.0 + jnp.exp(-zm))
    a_off, b_off = MLA_ROPE, MLA_ROPE + 2 * DN_HEADS
    ri = lax.broadcasted_iota(jnp.int32, (256, 256), 0)
    ci = lax.broadcasted_iota(jnp.int32, (256, 256), 1)
    same = (ri >> 6) == (ci >> 6)
    for d in range(2):
        b_s[d] = _widen([beta[:, b_off + 4 * d + hd:b_off + 4 * d + hd + 1] for hd in range(DN_HEADS)], seq)
        lw = _widen([logd[:, a_off + 4 * d + hd:a_off + 4 * d + hd + 1] for hd in range(DN_HEADS)], seq)
        tri = jnp.where(same & ((ci <= ri) if d == 0 else (ci >= ri)), 1.0, 0.0).astype(F32)
        for rb in range(seq // 256):
            rs = slice(rb * 256, (rb + 1) * 256)
            g_s[d, rs, :] = jnp.dot(tri, lw[rs], preferred_element_type=F32,
                                    precision=lax.Precision.HIGHEST)

    ii = lax.broadcasted_iota(jnp.int32, (DN_CHUNK, wide), 0)
    jj = _lane((DN_CHUNK, wide)) & (DN_CHUNK - 1)
    blk = _lane((DN_CHUNK, wide)) >> 6
    diag = ii == jj
    eye = jnp.where(diag, 1.0, 0.0)
    head_mask = [jnp.where(blk == hd, 1.0, 0.0).astype(BF16) for hd in range(DN_HEADS)]

    def bd(z):
        zb = z.astype(BF16)
        return jnp.concatenate([zb * hm for hm in head_mask], axis=0)

    def fold(gram):
        out = jnp.where(blk == 0, gram[0:DN_CHUNK], 0.0)
        for hd in range(1, DN_HEADS):
            out = out + jnp.where(blk == hd, gram[hd * DN_CHUNK:(hd + 1) * DN_CHUNK], 0.0)
        return out

    tri_masks = []
    for d in range(2):
        incl = (jj <= ii) if d == 0 else (jj >= ii)
        strict = (jj < ii) if d == 0 else (jj > ii)
        pair = [((ii >> (lvl + 1)) == (jj >> (lvl + 1)))
                & (((ii >> lvl) & 1) == (1 - d)) & (((jj >> lvl) & 1) == d) for lvl in range(6)]
        tri_masks.append((incl, strict, pair))

    def prepare(step, carry):
        chunks = [step * PREP_CHUNKS + i for i in range(PREP_CHUNKS)]
        rows = [pl.ds(pl.multiple_of(c * DN_CHUNK, DN_CHUNK), DN_CHUNK) for c in chunks]
        qkv = [(q_s[r, :], k_s[r, :], v_s[r, :]) for r in rows]
        inst = [(ci, d) for ci in range(PREP_CHUNKS) for d in range(2)]
        beta = {(ci, d): b_s[d, rows[ci], :] for ci, d in inst}
        kb = {(ci, d): qkv[ci][1] * beta[ci, d] for ci, d in inst}
        r = [lax.dot_general(jnp.concatenate([kb[ci, 0], kb[ci, 1], qkv[ci][0]], axis=0).astype(BF16),
                             bd(qkv[ci][1]), _NT, preferred_element_type=F32)
             for ci in range(PREP_CHUNKS)]
        g, a, qk, eg, t = {}, {}, {}, {}, {}
        for ci, d in inst:
            incl, strict, pair = tri_masks[d]
            g[ci, d] = g_s[d, rows[ci], :]
            g_row = jnp.sum(jnp.where(diag, g[ci, d], 0.0), axis=0, keepdims=True)
            dm = jnp.where(incl, jnp.exp(jnp.where(incl, g[ci, d] - g_row, 0.0)), 0.0)
            a[ci, d] = jnp.where(strict, r[ci][d * DN_CHUNK:(d + 1) * DN_CHUNK] * dm, 0.0)
            qk[ci, d] = (r[ci][2 * DN_CHUNK:] * dm).astype(BF16)
            eg[ci, d] = jnp.exp(g[ci, d])
            t[ci, d] = eye - jnp.where(pair[0], a[ci, d], 0.0)
        for lvl in range(1, 6):
            te = {i: _bdot(t[i].astype(BF16), bd(jnp.where(tri_masks[i[1]][2][lvl], a[i], 0.0))) for i in inst}
            t = {i: t[i] - _bdot(te[i].astype(BF16), bd(t[i])) for i in inst}
        nb = {i: jnp.where(diag, 0.0, t[i]).astype(BF16) for i in inst}
        rhs_u = {(ci, d): qkv[ci][2] * beta[ci, d] for ci, d in inst}
        rhs_w = {i: kb[i] * eg[i] for i in inst}
        u = {i: rhs_u[i] + _bdot(nb[i], bd(rhs_u[i])) for i in inst}
        w = {i: rhs_w[i] + _bdot(nb[i], bd(rhs_w[i])) for i in inst}
        kd = {}
        for ci, d in inst:
            g_last = g[ci, d][DN_CHUNK - 1:DN_CHUNK] if d == 0 else g[ci, d][0:1]
            kd[ci, d] = (qkv[ci][1] * jnp.exp(g_last - g[ci, d])).astype(BF16)
        p = {(ci, d): qkv[ci][0] * eg[ci, d] - _bdot(qk[ci, d], bd(w[ci, d])) for ci, d in inst}
        o0 = {i: _bdot(qk[i], bd(u[i])) for i in inst}
        m = {i: fold(lax.dot_general(kd[i], w[i].astype(BF16), _TN, preferred_element_type=F32)) for i in inst}
        cc = {i: fold(lax.dot_general(kd[i], u[i].astype(BF16), _TN, preferred_element_type=F32)) for i in inst}
        for ci, d in inst:
            mrow = pl.ds(pl.multiple_of(chunks[ci] * (2 * DN_CHUNK), 2 * DN_CHUNK), 2 * DN_CHUNK)
            c_s[d, rows[ci], :] = cc[ci, d]
            mp_s[d, mrow, :] = jnp.concatenate([m[ci, d], p[ci, d]], axis=0).astype(BF16)
        for ci in range(PREP_CHUNKS):
            o_s[rows[ci], :] = o0[ci, 0] + o0[ci, 1]
        return carry

    lax.fori_loop(0, n_chunks // PREP_CHUNKS, prepare, 0)

    def scan(i, states):
        new = []
        for d in range(2):
            c = i if d == 0 else n_chunks - 1 - i
            r0 = pl.multiple_of(c * DN_CHUNK, DN_CHUNK)
            rows = pl.ds(r0, DN_CHUNK)
            mrow = pl.ds(pl.multiple_of(c * (2 * DN_CHUNK), 2 * DN_CHUNK), 2 * DN_CHUNK)
            edge = pl.ds(pl.multiple_of(r0 + (DN_CHUNK - 8 if d == 0 else 0), 8), 8)
            g_edge = g_s[d, edge, :]
            g_last = g_edge[7:8] if d == 0 else g_edge[0:1]
            res = _bdot(mp_s[d, mrow, :], bd(states[d]))
            o_s[rows, :] = o_s[rows, :] + res[DN_CHUNK:]
            new.append(states[d] * jnp.exp(g_last) - res[0:DN_CHUNK] + c_s[d, rows, :])
        return tuple(new)

    init = tuple(s0_ref[d] if has_state else jnp.zeros((DN_DK, wide), F32) for d in range(2))
    fin = lax.fori_loop(0, n_chunks, scan, init)
    if not has_state:
        sfin_ref[0] = fin[0]
        sfin_ref[1] = fin[1]

    o_ref[...] = _head_rms(o_s[...], og_ref[...]) * _silu(zc_ref[:, 3 * wide:4 * wide])


def _deltanet(zc, zm, lw, state=None):
    b, seq, _ = zc.shape
    wide = DN_HEADS * DN_DV
    has_state = state is not None
    per_b = lambda w: pl.BlockSpec((None, seq, w), lambda bi: (bi, 0, 0))
    const = lambda shape: pl.BlockSpec(shape, lambda bi: (0,) * len(shape))
    st_spec = pl.BlockSpec((None, 2, DN_DK, wide), lambda bi: (bi, 0, 0, 0))
    in_specs = [per_b(4 * wide), per_b(LANES)]
    args = [zc, zm]
    if has_state:
        in_specs.append(st_spec)
        args.append(state)
    in_specs += [const((DN_CONV, DN_QKV)), const((1, LANES)), const((1, LANES)), const((1, wide))]
    args += [lw["dn_conv_w"], lw["dn_alog_row"], lw["dn_dtb_row"], lw["dn_out_g"]]
    out_specs = [per_b(wide)]
    out_shape = [jax.ShapeDtypeStruct((b, seq, wide), F32)]
    if not has_state:
        out_specs.append(st_spec)
        out_shape.append(jax.ShapeDtypeStruct((b, 2, DN_DK, wide), F32))
    res = pl.pallas_call(
        functools.partial(_deltanet_kernel, seq, has_state),
        grid=(b,),
        in_specs=in_specs,
        out_specs=out_specs,
        out_shape=out_shape,
        scratch_shapes=[pltpu.VMEM((seq, wide), F32), pltpu.VMEM((seq, wide), F32),
                        pltpu.VMEM((seq, wide), F32), pltpu.VMEM((2, seq, wide), F32),
                        pltpu.VMEM((2, seq, wide), F32), pltpu.VMEM((seq, wide), F32),
                        pltpu.VMEM((2, seq, wide), F32), pltpu.VMEM((2, 2 * seq, wide), BF16)],
        compiler_params=_cparams(1),
        name="deltanet_lat" if has_state else "deltanet_ctx",
    )(*args)
    return (res[0], None) if has_state else (res[0], res[1])


def _outffn_kernel(final, oa_ref, ob_ref, oc_ref, od_ref, x_ref, mod_ref, g2_ref, wo_ref, wg_ref,
                   wu_ref, wd_ref, fg_ref, y_ref):
    m = mod_ref[...]
    o = jnp.concatenate([oa_ref[...], ob_ref[...], oc_ref[...], od_ref[...]], axis=-1).astype(BF16)
    x1 = x_ref[...] + m[2:3] * _bdot(o, wo_ref[...])
    h = (_rms_full(x1, g2_ref[...]) * (1.0 + m[4:5]) + m[3:4]).astype(BF16)
    act = (_silu(_bdot(h, wg_ref[...])) * _bdot(h, wu_ref[...])).astype(BF16)
    x2 = x1 + m[5:6] * _bdot(act, wd_ref[...])
    y_ref[...] = _rms_full(x2, fg_ref[...]) if final else x2


def _outffn(outs, x, mods, per_batch_mods, lw, final_g, final):
    b, l, _ = x.shape
    tm = ROW_TILE
    row = lambda w: pl.BlockSpec((None, tm, w), lambda bi, i: (bi, i, 0))
    const = lambda shape: pl.BlockSpec(shape, lambda bi, i: (0,) * len(shape),
                                       pipeline_mode=pl.Buffered(1))
    mod_spec = pl.BlockSpec((None, MOD_CHUNKS, D_MODEL),
                            (lambda bi, i: (bi, 0, 0)) if per_batch_mods else (lambda bi, i: (0, 0, 0)))
    return pl.pallas_call(
        functools.partial(_outffn_kernel, final),
        grid=(b, l // tm),
        in_specs=[row(256), row(256), row(256), row(256), row(D_MODEL), mod_spec, const((1, D_MODEL)),
                  const((D_MODEL, D_MODEL)), const((D_MODEL, D_FF)), const((D_MODEL, D_FF)),
                  const((D_FF, D_MODEL)), const((1, D_MODEL))],
        out_specs=row(D_MODEL),
        out_shape=jax.ShapeDtypeStruct((b, l, D_MODEL), F32),
        compiler_params=_cparams(2),
        name="outffn",
    )(*outs, x, mods, lw["norm2_g"], lw["w_out"], lw["w_gate"], lw["w_up"], lw["w_down"], final_g)


def _rope_tables(n):
    t = jnp.arange(n)

    def axis(pos, half):
        inv = ROPE_BASE ** (-jnp.arange(half, dtype=F32) / half)
        ang = pos.astype(F32)[:, None] * inv[None, :]
        c, s = jnp.cos(ang), jnp.sin(ang)
        return jnp.concatenate([c, c], -1), jnp.concatenate([-s, s], -1)

    cr, sr = axis(t // GRID_W, 16)
    cc, sc = axis(t % GRID_W, 16)
    cos64 = jnp.tile(jnp.concatenate([cr, cc], -1), (1, 2))
    sin64 = jnp.tile(jnp.concatenate([sr, sc], -1), (1, 2))
    cr, sr = axis(t // GRID_W, 8)
    cc, sc = axis(t % GRID_W, 8)
    cos32, sin32 = jnp.concatenate([cr, cc], -1), jnp.concatenate([sr, sc], -1)
    one, zero = jnp.ones((n, 1), F32), jnp.zeros((n, 1), F32)
    cosm = jnp.concatenate([jnp.tile(one, (1, 64)), cos32, jnp.tile(one, (1, 32))], -1)
    sinm = jnp.concatenate([jnp.tile(zero, (1, 64)), sin32, jnp.tile(zero, (1, 32))], -1)
    coskr = jnp.concatenate([cos32, jnp.tile(one, (1, 96))], -1)
    sinkr = jnp.concatenate([sin32, jnp.tile(zero, (1, 96))], -1)
    return cos64, sin64, cosm, sinm, coskr, sinkr


_QA_PERM = np.concatenate([np.arange(0, 64), np.arange(128, 192), np.arange(64, 128), np.arange(192, 256)])


def _layer_weights(p, l):
    w_in = p["w_in"][l]
    cols = np.concatenate([_QA_PERM, np.arange(256, 2304), np.arange(2320, 2736), np.arange(2304, 2320)])
    w_in = jnp.pad(w_in[:, cols], ((0, 0), (0, IN_PAD_COLS - cols.size))).astype(BF16)
    w_out = p["w_out"][l]
    w_out = jnp.concatenate([w_out[_QA_PERM], w_out[256:]], axis=0).astype(BF16)
    wq = p["mla_wq_up"][l].reshape(MLA_Q_LORA, MLA_HEADS, MLA_NOPE + MLA_ROPE)
    wq = jnp.pad(wq, ((0, 0), (0, 0), (0, LANES - MLA_NOPE - MLA_ROPE))).reshape(MLA_Q_LORA, 4 * LANES)
    wkv = p["mla_wkv_up"][l].reshape(MLA_KV_LORA, MLA_HEADS, MLA_NOPE + MLA_V)
    wk_top = jnp.pad(wkv[:, :, :MLA_NOPE], ((0, 0), (0, 0), (0, LANES - MLA_NOPE)))
    place = jnp.pad(jnp.eye(MLA_ROPE, dtype=F32), ((0, LANES - MLA_ROPE), (MLA_NOPE, LANES - MLA_NOPE - MLA_ROPE)))
    wk_bot = jnp.broadcast_to(place[:, None, :], (LANES, MLA_HEADS, LANES))
    wk = jnp.concatenate([wk_top, wk_bot], axis=0).reshape(2 * LANES, 4 * LANES)
    wv = wkv[:, :, MLA_NOPE:].reshape(MLA_KV_LORA, MLA_HEADS * MLA_V)
    gate_row = lambda v: jnp.pad(v.reshape(1, 2 * DN_HEADS), ((0, 0), (MLA_ROPE, LANES - MLA_ROPE - 2 * DN_HEADS)))
    return {
        "norm1_g": p["norm1_g"][l][None], "norm2_g": p["norm2_g"][l][None],
        "w_in": w_in, "w_out": w_out,
        "qn_g": jnp.tile(p["gqa_qn_g"][l], 4)[None], "kn_g": jnp.tile(p["gqa_kn_g"][l], 2)[None],
        "mla_qn_g": p["mla_qn_g"][l][None], "mla_kvn_g": p["mla_kvn_g"][l][None],
        "wq": wq.astype(BF16), "wk": wk.astype(BF16), "wv": wv.astype(BF16),
        "dn_conv_w": p["dn_conv_w"][l], "dn_alog_row": gate_row(p["dn_a_log"][l]),
        "dn_dtb_row": gate_row(p["dn_dt_bias"][l]), "dn_out_g": jnp.tile(p["dn_out_g"][l], DN_HEADS)[None],
        "w_gate": p["ffn_w_gate"][l].astype(BF16), "w_up": p["ffn_w_up"][l].astype(BF16),
        "w_down": p["ffn_w_down"][l].astype(BF16),
    }


def _state_to_wide(s):
    b = s.shape[0]
    return s.transpose(0, 1, 3, 2, 4).reshape(b, 2, DN_DK, DN_HEADS * DN_DV)


def _state_from_wide(s):
    b = s.shape[0]
    return s.reshape(b, 2, DN_DK, DN_HEADS, DN_DV).transpose(0, 1, 3, 2, 4)


def kernel(x_prompt, x_sample, cache_gqa_k, cache_gqa_v, cache_na_k, cache_na_v, state_dn,
           cache_mla_ckv, cache_mla_krope, c, c_ctx, norm1_g, norm2_g, w_mod, b_mod, w_in, w_out,
           gqa_qn_g, gqa_kn_g, na_bias, dn_conv_w, dn_a_log, dn_dt_bias, dn_out_g, mla_qn_g,
           mla_wq_up, mla_kvn_g, mla_wkv_up, ffn_w_gate, ffn_w_up, ffn_w_down, final_g):
    p = {"norm1_g": norm1_g, "norm2_g": norm2_g, "w_in": w_in, "w_out": w_out, "gqa_qn_g": gqa_qn_g,
         "gqa_kn_g": gqa_kn_g, "dn_conv_w": dn_conv_w, "dn_a_log": dn_a_log, "dn_dt_bias": dn_dt_bias,
         "dn_out_g": dn_out_g, "mla_qn_g": mla_qn_g, "mla_wq_up": mla_wq_up, "mla_kvn_g": mla_kvn_g,
         "mla_wkv_up": mla_wkv_up, "ffn_w_gate": ffn_w_gate, "ffn_w_up": ffn_w_up, "ffn_w_down": ffn_w_down}
    nb_ctx, seq_ctx, _ = x_prompt.shape
    nb_lat, seq_lat, _ = x_sample.shape
    past = cache_gqa_k.shape[2]
    fg = final_g[None]

    cond = jnp.concatenate([c_ctx[None], c, jnp.zeros((16 - 1 - nb_lat, D_MODEL), F32)], axis=0)
    mods = _modulation(cond, w_mod, b_mod).reshape(DEPTH, 16, MOD_CHUNKS, D_MODEL)
    weights = [_layer_weights(p, l) for l in range(DEPTH)]

    x = x_prompt
    ctx_out = []
    for l in range(DEPTH):
        lw = weights[l]
        m = mods[l, 0:1]
        qa, ka, va, qb, kb, vb, zc, qd, ckv, zm, krr = _inproj(x, m, False, lw, None)
        o_a = _attn_pair(qa, ka, va, (0, 0), name="gqa_ctx")
        o_b = _attn_pair(qb, kb, vb, (0, 1), name="na_ctx")
        o_c, s_dn = _deltanet(zc, zm, lw)
        o_d = _mla(qd, ckv, krr, lw["wk"], lw["wv"], name="mla_ctx")
        x = _outffn((o_a, o_b, o_c, o_d), x, m, False, lw, fg, l == DEPTH - 1)
        ctx_out.append((ka.reshape(nb_ctx, seq_ctx, GQA_KV_HEADS, HEAD_DIM),
                        va.reshape(nb_ctx, seq_ctx, GQA_KV_HEADS, HEAD_DIM),
                        kb.reshape(nb_ctx, seq_ctx, NA_HEADS, HEAD_DIM),
                        vb.reshape(nb_ctx, seq_ctx, NA_HEADS, HEAD_DIM),
                        _state_from_wide(s_dn), ckv, zm[:, :, :MLA_ROPE]))
    y_prompt = x
    new = [jnp.stack([s[i] for s in ctx_out], axis=1) for i in range(7)]

    ropes = _rope_tables(seq_lat)
    ck_a = cache_gqa_k.reshape(nb_lat, DEPTH, past, GQA_KV_HEADS * HEAD_DIM)
    cv_a = cache_gqa_v.reshape(nb_lat, DEPTH, past, GQA_KV_HEADS * HEAD_DIM)
    ck_b = cache_na_k.reshape(nb_lat, DEPTH, past, NA_HEADS * HEAD_DIM)
    cv_b = cache_na_v.reshape(nb_lat, DEPTH, past, NA_HEADS * HEAD_DIM)
    c_kr = jnp.pad(cache_mla_krope, ((0, 0), (0, 0), (0, 0), (0, LANES - MLA_ROPE)))
    x = x_sample
    for l in range(DEPTH):
        lw = weights[l]
        m = mods[l, 1:1 + nb_lat]
        qa, ka, va, qb, kb, vb, zc, qd, ckv, zm, krr = _inproj(x, m, True, lw, ropes)
        o_a = _attn_pair(qa, ka, va, (0, 0), cache=(ck_a, cv_a, l), name="gqa_lat")
        o_b = _na_latent(qb, kb, vb, ck_b, cv_b, l, _na_bias_blocks(na_bias[l]))
        o_c, _ = _deltanet(zc, zm, lw, state=_state_to_wide(state_dn[:, l]))
        o_d = _mla(qd, ckv, krr, lw["wk"], lw["wv"], cache=(cache_mla_ckv, c_kr, l), name="mla_lat")
        x = _outffn((o_a, o_b, o_c, o_d), x, m, True, lw, fg, l == DEPTH - 1)
    y_sample = x

    return (y_prompt, y_sample, *new)
```

```python
import functools

import numpy as np
import jax
import jax.numpy as jnp
from jax import lax
from jax.experimental import pallas as pl
from jax.experimental.pallas import tpu as pltpu

F32 = jnp.float32
BF16 = jnp.bfloat16

D_MODEL = 1024
DEPTH = 2
GRID_W = 64
HEAD_DIM = 64
ROPE_BASE = 10000.0
NEG_INF = -1e30
MOD_CHUNKS = 6
GQA_HEADS, GQA_KV_HEADS = 4, 2
NA_HEADS, NA_KH, NA_KW = 4, 8, 16
DN_HEADS, DN_DK, DN_DV, DN_CONV, DN_CHUNK = 4, 64, 64, 4, 64
DN_QKV = DN_HEADS * (2 * DN_DK + DN_DV)
MLA_HEADS, MLA_Q_LORA, MLA_KV_LORA, MLA_NOPE, MLA_ROPE, MLA_V = 4, 256, 128, 64, 32, 64
MLA_SCALE = (MLA_NOPE + MLA_ROPE) ** -0.5
D_FF = -(-8 * D_MODEL // (3 * 256)) * 256
EPS = 1e-6
LOG2E = 1.4426950408889634

LANES = 128
ROW_TILE = 256
Q_TILE = 256
NA_BAND = 768
TILES_PER_STEP = 2
ATTN_Q_TILE = 512
HEAD_LOOKAHEAD = 1
PREP_CHUNKS = 4
IN_PAD_COLS = 2816
VMEM_LIMIT = 56 * 1024 * 1024

_NT = (((1,), (1,)), ((), ()))
_TN = (((0,), (0,)), ((), ()))


def _cparams(n_axes):
    return pltpu.CompilerParams(dimension_semantics=("arbitrary",) * n_axes,
                                vmem_limit_bytes=VMEM_LIMIT)


def _lane(shape):
    return lax.broadcasted_iota(jnp.int32, shape, len(shape) - 1)


def _silu(x):
    return x / (1.0 + jnp.exp(-x))


def _rms_full(x, g):
    return x * lax.rsqrt(jnp.mean(x * x, axis=-1, keepdims=True) + EPS) * g


def _seg64_sum(x):
    lo = _lane(x.shape) < HEAD_DIM
    s_lo = jnp.sum(jnp.where(lo, x, 0.0), axis=-1, keepdims=True)
    s_hi = jnp.sum(jnp.where(lo, 0.0, x), axis=-1, keepdims=True)
    return jnp.where(lo, s_lo, s_hi)


def _head_rms(x, g):
    parts = []
    for p in range(x.shape[-1] // LANES):
        xp = x[:, p * LANES:(p + 1) * LANES]
        ms = _seg64_sum(xp * xp) * (1.0 / HEAD_DIM)
        parts.append(xp * lax.rsqrt(ms + EPS))
    y = parts[0] if len(parts) == 1 else jnp.concatenate(parts, axis=-1)
    return y * g


def _head_l2(x):
    parts = []
    for p in range(x.shape[-1] // LANES):
        xp = x[:, p * LANES:(p + 1) * LANES]
        parts.append(xp * lax.rsqrt(_seg64_sum(xp * xp) + EPS))
    return parts[0] if len(parts) == 1 else jnp.concatenate(parts, axis=-1)


def _rope(x, cos, sin, half):
    first = (_lane(x.shape) & (2 * half - 1)) < half
    rot = jnp.where(first, pltpu.roll(x, LANES - half, 1), pltpu.roll(x, half, 1))
    return x * cos + rot * sin


def _softmax_parts(scores):
    m = jnp.max(scores[0], axis=-1, keepdims=True)
    for s in scores[1:]:
        m = jnp.maximum(m, jnp.max(s, axis=-1, keepdims=True))
    es = [jnp.exp2(s - m) for s in scores]
    l = jnp.sum(es[0], axis=-1, keepdims=True)
    for e in es[1:]:
        l = l + jnp.sum(e, axis=-1, keepdims=True)
    return es, 1.0 / l


def _bdot(a, b):
    return jnp.dot(a, b, preferred_element_type=F32)


def _pipelined_heads(n_heads, scores, attend):
    outs = []
    queue = [scores(hd) for hd in range(min(HEAD_LOOKAHEAD, n_heads))]
    for hd in range(n_heads):
        if hd + HEAD_LOOKAHEAD < n_heads:
            queue.append(scores(hd + HEAD_LOOKAHEAD))
        es, rl = _softmax_parts(queue.pop(0))
        outs.append(attend(hd, es) * rl)
    return outs


def _mod_kernel(c_ref, w_ref, b_ref, o_ref):
    s = _silu(c_ref[...]).astype(BF16)
    o_ref[...] = _bdot(s, w_ref[...].astype(BF16)) + b_ref[...]


def _modulation(cond, w_mod, b_mod):
    n = MOD_CHUNKS * D_MODEL
    tn = 1536
    return pl.pallas_call(
        _mod_kernel,
        grid=(DEPTH, n // tn),
        in_specs=[pl.BlockSpec((16, D_MODEL), lambda l, j: (0, 0)),
                  pl.BlockSpec((None, D_MODEL, tn), lambda l, j: (l, 0, j)),
                  pl.BlockSpec((None, 1, tn), lambda l, j: (l, 0, j))],
        out_specs=pl.BlockSpec((None, 16, tn), lambda l, j: (l, 0, j)),
        out_shape=jax.ShapeDtypeStruct((DEPTH, 16, n), F32),
        compiler_params=_cparams(2),
        name="modulation",
    )(cond, w_mod, b_mod.reshape(DEPTH, 1, n))


_IN_OUT_WIDTHS = (256, 128, 128, 256, 256, 256, 1024, 512, 128, 128, 128)


def _inproj_kernel(positioned, *refs):
    (x_ref, mod_ref, g1_ref, w_ref, qng_ref, kng_ref, mqg_ref, wq_ref, mkg_ref) = refs[:9]
    n_in = 9
    if positioned:
        cos64_ref, sin64_ref, cosm_ref, sinm_ref, coskr_ref, sinkr_ref = refs[9:15]
        n_in = 15
    (qa_ref, ka_ref, va_ref, qb_ref, kb_ref, vb_ref, zc_ref, qd_ref, ckv_ref, zm_ref,
     krr_ref) = refs[n_in:]
    tiles = range(TILES_PER_STEP)

    m = mod_ref[...]
    hb = [(_rms_full(x_ref[t], g1_ref[...]) * (1.0 + m[1:2]) + m[0:1]).astype(BF16) for t in tiles]
    za = [_bdot(hb[t], w_ref[:, 0:512]) for t in tiles]
    zb = [_bdot(hb[t], w_ref[:, 512:1280]) for t in tiles]

    for t in tiles:
        q = _head_rms(za[t][:, 0:256], qng_ref[...])
        k = _head_rms(za[t][:, 256:384], kng_ref[...])
        if positioned:
            cos, sin = cos64_ref[t], sin64_ref[t]
            q = jnp.concatenate([_rope(q[:, 0:128], cos, sin, 16), _rope(q[:, 128:256], cos, sin, 16)],
                                axis=-1)
            k = _rope(k, cos, sin, 16)
        qa_ref[t] = q * (HEAD_DIM ** -0.5 * LOG2E)
        ka_ref[t] = k
        va_ref[t] = za[t][:, 384:512]

    zd = [_bdot(hb[t], w_ref[:, 2304:2816]) for t in tiles]
    zc = [_bdot(hb[t], w_ref[:, 1280:2304]) for t in tiles]

    for t in tiles:
        qb_ref[t] = zb[t][:, 0:256] * (HEAD_DIM ** -0.5 * LOG2E)
        kb_ref[t] = zb[t][:, 256:512]
        vb_ref[t] = zb[t][:, 512:768]

    qm = [_bdot(_rms_full(zd[t][:, 0:256], mqg_ref[...]).astype(BF16), wq_ref[...]) for t in tiles]
    for t in tiles:
        zc_ref[t] = zc[t]
        ckv_ref[t] = _rms_full(zd[t][:, 256:384], mkg_ref[...])
        zmisc = zd[t][:, 384:512]
        zm_ref[t] = zmisc
        kr = jnp.where(_lane(zmisc.shape) < MLA_ROPE, zmisc, 0.0)
        q = qm[t]
        if positioned:
            kr = _rope(kr, coskr_ref[t], sinkr_ref[t], 8)
            cm, sm = cosm_ref[t], sinm_ref[t]
            q = jnp.concatenate([_rope(q[:, i * LANES:(i + 1) * LANES], cm, sm, 8)
                                 for i in range(MLA_HEADS)], axis=-1)
        krr_ref[t] = kr
        qd_ref[t] = q * (MLA_SCALE * LOG2E)


def _inproj(x, mods, per_batch_mods, lw, ropes):
    b, l, _ = x.shape
    tm, ts = ROW_TILE, TILES_PER_STEP
    n_tiles = b * l // tm
    tiles_per_seq = l // tm
    positioned = ropes is not None
    row = lambda w: pl.BlockSpec((ts, tm, w), lambda i: (i, 0, 0))
    const = lambda shape: pl.BlockSpec(shape, lambda i: (0,) * len(shape))
    assert not per_batch_mods or tiles_per_seq % ts == 0
    mod_spec = pl.BlockSpec((None, MOD_CHUNKS, D_MODEL),
                            (lambda i: (i * ts // tiles_per_seq, 0, 0)) if per_batch_mods else (lambda i: (0, 0, 0)))
    in_specs = [row(D_MODEL), mod_spec, const((1, D_MODEL)), const((D_MODEL, IN_PAD_COLS)),
                const((1, 256)), const((1, 128)), const((1, MLA_Q_LORA)),
                const((MLA_Q_LORA, 4 * LANES)), const((1, MLA_KV_LORA))]
    args = [x.reshape(n_tiles, tm, D_MODEL), mods, lw["norm1_g"], lw["w_in"], lw["qn_g"], lw["kn_g"],
            lw["mla_qn_g"], lw["wq"], lw["mla_kvn_g"]]
    if positioned:
        steps_per_seq = tiles_per_seq // ts
        in_specs += [pl.BlockSpec((ts, tm, LANES), lambda i: (i % steps_per_seq, 0, 0))] * 6
        args += [r.reshape(tiles_per_seq, tm, LANES) for r in ropes]
    outs = pl.pallas_call(
        functools.partial(_inproj_kernel, positioned),
        grid=(n_tiles // ts,),
        in_specs=in_specs,
        out_specs=[row(w) for w in _IN_OUT_WIDTHS],
        out_shape=[jax.ShapeDtypeStruct((n_tiles, tm, w), F32) for w in _IN_OUT_WIDTHS],
        compiler_params=_cparams(1),
        name="inproj_lat" if positioned else "inproj_ctx",
    )(*args)
    return [o.reshape(b, l, w) for o, w in zip(outs, _IN_OUT_WIDTHS)]


def _attn_pair_kernel(has_cache, qmap, *refs):
    if has_cache:
        q_ref, kc_ref, vc_ref, k_ref, v_ref, o_ref, kbuf, vbuf = refs
    else:
        q_ref, k_ref, v_ref, o_ref, kbuf, vbuf = refs

    @pl.when(pl.program_id(1) == 0)
    def _():
        off = 0
        if has_cache:
            off = kc_ref.shape[0]
            kbuf[0:off, :] = kc_ref[...].astype(BF16)
            vbuf[0:off, :] = vc_ref[...].astype(BF16)
        kbuf[off:, :] = k_ref[...].astype(BF16)
        vbuf[off:, :] = v_ref[...].astype(BF16)

    tq = q_ref.shape[0]
    lo = _lane((tq, LANES)) < HEAD_DIM

    def scores(hd):
        p, half = divmod(hd, 2)
        qp = q_ref[:, p * LANES:(p + 1) * LANES]
        qm = jnp.where(lo if half == 0 else jnp.logical_not(lo), qp, 0.0).astype(BF16)
        return [lax.dot_general(qm, kbuf[:, qmap[p] * LANES:(qmap[p] + 1) * LANES], _NT,
                                preferred_element_type=F32)]

    def attend(hd, es):
        kv = qmap[hd // 2]
        return _bdot(es[0].astype(BF16), vbuf[:, kv * LANES:(kv + 1) * LANES])

    outs = _pipelined_heads(2 * len(qmap), scores, attend)
    for p in range(len(qmap)):
        o_ref[:, p * LANES:(p + 1) * LANES] = jnp.where(lo, outs[2 * p], outs[2 * p + 1])


def _attn_pair(q, k, v, qmap, cache=None, name="attn_pair"):
    b, lq, wq = q.shape
    ls, wk = k.shape[1], k.shape[2]
    tq = min(ATTN_Q_TILE, lq)
    lc = 0 if cache is None else cache[0].shape[2]
    in_specs = [pl.BlockSpec((None, tq, wq), lambda bi, i: (bi, i, 0))]
    args = [q]
    if cache is not None:
        kc, vc, layer = cache
        cspec = pl.BlockSpec((None, None, lc, wk), lambda bi, i: (bi, layer, 0, 0))
        in_specs += [cspec, cspec]
        args += [kc, vc]
    sspec = pl.BlockSpec((None, ls, wk), lambda bi, i: (bi, 0, 0))
    in_specs += [sspec, sspec]
    args += [k, v]
    return pl.pallas_call(
        functools.partial(_attn_pair_kernel, cache is not None, qmap),
        grid=(b, lq // tq),
        in_specs=in_specs,
        out_specs=pl.BlockSpec((None, tq, wq), lambda bi, i: (bi, i, 0)),
        out_shape=jax.ShapeDtypeStruct((b, lq, wq), F32),
        scratch_shapes=[pltpu.VMEM((lc + ls, wk), BF16), pltpu.VMEM((lc + ls, wk), BF16)],
        compiler_params=_cparams(2),
        name=name,
    )(*args)


def _mla_kernel(has_cache, *refs):
    if has_cache:
        q_ref, ckvc_ref, krc_ref, ckv_ref, kr_ref, wk_ref, wv_ref, o_ref, kbuf, vbuf = refs
    else:
        q_ref, ckv_ref, kr_ref, wk_ref, wv_ref, o_ref, kbuf, vbuf = refs

    @pl.when(pl.program_id(1) == 0)
    def _():
        def expand(c_ref, r_ref, r0, r1):
            c = c_ref[...].astype(BF16)
            ckr = jnp.concatenate([c, r_ref[...].astype(BF16)], axis=-1)
            kbuf[r0:r1, :] = _bdot(ckr, wk_ref[...]).astype(BF16)
            vbuf[r0:r1, :] = _bdot(c, wv_ref[...]).astype(BF16)
        off = 0
        if has_cache:
            off = ckvc_ref.shape[0]
            expand(ckvc_ref, krc_ref, 0, off)
        expand(ckv_ref, kr_ref, off, kbuf.shape[0])

    tq = q_ref.shape[0]
    lo = _lane((tq, LANES)) < MLA_V

    def scores(hd):
        qh = q_ref[:, hd * LANES:(hd + 1) * LANES].astype(BF16)
        return [lax.dot_general(qh, kbuf[:, hd * LANES:(hd + 1) * LANES], _NT, preferred_element_type=F32)]

    def attend(hd, es):
        p = hd // 2
        return _bdot(es[0].astype(BF16), vbuf[:, p * LANES:(p + 1) * LANES])

    outs = _pipelined_heads(MLA_HEADS, scores, attend)
    for p in range(MLA_HEADS // 2):
        o_ref[:, p * LANES:(p + 1) * LANES] = jnp.where(lo, outs[2 * p], outs[2 * p + 1])


def _mla(q, ckv, kr, wk, wv, cache=None, name="mla"):
    b, lq, wq = q.shape
    ls = ckv.shape[1]
    tq = min(ATTN_Q_TILE, lq)
    lc = 0 if cache is None else cache[0].shape[2]
    in_specs = [pl.BlockSpec((None, tq, wq), lambda bi, i: (bi, i, 0))]
    args = [q]
    if cache is not None:
        ckvc, krc, layer = cache
        cspec = pl.BlockSpec((None, None, lc, LANES), lambda bi, i: (bi, layer, 0, 0))
        in_specs += [cspec, cspec]
        args += [ckvc, krc]
    sspec = pl.BlockSpec((None, ls, LANES), lambda bi, i: (bi, 0, 0))
    in_specs += [sspec, sspec,
                 pl.BlockSpec((2 * LANES, 4 * LANES), lambda bi, i: (0, 0)),
                 pl.BlockSpec((LANES, 2 * LANES), lambda bi, i: (0, 0))]
    args += [ckv, kr, wk, wv]
    return pl.pallas_call(
        functools.partial(_mla_kernel, cache is not None),
        grid=(b, lq // tq),
        in_specs=in_specs,
        out_specs=pl.BlockSpec((None, tq, 2 * LANES), lambda bi, i: (bi, i, 0)),
        out_shape=jax.ShapeDtypeStruct((b, lq, 2 * LANES), F32),
        scratch_shapes=[pltpu.VMEM((lc + ls, 4 * LANES), BF16), pltpu.VMEM((lc + ls, 2 * LANES), BF16)],
        compiler_params=_cparams(2),
        name=name,
    )(*args)


def _na_kernel(q_ref, k_ref, v_ref, kc_ref, vc_ref, bias_ref, o_ref):
    j = pl.program_id(0)
    start = pl.multiple_of((j >> 1) * 256, 256)
    kband = k_ref[pl.ds(start, NA_BAND), :].astype(BF16)
    vband = v_ref[pl.ds(start, NA_BAND), :].astype(BF16)
    kc = kc_ref[...].astype(BF16)
    vc = vc_ref[...].astype(BF16)
    tq = q_ref.shape[0]
    lo = _lane((tq, LANES)) < HEAD_DIM

    def scores(hd):
        p, half = divmod(hd, 2)
        sl = slice(p * LANES, (p + 1) * LANES)
        qm = jnp.where(lo if half == 0 else jnp.logical_not(lo), q_ref[:, sl], 0.0).astype(BF16)
        s_loc = lax.dot_general(qm, kband[:, sl], _NT, preferred_element_type=F32) + bias_ref[hd]
        return [s_loc, lax.dot_general(qm, kc[:, sl], _NT, preferred_element_type=F32)]

    def attend(hd, es):
        sl = slice((hd // 2) * LANES, (hd // 2 + 1) * LANES)
        return _bdot(es[0].astype(BF16), vband[:, sl]) + _bdot(es[1].astype(BF16), vc[:, sl])

    outs = _pipelined_heads(NA_HEADS, scores, attend)
    for p in range(NA_HEADS // 2):
        o_ref[:, p * LANES:(p + 1) * LANES] = jnp.where(lo, outs[2 * p], outs[2 * p + 1])


def _na_latent(q, k, v, kc, vc, layer, bias_blocks):
    b, n, w = q.shape
    lc = kc.shape[2]
    nq = n // Q_TILE
    full = pl.BlockSpec((None, n, w), lambda j, bi: (bi, 0, 0))
    cspec = pl.BlockSpec((None, None, lc, w), lambda j, bi: (bi, layer, 0, 0))
    return pl.pallas_call(
        _na_kernel,
        grid=(nq, b),
        in_specs=[pl.BlockSpec((None, Q_TILE, w), lambda j, bi: (bi, j, 0)), full, full, cspec, cspec,
                  pl.BlockSpec((None, NA_HEADS, Q_TILE, NA_BAND), lambda j, bi: (j, 0, 0, 0))],
        out_specs=pl.BlockSpec((None, Q_TILE, w), lambda j, bi: (bi, j, 0)),
        out_shape=jax.ShapeDtypeStruct((b, n, w), F32),
        compiler_params=_cparams(2),
        name="na_latent",
    )(q, k, v, kc, vc, bias_blocks)


NA_GRID_ROWS = 16
NA_BAND_ROW0 = (0, 0, 4, 4)


def _na_bias_kernel(b_ref, o_ref, tp_s):
    hd = pl.program_id(0)
    n_dr, n_dc = 2 * NA_KH - 1, 2 * NA_KW - 1
    shape = (GRID_W, LANES)
    c = lax.broadcasted_iota(jnp.int32, shape, 0)
    lane = _lane(shape)
    kc = lane & (GRID_W - 1)
    lo = lane < GRID_W
    diff = kc - c + (NA_KW - 1)
    c0 = jnp.clip(c - NA_KW // 2, 0, GRID_W - NA_KW)
    col_ok = (kc >= c0) & (kc < c0 + NA_KW)
    neg = jnp.full(shape, NEG_INF, F32)
    for dr0 in range(-1, n_dr):
        acc = neg
        for d in range(n_dc):
            v_lo = b_ref[hd * n_dr + dr0, d] if dr0 >= 0 else 0.0
            v_hi = b_ref[hd * n_dr + dr0 + 1, d] if dr0 + 1 < n_dr else 0.0
            acc = jnp.where(diff == d, jnp.where(lo, v_lo, v_hi), acc)
        tp_s[dr0 + 1] = jnp.where(col_ok, acc * LOG2E, NEG_INF)
    for j in range(NA_GRID_ROWS // 4):
        for ri in range(4):
            r = 4 * j + ri
            r0 = min(max(r - NA_KH // 2, 0), NA_GRID_ROWS - NA_KH)
            for kp in range(NA_BAND // LANES):
                kr = NA_BAND_ROW0[j] + 2 * kp
                ok_lo, ok_hi = r0 <= kr < r0 + NA_KH, r0 <= kr + 1 < r0 + NA_KH
                dr0 = kr - r + (NA_KH - 1)
                if ok_lo and ok_hi:
                    t = tp_s[dr0 + 1]
                elif ok_lo:
                    t = jnp.where(lo, tp_s[dr0 + 1], NEG_INF)
                elif ok_hi:
                    t = jnp.where(lo, NEG_INF, tp_s[dr0 + 1])
                else:
                    t = neg
                o_ref[j, ri * GRID_W:(ri + 1) * GRID_W, kp * LANES:(kp + 1) * LANES] = t


def _na_bias_blocks(bias):
    nq = NA_GRID_ROWS // 4
    return pl.pallas_call(
        _na_bias_kernel,
        grid=(NA_HEADS,),
        in_specs=[pl.BlockSpec(memory_space=pltpu.SMEM)],
        out_specs=pl.BlockSpec((nq, None, Q_TILE, NA_BAND), lambda h: (0, h, 0, 0)),
        out_shape=jax.ShapeDtypeStruct((nq, NA_HEADS, Q_TILE, NA_BAND), F32),
        scratch_shapes=[pltpu.VMEM((2 * NA_KH, GRID_W, LANES), F32)],
        compiler_params=_cparams(1),
        name="na_bias",
    )(bias.reshape(NA_HEADS * (2 * NA_KH - 1), 2 * NA_KW - 1))


def _widen(cols, n):
    blk = _lane((n, DN_HEADS * DN_DV)) >> 6
    return jnp.where(blk == 0, cols[0], jnp.where(blk == 1, cols[1], jnp.where(blk == 2, cols[2], cols[3])))


def _deltanet_kernel(seq, has_state, *refs):
    if has_state:
        (zc_ref, zm_ref, s0_ref, cw_ref, alog_ref, dtb_ref, og_ref, o_ref,
         q_s, k_s, v_s, b_s, g_s, o_s, c_s, mp_s) = refs
    else:
        (zc_ref, zm_ref, cw_ref, alog_ref, dtb_ref, og_ref, o_ref, sfin_ref,
         q_s, k_s, v_s, b_s, g_s, o_s, c_s, mp_s) = refs
    n_chunks = seq // DN_CHUNK
    wide = DN_HEADS * DN_DV

    row = lax.broadcasted_iota(jnp.int32, (seq, wide), 0)
    for part, dst in enumerate((q_s, k_s, v_s)):
        cs = slice(part * wide, (part + 1) * wide)
        x = zc_ref[:, cs]
        w = cw_ref[:, cs]
        y = (w[0:1] * jnp.where(row >= 1, pltpu.roll(x, 1, 0), 0.0) + w[1:2] * x
             + w[2:3] * jnp.where(row < seq - 1, pltpu.roll(x, seq - 1, 0), 0.0)
             + w[3:4] * jnp.where(row < seq - 2, pltpu.roll(x, seq - 2, 0), 0.0))
        y = _silu(y)
        if part == 0:
            y = _head_l2(y) * (DN_DK ** -0.5)
        elif part == 1:
            y = _head_l2(y)
        dst[...] = y

    zm = zm_ref[...]
    xa = zm + dtb_ref[...]
    logd = -jnp.exp(alog_ref[...]) * (jnp.maximum(xa, 0.0) + jnp.log1p(jnp.exp(-jnp.abs(xa))))
    beta = 1.0 / (1.0 + jnp.exp(-zm))
    a_off, b_off = MLA_ROPE, MLA_ROPE + 2 * DN_HEADS
    ri = lax.broadcasted_iota(jnp.int32, (256, 256), 0)
    ci = lax.broadcasted_iota(jnp.int32, (256, 256), 1)
    same = (ri >> 6) == (ci >> 6)
    for d in range(2):
        b_s[d] = _widen([beta[:, b_off + 4 * d + hd:b_off + 4 * d + hd + 1] for hd in range(DN_HEADS)], seq)
        lw = _widen([logd[:, a_off + 4 * d + hd:a_off + 4 * d + hd + 1] for hd in range(DN_HEADS)], seq)
        tri = jnp.where(same & ((ci <= ri) if d == 0 else (ci >= ri)), 1.0, 0.0).astype(F32)
        for rb in range(seq // 256):
            rs = slice(rb * 256, (rb + 1) * 256)
            g_s[d, rs, :] = jnp.dot(tri, lw[rs], preferred_element_type=F32,
                                    precision=lax.Precision.HIGHEST)

    ii = lax.broadcasted_iota(jnp.int32, (DN_CHUNK, wide), 0)
    jj = _lane((DN_CHUNK, wide)) & (DN_CHUNK - 1)
    blk = _lane((DN_CHUNK, wide)) >> 6
    diag = ii == jj
    eye = jnp.where(diag, 1.0, 0.0)
    head_mask = [jnp.where(blk == hd, 1.0, 0.0).astype(BF16) for hd in range(DN_HEADS)]

    def bd(z):
        zb = z.astype(BF16)
        return jnp.concatenate([zb * hm for hm in head_mask], axis=0)

    def fold(gram):
        out = jnp.where(blk == 0, gram[0:DN_CHUNK], 0.0)
        for hd in range(1, DN_HEADS):
            out = out + jnp.where(blk == hd, gram[hd * DN_CHUNK:(hd + 1) * DN_CHUNK], 0.0)
        return out

    tri_masks = []
    for d in range(2):
        incl = (jj <= ii) if d == 0 else (jj >= ii)
        strict = (jj < ii) if d == 0 else (jj > ii)
        pair = [((ii >> (lvl + 1)) == (jj >> (lvl + 1)))
                & (((ii >> lvl) & 1) == (1 - d)) & (((jj >> lvl) & 1) == d) for lvl in range(6)]
        tri_masks.append((incl, strict, pair))

    def prepare(step, carry):
        chunks = [step * PREP_CHUNKS + i for i in range(PREP_CHUNKS)]
        rows = [pl.ds(pl.multiple_of(c * DN_CHUNK, DN_CHUNK), DN_CHUNK) for c in chunks]
        qkv = [(q_s[r, :], k_s[r, :], v_s[r, :]) for r in rows]
        inst = [(ci, d) for ci in range(PREP_CHUNKS) for d in range(2)]
        beta = {(ci, d): b_s[d, rows[ci], :] for ci, d in inst}
        kb = {(ci, d): qkv[ci][1] * beta[ci, d] for ci, d in inst}
        r = [lax.dot_general(jnp.concatenate([kb[ci, 0], kb[ci, 1], qkv[ci][0]], axis=0).astype(BF16),
                             bd(qkv[ci][1]), _NT, preferred_element_type=F32)
             for ci in range(PREP_CHUNKS)]
        g, a, qk, eg, t = {}, {}, {}, {}, {}
        for ci, d in inst:
            incl, strict, pair = tri_masks[d]
            g[ci, d] = g_s[d, rows[ci], :]
            g_row = jnp.sum(jnp.where(diag, g[ci, d], 0.0), axis=0, keepdims=True)
            dm = jnp.where(incl, jnp.exp(jnp.where(incl, g[ci, d] - g_row, 0.0)), 0.0)
            a[ci, d] = jnp.where(strict, r[ci][d * DN_CHUNK:(d + 1) * DN_CHUNK] * dm, 0.0)
            qk[ci, d] = (r[ci][2 * DN_CHUNK:] * dm).astype(BF16)
            eg[ci, d] = jnp.exp(g[ci, d])
            t[ci, d] = eye - jnp.where(pair[0], a[ci, d], 0.0)
        for lvl in range(1, 6):
            te = {i: _bdot(t[i].astype(BF16), bd(jnp.where(tri_masks[i[1]][2][lvl], a[i], 0.0))) for i in inst}
            t = {i: t[i] - _bdot(te[i].astype(BF16), bd(t[i])) for i in inst}
        nb = {i: jnp.where(diag, 0.0, t[i]).astype(BF16) for i in inst}
        rhs_u = {(ci, d): qkv[ci][2] * beta[ci, d] for ci, d in inst}
        rhs_w = {i: kb[i] * eg[i] for i in inst}
        u = {i: rhs_u[i] + _bdot(nb[i], bd(rhs_u[i])) for i in inst}
        w = {i: rhs_w[i] + _bdot(nb[i], bd(rhs_w[i])) for i in inst}
        kd = {}
        for ci, d in inst:
            g_last = g[ci, d][DN_CHUNK - 1:DN_CHUNK] if d == 0 else g[ci, d][0:1]
            kd[ci, d] = (qkv[ci][1] * jnp.exp(g_last - g[ci, d])).astype(BF16)
        p = {(ci, d): qkv[ci][0] * eg[ci, d] - _bdot(qk[ci, d], bd(w[ci, d])) for ci, d in inst}
        o0 = {i: _bdot(qk[i], bd(u[i])) for i in inst}
        m = {i: fold(lax.dot_general(kd[i], w[i].astype(BF16), _TN, preferred_element_type=F32)) for i in inst}
        cc = {i: fold(lax.dot_general(kd[i], u[i].astype(BF16), _TN, preferred_element_type=F32)) for i in inst}
        for ci, d in inst:
            mrow = pl.ds(pl.multiple_of(chunks[ci] * (2 * DN_CHUNK), 2 * DN_CHUNK), 2 * DN_CHUNK)
            c_s[d, rows[ci], :] = cc[ci, d]
            mp_s[d, mrow, :] = jnp.concatenate([m[ci, d], p[ci, d]], axis=0).astype(BF16)
        for ci in range(PREP_CHUNKS):
            o_s[rows[ci], :] = o0[ci, 0] + o0[ci, 1]
        return carry

    lax.fori_loop(0, n_chunks // PREP_CHUNKS, prepare, 0)

    def scan(i, states):
        new = []
        for d in range(2):
            c = i if d == 0 else n_chunks - 1 - i
            r0 = pl.multiple_of(c * DN_CHUNK, DN_CHUNK)
            rows = pl.ds(r0, DN_CHUNK)
            mrow = pl.ds(pl.multiple_of(c * (2 * DN_CHUNK), 2 * DN_CHUNK), 2 * DN_CHUNK)
            edge = pl.ds(pl.multiple_of(r0 + (DN_CHUNK - 8 if d == 0 else 0), 8), 8)
            g_edge = g_s[d, edge, :]
            g_last = g_edge[7:8] if d == 0 else g_edge[0:1]
            res = _bdot(mp_s[d, mrow, :], bd(states[d]))
            o_s[rows, :] = o_s[rows, :] + res[DN_CHUNK:]
            new.append(states[d] * jnp.exp(g_last) - res[0:DN_CHUNK] + c_s[d, rows, :])
        return tuple(new)

    init = tuple(s0_ref[d] if has_state else jnp.zeros((DN_DK, wide), F32) for d in range(2))
    fin = lax.fori_loop(0, n_chunks, scan, init)
    if not has_state:
        sfin_ref[0] = fin[0]
        sfin_ref[1] = fin[1]

    o_ref[...] = _head_rms(o_s[...], og_ref[...]) * _silu(zc_ref[:, 3 * wide:4 * wide])


def _deltanet(zc, zm, lw, state=None):
    b, seq, _ = zc.shape
    wide = DN_HEADS * DN_DV
    has_state = state is not None
    per_b = lambda w: pl.BlockSpec((None, seq, w), lambda bi: (bi, 0, 0))
    const = lambda shape: pl.BlockSpec(shape, lambda bi: (0,) * len(shape))
    st_spec = pl.BlockSpec((None, 2, DN_DK, wide), lambda bi: (bi, 0, 0, 0))
    in_specs = [per_b(4 * wide), per_b(LANES)]
    args = [zc, zm]
    if has_state:
        in_specs.append(st_spec)
        args.append(state)
    in_specs += [const((DN_CONV, DN_QKV)), const((1, LANES)), const((1, LANES)), const((1, wide))]
    args += [lw["dn_conv_w"], lw["dn_alog_row"], lw["dn_dtb_row"], lw["dn_out_g"]]
    out_specs = [per_b(wide)]
    out_shape = [jax.ShapeDtypeStruct((b, seq, wide), F32)]
    if not has_state:
        out_specs.append(st_spec)
        out_shape.append(jax.ShapeDtypeStruct((b, 2, DN_DK, wide), F32))
    res = pl.pallas_call(
        functools.partial(_deltanet_kernel, seq, has_state),
        grid=(b,),
        in_specs=in_specs,
        out_specs=out_specs,
        out_shape=out_shape,
        scratch_shapes=[pltpu.VMEM((seq, wide), F32), pltpu.VMEM((seq, wide), F32),
                        pltpu.VMEM((seq, wide), F32), pltpu.VMEM((2, seq, wide), F32),
                        pltpu.VMEM((2, seq, wide), F32), pltpu.VMEM((seq, wide), F32),
                        pltpu.VMEM((2, seq, wide), F32), pltpu.VMEM((2, 2 * seq, wide), BF16)],
        compiler_params=_cparams(1),
        name="deltanet_lat" if has_state else "deltanet_ctx",
    )(*args)
    return (res[0], None) if has_state else (res[0], res[1])


def _outffn_kernel(final, oa_ref, ob_ref, oc_ref, od_ref, x_ref, mod_ref, g2_ref, wo_ref, wg_ref,
                   wu_ref, wd_ref, fg_ref, y_ref):
    m = mod_ref[...]
    o = jnp.concatenate([oa_ref[...], ob_ref[...], oc_ref[...], od_ref[...]], axis=-1).astype(BF16)
    x1 = x_ref[...] + m[2:3] * _bdot(o, wo_ref[...])
    h = (_rms_full(x1, g2_ref[...]) * (1.0 + m[4:5]) + m[3:4]).astype(BF16)
    act = (_silu(_bdot(h, wg_ref[...])) * _bdot(h, wu_ref[...])).astype(BF16)
    x2 = x1 + m[5:6] * _bdot(act, wd_ref[...])
    y_ref[...] = _rms_full(x2, fg_ref[...]) if final else x2


def _outffn(outs, x, mods, per_batch_mods, lw, final_g, final):
    b, l, _ = x.shape
    tm = ROW_TILE
    row = lambda w: pl.BlockSpec((None, tm, w), lambda bi, i: (bi, i, 0))
    const = lambda shape: pl.BlockSpec(shape, lambda bi, i: (0,) * len(shape),
                                       pipeline_mode=pl.Buffered(1))
    mod_spec = pl.BlockSpec((None, MOD_CHUNKS, D_MODEL),
                            (lambda bi, i: (bi, 0, 0)) if per_batch_mods else (lambda bi, i: (0, 0, 0)))
    return pl.pallas_call(
        functools.partial(_outffn_kernel, final),
        grid=(b, l // tm),
        in_specs=[row(256), row(256), row(256), row(256), row(D_MODEL), mod_spec, const((1, D_MODEL)),
                  const((D_MODEL, D_MODEL)), const((D_MODEL, D_FF)), const((D_MODEL, D_FF)),
                  const((D_FF, D_MODEL)), const((1, D_MODEL))],
        out_specs=row(D_MODEL),
        out_shape=jax.ShapeDtypeStruct((b, l, D_MODEL), F32),
        compiler_params=_cparams(2),
        name="outffn",
    )(*outs, x, mods, lw["norm2_g"], lw["w_out"], lw["w_gate"], lw["w_up"], lw["w_down"], final_g)


def _rope_tables(n):
    t = jnp.arange(n)

    def axis(pos, half):
        inv = ROPE_BASE ** (-jnp.arange(half, dtype=F32) / half)
        ang = pos.astype(F32)[:, None] * inv[None, :]
        c, s = jnp.cos(ang), jnp.sin(ang)
        return jnp.concatenate([c, c], -1), jnp.concatenate([-s, s], -1)

    cr, sr = axis(t // GRID_W, 16)
    cc, sc = axis(t % GRID_W, 16)
    cos64 = jnp.tile(jnp.concatenate([cr, cc], -1), (1, 2))
    sin64 = jnp.tile(jnp.concatenate([sr, sc], -1), (1, 2))
    cr, sr = axis(t // GRID_W, 8)
    cc, sc = axis(t % GRID_W, 8)
    cos32, sin32 = jnp.concatenate([cr, cc], -1), jnp.concatenate([sr, sc], -1)
    one, zero = jnp.ones((n, 1), F32), jnp.zeros((n, 1), F32)
    cosm = jnp.concatenate([jnp.tile(one, (1, 64)), cos32, jnp.tile(one, (1, 32))], -1)
    sinm = jnp.concatenate([jnp.tile(zero, (1, 64)), sin32, jnp.tile(zero, (1, 32))], -1)
    coskr = jnp.concatenate([cos32, jnp.tile(one, (1, 96))], -1)
    sinkr = jnp.concatenate([sin32, jnp.tile(zero, (1, 96))], -1)
    return cos64, sin64, cosm, sinm, coskr, sinkr


_QA_ORDER = ((0, 64), (128, 192), (64, 128), (192, 256))


def _layer_weights(p, l):
    w_in = p["w_in"][l]
    pieces = [w_in[:, a:b] for a, b in _QA_ORDER + ((256, 2304), (2320, 2736), (2304, 2320))]
    pieces.append(jnp.zeros((D_MODEL, IN_PAD_COLS - w_in.shape[1]), F32))
    w_in = jnp.concatenate(pieces, axis=1).astype(BF16)
    w_out = p["w_out"][l]
    w_out = jnp.concatenate([w_out[a:b] for a, b in _QA_ORDER + ((256, w_out.shape[0]),)], axis=0).astype(BF16)
    wq = p["mla_wq_up"][l].reshape(MLA_Q_LORA, MLA_HEADS, MLA_NOPE + MLA_ROPE)
    wq = jnp.pad(wq, ((0, 0), (0, 0), (0, LANES - MLA_NOPE - MLA_ROPE))).reshape(MLA_Q_LORA, 4 * LANES)
    wkv = p["mla_wkv_up"][l].reshape(MLA_KV_LORA, MLA_HEADS, MLA_NOPE + MLA_V)
    wk_top = jnp.pad(wkv[:, :, :MLA_NOPE], ((0, 0), (0, 0), (0, LANES - MLA_NOPE)))
    place = jnp.pad(jnp.eye(MLA_ROPE, dtype=F32), ((0, LANES - MLA_ROPE), (MLA_NOPE, LANES - MLA_NOPE - MLA_ROPE)))
    wk_bot = jnp.broadcast_to(place[:, None, :], (LANES, MLA_HEADS, LANES))
    wk = jnp.concatenate([wk_top, wk_bot], axis=0).reshape(2 * LANES, 4 * LANES)
    wv = wkv[:, :, MLA_NOPE:].reshape(MLA_KV_LORA, MLA_HEADS * MLA_V)
    gate_row = lambda v: jnp.pad(v.reshape(1, 2 * DN_HEADS), ((0, 0), (MLA_ROPE, LANES - MLA_ROPE - 2 * DN_HEADS)))
    return {
        "norm1_g": p["norm1_g"][l][None], "norm2_g": p["norm2_g"][l][None],
        "w_in": w_in, "w_out": w_out,
        "qn_g": jnp.tile(p["gqa_qn_g"][l], 4)[None], "kn_g": jnp.tile(p["gqa_kn_g"][l], 2)[None],
        "mla_qn_g": p["mla_qn_g"][l][None], "mla_kvn_g": p["mla_kvn_g"][l][None],
        "wq": wq.astype(BF16), "wk": wk.astype(BF16), "wv": wv.astype(BF16),
        "dn_conv_w": p["dn_conv_w"][l], "dn_alog_row": gate_row(p["dn_a_log"][l]),
        "dn_dtb_row": gate_row(p["dn_dt_bias"][l]), "dn_out_g": jnp.tile(p["dn_out_g"][l], DN_HEADS)[None],
        "w_gate": p["ffn_w_gate"][l].astype(BF16), "w_up": p["ffn_w_up"][l].astype(BF16),
        "w_down": p["ffn_w_down"][l].astype(BF16),
    }


def _state_to_wide(s):
    b = s.shape[0]
    return s.transpose(0, 1, 3, 2, 4).reshape(b, 2, DN_DK, DN_HEADS * DN_DV)


def _state_from_wide(s):
    b = s.shape[0]
    return s.reshape(b, 2, DN_DK, DN_HEADS, DN_DV).transpose(0, 1, 3, 2, 4)


def kernel(x_prompt, x_sample, cache_gqa_k, cache_gqa_v, cache_na_k, cache_na_v, state_dn,
           cache_mla_ckv, cache_mla_krope, c, c_ctx, norm1_g, norm2_g, w_mod, b_mod, w_in, w_out,
           gqa_qn_g, gqa_kn_g, na_bias, dn_conv_w, dn_a_log, dn_dt_bias, dn_out_g, mla_qn_g,
           mla_wq_up, mla_kvn_g, mla_wkv_up, ffn_w_gate, ffn_w_up, ffn_w_down, final_g):
    p = {"norm1_g": norm1_g, "norm2_g": norm2_g, "w_in": w_in, "w_out": w_out, "gqa_qn_g": gqa_qn_g,
         "gqa_kn_g": gqa_kn_g, "dn_conv_w": dn_conv_w, "dn_a_log": dn_a_log, "dn_dt_bias": dn_dt_bias,
         "dn_out_g": dn_out_g, "mla_qn_g": mla_qn_g, "mla_wq_up": mla_wq_up, "mla_kvn_g": mla_kvn_g,
         "mla_wkv_up": mla_wkv_up, "ffn_w_gate": ffn_w_gate, "ffn_w_up": ffn_w_up, "ffn_w_down": ffn_w_down}
    nb_ctx, seq_ctx, _ = x_prompt.shape
    nb_lat, seq_lat, _ = x_sample.shape
    past = cache_gqa_k.shape[2]
    fg = final_g[None]

    cond = jnp.concatenate([c_ctx[None], c, jnp.zeros((16 - 1 - nb_lat, D_MODEL), F32)], axis=0)
    mods = _modulation(cond, w_mod, b_mod).reshape(DEPTH, 16, MOD_CHUNKS, D_MODEL)
    weights = [_layer_weights(p, l) for l in range(DEPTH)]

    x = x_prompt
    ctx_out = []
    for l in range(DEPTH):
        lw = weights[l]
        m = mods[l, 0:1]
        qa, ka, va, qb, kb, vb, zc, qd, ckv, zm, krr = _inproj(x, m, False, lw, None)
        o_a = _attn_pair(qa, ka, va, (0, 0), name="gqa_ctx")
        o_b = _attn_pair(qb, kb, vb, (0, 1), name="na_ctx")
        o_c, s_dn = _deltanet(zc, zm, lw)
        o_d = _mla(qd, ckv, krr, lw["wk"], lw["wv"], name="mla_ctx")
        x = _outffn((o_a, o_b, o_c, o_d), x, m, False, lw, fg, l == DEPTH - 1)
        ctx_out.append((ka.reshape(nb_ctx, seq_ctx, GQA_KV_HEADS, HEAD_DIM),
                        va.reshape(nb_ctx, seq_ctx, GQA_KV_HEADS, HEAD_DIM),
                        kb.reshape(nb_ctx, seq_ctx, NA_HEADS, HEAD_DIM),
                        vb.reshape(nb_ctx, seq_ctx, NA_HEADS, HEAD_DIM),
                        _state_from_wide(s_dn), ckv, zm[:, :, :MLA_ROPE]))
    y_prompt = x
    new = [jnp.stack([s[i] for s in ctx_out], axis=1) for i in range(7)]

    ropes = _rope_tables(seq_lat)
    ck_a = cache_gqa_k.reshape(nb_lat, DEPTH, past, GQA_KV_HEADS * HEAD_DIM)
    cv_a = cache_gqa_v.reshape(nb_lat, DEPTH, past, GQA_KV_HEADS * HEAD_DIM)
    ck_b = cache_na_k.reshape(nb_lat, DEPTH, past, NA_HEADS * HEAD_DIM)
    cv_b = cache_na_v.reshape(nb_lat, DEPTH, past, NA_HEADS * HEAD_DIM)
    c_kr = jnp.pad(cache_mla_krope, ((0, 0), (0, 0), (0, 0), (0, LANES - MLA_ROPE)))
    x = x_sample
    for l in range(DEPTH):
        lw = weights[l]
        m = mods[l, 1:1 + nb_lat]
        qa, ka, va, qb, kb, vb, zc, qd, ckv, zm, krr = _inproj(x, m, True, lw, ropes)
        o_a = _attn_pair(qa, ka, va, (0, 0), cache=(ck_a, cv_a, l), name="gqa_lat")
        o_b = _na_latent(qb, kb, vb, ck_b, cv_b, l, _na_bias_blocks(na_bias[l]))
        o_c, _ = _deltanet(zc, zm, lw, state=_state_to_wide(state_dn[:, l]))
        o_d = _mla(qd, ckv, krr, lw["wk"], lw["wv"], cache=(cache_mla_ckv, c_kr, l), name="mla_lat")
        x = _outffn((o_a, o_b, o_c, o_d), x, m, True, lw, fg, l == DEPTH - 1)
    y_sample = x

    return (y_prompt, y_sample, *new)
```

```python
import functools

import numpy as np
import jax
import jax.numpy as jnp
from jax import lax
from jax.experimental import pallas as pl
from jax.experimental.pallas import tpu as pltpu

F32 = jnp.float32
BF16 = jnp.bfloat16

D_MODEL = 1024
DEPTH = 2
GRID_W = 64
HEAD_DIM = 64
ROPE_BASE = 10000.0
NEG_INF = -1e30
MOD_CHUNKS = 6
GQA_HEADS, GQA_KV_HEADS = 4, 2
NA_HEADS, NA_KH, NA_KW = 4, 8, 16
DN_HEADS, DN_DK, DN_DV, DN_CONV, DN_CHUNK = 4, 64, 64, 4, 64
DN_QKV = DN_HEADS * (2 * DN_DK + DN_DV)
MLA_HEADS, MLA_Q_LORA, MLA_KV_LORA, MLA_NOPE, MLA_ROPE, MLA_V = 4, 256, 128, 64, 32, 64
MLA_SCALE = (MLA_NOPE + MLA_ROPE) ** -0.5
D_FF = -(-8 * D_MODEL // (3 * 256)) * 256
EPS = 1e-6
LOG2E = 1.4426950408889634

LANES = 128
ROW_TILE = 256
Q_TILE = 256
NA_BAND = 768
TILES_PER_STEP = 2
ATTN_Q_TILE = 512
HEAD_LOOKAHEAD = 1
PREP_CHUNKS = 4
IN_PAD_COLS = 2816
VMEM_LIMIT = 56 * 1024 * 1024

_NT = (((1,), (1,)), ((), ()))
_TN = (((0,), (0,)), ((), ()))


def _cparams(n_axes):
    return pltpu.CompilerParams(dimension_semantics=("arbitrary",) * n_axes,
                                vmem_limit_bytes=VMEM_LIMIT)


def _lane(shape):
    return lax.broadcasted_iota(jnp.int32, shape, len(shape) - 1)


def _silu(x):
    return x / (1.0 + jnp.exp(-x))


def _rms_full(x, g):
    return x * lax.rsqrt(jnp.mean(x * x, axis=-1, keepdims=True) + EPS) * g


def _seg64_sum(x):
    lo = _lane(x.shape) < HEAD_DIM
    s_lo = jnp.sum(jnp.where(lo, x, 0.0), axis=-1, keepdims=True)
    s_hi = jnp.sum(jnp.where(lo, 0.0, x), axis=-1, keepdims=True)
    return jnp.where(lo, s_lo, s_hi)


def _head_rms(x, g):
    parts = []
    for p in range(x.shape[-1] // LANES):
        xp = x[:, p * LANES:(p + 1) * LANES]
        ms = _seg64_sum(xp * xp) * (1.0 / HEAD_DIM)
        parts.append(xp * lax.rsqrt(ms + EPS))
    y = parts[0] if len(parts) == 1 else jnp.concatenate(parts, axis=-1)
    return y * g


def _head_l2(x):
    parts = []
    for p in range(x.shape[-1] // LANES):
        xp = x[:, p * LANES:(p + 1) * LANES]
        parts.append(xp * lax.rsqrt(_seg64_sum(xp * xp) + EPS))
    return parts[0] if len(parts) == 1 else jnp.concatenate(parts, axis=-1)


def _rope(x, cos, sin, half):
    first = (_lane(x.shape) & (2 * half - 1)) < half
    rot = jnp.where(first, pltpu.roll(x, LANES - half, 1), pltpu.roll(x, half, 1))
    return x * cos + rot * sin


def _softmax_parts(scores):
    m = jnp.max(scores[0], axis=-1, keepdims=True)
    for s in scores[1:]:
        m = jnp.maximum(m, jnp.max(s, axis=-1, keepdims=True))
    es = [jnp.exp2(s - m) for s in scores]
    l = jnp.sum(es[0], axis=-1, keepdims=True)
    for e in es[1:]:
        l = l + jnp.sum(e, axis=-1, keepdims=True)
    return es, 1.0 / l


def _bdot(a, b):
    return jnp.dot(a, b, preferred_element_type=F32)


def _pipelined_heads(n_heads, scores, attend):
    outs = []
    queue = [scores(hd) for hd in range(min(HEAD_LOOKAHEAD, n_heads))]
    for hd in range(n_heads):
        if hd + HEAD_LOOKAHEAD < n_heads:
            queue.append(scores(hd + HEAD_LOOKAHEAD))
        es, rl = _softmax_parts(queue.pop(0))
        outs.append(attend(hd, es) * rl)
    return outs


def _mod_kernel(c_ref, w_ref, b_ref, o_ref):
    s = _silu(c_ref[...]).astype(BF16)
    o_ref[...] = _bdot(s, w_ref[...].astype(BF16)) + b_ref[...]


def _modulation(cond, w_mod, b_mod):
    n = MOD_CHUNKS * D_MODEL
    tn = 1536
    return pl.pallas_call(
        _mod_kernel,
        grid=(DEPTH, n // tn),
        in_specs=[pl.BlockSpec((16, D_MODEL), lambda l, j: (0, 0)),
                  pl.BlockSpec((None, D_MODEL, tn), lambda l, j: (l, 0, j)),
                  pl.BlockSpec((None, 1, tn), lambda l, j: (l, 0, j))],
        out_specs=pl.BlockSpec((None, 16, tn), lambda l, j: (l, 0, j)),
        out_shape=jax.ShapeDtypeStruct((DEPTH, 16, n), F32),
        compiler_params=_cparams(2),
        name="modulation",
    )(cond, w_mod, b_mod.reshape(DEPTH, 1, n))


_IN_OUT_WIDTHS = (256, 128, 128, 256, 256, 256, 1024, 512, 128, 128, 128)
_IN_OUT_DTYPES_LAT = (BF16, BF16, BF16, BF16, BF16, BF16, F32, BF16, BF16, F32, BF16)
_IN_OUT_DTYPES_CTX = (BF16, F32, F32, BF16, F32, F32, F32, BF16, F32, F32, BF16)


def _inproj_kernel(positioned, *refs):
    (x_ref, mod_ref, g1_ref, w_ref, qng_ref, kng_ref, mqg_ref, wq_ref, mkg_ref) = refs[:9]
    n_in = 9
    if positioned:
        cos64_ref, sin64_ref, cosm_ref, sinm_ref, coskr_ref, sinkr_ref = refs[9:15]
        n_in = 15
    (qa_ref, ka_ref, va_ref, qb_ref, kb_ref, vb_ref, zc_ref, qd_ref, ckv_ref, zm_ref,
     krr_ref) = refs[n_in:]
    tiles = range(TILES_PER_STEP)

    m = mod_ref[...]
    hb = [(_rms_full(x_ref[t], g1_ref[...]) * (1.0 + m[1:2]) + m[0:1]).astype(BF16) for t in tiles]
    za = [_bdot(hb[t], w_ref[:, 0:512]) for t in tiles]
    zb = [_bdot(hb[t], w_ref[:, 512:1280]) for t in tiles]

    for t in tiles:
        q = _head_rms(za[t][:, 0:256], qng_ref[...])
        k = _head_rms(za[t][:, 256:384], kng_ref[...])
        if positioned:
            cos, sin = cos64_ref[t], sin64_ref[t]
            q = jnp.concatenate([_rope(q[:, 0:128], cos, sin, 16), _rope(q[:, 128:256], cos, sin, 16)],
                                axis=-1)
            k = _rope(k, cos, sin, 16)
        qa_ref[t] = (q * (HEAD_DIM ** -0.5 * LOG2E)).astype(qa_ref.dtype)
        ka_ref[t] = k.astype(ka_ref.dtype)
        va_ref[t] = za[t][:, 384:512].astype(va_ref.dtype)

    zd = [_bdot(hb[t], w_ref[:, 2304:2816]) for t in tiles]
    zc = [_bdot(hb[t], w_ref[:, 1280:2304]) for t in tiles]

    for t in tiles:
        qb_ref[t] = (zb[t][:, 0:256] * (HEAD_DIM ** -0.5 * LOG2E)).astype(qb_ref.dtype)
        kb_ref[t] = zb[t][:, 256:512].astype(kb_ref.dtype)
        vb_ref[t] = zb[t][:, 512:768].astype(vb_ref.dtype)

    qm = [_bdot(_rms_full(zd[t][:, 0:256], mqg_ref[...]).astype(BF16), wq_ref[...]) for t in tiles]
    for t in tiles:
        zc_ref[t] = zc[t]
        ckv_ref[t] = _rms_full(zd[t][:, 256:384], mkg_ref[...]).astype(ckv_ref.dtype)
        zmisc = zd[t][:, 384:512]
        zm_ref[t] = zmisc
        kr = jnp.where(_lane(zmisc.shape) < MLA_ROPE, zmisc, 0.0)
        q = qm[t]
        if positioned:
            kr = _rope(kr, coskr_ref[t], sinkr_ref[t], 8)
            cm, sm = cosm_ref[t], sinm_ref[t]
            q = jnp.concatenate([_rope(q[:, i * LANES:(i + 1) * LANES], cm, sm, 8)
                                 for i in range(MLA_HEADS)], axis=-1)
        krr_ref[t] = kr.astype(krr_ref.dtype)
        qd_ref[t] = (q * (MLA_SCALE * LOG2E)).astype(qd_ref.dtype)


def _inproj(x, mods, per_batch_mods, lw, layer, ropes):
    b, l, _ = x.shape
    tm, ts = ROW_TILE, TILES_PER_STEP
    n_tiles = b * l // tm
    tiles_per_seq = l // tm
    positioned = ropes is not None
    row = lambda w: pl.BlockSpec((ts, tm, w), lambda i: (i, 0, 0))
    const = lambda shape: pl.BlockSpec((None,) + shape, lambda i: (layer,) + (0,) * len(shape))
    assert not per_batch_mods or tiles_per_seq % ts == 0
    mod_spec = pl.BlockSpec((None, None, MOD_CHUNKS, D_MODEL),
                            (lambda i: (layer, 1 + i * ts // tiles_per_seq, 0, 0)) if per_batch_mods
                            else (lambda i: (layer, 0, 0, 0)))
    in_specs = [row(D_MODEL), mod_spec, const((1, D_MODEL)), const((D_MODEL, IN_PAD_COLS)),
                const((1, 256)), const((1, 128)), const((1, MLA_Q_LORA)),
                const((MLA_Q_LORA, 4 * LANES)), const((1, MLA_KV_LORA))]
    args = [x.reshape(n_tiles, tm, D_MODEL), mods, lw["norm1_g"], lw["w_in"], lw["qn_g"], lw["kn_g"],
            lw["mla_qn_g"], lw["wq"], lw["mla_kvn_g"]]
    if positioned:
        steps_per_seq = tiles_per_seq // ts
        in_specs += [pl.BlockSpec((ts, tm, LANES), lambda i: (i % steps_per_seq, 0, 0))] * 6
        args += [r.reshape(tiles_per_seq, tm, LANES) for r in ropes]
    outs = pl.pallas_call(
        functools.partial(_inproj_kernel, positioned),
        grid=(n_tiles // ts,),
        in_specs=in_specs,
        out_specs=[row(w) for w in _IN_OUT_WIDTHS],
        out_shape=[jax.ShapeDtypeStruct((n_tiles, tm, w), dt)
                   for w, dt in zip(_IN_OUT_WIDTHS, _IN_OUT_DTYPES_LAT if positioned else _IN_OUT_DTYPES_CTX)],
        compiler_params=_cparams(1),
        name="inproj_lat" if positioned else "inproj_ctx",
    )(*args)
    return [o.reshape(b, l, w) for o, w in zip(outs, _IN_OUT_WIDTHS)]


def _attn_pair_kernel(has_cache, qmap, *refs):
    if has_cache:
        q_ref, kc_ref, vc_ref, k_ref, v_ref, o_ref, kbuf, vbuf = refs
    else:
        q_ref, k_ref, v_ref, o_ref, kbuf, vbuf = refs

    @pl.when(pl.program_id(1) == 0)
    def _():
        off = 0
        if has_cache:
            off = kc_ref.shape[0]
            kbuf[0:off, :] = kc_ref[...].astype(BF16)
            vbuf[0:off, :] = vc_ref[...].astype(BF16)
        kbuf[off:, :] = k_ref[...].astype(BF16)
        vbuf[off:, :] = v_ref[...].astype(BF16)

    tq = q_ref.shape[0]
    lo = _lane((tq, LANES)) < HEAD_DIM

    def scores(hd):
        p, half = divmod(hd, 2)
        qp = q_ref[:, p * LANES:(p + 1) * LANES].astype(F32)
        qm = jnp.where(lo if half == 0 else jnp.logical_not(lo), qp, 0.0).astype(BF16)
        return [lax.dot_general(qm, kbuf[:, qmap[p] * LANES:(qmap[p] + 1) * LANES], _NT,
                                preferred_element_type=F32)]

    def attend(hd, es):
        kv = qmap[hd // 2]
        return _bdot(es[0].astype(BF16), vbuf[:, kv * LANES:(kv + 1) * LANES])

    outs = _pipelined_heads(2 * len(qmap), scores, attend)
    for p in range(len(qmap)):
        o_ref[:, p * LANES:(p + 1) * LANES] = jnp.where(lo, outs[2 * p], outs[2 * p + 1]).astype(o_ref.dtype)


def _attn_pair(q, k, v, qmap, cache=None, name="attn_pair"):
    b, lq, wq = q.shape
    ls, wk = k.shape[1], k.shape[2]
    tq = min(ATTN_Q_TILE, lq)
    lc = 0 if cache is None else cache[0].shape[2]
    in_specs = [pl.BlockSpec((None, tq, wq), lambda bi, i: (bi, i, 0))]
    args = [q]
    if cache is not None:
        kc, vc, layer = cache
        cspec = pl.BlockSpec((None, None, lc, wk), lambda bi, i: (bi, layer, 0, 0))
        in_specs += [cspec, cspec]
        args += [kc, vc]
    sspec = pl.BlockSpec((None, ls, wk), lambda bi, i: (bi, 0, 0))
    in_specs += [sspec, sspec]
    args += [k, v]
    return pl.pallas_call(
        functools.partial(_attn_pair_kernel, cache is not None, qmap),
        grid=(b, lq // tq),
        in_specs=in_specs,
        out_specs=pl.BlockSpec((None, tq, wq), lambda bi, i: (bi, i, 0)),
        out_shape=jax.ShapeDtypeStruct((b, lq, wq), BF16),
        scratch_shapes=[pltpu.VMEM((lc + ls, wk), BF16), pltpu.VMEM((lc + ls, wk), BF16)],
        compiler_params=_cparams(2),
        name=name,
    )(*args)


def _mla_kernel(has_cache, *refs):
    if has_cache:
        q_ref, ckvc_ref, krc_ref, ckv_ref, kr_ref, wk_ref, wv_ref, o_ref, kbuf, vbuf = refs
    else:
        q_ref, ckv_ref, kr_ref, wk_ref, wv_ref, o_ref, kbuf, vbuf = refs

    @pl.when(pl.program_id(1) == 0)
    def _():
        def expand(c_ref, r_ref, r0, r1):
            c = c_ref[...].astype(BF16)
            ckr = jnp.concatenate([c, r_ref[...].astype(BF16)], axis=-1)
            kbuf[r0:r1, :] = _bdot(ckr, wk_ref[...]).astype(BF16)
            vbuf[r0:r1, :] = _bdot(c, wv_ref[...]).astype(BF16)
        off = 0
        if has_cache:
            off = ckvc_ref.shape[0]
            expand(ckvc_ref, krc_ref, 0, off)
        expand(ckv_ref, kr_ref, off, kbuf.shape[0])

    tq = q_ref.shape[0]
    lo = _lane((tq, LANES)) < MLA_V

    def scores(hd):
        qh = q_ref[:, hd * LANES:(hd + 1) * LANES].astype(BF16)
        return [lax.dot_general(qh, kbuf[:, hd * LANES:(hd + 1) * LANES], _NT, preferred_element_type=F32)]

    def attend(hd, es):
        p = hd // 2
        return _bdot(es[0].astype(BF16), vbuf[:, p * LANES:(p + 1) * LANES])

    outs = _pipelined_heads(MLA_HEADS, scores, attend)
    for p in range(MLA_HEADS // 2):
        o_ref[:, p * LANES:(p + 1) * LANES] = jnp.where(lo, outs[2 * p], outs[2 * p + 1]).astype(o_ref.dtype)


def _mla(q, ckv, kr, lw, layer, cache=None, name="mla"):
    b, lq, wq = q.shape
    ls = ckv.shape[1]
    tq = min(ATTN_Q_TILE, lq)
    lc = 0 if cache is None else cache[0].shape[2]
    in_specs = [pl.BlockSpec((None, tq, wq), lambda bi, i: (bi, i, 0))]
    args = [q]
    if cache is not None:
        ckvc, krc = cache
        cspec = pl.BlockSpec((None, None, lc, LANES), lambda bi, i: (bi, layer, 0, 0))
        in_specs += [cspec, cspec]
        args += [ckvc, krc]
    sspec = pl.BlockSpec((None, ls, LANES), lambda bi, i: (bi, 0, 0))
    in_specs += [sspec, sspec,
                 pl.BlockSpec((None, 2 * LANES, 4 * LANES), lambda bi, i: (layer, 0, 0)),
                 pl.BlockSpec((None, LANES, 2 * LANES), lambda bi, i: (layer, 0, 0))]
    args += [ckv, kr, lw["wk"], lw["wv"]]
    return pl.pallas_call(
        functools.partial(_mla_kernel, cache is not None),
        grid=(b, lq // tq),
        in_specs=in_specs,
        out_specs=pl.BlockSpec((None, tq, 2 * LANES), lambda bi, i: (bi, i, 0)),
        out_shape=jax.ShapeDtypeStruct((b, lq, 2 * LANES), BF16),
        scratch_shapes=[pltpu.VMEM((lc + ls, 4 * LANES), BF16), pltpu.VMEM((lc + ls, 2 * LANES), BF16)],
        compiler_params=_cparams(2),
        name=name,
    )(*args)


def _na_kernel(q_ref, k_ref, v_ref, kc_ref, vc_ref, bias_ref, o_ref):
    j = pl.program_id(0)
    start = pl.multiple_of((j >> 1) * 256, 256)
    kband = k_ref[pl.ds(start, NA_BAND), :].astype(BF16)
    vband = v_ref[pl.ds(start, NA_BAND), :].astype(BF16)
    kc = kc_ref[...].astype(BF16)
    vc = vc_ref[...].astype(BF16)
    tq = q_ref.shape[0]
    lo = _lane((tq, LANES)) < HEAD_DIM

    def scores(hd):
        p, half = divmod(hd, 2)
        sl = slice(p * LANES, (p + 1) * LANES)
        qm = jnp.where(lo if half == 0 else jnp.logical_not(lo), q_ref[:, sl].astype(F32), 0.0).astype(BF16)
        s_loc = lax.dot_general(qm, kband[:, sl], _NT, preferred_element_type=F32) + bias_ref[hd]
        return [s_loc, lax.dot_general(qm, kc[:, sl], _NT, preferred_element_type=F32)]

    def attend(hd, es):
        sl = slice((hd // 2) * LANES, (hd // 2 + 1) * LANES)
        return _bdot(es[0].astype(BF16), vband[:, sl]) + _bdot(es[1].astype(BF16), vc[:, sl])

    outs = _pipelined_heads(NA_HEADS, scores, attend)
    for p in range(NA_HEADS // 2):
        o_ref[:, p * LANES:(p + 1) * LANES] = jnp.where(lo, outs[2 * p], outs[2 * p + 1]).astype(o_ref.dtype)


def _na_latent(q, k, v, kc, vc, layer, bias_blocks):
    b, n, w = q.shape
    lc = kc.shape[2]
    nq = n // Q_TILE
    full = pl.BlockSpec((None, n, w), lambda j, bi: (bi, 0, 0))
    cspec = pl.BlockSpec((None, None, lc, w), lambda j, bi: (bi, layer, 0, 0))
    return pl.pallas_call(
        _na_kernel,
        grid=(nq, b),
        in_specs=[pl.BlockSpec((None, Q_TILE, w), lambda j, bi: (bi, j, 0)), full, full, cspec, cspec,
                  pl.BlockSpec((None, None, NA_HEADS, Q_TILE, NA_BAND), lambda j, bi: (layer, j, 0, 0, 0))],
        out_specs=pl.BlockSpec((None, Q_TILE, w), lambda j, bi: (bi, j, 0)),
        out_shape=jax.ShapeDtypeStruct((b, n, w), BF16),
        compiler_params=_cparams(2),
        name="na_latent",
    )(q, k, v, kc, vc, bias_blocks)


NA_GRID_ROWS = 16
NA_BAND_ROW0 = (0, 0, 4, 4)


def _na_bias_kernel(b_ref, o_ref, tp_s):
    hd = pl.program_id(0)
    n_dr, n_dc = 2 * NA_KH - 1, 2 * NA_KW - 1
    shape = (GRID_W, LANES)
    c = lax.broadcasted_iota(jnp.int32, shape, 0)
    lane = _lane(shape)
    kc = lane & (GRID_W - 1)
    lo = lane < GRID_W
    diff = kc - c + (NA_KW - 1)
    c0 = jnp.clip(c - NA_KW // 2, 0, GRID_W - NA_KW)
    col_ok = (kc >= c0) & (kc < c0 + NA_KW)
    neg = jnp.full(shape, NEG_INF, F32)
    for dr0 in range(-1, n_dr):
        acc = neg
        for d in range(n_dc):
            v_lo = b_ref[hd * n_dr + dr0, d] if dr0 >= 0 else 0.0
            v_hi = b_ref[hd * n_dr + dr0 + 1, d] if dr0 + 1 < n_dr else 0.0
            acc = jnp.where(diff == d, jnp.where(lo, v_lo, v_hi), acc)
        tp_s[dr0 + 1] = jnp.where(col_ok, acc * LOG2E, NEG_INF)
    for j in range(NA_GRID_ROWS // 4):
        for ri in range(4):
            r = 4 * j + ri
            r0 = min(max(r - NA_KH // 2, 0), NA_GRID_ROWS - NA_KH)
            for kp in range(NA_BAND // LANES):
                kr = NA_BAND_ROW0[j] + 2 * kp
                ok_lo, ok_hi = r0 <= kr < r0 + NA_KH, r0 <= kr + 1 < r0 + NA_KH
                dr0 = kr - r + (NA_KH - 1)
                if ok_lo and ok_hi:
                    t = tp_s[dr0 + 1]
                elif ok_lo:
                    t = jnp.where(lo, tp_s[dr0 + 1], NEG_INF)
                elif ok_hi:
                    t = jnp.where(lo, NEG_INF, tp_s[dr0 + 1])
                else:
                    t = neg
                o_ref[j, ri * GRID_W:(ri + 1) * GRID_W, kp * LANES:(kp + 1) * LANES] = t


def _na_bias_blocks(bias):
    nq = NA_GRID_ROWS // 4
    return pl.pallas_call(
        _na_bias_kernel,
        grid=(DEPTH * NA_HEADS,),
        in_specs=[pl.BlockSpec(memory_space=pltpu.SMEM)],
        out_specs=pl.BlockSpec((None, nq, None, Q_TILE, NA_BAND),
                               lambda i: (i // NA_HEADS, 0, i % NA_HEADS, 0, 0)),
        out_shape=jax.ShapeDtypeStruct((DEPTH, nq, NA_HEADS, Q_TILE, NA_BAND), F32),
        scratch_shapes=[pltpu.VMEM((2 * NA_KH, GRID_W, LANES), F32)],
        compiler_params=_cparams(1),
        name="na_bias",
    )(bias.reshape(DEPTH * NA_HEADS * (2 * NA_KH - 1), 2 * NA_KW - 1))


def _widen(cols, n):
    blk = _lane((n, DN_HEADS * DN_DV)) >> 6
    return jnp.where(blk == 0, cols[0], jnp.where(blk == 1, cols[1], jnp.where(blk == 2, cols[2], cols[3])))


def _deltanet_kernel(seq, has_state, *refs):
    if has_state:
        (zc_ref, zm_ref, s0_ref, cw_ref, alog_ref, dtb_ref, og_ref, o_ref,
         q_s, k_s, v_s, b_s, g_s, o_s, c_s, mp_s) = refs
    else:
        (zc_ref, zm_ref, cw_ref, alog_ref, dtb_ref, og_ref, o_ref, sfin_ref,
         q_s, k_s, v_s, b_s, g_s, o_s, c_s, mp_s) = refs
    n_chunks = seq // DN_CHUNK
    wide = DN_HEADS * DN_DV

    row = lax.broadcasted_iota(jnp.int32, (seq, wide), 0)
    for part, dst in enumerate((q_s, k_s, v_s)):
        cs = slice(part * wide, (part + 1) * wide)
        x = zc_ref[:, cs]
        w = cw_ref[:, cs]
        y = (w[0:1] * jnp.where(row >= 1, pltpu.roll(x, 1, 0), 0.0) + w[1:2] * x
             + w[2:3] * jnp.where(row < seq - 1, pltpu.roll(x, seq - 1, 0), 0.0)
             + w[3:4] * jnp.where(row < seq - 2, pltpu.roll(x, seq - 2, 0), 0.0))
        y = _silu(y)
        if part == 0:
            y = _head_l2(y) * (DN_DK ** -0.5)
        elif part == 1:
            y = _head_l2(y)
        dst[...] = y

    zm = zm_ref[...]
    xa = zm + dtb_ref[...]
    logd = -jnp.exp(alog_ref[...]) * (jnp.maximum(xa, 0.0) + jnp.log1p(jnp.exp(-jnp.abs(xa))))
    beta = 1.0 / (1.0 + jnp.exp(-zm))
    a_off, b_off = MLA_ROPE, MLA_ROPE + 2 * DN_HEADS
    ri = lax.broadcasted_iota(jnp.int32, (256, 256), 0)
    ci = lax.broadcasted_iota(jnp.int32, (256, 256), 1)
    same = (ri >> 6) == (ci >> 6)
    for d in range(2):
        b_s[d] = _widen([beta[:, b_off + 4 * d + hd:b_off + 4 * d + hd + 1] for hd in range(DN_HEADS)], seq)
        lw = _widen([logd[:, a_off + 4 * d + hd:a_off + 4 * d + hd + 1] for hd in range(DN_HEADS)], seq)
        tri = jnp.where(same & ((ci <= ri) if d == 0 else (ci >= ri)), 1.0, 0.0).astype(F32)
        for rb in range(seq // 256):
            rs = slice(rb * 256, (rb + 1) * 256)
            g_s[d, rs, :] = jnp.dot(tri, lw[rs], preferred_element_type=F32,
                                    precision=lax.Precision.HIGHEST)

    ii = lax.broadcasted_iota(jnp.int32, (DN_CHUNK, wide), 0)
    jj = _lane((DN_CHUNK, wide)) & (DN_CHUNK - 1)
    blk = _lane((DN_CHUNK, wide)) >> 6
    diag = ii == jj
    eye = jnp.where(diag, 1.0, 0.0)
    head_mask = [jnp.where(blk == hd, 1.0, 0.0).astype(BF16) for hd in range(DN_HEADS)]

    def bd(z):
        zb = z.astype(BF16)
        return jnp.concatenate([zb * hm for hm in head_mask], axis=0)

    def fold(gram):
        out = jnp.where(blk == 0, gram[0:DN_CHUNK], 0.0)
        for hd in range(1, DN_HEADS):
            out = out + jnp.where(blk == hd, gram[hd * DN_CHUNK:(hd + 1) * DN_CHUNK], 0.0)
        return out

    tri_masks = []
    for d in range(2):
        incl = (jj <= ii) if d == 0 else (jj >= ii)
        strict = (jj < ii) if d == 0 else (jj > ii)
        pair = [((ii >> (lvl + 1)) == (jj >> (lvl + 1)))
                & (((ii >> lvl) & 1) == (1 - d)) & (((jj >> lvl) & 1) == d) for lvl in range(6)]
        tri_masks.append((incl, strict, pair))

    def prepare(step, carry):
        chunks = [step * PREP_CHUNKS + i for i in range(PREP_CHUNKS)]
        rows = [pl.ds(pl.multiple_of(c * DN_CHUNK, DN_CHUNK), DN_CHUNK) for c in chunks]
        qkv = [(q_s[r, :], k_s[r, :], v_s[r, :]) for r in rows]
        inst = [(ci, d) for ci in range(PREP_CHUNKS) for d in range(2)]
        beta = {(ci, d): b_s[d, rows[ci], :] for ci, d in inst}
        kb = {(ci, d): qkv[ci][1] * beta[ci, d] for ci, d in inst}
        r = [lax.dot_general(jnp.concatenate([kb[ci, 0], kb[ci, 1], qkv[ci][0]], axis=0).astype(BF16),
                             bd(qkv[ci][1]), _NT, preferred_element_type=F32)
             for ci in range(PREP_CHUNKS)]
        g, a, qk, eg, t = {}, {}, {}, {}, {}
        for ci, d in inst:
            incl, strict, pair = tri_masks[d]
            g[ci, d] = g_s[d, rows[ci], :]
            g_row = jnp.sum(jnp.where(diag, g[ci, d], 0.0), axis=0, keepdims=True)
            dm = jnp.where(incl, jnp.exp(jnp.where(incl, g[ci, d] - g_row, 0.0)), 0.0)
            a[ci, d] = jnp.where(strict, r[ci][d * DN_CHUNK:(d + 1) * DN_CHUNK] * dm, 0.0)
            qk[ci, d] = (r[ci][2 * DN_CHUNK:] * dm).astype(BF16)
            eg[ci, d] = jnp.exp(g[ci, d])
            t[ci, d] = eye - jnp.where(pair[0], a[ci, d], 0.0)
        for lvl in range(1, 6):
            te = {i: _bdot(t[i].astype(BF16), bd(jnp.where(tri_masks[i[1]][2][lvl], a[i], 0.0))) for i in inst}
            t = {i: t[i] - _bdot(te[i].astype(BF16), bd(t[i])) for i in inst}
        nb = {i: jnp.where(diag, 0.0, t[i]).astype(BF16) for i in inst}
        rhs_u = {(ci, d): qkv[ci][2] * beta[ci, d] for ci, d in inst}
        rhs_w = {i: kb[i] * eg[i] for i in inst}
        u = {i: rhs_u[i] + _bdot(nb[i], bd(rhs_u[i])) for i in inst}
        w = {i: rhs_w[i] + _bdot(nb[i], bd(rhs_w[i])) for i in inst}
        kd = {}
        for ci, d in inst:
            g_last = g[ci, d][DN_CHUNK - 1:DN_CHUNK] if d == 0 else g[ci, d][0:1]
            kd[ci, d] = (qkv[ci][1] * jnp.exp(g_last - g[ci, d])).astype(BF16)
        p = {(ci, d): qkv[ci][0] * eg[ci, d] - _bdot(qk[ci, d], bd(w[ci, d])) for ci, d in inst}
        o0 = {i: _bdot(qk[i], bd(u[i])) for i in inst}
        m = {i: fold(lax.dot_general(kd[i], w[i].astype(BF16), _TN, preferred_element_type=F32)) for i in inst}
        cc = {i: fold(lax.dot_general(kd[i], u[i].astype(BF16), _TN, preferred_element_type=F32)) for i in inst}
        for ci, d in inst:
            mrow = pl.ds(pl.multiple_of(chunks[ci] * (2 * DN_CHUNK), 2 * DN_CHUNK), 2 * DN_CHUNK)
            c_s[d, rows[ci], :] = cc[ci, d]
            mp_s[d, mrow, :] = jnp.concatenate([m[ci, d], p[ci, d]], axis=0).astype(BF16)
        for ci in range(PREP_CHUNKS):
            o_s[rows[ci], :] = o0[ci, 0] + o0[ci, 1]
        return carry

    lax.fori_loop(0, n_chunks // PREP_CHUNKS, prepare, 0)

    def scan(i, states):
        new = []
        for d in range(2):
            c = i if d == 0 else n_chunks - 1 - i
            r0 = pl.multiple_of(c * DN_CHUNK, DN_CHUNK)
            rows = pl.ds(r0, DN_CHUNK)
            mrow = pl.ds(pl.multiple_of(c * (2 * DN_CHUNK), 2 * DN_CHUNK), 2 * DN_CHUNK)
            edge = pl.ds(pl.multiple_of(r0 + (DN_CHUNK - 8 if d == 0 else 0), 8), 8)
            g_edge = g_s[d, edge, :]
            g_last = g_edge[7:8] if d == 0 else g_edge[0:1]
            res = _bdot(mp_s[d, mrow, :], bd(states[d]))
            o_s[rows, :] = o_s[rows, :] + res[DN_CHUNK:]
            new.append(states[d] * jnp.exp(g_last) - res[0:DN_CHUNK] + c_s[d, rows, :])
        return tuple(new)

    init = tuple(s0_ref[d] if has_state else jnp.zeros((DN_DK, wide), F32) for d in range(2))
    fin = lax.fori_loop(0, n_chunks, scan, init)
    if not has_state:
        sfin_ref[0] = fin[0]
        sfin_ref[1] = fin[1]

    o_ref[...] = (_head_rms(o_s[...], og_ref[...]) * _silu(zc_ref[:, 3 * wide:4 * wide])).astype(o_ref.dtype)


def _deltanet(zc, zm, lw, layer, state=None):
    b, seq, _ = zc.shape
    wide = DN_HEADS * DN_DV
    has_state = state is not None
    per_b = lambda w: pl.BlockSpec((None, seq, w), lambda bi: (bi, 0, 0))
    const = lambda shape: pl.BlockSpec((None,) + shape, lambda bi: (layer,) + (0,) * len(shape))
    st_spec = pl.BlockSpec((None, 2, DN_DK, wide), lambda bi: (bi, 0, 0, 0))
    in_specs = [per_b(4 * wide), per_b(LANES)]
    args = [zc, zm]
    if has_state:
        in_specs.append(pl.BlockSpec((None, None, 2, DN_DK, wide), lambda bi: (bi, layer, 0, 0, 0)))
        args.append(state)
    in_specs += [const((DN_CONV, DN_QKV)), const((1, LANES)), const((1, LANES)), const((1, wide))]
    args += [lw["dn_conv_w"], lw["dn_alog_row"], lw["dn_dtb_row"], lw["dn_out_g"]]
    out_specs = [per_b(wide)]
    out_shape = [jax.ShapeDtypeStruct((b, seq, wide), BF16)]
    if not has_state:
        out_specs.append(st_spec)
        out_shape.append(jax.ShapeDtypeStruct((b, 2, DN_DK, wide), F32))
    res = pl.pallas_call(
        functools.partial(_deltanet_kernel, seq, has_state),
        grid=(b,),
        in_specs=in_specs,
        out_specs=out_specs,
        out_shape=out_shape,
        scratch_shapes=[pltpu.VMEM((seq, wide), F32), pltpu.VMEM((seq, wide), F32),
                        pltpu.VMEM((seq, wide), F32), pltpu.VMEM((2, seq, wide), F32),
                        pltpu.VMEM((2, seq, wide), F32), pltpu.VMEM((seq, wide), F32),
                        pltpu.VMEM((2, seq, wide), F32), pltpu.VMEM((2, 2 * seq, wide), BF16)],
        compiler_params=_cparams(1),
        name="deltanet_lat" if has_state else "deltanet_ctx",
    )(*args)
    return (res[0], None) if has_state else (res[0], res[1])


def _outffn_kernel(final, oa_ref, ob_ref, oc_ref, od_ref, x_ref, mod_ref, g2_ref, wo_ref, wg_ref,
                   wu_ref, wd_ref, fg_ref, y_ref):
    m = mod_ref[...]
    o = jnp.concatenate([oa_ref[...], ob_ref[...], oc_ref[...], od_ref[...]], axis=-1).astype(BF16)
    x1 = x_ref[...] + m[2:3] * _bdot(o, wo_ref[...])
    h = (_rms_full(x1, g2_ref[...]) * (1.0 + m[4:5]) + m[3:4]).astype(BF16)
    act = (_silu(_bdot(h, wg_ref[...])) * _bdot(h, wu_ref[...])).astype(BF16)
    x2 = x1 + m[5:6] * _bdot(act, wd_ref[...])
    y_ref[...] = _rms_full(x2, fg_ref[...]) if final else x2


def _outffn(outs, x, mods, per_batch_mods, lw, layer, final_g, final):
    b, l, _ = x.shape
    tm = ROW_TILE
    row = lambda w: pl.BlockSpec((None, tm, w), lambda bi, i: (bi, i, 0))
    const = lambda shape: pl.BlockSpec((None,) + shape, lambda bi, i: (layer,) + (0,) * len(shape),
                                       pipeline_mode=pl.Buffered(1))
    mod_spec = pl.BlockSpec((None, None, MOD_CHUNKS, D_MODEL),
                            (lambda bi, i: (layer, 1 + bi, 0, 0)) if per_batch_mods
                            else (lambda bi, i: (layer, 0, 0, 0)))
    return pl.pallas_call(
        functools.partial(_outffn_kernel, final),
        grid=(b, l // tm),
        in_specs=[row(256), row(256), row(256), row(256), row(D_MODEL), mod_spec, const((1, D_MODEL)),
                  const((D_MODEL, D_MODEL)), const((D_MODEL, D_FF)), const((D_MODEL, D_FF)),
                  const((D_FF, D_MODEL)), pl.BlockSpec((1, D_MODEL), lambda bi, i: (0, 0))],
        out_specs=row(D_MODEL),
        out_shape=jax.ShapeDtypeStruct((b, l, D_MODEL), F32),
        compiler_params=_cparams(2),
        name="outffn",
    )(*outs, x, mods, lw["norm2_g"], lw["w_out"], lw["w_gate"], lw["w_up"], lw["w_down"], final_g)


def _rope_tables(n):
    t = jnp.arange(n)

    def axis(pos, half):
        inv = ROPE_BASE ** (-jnp.arange(half, dtype=F32) / half)
        ang = pos.astype(F32)[:, None] * inv[None, :]
        c, s = jnp.cos(ang), jnp.sin(ang)
        return jnp.concatenate([c, c], -1), jnp.concatenate([-s, s], -1)

    cr, sr = axis(t // GRID_W, 16)
    cc, sc = axis(t % GRID_W, 16)
    cos64 = jnp.tile(jnp.concatenate([cr, cc], -1), (1, 2))
    sin64 = jnp.tile(jnp.concatenate([sr, sc], -1), (1, 2))
    cr, sr = axis(t // GRID_W, 8)
    cc, sc = axis(t % GRID_W, 8)
    cos32, sin32 = jnp.concatenate([cr, cc], -1), jnp.concatenate([sr, sc], -1)
    one, zero = jnp.ones((n, 1), F32), jnp.zeros((n, 1), F32)
    cosm = jnp.concatenate([jnp.tile(one, (1, 64)), cos32, jnp.tile(one, (1, 32))], -1)
    sinm = jnp.concatenate([jnp.tile(zero, (1, 64)), sin32, jnp.tile(zero, (1, 32))], -1)
    coskr = jnp.concatenate([cos32, jnp.tile(one, (1, 96))], -1)
    sinkr = jnp.concatenate([sin32, jnp.tile(zero, (1, 96))], -1)
    return cos64, sin64, cosm, sinm, coskr, sinkr


_QA_ORDER = ((0, 64), (128, 192), (64, 128), (192, 256))


def _stacked_weights(p):
    w_in = p["w_in"]
    pieces = [w_in[:, :, a:b] for a, b in _QA_ORDER + ((256, 2304), (2320, 2736), (2304, 2320))]
    pieces.append(jnp.zeros((DEPTH, D_MODEL, IN_PAD_COLS - w_in.shape[2]), F32))
    w_in = jnp.concatenate(pieces, axis=2).astype(BF16)
    w_out = p["w_out"]
    w_out = jnp.concatenate([w_out[:, a:b] for a, b in _QA_ORDER + ((256, w_out.shape[1]),)], axis=1).astype(BF16)
    wq = p["mla_wq_up"].reshape(DEPTH, MLA_Q_LORA, MLA_HEADS, MLA_NOPE + MLA_ROPE)
    wq = jnp.pad(wq, ((0, 0), (0, 0), (0, 0), (0, LANES - MLA_NOPE - MLA_ROPE))).reshape(DEPTH, MLA_Q_LORA, 4 * LANES)
    wkv = p["mla_wkv_up"].reshape(DEPTH, MLA_KV_LORA, MLA_HEADS, MLA_NOPE + MLA_V)
    wk_top = jnp.pad(wkv[..., :MLA_NOPE], ((0, 0), (0, 0), (0, 0), (0, LANES - MLA_NOPE)))
    place = jnp.pad(jnp.eye(MLA_ROPE, dtype=F32), ((0, LANES - MLA_ROPE), (MLA_NOPE, LANES - MLA_NOPE - MLA_ROPE)))
    wk_bot = jnp.broadcast_to(place[None, :, None, :], (DEPTH, LANES, MLA_HEADS, LANES))
    wk = jnp.concatenate([wk_top, wk_bot], axis=1).reshape(DEPTH, 2 * LANES, 4 * LANES)
    wv = wkv[..., MLA_NOPE:].reshape(DEPTH, MLA_KV_LORA, MLA_HEADS * MLA_V)
    gate_row = lambda v: jnp.pad(v.reshape(DEPTH, 1, 2 * DN_HEADS),
                                 ((0, 0), (0, 0), (MLA_ROPE, LANES - MLA_ROPE - 2 * DN_HEADS)))
    row = lambda v: v[:, None, :]
    return {
        "norm1_g": row(p["norm1_g"]), "norm2_g": row(p["norm2_g"]),
        "w_in": w_in, "w_out": w_out,
        "qn_g": row(jnp.tile(p["gqa_qn_g"], (1, 4))), "kn_g": row(jnp.tile(p["gqa_kn_g"], (1, 2))),
        "mla_qn_g": row(p["mla_qn_g"]), "mla_kvn_g": row(p["mla_kvn_g"]),
        "wq": wq.astype(BF16), "wk": wk.astype(BF16), "wv": wv.astype(BF16),
        "dn_conv_w": p["dn_conv_w"], "dn_alog_row": gate_row(p["dn_a_log"]),
        "dn_dtb_row": gate_row(p["dn_dt_bias"]), "dn_out_g": row(jnp.tile(p["dn_out_g"], (1, DN_HEADS))),
        "w_gate": p["ffn_w_gate"].astype(BF16), "w_up": p["ffn_w_up"].astype(BF16),
        "w_down": p["ffn_w_down"].astype(BF16),
    }


def _state_to_wide(s):
    b = s.shape[0]
    return s.transpose(0, 1, 2, 4, 3, 5).reshape(b, DEPTH, 2, DN_DK, DN_HEADS * DN_DV)


def _state_from_wide(s):
    b = s.shape[0]
    return s.reshape(b, 2, DN_DK, DN_HEADS, DN_DV).transpose(0, 1, 3, 2, 4)


def kernel(x_prompt, x_sample, cache_gqa_k, cache_gqa_v, cache_na_k, cache_na_v, state_dn,
           cache_mla_ckv, cache_mla_krope, c, c_ctx, norm1_g, norm2_g, w_mod, b_mod, w_in, w_out,
           gqa_qn_g, gqa_kn_g, na_bias, dn_conv_w, dn_a_log, dn_dt_bias, dn_out_g, mla_qn_g,
           mla_wq_up, mla_kvn_g, mla_wkv_up, ffn_w_gate, ffn_w_up, ffn_w_down, final_g):
    p = {"norm1_g": norm1_g, "norm2_g": norm2_g, "w_in": w_in, "w_out": w_out, "gqa_qn_g": gqa_qn_g,
         "gqa_kn_g": gqa_kn_g, "dn_conv_w": dn_conv_w, "dn_a_log": dn_a_log, "dn_dt_bias": dn_dt_bias,
         "dn_out_g": dn_out_g, "mla_qn_g": mla_qn_g, "mla_wq_up": mla_wq_up, "mla_kvn_g": mla_kvn_g,
         "mla_wkv_up": mla_wkv_up, "ffn_w_gate": ffn_w_gate, "ffn_w_up": ffn_w_up, "ffn_w_down": ffn_w_down}
    nb_ctx, seq_ctx, _ = x_prompt.shape
    nb_lat, seq_lat, _ = x_sample.shape
    past = cache_gqa_k.shape[2]
    fg = final_g[None]

    cond = jnp.concatenate([c_ctx[None], c, jnp.zeros((16 - 1 - nb_lat, D_MODEL), F32)], axis=0)
    mods = _modulation(cond, w_mod, b_mod).reshape(DEPTH, 16, MOD_CHUNKS, D_MODEL)
    lw = _stacked_weights(p)

    x = x_prompt
    ctx_out = []
    for l in range(DEPTH):
        qa, ka, va, qb, kb, vb, zc, qd, ckv, zm, krr = _inproj(x, mods, False, lw, l, None)
        o_a = _attn_pair(qa, ka, va, (0, 0), name="gqa_ctx")
        o_b = _attn_pair(qb, kb, vb, (0, 1), name="na_ctx")
        o_c, s_dn = _deltanet(zc, zm, lw, l)
        o_d = _mla(qd, ckv, krr, lw, l, name="mla_ctx")
        x = _outffn((o_a, o_b, o_c, o_d), x, mods, False, lw, l, fg, l == DEPTH - 1)
        ctx_out.append((ka.reshape(nb_ctx, seq_ctx, GQA_KV_HEADS, HEAD_DIM),
                        va.reshape(nb_ctx, seq_ctx, GQA_KV_HEADS, HEAD_DIM),
                        kb.reshape(nb_ctx, seq_ctx, NA_HEADS, HEAD_DIM),
                        vb.reshape(nb_ctx, seq_ctx, NA_HEADS, HEAD_DIM),
                        _state_from_wide(s_dn), ckv, zm[:, :, :MLA_ROPE]))
    y_prompt = x
    new = [jnp.stack([s[i] for s in ctx_out], axis=1) for i in range(7)]

    ropes = _rope_tables(seq_lat)
    ck_a = cache_gqa_k.reshape(nb_lat, DEPTH, past, GQA_KV_HEADS * HEAD_DIM)
    cv_a = cache_gqa_v.reshape(nb_lat, DEPTH, past, GQA_KV_HEADS * HEAD_DIM)
    ck_b = cache_na_k.reshape(nb_lat, DEPTH, past, NA_HEADS * HEAD_DIM)
    cv_b = cache_na_v.reshape(nb_lat, DEPTH, past, NA_HEADS * HEAD_DIM)
    c_kr = jnp.pad(cache_mla_krope, ((0, 0), (0, 0), (0, 0), (0, LANES - MLA_ROPE)))
    bias_blocks = _na_bias_blocks(na_bias)
    state = _state_to_wide(state_dn)
    x = x_sample
    for l in range(DEPTH):
        qa, ka, va, qb, kb, vb, zc, qd, ckv, zm, krr = _inproj(x, mods, True, lw, l, ropes)
        o_a = _attn_pair(qa, ka, va, (0, 0), cache=(ck_a, cv_a, l), name="gqa_lat")
        o_b = _na_latent(qb, kb, vb, ck_b, cv_b, l, bias_blocks)
        o_c, _ = _deltanet(zc, zm, lw, l, state=state)
        o_d = _mla(qd, ckv, krr, lw, l, cache=(cache_mla_ckv, c_kr), name="mla_lat")
        x = _outffn((o_a, o_b, o_c, o_d), x, mods, True, lw, l, fg, l == DEPTH - 1)
    y_sample = x

    return (y_prompt, y_sample, *new)
```

```python
import functools

import numpy as np
import jax
import jax.numpy as jnp
from jax import lax
from jax.experimental import pallas as pl
from jax.experimental.pallas import tpu as pltpu

F32 = jnp.float32
BF16 = jnp.bfloat16

D_MODEL = 1024
DEPTH = 2
GRID_W = 64
HEAD_DIM = 64
ROPE_BASE = 10000.0
NEG_INF = -1e30
MOD_CHUNKS = 6
GQA_HEADS, GQA_KV_HEADS = 4, 2
NA_HEADS, NA_KH, NA_KW = 4, 8, 16
DN_HEADS, DN_DK, DN_DV, DN_CONV, DN_CHUNK = 4, 64, 64, 4, 64
DN_QKV = DN_HEADS * (2 * DN_DK + DN_DV)
DN_GATES = 2 * DN_HEADS
MLA_HEADS, MLA_Q_LORA, MLA_KV_LORA, MLA_NOPE, MLA_ROPE, MLA_V = 4, 256, 128, 64, 32, 64
MLA_SCALE = (MLA_NOPE + MLA_ROPE) ** -0.5
D_FF = -(-8 * D_MODEL // (3 * 256)) * 256
EPS = 1e-6
LOG2E = 1.4426950408889634

LANES = 128
ROW_TILE = 256
Q_TILE = 256
NA_BAND = 768
TILES_PER_STEP = 2
ATTN_Q_TILE = 512
HEAD_LOOKAHEAD = 1
DN_ROWS = 256
PREP_CHUNKS = 4
IN_PAD_COLS = 2816
VMEM_LIMIT = 56 * 1024 * 1024

_NT = (((1,), (1,)), ((), ()))
_TN = (((0,), (0,)), ((), ()))


def _cparams(n_axes):
    return pltpu.CompilerParams(dimension_semantics=("arbitrary",) * n_axes,
                                vmem_limit_bytes=VMEM_LIMIT)


def _lane(shape):
    return lax.broadcasted_iota(jnp.int32, shape, len(shape) - 1)


def _silu(x):
    return x / (1.0 + jnp.exp(-x))


def _rms_full(x, g):
    return x * lax.rsqrt(jnp.mean(x * x, axis=-1, keepdims=True) + EPS) * g


def _seg64_sum(x):
    lo = _lane(x.shape) < HEAD_DIM
    s_lo = jnp.sum(jnp.where(lo, x, 0.0), axis=-1, keepdims=True)
    s_hi = jnp.sum(jnp.where(lo, 0.0, x), axis=-1, keepdims=True)
    return jnp.where(lo, s_lo, s_hi)


def _head_rms(x, g):
    parts = []
    for p in range(x.shape[-1] // LANES):
        xp = x[:, p * LANES:(p + 1) * LANES]
        ms = _seg64_sum(xp * xp) * (1.0 / HEAD_DIM)
        parts.append(xp * lax.rsqrt(ms + EPS))
    y = parts[0] if len(parts) == 1 else jnp.concatenate(parts, axis=-1)
    return y * g


def _head_l2(x):
    parts = []
    for p in range(x.shape[-1] // LANES):
        xp = x[:, p * LANES:(p + 1) * LANES]
        parts.append(xp * lax.rsqrt(_seg64_sum(xp * xp) + EPS))
    return parts[0] if len(parts) == 1 else jnp.concatenate(parts, axis=-1)


def _rope(x, cos, sin, half):
    first = (_lane(x.shape) & (2 * half - 1)) < half
    rot = jnp.where(first, pltpu.roll(x, LANES - half, 1), pltpu.roll(x, half, 1))
    return x * cos + rot * sin


def _softmax_parts(scores):
    m = jnp.max(scores[0], axis=-1, keepdims=True)
    for s in scores[1:]:
        m = jnp.maximum(m, jnp.max(s, axis=-1, keepdims=True))
    es = [jnp.exp2(s - m) for s in scores]
    l = jnp.sum(es[0], axis=-1, keepdims=True)
    for e in es[1:]:
        l = l + jnp.sum(e, axis=-1, keepdims=True)
    return es, 1.0 / l


def _bdot(a, b):
    return jnp.dot(a, b, preferred_element_type=F32)


def _pipelined_heads(n_heads, scores, attend):
    outs = []
    queue = [scores(hd) for hd in range(min(HEAD_LOOKAHEAD, n_heads))]
    for hd in range(n_heads):
        if hd + HEAD_LOOKAHEAD < n_heads:
            queue.append(scores(hd + HEAD_LOOKAHEAD))
        es, rl = _softmax_parts(queue.pop(0))
        outs.append(attend(hd, es) * rl)
    return outs


def _mod_kernel(c_ref, w_ref, b_ref, o_ref):
    s = _silu(c_ref[...]).astype(BF16)
    o_ref[...] = _bdot(s, w_ref[...].astype(BF16)) + b_ref[...]


def _modulation(cond, w_mod, b_mod):
    n = MOD_CHUNKS * D_MODEL
    tn = 1536
    return pl.pallas_call(
        _mod_kernel,
        grid=(DEPTH, n // tn),
        in_specs=[pl.BlockSpec((16, D_MODEL), lambda l, j: (0, 0)),
                  pl.BlockSpec((None, D_MODEL, tn), lambda l, j: (l, 0, j)),
                  pl.BlockSpec((None, 1, tn), lambda l, j: (l, 0, j))],
        out_specs=pl.BlockSpec((None, 16, tn), lambda l, j: (l, 0, j)),
        out_shape=jax.ShapeDtypeStruct((DEPTH, 16, n), F32),
        compiler_params=_cparams(2),
        name="modulation",
    )(cond, w_mod, b_mod.reshape(DEPTH, 1, n))


_IN_OUT_WIDTHS = (256, 128, 128, 256, 256, 256, 1024, 512, 128, 128, 128)
_IN_OUT_DTYPES_LAT = (BF16, BF16, BF16, BF16, BF16, BF16, F32, BF16, BF16, F32, BF16)
_IN_OUT_DTYPES_CTX = (BF16, F32, F32, BF16, F32, F32, F32, BF16, F32, F32, F32)


def _inproj_kernel(positioned, *refs):
    (x_ref, mod_ref, g1_ref, w_ref, qng_ref, kng_ref, mqg_ref, wq_ref, mkg_ref) = refs[:9]
    n_in = 9
    if positioned:
        cos64_ref, sin64_ref, cosm_ref, sinm_ref, coskr_ref, sinkr_ref = refs[9:15]
        n_in = 15
    (qa_ref, ka_ref, va_ref, qb_ref, kb_ref, vb_ref, zc_ref, qd_ref, ckv_ref, zm_ref,
     krr_ref) = refs[n_in:]
    tiles = range(TILES_PER_STEP)
    tm = x_ref.shape[1]
    lane = _lane((tm, LANES))
    lo = lane < HEAD_DIM

    m = mod_ref[...]
    hb = jnp.concatenate([(_rms_full(x_ref[t], g1_ref[...]) * (1.0 + m[1:2]) + m[0:1]).astype(BF16)
                          for t in tiles], axis=0)

    def project(c0, c1):
        z = lax.dot_general(hb, w_ref[c0:c1, :], _NT, preferred_element_type=F32)
        return [z[t * tm:(t + 1) * tm] for t in tiles]

    za = project(0, 512)
    zb = project(512, 1280)

    for t in tiles:
        q = _head_rms(za[t][:, 0:256], qng_ref[...])
        k = _head_rms(za[t][:, 256:384], kng_ref[...])
        q0, q1 = q[:, 0:128], q[:, 128:256]
        q0, q1 = jnp.where(lo, q0, pltpu.roll(q1, HEAD_DIM, 1)), jnp.where(lo, pltpu.roll(q0, HEAD_DIM, 1), q1)
        if positioned:
            cos, sin = cos64_ref[t], sin64_ref[t]
            q0, q1 = _rope(q0, cos, sin, 16), _rope(q1, cos, sin, 16)
            k = _rope(k, cos, sin, 16)
        qa_ref[t] = (jnp.concatenate([q0, q1], axis=-1) * (HEAD_DIM ** -0.5 * LOG2E)).astype(qa_ref.dtype)
        ka_ref[t] = k.astype(ka_ref.dtype)
        va_ref[t] = za[t][:, 384:512].astype(va_ref.dtype)

    zd = project(2304, IN_PAD_COLS)
    zc = project(1280, 2304)

    for t in tiles:
        qb_ref[t] = (zb[t][:, 0:256] * (HEAD_DIM ** -0.5 * LOG2E)).astype(qb_ref.dtype)
        kb_ref[t] = zb[t][:, 256:512].astype(kb_ref.dtype)
        vb_ref[t] = zb[t][:, 512:768].astype(vb_ref.dtype)

    shifted = []
    for t in tiles:
        rolled = [pltpu.roll(zd[t][:, j * LANES:(j + 1) * LANES], LANES - 2 * DN_GATES, 1) for j in range(4)]
        keep = lane < LANES - 2 * DN_GATES
        shifted.append([jnp.where(keep, rolled[j], rolled[(j + 1) % 4]) for j in range(4)])

    cq = jnp.concatenate([_rms_full(jnp.concatenate(shifted[t][0:2], axis=-1), mqg_ref[...]).astype(BF16)
                          for t in tiles], axis=0)
    qm = _bdot(cq, wq_ref[...])
    for t in tiles:
        zc_ref[t] = zc[t]
        zm_ref[t] = zd[t][:, 0:LANES]
        ckv_ref[t] = _rms_full(shifted[t][2], mkg_ref[...]).astype(ckv_ref.dtype)
        kr = jnp.where(lane < MLA_ROPE, shifted[t][3], 0.0)
        q = qm[t * tm:(t + 1) * tm]
        if positioned:
            kr = _rope(kr, coskr_ref[t], sinkr_ref[t], 8)
            cm, sm = cosm_ref[t], sinm_ref[t]
            q = jnp.concatenate([_rope(q[:, i * LANES:(i + 1) * LANES], cm, sm, 8)
                                 for i in range(MLA_HEADS)], axis=-1)
        krr_ref[t] = kr.astype(krr_ref.dtype)
        qd_ref[t] = (q * (MLA_SCALE * LOG2E)).astype(qd_ref.dtype)


def _inproj(x, mods, per_batch_mods, lw, layer, ropes):
    b, l, _ = x.shape
    tm, ts = ROW_TILE, TILES_PER_STEP
    n_tiles = b * l // tm
    tiles_per_seq = l // tm
    positioned = ropes is not None
    row = lambda w: pl.BlockSpec((ts, tm, w), lambda i: (i, 0, 0))
    const = lambda shape: pl.BlockSpec((None,) + shape, lambda i: (layer,) + (0,) * len(shape))
    assert not per_batch_mods or tiles_per_seq % ts == 0
    mod_spec = pl.BlockSpec((None, None, MOD_CHUNKS, D_MODEL),
                            (lambda i: (layer, 1 + i * ts // tiles_per_seq, 0, 0)) if per_batch_mods
                            else (lambda i: (layer, 0, 0, 0)))
    in_specs = [row(D_MODEL), mod_spec, const((1, D_MODEL)), const((IN_PAD_COLS, D_MODEL)),
                const((1, 256)), const((1, 128)), const((1, MLA_Q_LORA)),
                const((MLA_Q_LORA, 4 * LANES)), const((1, MLA_KV_LORA))]
    args = [x.reshape(n_tiles, tm, D_MODEL), mods, lw["norm1_g"], lw["w_in"], lw["qn_g"], lw["kn_g"],
            lw["mla_qn_g"], lw["wq"], lw["mla_kvn_g"]]
    if positioned:
        steps_per_seq = tiles_per_seq // ts
        in_specs += [pl.BlockSpec((ts, tm, LANES), lambda i: (i % steps_per_seq, 0, 0))] * 6
        args += [r.reshape(tiles_per_seq, tm, LANES) for r in ropes]
    outs = pl.pallas_call(
        functools.partial(_inproj_kernel, positioned),
        grid=(n_tiles // ts,),
        in_specs=in_specs,
        out_specs=[row(w) for w in _IN_OUT_WIDTHS],
        out_shape=[jax.ShapeDtypeStruct((n_tiles, tm, w), dt)
                   for w, dt in zip(_IN_OUT_WIDTHS, _IN_OUT_DTYPES_LAT if positioned else _IN_OUT_DTYPES_CTX)],
        compiler_params=_cparams(1),
        name="inproj_lat" if positioned else "inproj_ctx",
    )(*args)
    return [o.reshape(b, l, w) for o, w in zip(outs, _IN_OUT_WIDTHS)]


def _attn_pair_kernel(has_cache, qmap, *refs):
    if has_cache:
        q_ref, kc_ref, vc_ref, k_ref, v_ref, o_ref, kbuf, vbuf = refs
    else:
        q_ref, k_ref, v_ref, o_ref, kbuf, vbuf = refs

    @pl.when(pl.program_id(1) == 0)
    def _():
        off = 0
        if has_cache:
            off = kc_ref.shape[0]
            kbuf[0:off, :] = kc_ref[...].astype(BF16)
            vbuf[0:off, :] = vc_ref[...].astype(BF16)
        kbuf[off:, :] = k_ref[...].astype(BF16)
        vbuf[off:, :] = v_ref[...].astype(BF16)

    tq = q_ref.shape[0]
    lo = _lane((tq, LANES)) < HEAD_DIM

    def scores(hd):
        p, half = divmod(hd, 2)
        qp = q_ref[:, p * LANES:(p + 1) * LANES].astype(F32)
        qm = jnp.where(lo if half == 0 else jnp.logical_not(lo), qp, 0.0).astype(BF16)
        return [lax.dot_general(qm, kbuf[:, qmap[p] * LANES:(qmap[p] + 1) * LANES], _NT,
                                preferred_element_type=F32)]

    def attend(hd, es):
        kv = qmap[hd // 2]
        return _bdot(es[0].astype(BF16), vbuf[:, kv * LANES:(kv + 1) * LANES])

    outs = _pipelined_heads(2 * len(qmap), scores, attend)
    for p in range(len(qmap)):
        o_ref[:, p * LANES:(p + 1) * LANES] = jnp.where(lo, outs[2 * p], outs[2 * p + 1]).astype(o_ref.dtype)


def _attn_pair(q, k, v, qmap, cache=None, name="attn_pair"):
    b, lq, wq = q.shape
    ls, wk = k.shape[1], k.shape[2]
    tq = min(ATTN_Q_TILE, lq)
    lc = 0 if cache is None else cache[0].shape[2]
    in_specs = [pl.BlockSpec((None, tq, wq), lambda bi, i: (bi, i, 0))]
    args = [q]
    if cache is not None:
        kc, vc, layer = cache
        cspec = pl.BlockSpec((None, None, lc, wk), lambda bi, i: (bi, layer, 0, 0))
        in_specs += [cspec, cspec]
        args += [kc, vc]
    sspec = pl.BlockSpec((None, ls, wk), lambda bi, i: (bi, 0, 0))
    in_specs += [sspec, sspec]
    args += [k, v]
    return pl.pallas_call(
        functools.partial(_attn_pair_kernel, cache is not None, qmap),
        grid=(b, lq // tq),
        in_specs=in_specs,
        out_specs=pl.BlockSpec((None, tq, wq), lambda bi, i: (bi, i, 0)),
        out_shape=jax.ShapeDtypeStruct((b, lq, wq), BF16),
        scratch_shapes=[pltpu.VMEM((lc + ls, wk), BF16), pltpu.VMEM((lc + ls, wk), BF16)],
        compiler_params=_cparams(2),
        name=name,
    )(*args)


def _mla_kernel(has_cache, *refs):
    if has_cache:
        q_ref, ckvc_ref, krc_ref, ckv_ref, kr_ref, wk_ref, wv_ref, o_ref, kbuf, vbuf = refs
    else:
        q_ref, ckv_ref, kr_ref, wk_ref, wv_ref, o_ref, kbuf, vbuf = refs

    @pl.when(pl.program_id(1) == 0)
    def _():
        def expand(c_ref, r_ref, r0, r1):
            c = c_ref[...].astype(BF16)
            ckr = jnp.concatenate([c, r_ref[...].astype(BF16)], axis=-1)
            kbuf[r0:r1, :] = _bdot(ckr, wk_ref[...]).astype(BF16)
            vbuf[r0:r1, :] = _bdot(c, wv_ref[...]).astype(BF16)
        off = 0
        if has_cache:
            off = ckvc_ref.shape[0]
            expand(ckvc_ref, krc_ref, 0, off)
        expand(ckv_ref, kr_ref, off, kbuf.shape[0])

    tq = q_ref.shape[0]
    lo = _lane((tq, LANES)) < MLA_V

    def scores(hd):
        qh = q_ref[:, hd * LANES:(hd + 1) * LANES].astype(BF16)
        return [lax.dot_general(qh, kbuf[:, hd * LANES:(hd + 1) * LANES], _NT, preferred_element_type=F32)]

    def attend(hd, es):
        p = hd // 2
        return _bdot(es[0].astype(BF16), vbuf[:, p * LANES:(p + 1) * LANES])

    outs = _pipelined_heads(MLA_HEADS, scores, attend)
    for p in range(MLA_HEADS // 2):
        o_ref[:, p * LANES:(p + 1) * LANES] = jnp.where(lo, outs[2 * p], outs[2 * p + 1]).astype(o_ref.dtype)


def _mla(q, ckv, kr, lw, layer, cache=None, name="mla"):
    b, lq, wq = q.shape
    ls = ckv.shape[1]
    tq = min(ATTN_Q_TILE, lq)
    lc = 0 if cache is None else cache[0].shape[2]
    in_specs = [pl.BlockSpec((None, tq, wq), lambda bi, i: (bi, i, 0))]
    args = [q]
    if cache is not None:
        ckvc, krc = cache
        cspec = pl.BlockSpec((None, None, lc, LANES), lambda bi, i: (bi, layer, 0, 0))
        in_specs += [cspec, cspec]
        args += [ckvc, krc]
    sspec = pl.BlockSpec((None, ls, LANES), lambda bi, i: (bi, 0, 0))
    in_specs += [sspec, sspec,
                 pl.BlockSpec((None, 2 * LANES, 4 * LANES), lambda bi, i: (layer, 0, 0)),
                 pl.BlockSpec((None, LANES, 2 * LANES), lambda bi, i: (layer, 0, 0))]
    args += [ckv, kr, lw["wk"], lw["wv"]]
    return pl.pallas_call(
        functools.partial(_mla_kernel, cache is not None),
        grid=(b, lq // tq),
        in_specs=in_specs,
        out_specs=pl.BlockSpec((None, tq, 2 * LANES), lambda bi, i: (bi, i, 0)),
        out_shape=jax.ShapeDtypeStruct((b, lq, 2 * LANES), BF16),
        scratch_shapes=[pltpu.VMEM((lc + ls, 4 * LANES), BF16), pltpu.VMEM((lc + ls, 2 * LANES), BF16)],
        compiler_params=_cparams(2),
        name=name,
    )(*args)


def _na_kernel(q_ref, k_ref, v_ref, kc_ref, vc_ref, bias_ref, o_ref):
    j = pl.program_id(0)
    start = pl.multiple_of((j >> 1) * 256, 256)
    kband = k_ref[pl.ds(start, NA_BAND), :].astype(BF16)
    vband = v_ref[pl.ds(start, NA_BAND), :].astype(BF16)
    kc = kc_ref[...].astype(BF16)
    vc = vc_ref[...].astype(BF16)
    tq = q_ref.shape[0]
    lo = _lane((tq, LANES)) < HEAD_DIM

    def scores(hd):
        p, half = divmod(hd, 2)
        sl = slice(p * LANES, (p + 1) * LANES)
        qm = jnp.where(lo if half == 0 else jnp.logical_not(lo), q_ref[:, sl].astype(F32), 0.0).astype(BF16)
        s_loc = lax.dot_general(qm, kband[:, sl], _NT, preferred_element_type=F32) + bias_ref[hd]
        return [s_loc, lax.dot_general(qm, kc[:, sl], _NT, preferred_element_type=F32)]

    def attend(hd, es):
        sl = slice((hd // 2) * LANES, (hd // 2 + 1) * LANES)
        return _bdot(es[0].astype(BF16), vband[:, sl]) + _bdot(es[1].astype(BF16), vc[:, sl])

    outs = _pipelined_heads(NA_HEADS, scores, attend)
    for p in range(NA_HEADS // 2):
        o_ref[:, p * LANES:(p + 1) * LANES] = jnp.where(lo, outs[2 * p], outs[2 * p + 1]).astype(o_ref.dtype)


def _na_latent(q, k, v, kc, vc, layer, bias_blocks):
    b, n, w = q.shape
    lc = kc.shape[2]
    nq = n // Q_TILE
    full = pl.BlockSpec((None, n, w), lambda j, bi: (bi, 0, 0))
    cspec = pl.BlockSpec((None, None, lc, w), lambda j, bi: (bi, layer, 0, 0))
    return pl.pallas_call(
        _na_kernel,
        grid=(nq, b),
        in_specs=[pl.BlockSpec((None, Q_TILE, w), lambda j, bi: (bi, j, 0)), full, full, cspec, cspec,
                  pl.BlockSpec((None, None, NA_HEADS, Q_TILE, NA_BAND), lambda j, bi: (layer, j, 0, 0, 0))],
        out_specs=pl.BlockSpec((None, Q_TILE, w), lambda j, bi: (bi, j, 0)),
        out_shape=jax.ShapeDtypeStruct((b, n, w), BF16),
        compiler_params=_cparams(2),
        name="na_latent",
    )(q, k, v, kc, vc, bias_blocks)


NA_GRID_ROWS = 16
NA_BAND_ROW0 = (0, 0, 4, 4)


def _na_bias_kernel(b_ref, o_ref, tp_s):
    hd = pl.program_id(0)
    n_dr, n_dc = 2 * NA_KH - 1, 2 * NA_KW - 1
    shape = (GRID_W, LANES)
    c = lax.broadcasted_iota(jnp.int32, shape, 0)
    lane = _lane(shape)
    kc = lane & (GRID_W - 1)
    lo = lane < GRID_W
    diff = kc - c + (NA_KW - 1)
    c0 = jnp.clip(c - NA_KW // 2, 0, GRID_W - NA_KW)
    col_ok = (kc >= c0) & (kc < c0 + NA_KW)
    neg = jnp.full(shape, NEG_INF, F32)
    for dr0 in range(-1, n_dr):
        acc = neg
        for d in range(n_dc):
            v_lo = b_ref[hd * n_dr + dr0, d] if dr0 >= 0 else 0.0
            v_hi = b_ref[hd * n_dr + dr0 + 1, d] if dr0 + 1 < n_dr else 0.0
            acc = jnp.where(diff == d, jnp.where(lo, v_lo, v_hi), acc)
        tp_s[dr0 + 1] = jnp.where(col_ok, acc * LOG2E, NEG_INF)
    for j in range(NA_GRID_ROWS // 4):
        for ri in range(4):
            r = 4 * j + ri
            r0 = min(max(r - NA_KH // 2, 0), NA_GRID_ROWS - NA_KH)
            for kp in range(NA_BAND // LANES):
                kr = NA_BAND_ROW0[j] + 2 * kp
                ok_lo, ok_hi = r0 <= kr < r0 + NA_KH, r0 <= kr + 1 < r0 + NA_KH
                dr0 = kr - r + (NA_KH - 1)
                if ok_lo and ok_hi:
                    t = tp_s[dr0 + 1]
                elif ok_lo:
                    t = jnp.where(lo, tp_s[dr0 + 1], NEG_INF)
                elif ok_hi:
                    t = jnp.where(lo, NEG_INF, tp_s[dr0 + 1])
                else:
                    t = neg
                o_ref[j, ri * GRID_W:(ri + 1) * GRID_W, kp * LANES:(kp + 1) * LANES] = t


def _na_bias_blocks(bias):
    nq = NA_GRID_ROWS // 4
    return pl.pallas_call(
        _na_bias_kernel,
        grid=(DEPTH * NA_HEADS,),
        in_specs=[pl.BlockSpec(memory_space=pltpu.SMEM)],
        out_specs=pl.BlockSpec((None, nq, None, Q_TILE, NA_BAND),
                               lambda i: (i // NA_HEADS, 0, i % NA_HEADS, 0, 0)),
        out_shape=jax.ShapeDtypeStruct((DEPTH, nq, NA_HEADS, Q_TILE, NA_BAND), F32),
        scratch_shapes=[pltpu.VMEM((2 * NA_KH, GRID_W, LANES), F32)],
        compiler_params=_cparams(1),
        name="na_bias",
    )(bias.reshape(DEPTH * NA_HEADS * (2 * NA_KH - 1), 2 * NA_KW - 1))


def _widen(cols, n):
    blk = _lane((n, DN_HEADS * DN_DV)) >> 6
    return jnp.where(blk == 0, cols[0], jnp.where(blk == 1, cols[1], jnp.where(blk == 2, cols[2], cols[3])))


def _deltanet_kernel(seq, has_state, *refs):
    if has_state:
        (zc_ref, zm_ref, s0_ref, cw_ref, alog_ref, dtb_ref, og_ref, o_ref,
         q_s, k_s, v_s, b_s, g_s, o_s, c_s, mp_s) = refs
    else:
        (zc_ref, zm_ref, cw_ref, alog_ref, dtb_ref, og_ref, o_ref, sfin_ref,
         q_s, k_s, v_s, b_s, g_s, o_s, c_s, mp_s) = refs
    n_chunks = seq // DN_CHUNK
    wide = DN_HEADS * DN_DV

    a_off, b_off = 0, DN_GATES
    ri = lax.broadcasted_iota(jnp.int32, (DN_ROWS, DN_ROWS), 0)
    ci = lax.broadcasted_iota(jnp.int32, (DN_ROWS, DN_ROWS), 1)
    same = (ri >> 6) == (ci >> 6)
    tri = [jnp.tile(jnp.where(same & ((ci <= ri) if d == 0 else (ci >= ri)), 1.0, 0.0).astype(BF16), (1, 3))
           for d in range(2)]
    n_blocks = seq // DN_ROWS

    def split3(x):
        hi = x.astype(BF16)
        rest = x - hi.astype(F32)
        mid = rest.astype(BF16)
        return jnp.concatenate([hi, mid, (rest - mid.astype(F32)).astype(BF16)], axis=0)

    def preprocess(rb, carry):
        r0 = pl.multiple_of(rb * DN_ROWS, DN_ROWS)
        rows = pl.ds(r0, DN_ROWS)
        before = pl.ds(pl.multiple_of(jnp.maximum(r0 - 8, 0), 8), 8)
        after = pl.ds(pl.multiple_of(jnp.minimum(r0 + DN_ROWS, seq - 8), 8), 8)
        for part, dst in enumerate((q_s, k_s, v_s)):
            cs = slice(part * wide, (part + 1) * wide)
            head = jnp.where(rb > 0, zc_ref[before, cs], 0.0)
            tail = jnp.where(rb < n_blocks - 1, zc_ref[after, cs], 0.0)
            xe = jnp.concatenate([head, zc_ref[rows, cs], tail], axis=0)
            w = cw_ref[:, cs]
            y = (w[0:1] * xe[7:7 + DN_ROWS] + w[1:2] * xe[8:8 + DN_ROWS]
                 + w[2:3] * xe[9:9 + DN_ROWS] + w[3:4] * xe[10:10 + DN_ROWS])
            y = _silu(y)
            if part == 0:
                y = _head_l2(y) * (DN_DK ** -0.5)
            elif part == 1:
                y = _head_l2(y)
            dst[rows, :] = y
        zm = zm_ref[rows, :]
        xa = zm + dtb_ref[...]
        logd = -jnp.exp(alog_ref[...]) * (jnp.maximum(xa, 0.0) + jnp.log1p(jnp.exp(-jnp.abs(xa))))
        beta = 1.0 / (1.0 + jnp.exp(-zm))
        logd3 = split3(logd)
        for d in range(2):
            b_s[d, rows, :] = _widen([beta[:, b_off + 4 * d + hd:b_off + 4 * d + hd + 1]
                                      for hd in range(DN_HEADS)], DN_ROWS)
            g = _bdot(tri[d], logd3)
            g_s[d, rows, :] = _widen([g[:, a_off + 4 * d + hd:a_off + 4 * d + hd + 1]
                                      for hd in range(DN_HEADS)], DN_ROWS)
        return carry

    lax.fori_loop(0, n_blocks, preprocess, 0)

    ii = lax.broadcasted_iota(jnp.int32, (DN_CHUNK, wide), 0)
    jj = _lane((DN_CHUNK, wide)) & (DN_CHUNK - 1)
    blk = _lane((DN_CHUNK, wide)) >> 6
    diag = ii == jj
    eye = jnp.where(diag, 1.0, 0.0)
    head_mask = [jnp.where(blk == hd, 1.0, 0.0).astype(BF16) for hd in range(DN_HEADS)]

    def bd(z):
        zb = z.astype(BF16)
        return jnp.concatenate([zb * hm for hm in head_mask], axis=0)

    def fold(gram):
        out = jnp.where(blk == 0, gram[0:DN_CHUNK], 0.0)
        for hd in range(1, DN_HEADS):
            out = out + jnp.where(blk == hd, gram[hd * DN_CHUNK:(hd + 1) * DN_CHUNK], 0.0)
        return out

    tri_masks = []
    for d in range(2):
        incl = (jj <= ii) if d == 0 else (jj >= ii)
        strict = (jj < ii) if d == 0 else (jj > ii)
        pair = [((ii >> (lvl + 1)) == (jj >> (lvl + 1)))
                & (((ii >> lvl) & 1) == (1 - d)) & (((jj >> lvl) & 1) == d) for lvl in range(6)]
        tri_masks.append((incl, strict, pair))

    def prepare(step, carry):
        chunks = [step * PREP_CHUNKS + i for i in range(PREP_CHUNKS)]
        rows = [pl.ds(pl.multiple_of(c * DN_CHUNK, DN_CHUNK), DN_CHUNK) for c in chunks]
        qkv = [(q_s[r, :], k_s[r, :], v_s[r, :]) for r in rows]
        inst = [(ci, d) for ci in range(PREP_CHUNKS) for d in range(2)]
        beta = {(ci, d): b_s[d, rows[ci], :] for ci, d in inst}
        kb = {(ci, d): qkv[ci][1] * beta[ci, d] for ci, d in inst}
        r = [lax.dot_general(jnp.concatenate([kb[ci, 0], kb[ci, 1], qkv[ci][0]], axis=0).astype(BF16),
                             bd(qkv[ci][1]), _NT, preferred_element_type=F32)
             for ci in range(PREP_CHUNKS)]
        g, a, qk, eg, t = {}, {}, {}, {}, {}
        for ci, d in inst:
            incl, strict, pair = tri_masks[d]
            g[ci, d] = g_s[d, rows[ci], :]
            g_row = jnp.sum(jnp.where(diag, g[ci, d], 0.0), axis=0, keepdims=True)
            dm = jnp.where(incl, jnp.exp(jnp.where(incl, g[ci, d] - g_row, 0.0)), 0.0)
            a[ci, d] = jnp.where(strict, r[ci][d * DN_CHUNK:(d + 1) * DN_CHUNK] * dm, 0.0)
            qk[ci, d] = (r[ci][2 * DN_CHUNK:] * dm).astype(BF16)
            eg[ci, d] = jnp.exp(g[ci, d])
            t[ci, d] = eye - jnp.where(pair[0], a[ci, d], 0.0)
        for lvl in range(1, 6):
            te = {i: _bdot(t[i].astype(BF16), bd(jnp.where(tri_masks[i[1]][2][lvl], a[i], 0.0))) for i in inst}
            t = {i: t[i] - _bdot(te[i].astype(BF16), bd(t[i])) for i in inst}
        nb = {i: jnp.where(diag, 0.0, t[i]).astype(BF16) for i in inst}
        rhs_u = {(ci, d): qkv[ci][2] * beta[ci, d] for ci, d in inst}
        rhs_w = {i: kb[i] * eg[i] for i in inst}
        u = {i: rhs_u[i] + _bdot(nb[i], bd(rhs_u[i])) for i in inst}
        w = {i: rhs_w[i] + _bdot(nb[i], bd(rhs_w[i])) for i in inst}
        kd = {}
        for ci, d in inst:
            g_last = g[ci, d][DN_CHUNK - 1:DN_CHUNK] if d == 0 else g[ci, d][0:1]
            kd[ci, d] = (qkv[ci][1] * jnp.exp(g_last - g[ci, d])).astype(BF16)
        p = {(ci, d): qkv[ci][0] * eg[ci, d] - _bdot(qk[ci, d], bd(w[ci, d])) for ci, d in inst}
        o0 = {i: _bdot(qk[i], bd(u[i])) for i in inst}
        m = {i: fold(lax.dot_general(kd[i], w[i].astype(BF16), _TN, preferred_element_type=F32)) for i in inst}
        cc = {i: fold(lax.dot_general(kd[i], u[i].astype(BF16), _TN, preferred_element_type=F32)) for i in inst}
        for ci, d in inst:
            mrow = pl.ds(pl.multiple_of(chunks[ci] * (2 * DN_CHUNK), 2 * DN_CHUNK), 2 * DN_CHUNK)
            c_s[d, rows[ci], :] = cc[ci, d]
            mp_s[d, mrow, :] = jnp.concatenate([m[ci, d], p[ci, d]], axis=0).astype(BF16)
        for ci in range(PREP_CHUNKS):
            o_s[rows[ci], :] = o0[ci, 0] + o0[ci, 1]
        return carry

    lax.fori_loop(0, n_chunks // PREP_CHUNKS, prepare, 0)

    def scan(i, states):
        new = []
        for d in range(2):
            c = i if d == 0 else n_chunks - 1 - i
            r0 = pl.multiple_of(c * DN_CHUNK, DN_CHUNK)
            rows = pl.ds(r0, DN_CHUNK)
            mrow = pl.ds(pl.multiple_of(c * (2 * DN_CHUNK), 2 * DN_CHUNK), 2 * DN_CHUNK)
            edge = pl.ds(pl.multiple_of(r0 + (DN_CHUNK - 8 if d == 0 else 0), 8), 8)
            g_edge = g_s[d, edge, :]
            g_last = g_edge[7:8] if d == 0 else g_edge[0:1]
            res = _bdot(mp_s[d, mrow, :], bd(states[d]))
            o_s[rows, :] = o_s[rows, :] + res[DN_CHUNK:]
            new.append(states[d] * jnp.exp(g_last) - res[0:DN_CHUNK] + c_s[d, rows, :])
        return tuple(new)

    init = tuple(s0_ref[d] if has_state else jnp.zeros((DN_DK, wide), F32) for d in range(2))
    fin = lax.fori_loop(0, n_chunks, scan, init)
    if not has_state:
        sfin_ref[0] = fin[0]
        sfin_ref[1] = fin[1]

    o_ref[...] = (_head_rms(o_s[...], og_ref[...]) * _silu(zc_ref[:, 3 * wide:4 * wide])).astype(o_ref.dtype)


def _deltanet(zc, zm, lw, layer, state=None):
    b, seq, _ = zc.shape
    wide = DN_HEADS * DN_DV
    has_state = state is not None
    per_b = lambda w: pl.BlockSpec((None, seq, w), lambda bi: (bi, 0, 0))
    const = lambda shape: pl.BlockSpec((None,) + shape, lambda bi: (layer,) + (0,) * len(shape))
    st_spec = pl.BlockSpec((None, 2, DN_DK, wide), lambda bi: (bi, 0, 0, 0))
    in_specs = [per_b(4 * wide), per_b(LANES)]
    args = [zc, zm]
    if has_state:
        in_specs.append(pl.BlockSpec((None, None, 2, DN_DK, wide), lambda bi: (bi, layer, 0, 0, 0)))
        args.append(state)
    in_specs += [const((DN_CONV, DN_QKV)), const((1, LANES)), const((1, LANES)), const((1, wide))]
    args += [lw["dn_conv_w"], lw["dn_alog_row"], lw["dn_dtb_row"], lw["dn_out_g"]]
    out_specs = [per_b(wide)]
    out_shape = [jax.ShapeDtypeStruct((b, seq, wide), BF16)]
    if not has_state:
        out_specs.append(st_spec)
        out_shape.append(jax.ShapeDtypeStruct((b, 2, DN_DK, wide), F32))
    res = pl.pallas_call(
        functools.partial(_deltanet_kernel, seq, has_state),
        grid=(b,),
        in_specs=in_specs,
        out_specs=out_specs,
        out_shape=out_shape,
        scratch_shapes=[pltpu.VMEM((seq, wide), F32), pltpu.VMEM((seq, wide), F32),
                        pltpu.VMEM((seq, wide), F32), pltpu.VMEM((2, seq, wide), F32),
                        pltpu.VMEM((2, seq, wide), F32), pltpu.VMEM((seq, wide), F32),
                        pltpu.VMEM((2, seq, wide), F32), pltpu.VMEM((2, 2 * seq, wide), BF16)],
        compiler_params=_cparams(1),
        name="deltanet_lat" if has_state else "deltanet_ctx",
    )(*args)
    return (res[0], None) if has_state else (res[0], res[1])


def _outffn_kernel(final, oa_ref, ob_ref, oc_ref, od_ref, x_ref, mod_ref, g2_ref, wo_ref, wg_ref,
                   wu_ref, wd_ref, fg_ref, y_ref):
    tiles = range(TILES_PER_STEP)
    m = mod_ref[...]
    o = [jnp.concatenate([oa_ref[t], ob_ref[t], oc_ref[t], od_ref[t]], axis=-1).astype(BF16) for t in tiles]
    x1 = [x_ref[t] + m[2:3] * _bdot(o[t], wo_ref[...]) for t in tiles]
    h = [(_rms_full(x1[t], g2_ref[...]) * (1.0 + m[4:5]) + m[3:4]).astype(BF16) for t in tiles]
    gate = [_bdot(h[t], wg_ref[...]) for t in tiles]
    up = [_bdot(h[t], wu_ref[...]) for t in tiles]
    act = [(_silu(gate[t]) * up[t]).astype(BF16) for t in tiles]
    x2 = [x1[t] + m[5:6] * _bdot(act[t], wd_ref[...]) for t in tiles]
    for t in tiles:
        y_ref[t] = _rms_full(x2[t], fg_ref[...]) if final else x2[t]


def _outffn(outs, x, mods, per_batch_mods, lw, layer, final_g, final):
    b, l, _ = x.shape
    tm, ts = ROW_TILE, TILES_PER_STEP
    n_tiles = b * l // tm
    tiles_per_seq = l // tm
    assert not per_batch_mods or tiles_per_seq % ts == 0
    row = lambda w: pl.BlockSpec((ts, tm, w), lambda i: (i, 0, 0))
    const = lambda shape: pl.BlockSpec((None,) + shape, lambda i: (layer,) + (0,) * len(shape),
                                       pipeline_mode=pl.Buffered(1))
    mod_spec = pl.BlockSpec((None, None, MOD_CHUNKS, D_MODEL),
                            (lambda i: (layer, 1 + i * ts // tiles_per_seq, 0, 0)) if per_batch_mods
                            else (lambda i: (layer, 0, 0, 0)))
    tiled = lambda a: a.reshape(n_tiles, tm, a.shape[-1])
    y = pl.pallas_call(
        functools.partial(_outffn_kernel, final),
        grid=(n_tiles // ts,),
        in_specs=[row(256), row(256), row(256), row(256), row(D_MODEL), mod_spec, const((1, D_MODEL)),
                  const((D_MODEL, D_MODEL)), const((D_MODEL, D_FF)), const((D_MODEL, D_FF)),
                  const((D_FF, D_MODEL)), pl.BlockSpec((1, D_MODEL), lambda i: (0, 0))],
        out_specs=row(D_MODEL),
        out_shape=jax.ShapeDtypeStruct((n_tiles, tm, D_MODEL), F32),
        compiler_params=_cparams(1),
        name="outffn",
    )(*[tiled(a) for a in outs], tiled(x), mods, lw["norm2_g"], lw["w_out"], lw["w_gate"], lw["w_up"],
      lw["w_down"], final_g)
    return y.reshape(b, l, D_MODEL)


def _rope_tables(n):
    t = jnp.arange(n)

    def axis(pos, half):
        inv = ROPE_BASE ** (-jnp.arange(half, dtype=F32) / half)
        ang = pos.astype(F32)[:, None] * inv[None, :]
        c, s = jnp.cos(ang), jnp.sin(ang)
        return jnp.concatenate([c, c], -1), jnp.concatenate([-s, s], -1)

    cr, sr = axis(t // GRID_W, 16)
    cc, sc = axis(t % GRID_W, 16)
    cos64 = jnp.tile(jnp.concatenate([cr, cc], -1), (1, 2))
    sin64 = jnp.tile(jnp.concatenate([sr, sc], -1), (1, 2))
    cr, sr = axis(t // GRID_W, 8)
    cc, sc = axis(t % GRID_W, 8)
    cos32, sin32 = jnp.concatenate([cr, cc], -1), jnp.concatenate([sr, sc], -1)
    one, zero = jnp.ones((n, 1), F32), jnp.zeros((n, 1), F32)
    cosm = jnp.concatenate([jnp.tile(one, (1, 64)), cos32, jnp.tile(one, (1, 32))], -1)
    sinm = jnp.concatenate([jnp.tile(zero, (1, 64)), sin32, jnp.tile(zero, (1, 32))], -1)
    coskr = jnp.concatenate([cos32, jnp.tile(one, (1, 96))], -1)
    sinkr = jnp.concatenate([sin32, jnp.tile(zero, (1, 96))], -1)
    return cos64, sin64, cosm, sinm, coskr, sinkr


_QA_ORDER = ((0, 64), (128, 192), (64, 128), (192, 256))


def _stacked_weights(p):
    w_in = jnp.swapaxes(p["w_in"], 1, 2).astype(BF16)
    w_in = jnp.pad(w_in, ((0, 0), (0, IN_PAD_COLS - w_in.shape[1]), (0, 0)))
    w_out = p["w_out"]
    w_out = jnp.concatenate([w_out[:, a:b] for a, b in _QA_ORDER + ((256, w_out.shape[1]),)], axis=1).astype(BF16)
    wq = p["mla_wq_up"].reshape(DEPTH, MLA_Q_LORA, MLA_HEADS, MLA_NOPE + MLA_ROPE)
    wq = jnp.pad(wq, ((0, 0), (0, 0), (0, 0), (0, LANES - MLA_NOPE - MLA_ROPE))).reshape(DEPTH, MLA_Q_LORA, 4 * LANES)
    wkv = p["mla_wkv_up"].reshape(DEPTH, MLA_KV_LORA, MLA_HEADS, MLA_NOPE + MLA_V)
    wk_top = jnp.pad(wkv[..., :MLA_NOPE], ((0, 0), (0, 0), (0, 0), (0, LANES - MLA_NOPE)))
    place = jnp.pad(jnp.eye(MLA_ROPE, dtype=F32), ((0, LANES - MLA_ROPE), (MLA_NOPE, LANES - MLA_NOPE - MLA_ROPE)))
    wk_bot = jnp.broadcast_to(place[None, :, None, :], (DEPTH, LANES, MLA_HEADS, LANES))
    wk = jnp.concatenate([wk_top, wk_bot], axis=1).reshape(DEPTH, 2 * LANES, 4 * LANES)
    wv = wkv[..., MLA_NOPE:].reshape(DEPTH, MLA_KV_LORA, MLA_HEADS * MLA_V)
    gate_row = lambda v: jnp.pad(v.reshape(DEPTH, 1, DN_GATES), ((0, 0), (0, 0), (0, LANES - DN_GATES)))
    row = lambda v: v[:, None, :]
    return {
        "norm1_g": row(p["norm1_g"]), "norm2_g": row(p["norm2_g"]),
        "w_in": w_in, "w_out": w_out,
        "qn_g": row(jnp.tile(p["gqa_qn_g"], (1, 4))), "kn_g": row(jnp.tile(p["gqa_kn_g"], (1, 2))),
        "mla_qn_g": row(p["mla_qn_g"]), "mla_kvn_g": row(p["mla_kvn_g"]),
        "wq": wq.astype(BF16), "wk": wk.astype(BF16), "wv": wv.astype(BF16),
        "dn_conv_w": p["dn_conv_w"], "dn_alog_row": gate_row(p["dn_a_log"]),
        "dn_dtb_row": gate_row(p["dn_dt_bias"]), "dn_out_g": row(jnp.tile(p["dn_out_g"], (1, DN_HEADS))),
        "w_gate": p["ffn_w_gate"].astype(BF16), "w_up": p["ffn_w_up"].astype(BF16),
        "w_down": p["ffn_w_down"].astype(BF16),
    }


def _state_to_wide(s):
    b = s.shape[0]
    return s.transpose(0, 1, 2, 4, 3, 5).reshape(b, DEPTH, 2, DN_DK, DN_HEADS * DN_DV)


def _state_from_wide(s):
    b = s.shape[0]
    return s.reshape(b, 2, DN_DK, DN_HEADS, DN_DV).transpose(0, 1, 3, 2, 4)


def kernel(x_prompt, x_sample, cache_gqa_k, cache_gqa_v, cache_na_k, cache_na_v, state_dn,
           cache_mla_ckv, cache_mla_krope, c, c_ctx, norm1_g, norm2_g, w_mod, b_mod, w_in, w_out,
           gqa_qn_g, gqa_kn_g, na_bias, dn_conv_w, dn_a_log, dn_dt_bias, dn_out_g, mla_qn_g,
           mla_wq_up, mla_kvn_g, mla_wkv_up, ffn_w_gate, ffn_w_up, ffn_w_down, final_g):
    p = {"norm1_g": norm1_g, "norm2_g": norm2_g, "w_in": w_in, "w_out": w_out, "gqa_qn_g": gqa_qn_g,
         "gqa_kn_g": gqa_kn_g, "dn_conv_w": dn_conv_w, "dn_a_log": dn_a_log, "dn_dt_bias": dn_dt_bias,
         "dn_out_g": dn_out_g, "mla_qn_g": mla_qn_g, "mla_wq_up": mla_wq_up, "mla_kvn_g": mla_kvn_g,
         "mla_wkv_up": mla_wkv_up, "ffn_w_gate": ffn_w_gate, "ffn_w_up": ffn_w_up, "ffn_w_down": ffn_w_down}
    nb_ctx, seq_ctx, _ = x_prompt.shape
    nb_lat, seq_lat, _ = x_sample.shape
    past = cache_gqa_k.shape[2]
    fg = final_g[None]

    cond = jnp.concatenate([c_ctx[None], c, jnp.zeros((16 - 1 - nb_lat, D_MODEL), F32)], axis=0)
    mods = _modulation(cond, w_mod, b_mod).reshape(DEPTH, 16, MOD_CHUNKS, D_MODEL)
    lw = _stacked_weights(p)

    x = x_prompt
    ctx_out = []
    for l in range(DEPTH):
        qa, ka, va, qb, kb, vb, zc, qd, ckv, zm, krr = _inproj(x, mods, False, lw, l, None)
        o_a = _attn_pair(qa, ka, va, (0, 0), name="gqa_ctx")
        o_b = _attn_pair(qb, kb, vb, (0, 1), name="na_ctx")
        o_c, s_dn = _deltanet(zc, zm, lw, l)
        o_d = _mla(qd, ckv, krr, lw, l, name="mla_ctx")
        x = _outffn((o_a, o_b, o_c, o_d), x, mods, False, lw, l, fg, l == DEPTH - 1)
        ctx_out.append((ka.reshape(nb_ctx, seq_ctx, GQA_KV_HEADS, HEAD_DIM),
                        va.reshape(nb_ctx, seq_ctx, GQA_KV_HEADS, HEAD_DIM),
                        kb.reshape(nb_ctx, seq_ctx, NA_HEADS, HEAD_DIM),
                        vb.reshape(nb_ctx, seq_ctx, NA_HEADS, HEAD_DIM),
                        _state_from_wide(s_dn), ckv, krr[:, :, :MLA_ROPE]))
    y_prompt = x
    new = [jnp.stack([s[i] for s in ctx_out], axis=1) for i in range(7)]

    ropes = _rope_tables(seq_lat)
    ck_a = cache_gqa_k.reshape(nb_lat, DEPTH, past, GQA_KV_HEADS * HEAD_DIM)
    cv_a = cache_gqa_v.reshape(nb_lat, DEPTH, past, GQA_KV_HEADS * HEAD_DIM)
    ck_b = cache_na_k.reshape(nb_lat, DEPTH, past, NA_HEADS * HEAD_DIM)
    cv_b = cache_na_v.reshape(nb_lat, DEPTH, past, NA_HEADS * HEAD_DIM)
    c_kr = jnp.pad(cache_mla_krope, ((0, 0), (0, 0), (0, 0), (0, LANES - MLA_ROPE)))
    bias_blocks = _na_bias_blocks(na_bias)
    state = _state_to_wide(state_dn)
    x = x_sample
    for l in range(DEPTH):
        qa, ka, va, qb, kb, vb, zc, qd, ckv, zm, krr = _inproj(x, mods, True, lw, l, ropes)
        o_a = _attn_pair(qa, ka, va, (0, 0), cache=(ck_a, cv_a, l), name="gqa_lat")
        o_b = _na_latent(qb, kb, vb, ck_b, cv_b, l, bias_blocks)
        o_c, _ = _deltanet(zc, zm, lw, l, state=state)
        o_d = _mla(qd, ckv, krr, lw, l, cache=(cache_mla_ckv, c_kr), name="mla_lat")
        x = _outffn((o_a, o_b, o_c, o_d), x, mods, True, lw, l, fg, l == DEPTH - 1)
    y_sample = x

    return (y_prompt, y_sample, *new)
```

```python
import functools

import numpy as np
import jax
import jax.numpy as jnp
from jax import lax
from jax.experimental import pallas as pl
from jax.experimental.pallas import tpu as pltpu

F32 = jnp.float32
BF16 = jnp.bfloat16

D_MODEL = 1024
DEPTH = 2
GRID_W = 64
HEAD_DIM = 64
ROPE_BASE = 10000.0
NEG_INF = -1e30
MOD_CHUNKS = 6
GQA_HEADS, GQA_KV_HEADS = 4, 2
NA_HEADS, NA_KH, NA_KW = 4, 8, 16
DN_HEADS, DN_DK, DN_DV, DN_CONV, DN_CHUNK = 4, 64, 64, 4, 64
DN_QKV = DN_HEADS * (2 * DN_DK + DN_DV)
DN_GATES = 2 * DN_HEADS
MLA_HEADS, MLA_Q_LORA, MLA_KV_LORA, MLA_NOPE, MLA_ROPE, MLA_V = 4, 256, 128, 64, 32, 64
MLA_SCALE = (MLA_NOPE + MLA_ROPE) ** -0.5
D_FF = -(-8 * D_MODEL // (3 * 256)) * 256
EPS = 1e-6
LOG2E = 1.4426950408889634

LANES = 128
ROW_TILE = 256
Q_TILE = 256
NA_BAND = 768
TILES_PER_STEP = 2
ATTN_Q_TILE = 512
HEAD_LOOKAHEAD = 1
DN_ROWS = 256
PREP_CHUNKS = 4
IN_PAD_COLS = 2816
VMEM_LIMIT = 56 * 1024 * 1024

_NT = (((1,), (1,)), ((), ()))
_TN = (((0,), (0,)), ((), ()))


def _cparams(n_axes):
    return pltpu.CompilerParams(dimension_semantics=("arbitrary",) * n_axes,
                                vmem_limit_bytes=VMEM_LIMIT)


def _lane(shape):
    return lax.broadcasted_iota(jnp.int32, shape, len(shape) - 1)


def _silu(x):
    return x / (1.0 + jnp.exp(-x))


def _rms_full(x, g):
    return x * lax.rsqrt(jnp.mean(x * x, axis=-1, keepdims=True) + EPS) * g


def _seg64_sum(x):
    lo = _lane(x.shape) < HEAD_DIM
    s_lo = jnp.sum(jnp.where(lo, x, 0.0), axis=-1, keepdims=True)
    s_hi = jnp.sum(jnp.where(lo, 0.0, x), axis=-1, keepdims=True)
    return jnp.where(lo, s_lo, s_hi)


def _head_rms(x, g):
    parts = []
    for p in range(x.shape[-1] // LANES):
        xp = x[:, p * LANES:(p + 1) * LANES]
        ms = _seg64_sum(xp * xp) * (1.0 / HEAD_DIM)
        parts.append(xp * lax.rsqrt(ms + EPS))
    y = parts[0] if len(parts) == 1 else jnp.concatenate(parts, axis=-1)
    return y * g


def _head_l2(x):
    parts = []
    for p in range(x.shape[-1] // LANES):
        xp = x[:, p * LANES:(p + 1) * LANES]
        parts.append(xp * lax.rsqrt(_seg64_sum(xp * xp) + EPS))
    return parts[0] if len(parts) == 1 else jnp.concatenate(parts, axis=-1)


def _rope(x, cos, sin, half):
    first = (_lane(x.shape) & (2 * half - 1)) < half
    rot = jnp.where(first, pltpu.roll(x, LANES - half, 1), pltpu.roll(x, half, 1))
    return x * cos + rot * sin


def _softmax_parts(scores):
    m = jnp.max(scores[0], axis=-1, keepdims=True)
    for s in scores[1:]:
        m = jnp.maximum(m, jnp.max(s, axis=-1, keepdims=True))
    es = [jnp.exp2(s - m) for s in scores]
    l = jnp.sum(es[0], axis=-1, keepdims=True)
    for e in es[1:]:
        l = l + jnp.sum(e, axis=-1, keepdims=True)
    return es, 1.0 / l


def _bdot(a, b):
    return jnp.dot(a, b, preferred_element_type=F32)


def _pipelined_heads(n_heads, scores, attend):
    outs = []
    queue = [scores(hd) for hd in range(min(HEAD_LOOKAHEAD, n_heads))]
    for hd in range(n_heads):
        if hd + HEAD_LOOKAHEAD < n_heads:
            queue.append(scores(hd + HEAD_LOOKAHEAD))
        es, rl = _softmax_parts(queue.pop(0))
        outs.append(attend(hd, es) * rl)
    return outs


def _pair_attention(q_blocks, k_block, v_block, qmap):
    lo = _lane(q_blocks[0].shape) < HEAD_DIM

    def scores(hd):
        p, half = divmod(hd, 2)
        qm = jnp.where(lo if half == 0 else jnp.logical_not(lo), q_blocks[p], 0.0).astype(BF16)
        return [lax.dot_general(qm, k_block(qmap[p]), _NT, preferred_element_type=F32)]

    def attend(hd, es):
        return _bdot(es[0].astype(BF16), v_block(qmap[hd // 2]))

    outs = _pipelined_heads(2 * len(qmap), scores, attend)
    return [jnp.where(lo, outs[2 * p], outs[2 * p + 1]) for p in range(len(qmap))]


def _mla_attention(q_heads, k_head, v_block):
    lo = _lane(q_heads[0].shape) < MLA_V

    def scores(hd):
        return [lax.dot_general(q_heads[hd], k_head(hd), _NT, preferred_element_type=F32)]

    def attend(hd, es):
        return _bdot(es[0].astype(BF16), v_block(hd // 2))

    outs = _pipelined_heads(MLA_HEADS, scores, attend)
    return [jnp.where(lo, outs[2 * p], outs[2 * p + 1]) for p in range(MLA_HEADS // 2)]


def _mod_kernel(c_ref, w_ref, b_ref, o_ref):
    s = _silu(c_ref[...]).astype(BF16)
    o_ref[...] = _bdot(s, w_ref[...].astype(BF16)) + b_ref[...]


def _modulation(cond, w_mod, b_mod):
    n = MOD_CHUNKS * D_MODEL
    tn = 1536
    return pl.pallas_call(
        _mod_kernel,
        grid=(DEPTH, n // tn),
        in_specs=[pl.BlockSpec((16, D_MODEL), lambda l, j: (0, 0)),
                  pl.BlockSpec((None, D_MODEL, tn), lambda l, j: (l, 0, j)),
                  pl.BlockSpec((None, 1, tn), lambda l, j: (l, 0, j))],
        out_specs=pl.BlockSpec((None, 16, tn), lambda l, j: (l, 0, j)),
        out_shape=jax.ShapeDtypeStruct((DEPTH, 16, n), F32),
        compiler_params=_cparams(2),
        name="modulation",
    )(cond, w_mod, b_mod.reshape(DEPTH, 1, n))


_IN_OUT_WIDTHS = (256, 128, 128, 256, 256, 256, 1024, 512, 128, 128, 128)
_IN_OUT_WIDTHS_CTX = (256, 128, 128, 256, 256, 256, 1024, 256, 128, 128, 128)
_IN_OUT_DTYPES_LAT = (BF16, BF16, BF16, BF16, BF16, BF16, F32, BF16, BF16, F32, BF16)
_IN_OUT_DTYPES_CTX = (BF16, F32, F32, BF16, F32, F32, F32, BF16, F32, F32, F32)


def _inproj_kernel(positioned, *refs):
    (x_ref, mod_ref, g1_ref, w_ref, qng_ref, kng_ref, mqg_ref, wq_ref, mkg_ref) = refs[:9]
    if positioned:
        cos64_ref, sin64_ref, cosm_ref, sinm_ref, coskr_ref, sinkr_ref = refs[9:15]
        n_in = 15
    else:
        wk_ref, wv_ref = refs[9:11]
        n_in = 11
    (qa_ref, ka_ref, va_ref, qb_ref, kb_ref, vb_ref, zc_ref, qd_ref, ckv_ref, zm_ref,
     krr_ref) = refs[n_in:]
    tiles = range(TILES_PER_STEP)
    tm = x_ref.shape[1]
    lane = _lane((tm, LANES))
    lo = lane < HEAD_DIM

    m = mod_ref[...]
    hb = jnp.concatenate([(_rms_full(x_ref[t], g1_ref[...]) * (1.0 + m[1:2]) + m[0:1]).astype(BF16)
                          for t in tiles], axis=0)

    def project(c0, c1):
        z = lax.dot_general(hb, w_ref[c0:c1, :], _NT, preferred_element_type=F32)
        return [z[t * tm:(t + 1) * tm] for t in tiles]

    za = project(0, 512)
    zb = project(512, 1280)

    for t in tiles:
        q = _head_rms(za[t][:, 0:256], qng_ref[...])
        k = _head_rms(za[t][:, 256:384], kng_ref[...])
        q0, q1 = q[:, 0:128], q[:, 128:256]
        q0, q1 = jnp.where(lo, q0, pltpu.roll(q1, HEAD_DIM, 1)), jnp.where(lo, pltpu.roll(q0, HEAD_DIM, 1), q1)
        if positioned:
            cos, sin = cos64_ref[t], sin64_ref[t]
            q0, q1 = _rope(q0, cos, sin, 16), _rope(q1, cos, sin, 16)
            k = _rope(k, cos, sin, 16)
        q0, q1 = q0 * (HEAD_DIM ** -0.5 * LOG2E), q1 * (HEAD_DIM ** -0.5 * LOG2E)
        v = za[t][:, 384:512]
        if not positioned:
            kb16, vb16 = k.astype(BF16), v.astype(BF16)
            q0, q1 = _pair_attention([q0, q1], lambda i: kb16, lambda i: vb16, (0, 0))
        qa_ref[t] = jnp.concatenate([q0, q1], axis=-1).astype(qa_ref.dtype)
        ka_ref[t] = k.astype(ka_ref.dtype)
        va_ref[t] = v.astype(va_ref.dtype)

    zd = project(2304, IN_PAD_COLS)
    zc = project(1280, 2304)

    for t in tiles:
        q = zb[t][:, 0:256] * (HEAD_DIM ** -0.5 * LOG2E)
        k, v = zb[t][:, 256:512], zb[t][:, 512:768]
        if not positioned:
            kb16, vb16 = k.astype(BF16), v.astype(BF16)
            q = jnp.concatenate(_pair_attention(
                [q[:, 0:LANES], q[:, LANES:]], lambda i, kb16=kb16: kb16[:, i * LANES:(i + 1) * LANES],
                lambda i, vb16=vb16: vb16[:, i * LANES:(i + 1) * LANES], (0, 1)), axis=-1)
        qb_ref[t] = q.astype(qb_ref.dtype)
        kb_ref[t] = k.astype(kb_ref.dtype)
        vb_ref[t] = v.astype(vb_ref.dtype)

    shifted = []
    for t in tiles:
        rolled = [pltpu.roll(zd[t][:, j * LANES:(j + 1) * LANES], LANES - 2 * DN_GATES, 1) for j in range(4)]
        keep = lane < LANES - 2 * DN_GATES
        shifted.append([jnp.where(keep, rolled[j], rolled[(j + 1) % 4]) for j in range(4)])

    cq = jnp.concatenate([_rms_full(jnp.concatenate(shifted[t][0:2], axis=-1), mqg_ref[...]).astype(BF16)
                          for t in tiles], axis=0)
    qm = _bdot(cq, wq_ref[...])
    for t in tiles:
        zc_ref[t] = zc[t]
        zm_ref[t] = zd[t][:, 0:LANES]
        ckv = _rms_full(shifted[t][2], mkg_ref[...])
        ckv_ref[t] = ckv.astype(ckv_ref.dtype)
        kr = jnp.where(lane < MLA_ROPE, shifted[t][3], 0.0)
        q = qm[t * tm:(t + 1) * tm]
        if positioned:
            kr = _rope(kr, coskr_ref[t], sinkr_ref[t], 8)
            cm, sm = cosm_ref[t], sinm_ref[t]
            q = jnp.concatenate([_rope(q[:, i * LANES:(i + 1) * LANES], cm, sm, 8)
                                 for i in range(MLA_HEADS)], axis=-1)
        krr_ref[t] = kr.astype(krr_ref.dtype)
        q = q * (MLA_SCALE * LOG2E)
        if not positioned:
            c16 = ckv.astype(BF16)
            k16 = _bdot(jnp.concatenate([c16, kr.astype(BF16)], axis=-1), wk_ref[...]).astype(BF16)
            v16 = _bdot(c16, wv_ref[...]).astype(BF16)
            q = jnp.concatenate(_mla_attention(
                [q[:, i * LANES:(i + 1) * LANES].astype(BF16) for i in range(MLA_HEADS)],
                lambda i, k16=k16: k16[:, i * LANES:(i + 1) * LANES],
                lambda i, v16=v16: v16[:, i * LANES:(i + 1) * LANES]), axis=-1)
        qd_ref[t] = q.astype(qd_ref.dtype)


def _inproj(x, mods, per_batch_mods, lw, layer, ropes):
    b, l, _ = x.shape
    tm, ts = ROW_TILE, TILES_PER_STEP
    n_tiles = b * l // tm
    tiles_per_seq = l // tm
    positioned = ropes is not None
    row = lambda w: pl.BlockSpec((ts, tm, w), lambda i: (i, 0, 0))
    const = lambda shape: pl.BlockSpec((None,) + shape, lambda i: (layer,) + (0,) * len(shape))
    assert not per_batch_mods or tiles_per_seq % ts == 0
    mod_spec = pl.BlockSpec((None, None, MOD_CHUNKS, D_MODEL),
                            (lambda i: (layer, 1 + i * ts // tiles_per_seq, 0, 0)) if per_batch_mods
                            else (lambda i: (layer, 0, 0, 0)))
    in_specs = [row(D_MODEL), mod_spec, const((1, D_MODEL)), const((IN_PAD_COLS, D_MODEL)),
                const((1, 256)), const((1, 128)), const((1, MLA_Q_LORA)),
                const((MLA_Q_LORA, 4 * LANES)), const((1, MLA_KV_LORA))]
    args = [x.reshape(n_tiles, tm, D_MODEL), mods, lw["norm1_g"], lw["w_in"], lw["qn_g"], lw["kn_g"],
            lw["mla_qn_g"], lw["wq"], lw["mla_kvn_g"]]
    if positioned:
        steps_per_seq = tiles_per_seq // ts
        in_specs += [pl.BlockSpec((ts, tm, LANES), lambda i: (i % steps_per_seq, 0, 0))] * 6
        args += [r.reshape(tiles_per_seq, tm, LANES) for r in ropes]
        widths, dtypes = _IN_OUT_WIDTHS, _IN_OUT_DTYPES_LAT
    else:
        assert tiles_per_seq == 1
        in_specs += [const((2 * LANES, 4 * LANES)), const((LANES, 2 * LANES))]
        args += [lw["wk"], lw["wv"]]
        widths, dtypes = _IN_OUT_WIDTHS_CTX, _IN_OUT_DTYPES_CTX
    outs = pl.pallas_call(
        functools.partial(_inproj_kernel, positioned),
        grid=(n_tiles // ts,),
        in_specs=in_specs,
        out_specs=[row(w) for w in widths],
        out_shape=[jax.ShapeDtypeStruct((n_tiles, tm, w), dt) for w, dt in zip(widths, dtypes)],
        compiler_params=_cparams(1),
        name="inproj_lat" if positioned else "inproj_attn_ctx",
    )(*args)
    return [o.reshape(b, l, w) for o, w in zip(outs, widths)]


def _attn_pair_kernel(has_cache, qmap, *refs):
    if has_cache:
        q_ref, kc_ref, vc_ref, k_ref, v_ref, o_ref, kbuf, vbuf = refs
    else:
        q_ref, k_ref, v_ref, o_ref, kbuf, vbuf = refs

    @pl.when(pl.program_id(1) == 0)
    def _():
        off = 0
        if has_cache:
            off = kc_ref.shape[0]
            kbuf[0:off, :] = kc_ref[...].astype(BF16)
            vbuf[0:off, :] = vc_ref[...].astype(BF16)
        kbuf[off:, :] = k_ref[...].astype(BF16)
        vbuf[off:, :] = v_ref[...].astype(BF16)

    q_blocks = [q_ref[:, p * LANES:(p + 1) * LANES].astype(F32) for p in range(len(qmap))]
    outs = _pair_attention(q_blocks, lambda i: kbuf[:, i * LANES:(i + 1) * LANES],
                           lambda i: vbuf[:, i * LANES:(i + 1) * LANES], qmap)
    for p, o in enumerate(outs):
        o_ref[:, p * LANES:(p + 1) * LANES] = o.astype(o_ref.dtype)


def _attn_pair(q, k, v, qmap, cache=None, name="attn_pair"):
    b, lq, wq = q.shape
    ls, wk = k.shape[1], k.shape[2]
    tq = min(ATTN_Q_TILE, lq)
    lc = 0 if cache is None else cache[0].shape[2]
    in_specs = [pl.BlockSpec((None, tq, wq), lambda bi, i: (bi, i, 0))]
    args = [q]
    if cache is not None:
        kc, vc, layer = cache
        cspec = pl.BlockSpec((None, None, lc, wk), lambda bi, i: (bi, layer, 0, 0))
        in_specs += [cspec, cspec]
        args += [kc, vc]
    sspec = pl.BlockSpec((None, ls, wk), lambda bi, i: (bi, 0, 0))
    in_specs += [sspec, sspec]
    args += [k, v]
    return pl.pallas_call(
        functools.partial(_attn_pair_kernel, cache is not None, qmap),
        grid=(b, lq // tq),
        in_specs=in_specs,
        out_specs=pl.BlockSpec((None, tq, wq), lambda bi, i: (bi, i, 0)),
        out_shape=jax.ShapeDtypeStruct((b, lq, wq), BF16),
        scratch_shapes=[pltpu.VMEM((lc + ls, wk), BF16), pltpu.VMEM((lc + ls, wk), BF16)],
        compiler_params=_cparams(2),
        name=name,
    )(*args)


def _mla_kernel(has_cache, *refs):
    if has_cache:
        q_ref, ckvc_ref, krc_ref, ckv_ref, kr_ref, wk_ref, wv_ref, o_ref, kbuf, vbuf = refs
    else:
        q_ref, ckv_ref, kr_ref, wk_ref, wv_ref, o_ref, kbuf, vbuf = refs

    @pl.when(pl.program_id(1) == 0)
    def _():
        def expand(c_ref, r_ref, r0, r1):
            c = c_ref[...].astype(BF16)
            ckr = jnp.concatenate([c, r_ref[...].astype(BF16)], axis=-1)
            kbuf[r0:r1, :] = _bdot(ckr, wk_ref[...]).astype(BF16)
            vbuf[r0:r1, :] = _bdot(c, wv_ref[...]).astype(BF16)
        off = 0
        if has_cache:
            off = ckvc_ref.shape[0]
            expand(ckvc_ref, krc_ref, 0, off)
        expand(ckv_ref, kr_ref, off, kbuf.shape[0])

    outs = _mla_attention([q_ref[:, hd * LANES:(hd + 1) * LANES].astype(BF16) for hd in range(MLA_HEADS)],
                          lambda i: kbuf[:, i * LANES:(i + 1) * LANES],
                          lambda i: vbuf[:, i * LANES:(i + 1) * LANES])
    for p, o in enumerate(outs):
        o_ref[:, p * LANES:(p + 1) * LANES] = o.astype(o_ref.dtype)


def _mla(q, ckv, kr, lw, layer, cache=None, name="mla"):
    b, lq, wq = q.shape
    ls = ckv.shape[1]
    tq = min(ATTN_Q_TILE, lq)
    lc = 0 if cache is None else cache[0].shape[2]
    in_specs = [pl.BlockSpec((None, tq, wq), lambda bi, i: (bi, i, 0))]
    args = [q]
    if cache is not None:
        ckvc, krc = cache
        cspec = pl.BlockSpec((None, None, lc, LANES), lambda bi, i: (bi, layer, 0, 0))
        in_specs += [cspec, cspec]
        args += [ckvc, krc]
    sspec = pl.BlockSpec((None, ls, LANES), lambda bi, i: (bi, 0, 0))
    in_specs += [sspec, sspec,
                 pl.BlockSpec((None, 2 * LANES, 4 * LANES), lambda bi, i: (layer, 0, 0)),
                 pl.BlockSpec((None, LANES, 2 * LANES), lambda bi, i: (layer, 0, 0))]
    args += [ckv, kr, lw["wk"], lw["wv"]]
    return pl.pallas_call(
        functools.partial(_mla_kernel, cache is not None),
        grid=(b, lq // tq),
        in_specs=in_specs,
        out_specs=pl.BlockSpec((None, tq, 2 * LANES), lambda bi, i: (bi, i, 0)),
        out_shape=jax.ShapeDtypeStruct((b, lq, 2 * LANES), BF16),
        scratch_shapes=[pltpu.VMEM((lc + ls, 4 * LANES), BF16), pltpu.VMEM((lc + ls, 2 * LANES), BF16)],
        compiler_params=_cparams(2),
        name=name,
    )(*args)


def _na_kernel(q_ref, k_ref, v_ref, kc_ref, vc_ref, bias_ref, o_ref):
    j = pl.program_id(0)
    start = pl.multiple_of((j >> 1) * 256, 256)
    kband = k_ref[pl.ds(start, NA_BAND), :].astype(BF16)
    vband = v_ref[pl.ds(start, NA_BAND), :].astype(BF16)
    kc = kc_ref[...].astype(BF16)
    vc = vc_ref[...].astype(BF16)
    tq = q_ref.shape[0]
    lo = _lane((tq, LANES)) < HEAD_DIM

    def scores(hd):
        p, half = divmod(hd, 2)
        sl = slice(p * LANES, (p + 1) * LANES)
        qm = jnp.where(lo if half == 0 else jnp.logical_not(lo), q_ref[:, sl].astype(F32), 0.0).astype(BF16)
        s_loc = lax.dot_general(qm, kband[:, sl], _NT, preferred_element_type=F32) + bias_ref[hd]
        return [s_loc, lax.dot_general(qm, kc[:, sl], _NT, preferred_element_type=F32)]

    def attend(hd, es):
        sl = slice((hd // 2) * LANES, (hd // 2 + 1) * LANES)
        return _bdot(es[0].astype(BF16), vband[:, sl]) + _bdot(es[1].astype(BF16), vc[:, sl])

    outs = _pipelined_heads(NA_HEADS, scores, attend)
    for p in range(NA_HEADS // 2):
        o_ref[:, p * LANES:(p + 1) * LANES] = jnp.where(lo, outs[2 * p], outs[2 * p + 1]).astype(o_ref.dtype)


def _na_latent(q, k, v, kc, vc, layer, bias_blocks):
    b, n, w = q.shape
    lc = kc.shape[2]
    nq = n // Q_TILE
    full = pl.BlockSpec((None, n, w), lambda j, bi: (bi, 0, 0))
    cspec = pl.BlockSpec((None, None, lc, w), lambda j, bi: (bi, layer, 0, 0))
    return pl.pallas_call(
        _na_kernel,
        grid=(nq, b),
        in_specs=[pl.BlockSpec((None, Q_TILE, w), lambda j, bi: (bi, j, 0)), full, full, cspec, cspec,
                  pl.BlockSpec((None, None, NA_HEADS, Q_TILE, NA_BAND), lambda j, bi: (layer, j, 0, 0, 0))],
        out_specs=pl.BlockSpec((None, Q_TILE, w), lambda j, bi: (bi, j, 0)),
        out_shape=jax.ShapeDtypeStruct((b, n, w), BF16),
        compiler_params=_cparams(2),
        name="na_latent",
    )(q, k, v, kc, vc, bias_blocks)


NA_GRID_ROWS = 16
NA_BAND_ROW0 = (0, 0, 4, 4)


def _na_bias_kernel(b_ref, o_ref, tp_s):
    hd = pl.program_id(0)
    n_dr, n_dc = 2 * NA_KH - 1, 2 * NA_KW - 1
    shape = (GRID_W, LANES)
    c = lax.broadcasted_iota(jnp.int32, shape, 0)
    lane = _lane(shape)
    kc = lane & (GRID_W - 1)
    lo = lane < GRID_W
    diff = kc - c + (NA_KW - 1)
    c0 = jnp.clip(c - NA_KW // 2, 0, GRID_W - NA_KW)
    col_ok = (kc >= c0) & (kc < c0 + NA_KW)
    neg = jnp.full(shape, NEG_INF, F32)
    for dr0 in range(-1, n_dr):
        acc = neg
        for d in range(n_dc):
            v_lo = b_ref[hd * n_dr + dr0, d] if dr0 >= 0 else 0.0
            v_hi = b_ref[hd * n_dr + dr0 + 1, d] if dr0 + 1 < n_dr else 0.0
            acc = jnp.where(diff == d, jnp.where(lo, v_lo, v_hi), acc)
        tp_s[dr0 + 1] = jnp.where(col_ok, acc * LOG2E, NEG_INF)
    for j in range(NA_GRID_ROWS // 4):
        for ri in range(4):
            r = 4 * j + ri
            r0 = min(max(r - NA_KH // 2, 0), NA_GRID_ROWS - NA_KH)
            for kp in range(NA_BAND // LANES):
                kr = NA_BAND_ROW0[j] + 2 * kp
                ok_lo, ok_hi = r0 <= kr < r0 + NA_KH, r0 <= kr + 1 < r0 + NA_KH
                dr0 = kr - r + (NA_KH - 1)
                if ok_lo and ok_hi:
                    t = tp_s[dr0 + 1]
                elif ok_lo:
                    t = jnp.where(lo, tp_s[dr0 + 1], NEG_INF)
                elif ok_hi:
                    t = jnp.where(lo, NEG_INF, tp_s[dr0 + 1])
                else:
                    t = neg
                o_ref[j, ri * GRID_W:(ri + 1) * GRID_W, kp * LANES:(kp + 1) * LANES] = t


def _na_bias_blocks(bias):
    nq = NA_GRID_ROWS // 4
    return pl.pallas_call(
        _na_bias_kernel,
        grid=(DEPTH * NA_HEADS,),
        in_specs=[pl.BlockSpec(memory_space=pltpu.SMEM)],
        out_specs=pl.BlockSpec((None, nq, None, Q_TILE, NA_BAND),
                               lambda i: (i // NA_HEADS, 0, i % NA_HEADS, 0, 0)),
        out_shape=jax.ShapeDtypeStruct((DEPTH, nq, NA_HEADS, Q_TILE, NA_BAND), F32),
        scratch_shapes=[pltpu.VMEM((2 * NA_KH, GRID_W, LANES), F32)],
        compiler_params=_cparams(1),
        name="na_bias",
    )(bias.reshape(DEPTH * NA_HEADS * (2 * NA_KH - 1), 2 * NA_KW - 1))


def _widen(cols, n):
    blk = _lane((n, DN_HEADS * DN_DV)) >> 6
    return jnp.where(blk == 0, cols[0], jnp.where(blk == 1, cols[1], jnp.where(blk == 2, cols[2], cols[3])))


def _deltanet_kernel(seq, has_state, *refs):
    if has_state:
        (zc_ref, zm_ref, s0_ref, cw_ref, alog_ref, dtb_ref, og_ref, o_ref,
         q_s, k_s, v_s, b_s, g_s, o_s, c_s, mp_s) = refs
    else:
        (zc_ref, zm_ref, cw_ref, alog_ref, dtb_ref, og_ref, o_ref, sfin_ref,
         q_s, k_s, v_s, b_s, g_s, o_s, c_s, mp_s) = refs
    n_chunks = seq // DN_CHUNK
    wide = DN_HEADS * DN_DV

    a_off, b_off = 0, DN_GATES
    ri = lax.broadcasted_iota(jnp.int32, (DN_ROWS, DN_ROWS), 0)
    ci = lax.broadcasted_iota(jnp.int32, (DN_ROWS, DN_ROWS), 1)
    same = (ri >> 6) == (ci >> 6)
    tri = [jnp.tile(jnp.where(same & ((ci <= ri) if d == 0 else (ci >= ri)), 1.0, 0.0).astype(BF16), (1, 3))
           for d in range(2)]
    n_blocks = seq // DN_ROWS

    def split3(x):
        hi = x.astype(BF16)
        rest = x - hi.astype(F32)
        mid = rest.astype(BF16)
        return jnp.concatenate([hi, mid, (rest - mid.astype(F32)).astype(BF16)], axis=0)

    def preprocess(rb, carry):
        r0 = pl.multiple_of(rb * DN_ROWS, DN_ROWS)
        rows = pl.ds(r0, DN_ROWS)
        before = pl.ds(pl.multiple_of(jnp.maximum(r0 - 8, 0), 8), 8)
        after = pl.ds(pl.multiple_of(jnp.minimum(r0 + DN_ROWS, seq - 8), 8), 8)
        for part, dst in enumerate((q_s, k_s, v_s)):
            cs = slice(part * wide, (part + 1) * wide)
            head = jnp.where(rb > 0, zc_ref[before, cs], 0.0)
            tail = jnp.where(rb < n_blocks - 1, zc_ref[after, cs], 0.0)
            xe = jnp.concatenate([head, zc_ref[rows, cs], tail], axis=0)
            w = cw_ref[:, cs]
            y = (w[0:1] * xe[7:7 + DN_ROWS] + w[1:2] * xe[8:8 + DN_ROWS]
                 + w[2:3] * xe[9:9 + DN_ROWS] + w[3:4] * xe[10:10 + DN_ROWS])
            y = _silu(y)
            if part == 0:
                y = _head_l2(y) * (DN_DK ** -0.5)
            elif part == 1:
                y = _head_l2(y)
            dst[rows, :] = y
        zm = zm_ref[rows, :]
        xa = zm + dtb_ref[...]
        logd = -jnp.exp(alog_ref[...]) * (jnp.maximum(xa, 0.0) + jnp.log1p(jnp.exp(-jnp.abs(xa))))
        beta = 1.0 / (1.0 + jnp.exp(-zm))
        logd3 = split3(logd)
        for d in range(2):
            b_s[d, rows, :] = _widen([beta[:, b_off + 4 * d + hd:b_off + 4 * d + hd + 1]
                                      for hd in range(DN_HEADS)], DN_ROWS)
            g = _bdot(tri[d], logd3)
            g_s[d, rows, :] = _widen([g[:, a_off + 4 * d + hd:a_off + 4 * d + hd + 1]
                                      for hd in range(DN_HEADS)], DN_ROWS)
        return carry

    lax.fori_loop(0, n_blocks, preprocess, 0)

    ii = lax.broadcasted_iota(jnp.int32, (DN_CHUNK, wide), 0)
    jj = _lane((DN_CHUNK, wide)) & (DN_CHUNK - 1)
    blk = _lane((DN_CHUNK, wide)) >> 6
    diag = ii == jj
    eye = jnp.where(diag, 1.0, 0.0)
    head_mask = [jnp.where(blk == hd, 1.0, 0.0).astype(BF16) for hd in range(DN_HEADS)]

    def bd(z):
        zb = z.astype(BF16)
        return jnp.concatenate([zb * hm for hm in head_mask], axis=0)

    def fold(gram):
        out = jnp.where(blk == 0, gram[0:DN_CHUNK], 0.0)
        for hd in range(1, DN_HEADS):
            out = out + jnp.where(blk == hd, gram[hd * DN_CHUNK:(hd + 1) * DN_CHUNK], 0.0)
        return out

    tri_masks = []
    for d in range(2):
        incl = (jj <= ii) if d == 0 else (jj >= ii)
        strict = (jj < ii) if d == 0 else (jj > ii)
        pair = [((ii >> (lvl + 1)) == (jj >> (lvl + 1)))
                & (((ii >> lvl) & 1) == (1 - d)) & (((jj >> lvl) & 1) == d) for lvl in range(6)]
        tri_masks.append((incl, strict, pair))

    def prepare(step, carry):
        chunks = [step * PREP_CHUNKS + i for i in range(PREP_CHUNKS)]
        rows = [pl.ds(pl.multiple_of(c * DN_CHUNK, DN_CHUNK), DN_CHUNK) for c in chunks]
        qkv = [(q_s[r, :], k_s[r, :], v_s[r, :]) for r in rows]
        inst = [(ci, d) for ci in range(PREP_CHUNKS) for d in range(2)]
        beta = {(ci, d): b_s[d, rows[ci], :] for ci, d in inst}
        kb = {(ci, d): qkv[ci][1] * beta[ci, d] for ci, d in inst}
        r = [lax.dot_general(jnp.concatenate([kb[ci, 0], kb[ci, 1], qkv[ci][0]], axis=0).astype(BF16),
                             bd(qkv[ci][1]), _NT, preferred_element_type=F32)
             for ci in range(PREP_CHUNKS)]
        g, a, qk, eg, t = {}, {}, {}, {}, {}
        for ci, d in inst:
            incl, strict, pair = tri_masks[d]
            g[ci, d] = g_s[d, rows[ci], :]
            g_row = jnp.sum(jnp.where(diag, g[ci, d], 0.0), axis=0, keepdims=True)
            dm = jnp.where(incl, jnp.exp(jnp.where(incl, g[ci, d] - g_row, 0.0)), 0.0)
            a[ci, d] = jnp.where(strict, r[ci][d * DN_CHUNK:(d + 1) * DN_CHUNK] * dm, 0.0)
            qk[ci, d] = (r[ci][2 * DN_CHUNK:] * dm).astype(BF16)
            eg[ci, d] = jnp.exp(g[ci, d])
            t[ci, d] = eye - jnp.where(pair[0], a[ci, d], 0.0)
        for lvl in range(1, 6):
            te = {i: _bdot(t[i].astype(BF16), bd(jnp.where(tri_masks[i[1]][2][lvl], a[i], 0.0))) for i in inst}
            t = {i: t[i] - _bdot(te[i].astype(BF16), bd(t[i])) for i in inst}
        nb = {i: jnp.where(diag, 0.0, t[i]).astype(BF16) for i in inst}
        rhs_u = {(ci, d): qkv[ci][2] * beta[ci, d] for ci, d in inst}
        rhs_w = {i: kb[i] * eg[i] for i in inst}
        u = {i: rhs_u[i] + _bdot(nb[i], bd(rhs_u[i])) for i in inst}
        w = {i: rhs_w[i] + _bdot(nb[i], bd(rhs_w[i])) for i in inst}
        kd = {}
        for ci, d in inst:
            g_last = g[ci, d][DN_CHUNK - 1:DN_CHUNK] if d == 0 else g[ci, d][0:1]
            kd[ci, d] = (qkv[ci][1] * jnp.exp(g_last - g[ci, d])).astype(BF16)
        p = {(ci, d): qkv[ci][0] * eg[ci, d] - _bdot(qk[ci, d], bd(w[ci, d])) for ci, d in inst}
        o0 = {i: _bdot(qk[i], bd(u[i])) for i in inst}
        m = {i: fold(lax.dot_general(kd[i], w[i].astype(BF16), _TN, preferred_element_type=F32)) for i in inst}
        cc = {i: fold(lax.dot_general(kd[i], u[i].astype(BF16), _TN, preferred_element_type=F32)) for i in inst}
        for ci, d in inst:
            mrow = pl.ds(pl.multiple_of(chunks[ci] * (2 * DN_CHUNK), 2 * DN_CHUNK), 2 * DN_CHUNK)
            c_s[d, rows[ci], :] = cc[ci, d]
            mp_s[d, mrow, :] = jnp.concatenate([m[ci, d], p[ci, d]], axis=0).astype(BF16)
        for ci in range(PREP_CHUNKS):
            o_s[rows[ci], :] = o0[ci, 0] + o0[ci, 1]
        return carry

    lax.fori_loop(0, n_chunks // PREP_CHUNKS, prepare, 0)

    def scan(i, states):
        new = []
        for d in range(2):
            c = i if d == 0 else n_chunks - 1 - i
            r0 = pl.multiple_of(c * DN_CHUNK, DN_CHUNK)
            rows = pl.ds(r0, DN_CHUNK)
            mrow = pl.ds(pl.multiple_of(c * (2 * DN_CHUNK), 2 * DN_CHUNK), 2 * DN_CHUNK)
            edge = pl.ds(pl.multiple_of(r0 + (DN_CHUNK - 8 if d == 0 else 0), 8), 8)
            g_edge = g_s[d, edge, :]
            g_last = g_edge[7:8] if d == 0 else g_edge[0:1]
            res = _bdot(mp_s[d, mrow, :], bd(states[d]))
            o_s[rows, :] = o_s[rows, :] + res[DN_CHUNK:]
            new.append(states[d] * jnp.exp(g_last) - res[0:DN_CHUNK] + c_s[d, rows, :])
        return tuple(new)

    init = tuple(s0_ref[d] if has_state else jnp.zeros((DN_DK, wide), F32) for d in range(2))
    fin = lax.fori_loop(0, n_chunks, scan, init)
    if not has_state:
        sfin_ref[0] = fin[0]
        sfin_ref[1] = fin[1]

    o_ref[...] = (_head_rms(o_s[...], og_ref[...]) * _silu(zc_ref[:, 3 * wide:4 * wide])).astype(o_ref.dtype)


def _deltanet(zc, zm, lw, layer, state=None):
    b, seq, _ = zc.shape
    wide = DN_HEADS * DN_DV
    has_state = state is not None
    per_b = lambda w: pl.BlockSpec((None, seq, w), lambda bi: (bi, 0, 0))
    const = lambda shape: pl.BlockSpec((None,) + shape, lambda bi: (layer,) + (0,) * len(shape))
    st_spec = pl.BlockSpec((None, 2, DN_DK, wide), lambda bi: (bi, 0, 0, 0))
    in_specs = [per_b(4 * wide), per_b(LANES)]
    args = [zc, zm]
    if has_state:
        in_specs.append(pl.BlockSpec((None, None, 2, DN_DK, wide), lambda bi: (bi, layer, 0, 0, 0)))
        args.append(state)
    in_specs += [const((DN_CONV, DN_QKV)), const((1, LANES)), const((1, LANES)), const((1, wide))]
    args += [lw["dn_conv_w"], lw["dn_alog_row"], lw["dn_dtb_row"], lw["dn_out_g"]]
    out_specs = [per_b(wide)]
    out_shape = [jax.ShapeDtypeStruct((b, seq, wide), BF16)]
    if not has_state:
        out_specs.append(st_spec)
        out_shape.append(jax.ShapeDtypeStruct((b, 2, DN_DK, wide), F32))
    res = pl.pallas_call(
        functools.partial(_deltanet_kernel, seq, has_state),
        grid=(b,),
        in_specs=in_specs,
        out_specs=out_specs,
        out_shape=out_shape,
        scratch_shapes=[pltpu.VMEM((seq, wide), F32), pltpu.VMEM((seq, wide), F32),
                        pltpu.VMEM((seq, wide), F32), pltpu.VMEM((2, seq, wide), F32),
                        pltpu.VMEM((2, seq, wide), F32), pltpu.VMEM((seq, wide), F32),
                        pltpu.VMEM((2, seq, wide), F32), pltpu.VMEM((2, 2 * seq, wide), BF16)],
        compiler_params=_cparams(1),
        name="deltanet_lat" if has_state else "deltanet_ctx",
    )(*args)
    return (res[0], None) if has_state else (res[0], res[1])


def _outffn_kernel(final, oa_ref, ob_ref, oc_ref, od_ref, x_ref, mod_ref, g2_ref, wo_ref, wg_ref,
                   wu_ref, wd_ref, fg_ref, y_ref):
    tiles = range(TILES_PER_STEP)
    m = mod_ref[...]
    o = [jnp.concatenate([oa_ref[t], ob_ref[t], oc_ref[t], od_ref[t]], axis=-1).astype(BF16) for t in tiles]
    x1 = [x_ref[t] + m[2:3] * _bdot(o[t], wo_ref[...]) for t in tiles]
    h = [(_rms_full(x1[t], g2_ref[...]) * (1.0 + m[4:5]) + m[3:4]).astype(BF16) for t in tiles]
    gate = [_bdot(h[t], wg_ref[...]) for t in tiles]
    up = [_bdot(h[t], wu_ref[...]) for t in tiles]
    act = [(_silu(gate[t]) * up[t]).astype(BF16) for t in tiles]
    x2 = [x1[t] + m[5:6] * _bdot(act[t], wd_ref[...]) for t in tiles]
    for t in tiles:
        y_ref[t] = _rms_full(x2[t], fg_ref[...]) if final else x2[t]


def _outffn(outs, x, mods, per_batch_mods, lw, layer, final_g, final):
    b, l, _ = x.shape
    tm, ts = ROW_TILE, TILES_PER_STEP
    n_tiles = b * l // tm
    tiles_per_seq = l // tm
    assert not per_batch_mods or tiles_per_seq % ts == 0
    row = lambda w: pl.BlockSpec((ts, tm, w), lambda i: (i, 0, 0))
    const = lambda shape: pl.BlockSpec((None,) + shape, lambda i: (layer,) + (0,) * len(shape),
                                       pipeline_mode=pl.Buffered(1))
    mod_spec = pl.BlockSpec((None, None, MOD_CHUNKS, D_MODEL),
                            (lambda i: (layer, 1 + i * ts // tiles_per_seq, 0, 0)) if per_batch_mods
                            else (lambda i: (layer, 0, 0, 0)))
    tiled = lambda a: a.reshape(n_tiles, tm, a.shape[-1])
    y = pl.pallas_call(
        functools.partial(_outffn_kernel, final),
        grid=(n_tiles // ts,),
        in_specs=[row(256), row(256), row(256), row(256), row(D_MODEL), mod_spec, const((1, D_MODEL)),
                  const((D_MODEL, D_MODEL)), const((D_MODEL, D_FF)), const((D_MODEL, D_FF)),
                  const((D_FF, D_MODEL)), pl.BlockSpec((1, D_MODEL), lambda i: (0, 0))],
        out_specs=row(D_MODEL),
        out_shape=jax.ShapeDtypeStruct((n_tiles, tm, D_MODEL), F32),
        compiler_params=_cparams(1),
        name="outffn",
    )(*[tiled(a) for a in outs], tiled(x), mods, lw["norm2_g"], lw["w_out"], lw["w_gate"], lw["w_up"],
      lw["w_down"], final_g)
    return y.reshape(b, l, D_MODEL)


def _rope_tables(n):
    t = jnp.arange(n)

    def axis(pos, half):
        inv = ROPE_BASE ** (-jnp.arange(half, dtype=F32) / half)
        ang = pos.astype(F32)[:, None] * inv[None, :]
        c, s = jnp.cos(ang), jnp.sin(ang)
        return jnp.concatenate([c, c], -1), jnp.concatenate([-s, s], -1)

    cr, sr = axis(t // GRID_W, 16)
    cc, sc = axis(t % GRID_W, 16)
    cos64 = jnp.tile(jnp.concatenate([cr, cc], -1), (1, 2))
    sin64 = jnp.tile(jnp.concatenate([sr, sc], -1), (1, 2))
    cr, sr = axis(t // GRID_W, 8)
    cc, sc = axis(t % GRID_W, 8)
    cos32, sin32 = jnp.concatenate([cr, cc], -1), jnp.concatenate([sr, sc], -1)
    one, zero = jnp.ones((n, 1), F32), jnp.zeros((n, 1), F32)
    cosm = jnp.concatenate([jnp.tile(one, (1, 64)), cos32, jnp.tile(one, (1, 32))], -1)
    sinm = jnp.concatenate([jnp.tile(zero, (1, 64)), sin32, jnp.tile(zero, (1, 32))], -1)
    coskr = jnp.concatenate([cos32, jnp.tile(one, (1, 96))], -1)
    sinkr = jnp.concatenate([sin32, jnp.tile(zero, (1, 96))], -1)
    return cos64, sin64, cosm, sinm, coskr, sinkr


_QA_ORDER = ((0, 64), (128, 192), (64, 128), (192, 256))


def _stacked_weights(p):
    w_in = jnp.swapaxes(p["w_in"], 1, 2).astype(BF16)
    w_in = jnp.pad(w_in, ((0, 0), (0, IN_PAD_COLS - w_in.shape[1]), (0, 0)))
    w_out = p["w_out"]
    w_out = jnp.concatenate([w_out[:, a:b] for a, b in _QA_ORDER + ((256, w_out.shape[1]),)], axis=1).astype(BF16)
    wq = p["mla_wq_up"].reshape(DEPTH, MLA_Q_LORA, MLA_HEADS, MLA_NOPE + MLA_ROPE)
    wq = jnp.pad(wq, ((0, 0), (0, 0), (0, 0), (0, LANES - MLA_NOPE - MLA_ROPE))).reshape(DEPTH, MLA_Q_LORA, 4 * LANES)
    wkv = p["mla_wkv_up"].reshape(DEPTH, MLA_KV_LORA, MLA_HEADS, MLA_NOPE + MLA_V)
    wk_top = jnp.pad(wkv[..., :MLA_NOPE], ((0, 0), (0, 0), (0, 0), (0, LANES - MLA_NOPE)))
    place = jnp.pad(jnp.eye(MLA_ROPE, dtype=F32), ((0, LANES - MLA_ROPE), (MLA_NOPE, LANES - MLA_NOPE - MLA_ROPE)))
    wk_bot = jnp.broadcast_to(place[None, :, None, :], (DEPTH, LANES, MLA_HEADS, LANES))
    wk = jnp.concatenate([wk_top, wk_bot], axis=1).reshape(DEPTH, 2 * LANES, 4 * LANES)
    wv = wkv[..., MLA_NOPE:].reshape(DEPTH, MLA_KV_LORA, MLA_HEADS * MLA_V)
    gate_row = lambda v: jnp.pad(v.reshape(DEPTH, 1, DN_GATES), ((0, 0), (0, 0), (0, LANES - DN_GATES)))
    row = lambda v: v[:, None, :]
    return {
        "norm1_g": row(p["norm1_g"]), "norm2_g": row(p["norm2_g"]),
        "w_in": w_in, "w_out": w_out,
        "qn_g": row(jnp.tile(p["gqa_qn_g"], (1, 4))), "kn_g": row(jnp.tile(p["gqa_kn_g"], (1, 2))),
        "mla_qn_g": row(p["mla_qn_g"]), "mla_kvn_g": row(p["mla_kvn_g"]),
        "wq": wq.astype(BF16), "wk": wk.astype(BF16), "wv": wv.astype(BF16),
        "dn_conv_w": p["dn_conv_w"], "dn_alog_row": gate_row(p["dn_a_log"]),
        "dn_dtb_row": gate_row(p["dn_dt_bias"]), "dn_out_g": row(jnp.tile(p["dn_out_g"], (1, DN_HEADS))),
        "w_gate": p["ffn_w_gate"].astype(BF16), "w_up": p["ffn_w_up"].astype(BF16),
        "w_down": p["ffn_w_down"].astype(BF16),
    }


def _state_to_wide(s):
    b = s.shape[0]
    return s.transpose(0, 1, 2, 4, 3, 5).reshape(b, DEPTH, 2, DN_DK, DN_HEADS * DN_DV)


def _state_from_wide(s):
    b = s.shape[0]
    return s.reshape(b, 2, DN_DK, DN_HEADS, DN_DV).transpose(0, 1, 3, 2, 4)


def kernel(x_prompt, x_sample, cache_gqa_k, cache_gqa_v, cache_na_k, cache_na_v, state_dn,
           cache_mla_ckv, cache_mla_krope, c, c_ctx, norm1_g, norm2_g, w_mod, b_mod, w_in, w_out,
           gqa_qn_g, gqa_kn_g, na_bias, dn_conv_w, dn_a_log, dn_dt_bias, dn_out_g, mla_qn_g,
           mla_wq_up, mla_kvn_g, mla_wkv_up, ffn_w_gate, ffn_w_up, ffn_w_down, final_g):
    p = {"norm1_g": norm1_g, "norm2_g": norm2_g, "w_in": w_in, "w_out": w_out, "gqa_qn_g": gqa_qn_g,
         "gqa_kn_g": gqa_kn_g, "dn_conv_w": dn_conv_w, "dn_a_log": dn_a_log, "dn_dt_bias": dn_dt_bias,
         "dn_out_g": dn_out_g, "mla_qn_g": mla_qn_g, "mla_wq_up": mla_wq_up, "mla_kvn_g": mla_kvn_g,
         "mla_wkv_up": mla_wkv_up, "ffn_w_gate": ffn_w_gate, "ffn_w_up": ffn_w_up, "ffn_w_down": ffn_w_down}
    nb_ctx, seq_ctx, _ = x_prompt.shape
    nb_lat, seq_lat, _ = x_sample.shape
    past = cache_gqa_k.shape[2]
    fg = final_g[None]

    cond = jnp.concatenate([c_ctx[None], c, jnp.zeros((16 - 1 - nb_lat, D_MODEL), F32)], axis=0)
    mods = _modulation(cond, w_mod, b_mod).reshape(DEPTH, 16, MOD_CHUNKS, D_MODEL)
    lw = _stacked_weights(p)

    x = x_prompt
    ctx_out = []
    for l in range(DEPTH):
        o_a, ka, va, o_b, kb, vb, zc, o_d, ckv, zm, krr = _inproj(x, mods, False, lw, l, None)
        o_c, s_dn = _deltanet(zc, zm, lw, l)
        x = _outffn((o_a, o_b, o_c, o_d), x, mods, False, lw, l, fg, l == DEPTH - 1)
        ctx_out.append((ka.reshape(nb_ctx, seq_ctx, GQA_KV_HEADS, HEAD_DIM),
                        va.reshape(nb_ctx, seq_ctx, GQA_KV_HEADS, HEAD_DIM),
                        kb.reshape(nb_ctx, seq_ctx, NA_HEADS, HEAD_DIM),
                        vb.reshape(nb_ctx, seq_ctx, NA_HEADS, HEAD_DIM),
                        _state_from_wide(s_dn), ckv, krr[:, :, :MLA_ROPE]))
    y_prompt = x
    new = [jnp.stack([s[i] for s in ctx_out], axis=1) for i in range(7)]

    ropes = _rope_tables(seq_lat)
    ck_a = cache_gqa_k.reshape(nb_lat, DEPTH, past, GQA_KV_HEADS * HEAD_DIM)
    cv_a = cache_gqa_v.reshape(nb_lat, DEPTH, past, GQA_KV_HEADS * HEAD_DIM)
    ck_b = cache_na_k.reshape(nb_lat, DEPTH, past, NA_HEADS * HEAD_DIM)
    cv_b = cache_na_v.reshape(nb_lat, DEPTH, past, NA_HEADS * HEAD_DIM)
    c_kr = jnp.pad(cache_mla_krope, ((0, 0), (0, 0), (0, 0), (0, LANES - MLA_ROPE)))
    bias_blocks = _na_bias_blocks(na_bias)
    state = _state_to_wide(state_dn)
    x = x_sample
    for l in range(DEPTH):
        qa, ka, va, qb, kb, vb, zc, qd, ckv, zm, krr = _inproj(x, mods, True, lw, l, ropes)
        o_a = _attn_pair(qa, ka, va, (0, 0), cache=(ck_a, cv_a, l), name="gqa_lat")
        o_b = _na_latent(qb, kb, vb, ck_b, cv_b, l, bias_blocks)
        o_c, _ = _deltanet(zc, zm, lw, l, state=state)
        o_d = _mla(qd, ckv, krr, lw, l, cache=(cache_mla_ckv, c_kr), name="mla_lat")
        x = _outffn((o_a, o_b, o_c, o_d), x, mods, True, lw, l, fg, l == DEPTH - 1)
    y_sample = x

    return (y_prompt, y_sample, *new)
```

```python
import functools

import numpy as np
import jax
import jax.numpy as jnp
from jax import lax
from jax.experimental import pallas as pl
from jax.experimental.pallas import tpu as pltpu

F32 = jnp.float32
BF16 = jnp.bfloat16

D_MODEL = 1024
DEPTH = 2
GRID_W = 64
HEAD_DIM = 64
ROPE_BASE = 10000.0
NEG_INF = -1e30
MOD_CHUNKS = 6
GQA_HEADS, GQA_KV_HEADS = 4, 2
NA_HEADS, NA_KH, NA_KW = 4, 8, 16
DN_HEADS, DN_DK, DN_DV, DN_CONV, DN_CHUNK = 4, 64, 64, 4, 64
DN_QKV = DN_HEADS * (2 * DN_DK + DN_DV)
DN_GATES = 2 * DN_HEADS
MLA_HEADS, MLA_Q_LORA, MLA_KV_LORA, MLA_NOPE, MLA_ROPE, MLA_V = 4, 256, 128, 64, 32, 64
MLA_SCALE = (MLA_NOPE + MLA_ROPE) ** -0.5
D_FF = -(-8 * D_MODEL // (3 * 256)) * 256
EPS = 1e-6
LOG2E = 1.4426950408889634

LANES = 128
ROW_TILE = 256
Q_TILE = 256
NA_BAND = 768
TILES_PER_STEP = 2
ATTN_Q_TILE = 512
HEAD_LOOKAHEAD = 1
DN_ROWS = 256
PREP_CHUNKS = 4
IN_PAD_COLS = 2816
VMEM_LIMIT = 56 * 1024 * 1024

_NT = (((1,), (1,)), ((), ()))
_TN = (((0,), (0,)), ((), ()))


def _cparams(n_axes):
    return pltpu.CompilerParams(dimension_semantics=("arbitrary",) * n_axes,
                                vmem_limit_bytes=VMEM_LIMIT)


def _lane(shape):
    return lax.broadcasted_iota(jnp.int32, shape, len(shape) - 1)


def _silu(x):
    return x / (1.0 + jnp.exp(-x))


def _rms_full(x, g):
    return x * lax.rsqrt(jnp.mean(x * x, axis=-1, keepdims=True) + EPS) * g


def _seg64_sum(x):
    lo = _lane(x.shape) < HEAD_DIM
    s_lo = jnp.sum(jnp.where(lo, x, 0.0), axis=-1, keepdims=True)
    s_hi = jnp.sum(jnp.where(lo, 0.0, x), axis=-1, keepdims=True)
    return jnp.where(lo, s_lo, s_hi)


def _head_rms(x, g):
    parts = []
    for p in range(x.shape[-1] // LANES):
        xp = x[:, p * LANES:(p + 1) * LANES]
        ms = _seg64_sum(xp * xp) * (1.0 / HEAD_DIM)
        parts.append(xp * lax.rsqrt(ms + EPS))
    y = parts[0] if len(parts) == 1 else jnp.concatenate(parts, axis=-1)
    return y * g


def _head_l2(x):
    parts = []
    for p in range(x.shape[-1] // LANES):
        xp = x[:, p * LANES:(p + 1) * LANES]
        parts.append(xp * lax.rsqrt(_seg64_sum(xp * xp) + EPS))
    return parts[0] if len(parts) == 1 else jnp.concatenate(parts, axis=-1)


def _rope(x, cos, sin, half):
    first = (_lane(x.shape) & (2 * half - 1)) < half
    rot = jnp.where(first, pltpu.roll(x, LANES - half, 1), pltpu.roll(x, half, 1))
    return x * cos + rot * sin


def _softmax_parts(scores):
    m = jnp.max(scores[0], axis=-1, keepdims=True)
    for s in scores[1:]:
        m = jnp.maximum(m, jnp.max(s, axis=-1, keepdims=True))
    es = [jnp.exp2(s - m) for s in scores]
    l = jnp.sum(es[0], axis=-1, keepdims=True)
    for e in es[1:]:
        l = l + jnp.sum(e, axis=-1, keepdims=True)
    return es, 1.0 / l


def _bdot(a, b):
    return jnp.dot(a, b, preferred_element_type=F32)


def _pipelined_heads(n_heads, scores, attend):
    outs = []
    queue = [scores(hd) for hd in range(min(HEAD_LOOKAHEAD, n_heads))]
    for hd in range(n_heads):
        if hd + HEAD_LOOKAHEAD < n_heads:
            queue.append(scores(hd + HEAD_LOOKAHEAD))
        es, rl = _softmax_parts(queue.pop(0))
        outs.append(attend(hd, es) * rl)
    return outs


def _pair_attention(q_blocks, sources, qmap):
    lo = _lane(q_blocks[0].shape) < HEAD_DIM

    def scores(hd):
        p, half = divmod(hd, 2)
        qm = jnp.where(lo if half == 0 else jnp.logical_not(lo), q_blocks[p], 0.0).astype(BF16)
        return [_bdot(qm, k_block(qmap[p])) if transposed
                else lax.dot_general(qm, k_block(qmap[p]), _NT, preferred_element_type=F32)
                for k_block, _, transposed in sources]

    def attend(hd, es):
        kv = qmap[hd // 2]
        parts = [lax.dot_general(e.astype(BF16), v_block(kv), _NT, preferred_element_type=F32) if transposed
                 else _bdot(e.astype(BF16), v_block(kv)) for e, (_, v_block, transposed) in zip(es, sources)]
        return functools.reduce(lambda a, b: a + b, parts)

    outs = _pipelined_heads(2 * len(qmap), scores, attend)
    return [jnp.where(lo, outs[2 * p], outs[2 * p + 1]) for p in range(len(qmap))]


def _mla_attention(q_heads, k_head, v_block):
    lo = _lane(q_heads[0].shape) < MLA_V

    def scores(hd):
        return [lax.dot_general(q_heads[hd], k_head(hd), _NT, preferred_element_type=F32)]

    def attend(hd, es):
        return _bdot(es[0].astype(BF16), v_block(hd // 2))

    outs = _pipelined_heads(MLA_HEADS, scores, attend)
    return [jnp.where(lo, outs[2 * p], outs[2 * p + 1]) for p in range(MLA_HEADS // 2)]


def _mod_kernel(c_ref, w_ref, b_ref, o_ref):
    s = _silu(c_ref[...]).astype(BF16)
    o_ref[...] = _bdot(s, w_ref[...].astype(BF16)) + b_ref[...]


def _modulation(cond, w_mod, b_mod):
    n = MOD_CHUNKS * D_MODEL
    tn = 1536
    return pl.pallas_call(
        _mod_kernel,
        grid=(DEPTH, n // tn),
        in_specs=[pl.BlockSpec((16, D_MODEL), lambda l, j: (0, 0)),
                  pl.BlockSpec((None, D_MODEL, tn), lambda l, j: (l, 0, j)),
                  pl.BlockSpec((None, 1, tn), lambda l, j: (l, 0, j))],
        out_specs=pl.BlockSpec((None, 16, tn), lambda l, j: (l, 0, j)),
        out_shape=jax.ShapeDtypeStruct((DEPTH, 16, n), F32),
        compiler_params=_cparams(2),
        name="modulation",
    )(cond, w_mod, b_mod.reshape(DEPTH, 1, n))


_IN_OUT_WIDTHS = (256, 128, 128, 256, 256, 256, 1024, 512, 128, 128, 128)
_IN_OUT_WIDTHS_CTX = (256, 128, 128, 256, 256, 256, 1024, 256, 128, 128, 128)
_IN_OUT_DTYPES_LAT = (BF16, BF16, BF16, BF16, BF16, BF16, F32, BF16, BF16, F32, BF16)
_IN_OUT_DTYPES_CTX = (BF16, F32, F32, BF16, F32, F32, F32, BF16, F32, F32, F32)


def _inproj_kernel(positioned, *refs):
    (x_ref, mod_ref, g1_ref, w_ref, qng_ref, kng_ref, mqg_ref, wq_ref, mkg_ref) = refs[:9]
    if positioned:
        cos64_ref, sin64_ref, cosm_ref, sinm_ref, coskr_ref, sinkr_ref = refs[9:15]
        n_in = 15
    else:
        wk_ref, wv_ref = refs[9:11]
        n_in = 11
    (qa_ref, ka_ref, va_ref, qb_ref, kb_ref, vb_ref, zc_ref, qd_ref, ckv_ref, zm_ref,
     krr_ref) = refs[n_in:]
    tiles = range(TILES_PER_STEP)
    tm = x_ref.shape[1]
    lane = _lane((tm, LANES))
    lo = lane < HEAD_DIM

    m = mod_ref[...]
    hb = jnp.concatenate([(_rms_full(x_ref[t], g1_ref[...]) * (1.0 + m[1:2]) + m[0:1]).astype(BF16)
                          for t in tiles], axis=0)

    def project(c0, c1):
        z = lax.dot_general(hb, w_ref[c0:c1, :], _NT, preferred_element_type=F32)
        return [z[t * tm:(t + 1) * tm] for t in tiles]

    za = project(0, 512)
    zb = project(512, 1280)

    for t in tiles:
        q = _head_rms(za[t][:, 0:256], qng_ref[...])
        k = _head_rms(za[t][:, 256:384], kng_ref[...])
        q0, q1 = q[:, 0:128], q[:, 128:256]
        q0, q1 = jnp.where(lo, q0, pltpu.roll(q1, HEAD_DIM, 1)), jnp.where(lo, pltpu.roll(q0, HEAD_DIM, 1), q1)
        if positioned:
            cos, sin = cos64_ref[t], sin64_ref[t]
            q0, q1 = _rope(q0, cos, sin, 16), _rope(q1, cos, sin, 16)
            k = _rope(k, cos, sin, 16)
        q0, q1 = q0 * (HEAD_DIM ** -0.5 * LOG2E), q1 * (HEAD_DIM ** -0.5 * LOG2E)
        v = za[t][:, 384:512]
        if not positioned:
            kb16, vb16 = k.astype(BF16), v.astype(BF16)
            q0, q1 = _pair_attention([q0, q1], [(lambda i: kb16, lambda i: vb16, False)], (0, 0))
        qa_ref[t] = jnp.concatenate([q0, q1], axis=-1).astype(qa_ref.dtype)
        ka_ref[t] = k.astype(ka_ref.dtype) if positioned else k.T
        va_ref[t] = v.astype(va_ref.dtype) if positioned else v.T

    zd = project(2304, IN_PAD_COLS)
    zc = project(1280, 2304)

    for t in tiles:
        q = zb[t][:, 0:256] * (HEAD_DIM ** -0.5 * LOG2E)
        k, v = zb[t][:, 256:512], zb[t][:, 512:768]
        if not positioned:
            kb16, vb16 = k.astype(BF16), v.astype(BF16)
            q = jnp.concatenate(_pair_attention(
                [q[:, 0:LANES], q[:, LANES:]],
                [(lambda i, kb16=kb16: kb16[:, i * LANES:(i + 1) * LANES],
                  lambda i, vb16=vb16: vb16[:, i * LANES:(i + 1) * LANES], False)], (0, 1)), axis=-1)
        qb_ref[t] = q.astype(qb_ref.dtype)
        kb_ref[t] = k.astype(kb_ref.dtype) if positioned else k.T
        vb_ref[t] = v.astype(vb_ref.dtype) if positioned else v.T

    shifted = []
    for t in tiles:
        rolled = [pltpu.roll(zd[t][:, j * LANES:(j + 1) * LANES], LANES - 2 * DN_GATES, 1) for j in range(4)]
        keep = lane < LANES - 2 * DN_GATES
        shifted.append([jnp.where(keep, rolled[j], rolled[(j + 1) % 4]) for j in range(4)])

    cq = jnp.concatenate([_rms_full(jnp.concatenate(shifted[t][0:2], axis=-1), mqg_ref[...]).astype(BF16)
                          for t in tiles], axis=0)
    qm = _bdot(cq, wq_ref[...])
    for t in tiles:
        zc_ref[t] = zc[t]
        zm_ref[t] = zd[t][:, 0:LANES]
        ckv = _rms_full(shifted[t][2], mkg_ref[...])
        ckv_ref[t] = ckv.astype(ckv_ref.dtype)
        kr = jnp.where(lane < MLA_ROPE, shifted[t][3], 0.0)
        q = qm[t * tm:(t + 1) * tm]
        if positioned:
            kr = _rope(kr, coskr_ref[t], sinkr_ref[t], 8)
            cm, sm = cosm_ref[t], sinm_ref[t]
            q = jnp.concatenate([_rope(q[:, i * LANES:(i + 1) * LANES], cm, sm, 8)
                                 for i in range(MLA_HEADS)], axis=-1)
        krr_ref[t] = kr.astype(krr_ref.dtype)
        q = q * (MLA_SCALE * LOG2E)
        if not positioned:
            c16 = ckv.astype(BF16)
            k16 = _bdot(jnp.concatenate([c16, kr.astype(BF16)], axis=-1), wk_ref[...]).astype(BF16)
            v16 = _bdot(c16, wv_ref[...]).astype(BF16)
            q = jnp.concatenate(_mla_attention(
                [q[:, i * LANES:(i + 1) * LANES].astype(BF16) for i in range(MLA_HEADS)],
                lambda i, k16=k16: k16[:, i * LANES:(i + 1) * LANES],
                lambda i, v16=v16: v16[:, i * LANES:(i + 1) * LANES]), axis=-1)
        qd_ref[t] = q.astype(qd_ref.dtype)


def _inproj(x, mods, per_batch_mods, lw, layer, ropes):
    b, l, _ = x.shape
    tm, ts = ROW_TILE, TILES_PER_STEP
    n_tiles = b * l // tm
    tiles_per_seq = l // tm
    positioned = ropes is not None
    row = lambda w: pl.BlockSpec((ts, tm, w), lambda i: (i, 0, 0))
    const = lambda shape: pl.BlockSpec((None,) + shape, lambda i: (layer,) + (0,) * len(shape))
    assert not per_batch_mods or tiles_per_seq % ts == 0
    mod_spec = pl.BlockSpec((None, None, MOD_CHUNKS, D_MODEL),
                            (lambda i: (layer, 1 + i * ts // tiles_per_seq, 0, 0)) if per_batch_mods
                            else (lambda i: (layer, 0, 0, 0)))
    in_specs = [row(D_MODEL), mod_spec, const((1, D_MODEL)), const((IN_PAD_COLS, D_MODEL)),
                const((1, 256)), const((1, 128)), const((1, MLA_Q_LORA)),
                const((MLA_Q_LORA, 4 * LANES)), const((1, MLA_KV_LORA))]
    args = [x.reshape(n_tiles, tm, D_MODEL), mods, lw["norm1_g"], lw["w_in"], lw["qn_g"], lw["kn_g"],
            lw["mla_qn_g"], lw["wq"], lw["mla_kvn_g"]]
    if positioned:
        steps_per_seq = tiles_per_seq // ts
        in_specs += [pl.BlockSpec((ts, tm, LANES), lambda i: (i % steps_per_seq, 0, 0))] * 6
        args += [r.reshape(tiles_per_seq, tm, LANES) for r in ropes]
        widths, dtypes = _IN_OUT_WIDTHS, _IN_OUT_DTYPES_LAT
    else:
        assert tiles_per_seq == 1
        in_specs += [const((2 * LANES, 4 * LANES)), const((LANES, 2 * LANES))]
        args += [lw["wk"], lw["wv"]]
        widths, dtypes = _IN_OUT_WIDTHS_CTX, _IN_OUT_DTYPES_CTX
    flipped = () if positioned else (1, 2, 4, 5)
    shapes = [(n_tiles, w, tm) if i in flipped else (n_tiles, tm, w) for i, w in enumerate(widths)]
    outs = pl.pallas_call(
        functools.partial(_inproj_kernel, positioned),
        grid=(n_tiles // ts,),
        in_specs=in_specs,
        out_specs=[pl.BlockSpec((ts,) + shp[1:], lambda i: (i, 0, 0)) for shp in shapes],
        out_shape=[jax.ShapeDtypeStruct(shp, dt) for shp, dt in zip(shapes, dtypes)],
        compiler_params=_cparams(1),
        name="inproj_lat" if positioned else "inproj_attn_ctx",
    )(*args)
    return [o if i in flipped else o.reshape(b, l, o.shape[-1]) for i, o in enumerate(outs)]


def _attn_pair_kernel(qmap, q_ref, kc_ref, vc_ref, k_ref, v_ref, o_ref, kcbuf, vcbuf, kbuf, vbuf):
    @pl.when(pl.program_id(1) == 0)
    def _():
        kcbuf[...] = kc_ref[...].astype(BF16)
        vcbuf[...] = vc_ref[...].astype(BF16)
        kbuf[...] = k_ref[...].astype(BF16)
        vbuf[...] = v_ref[...].astype(BF16)

    rows = lambda buf: (lambda i: buf[i * LANES:(i + 1) * LANES, :])
    cols = lambda buf: (lambda i: buf[:, i * LANES:(i + 1) * LANES])
    q_blocks = [q_ref[:, p * LANES:(p + 1) * LANES].astype(F32) for p in range(len(qmap))]
    outs = _pair_attention(q_blocks, [(rows(kcbuf), rows(vcbuf), True), (cols(kbuf), cols(vbuf), False)], qmap)
    for p, o in enumerate(outs):
        o_ref[:, p * LANES:(p + 1) * LANES] = o.astype(o_ref.dtype)


def _attn_pair(q, k, v, qmap, kc_t, vc_t, layer, name):
    b, lq, wq = q.shape
    ls, wk = k.shape[1], k.shape[2]
    lc = kc_t.shape[3]
    tq = min(ATTN_Q_TILE, lq)
    cspec = pl.BlockSpec((None, None, wk, lc), lambda bi, i: (bi, layer, 0, 0))
    sspec = pl.BlockSpec((None, ls, wk), lambda bi, i: (bi, 0, 0))
    return pl.pallas_call(
        functools.partial(_attn_pair_kernel, qmap),
        grid=(b, lq // tq),
        in_specs=[pl.BlockSpec((None, tq, wq), lambda bi, i: (bi, i, 0)), cspec, cspec, sspec, sspec],
        out_specs=pl.BlockSpec((None, tq, wq), lambda bi, i: (bi, i, 0)),
        out_shape=jax.ShapeDtypeStruct((b, lq, wq), BF16),
        scratch_shapes=[pltpu.VMEM((wk, lc), BF16), pltpu.VMEM((wk, lc), BF16),
                        pltpu.VMEM((ls, wk), BF16), pltpu.VMEM((ls, wk), BF16)],
        compiler_params=_cparams(2),
        name=name,
    )(q, kc_t, vc_t, k, v)


def _mla_kernel(has_cache, *refs):
    if has_cache:
        q_ref, ckvc_ref, krc_ref, ckv_ref, kr_ref, wk_ref, wv_ref, o_ref, kbuf, vbuf = refs
    else:
        q_ref, ckv_ref, kr_ref, wk_ref, wv_ref, o_ref, kbuf, vbuf = refs

    @pl.when(pl.program_id(1) == 0)
    def _():
        def expand(c_ref, r_ref, r0, r1):
            c = c_ref[...].astype(BF16)
            ckr = jnp.concatenate([c, r_ref[...].astype(BF16)], axis=-1)
            kbuf[r0:r1, :] = _bdot(ckr, wk_ref[...]).astype(BF16)
            vbuf[r0:r1, :] = _bdot(c, wv_ref[...]).astype(BF16)
        off = 0
        if has_cache:
            off = ckvc_ref.shape[0]
            expand(ckvc_ref, krc_ref, 0, off)
        expand(ckv_ref, kr_ref, off, kbuf.shape[0])

    outs = _mla_attention([q_ref[:, hd * LANES:(hd + 1) * LANES].astype(BF16) for hd in range(MLA_HEADS)],
                          lambda i: kbuf[:, i * LANES:(i + 1) * LANES],
                          lambda i: vbuf[:, i * LANES:(i + 1) * LANES])
    for p, o in enumerate(outs):
        o_ref[:, p * LANES:(p + 1) * LANES] = o.astype(o_ref.dtype)


def _mla(q, ckv, kr, lw, layer, cache=None, name="mla"):
    b, lq, wq = q.shape
    ls = ckv.shape[1]
    tq = min(ATTN_Q_TILE, lq)
    lc = 0 if cache is None else cache[0].shape[2]
    in_specs = [pl.BlockSpec((None, tq, wq), lambda bi, i: (bi, i, 0))]
    args = [q]
    if cache is not None:
        ckvc, krc = cache
        cspec = pl.BlockSpec((None, None, lc, LANES), lambda bi, i: (bi, layer, 0, 0))
        in_specs += [cspec, cspec]
        args += [ckvc, krc]
    sspec = pl.BlockSpec((None, ls, LANES), lambda bi, i: (bi, 0, 0))
    in_specs += [sspec, sspec,
                 pl.BlockSpec((None, 2 * LANES, 4 * LANES), lambda bi, i: (layer, 0, 0)),
                 pl.BlockSpec((None, LANES, 2 * LANES), lambda bi, i: (layer, 0, 0))]
    args += [ckv, kr, lw["wk"], lw["wv"]]
    return pl.pallas_call(
        functools.partial(_mla_kernel, cache is not None),
        grid=(b, lq // tq),
        in_specs=in_specs,
        out_specs=pl.BlockSpec((None, tq, 2 * LANES), lambda bi, i: (bi, i, 0)),
        out_shape=jax.ShapeDtypeStruct((b, lq, 2 * LANES), BF16),
        scratch_shapes=[pltpu.VMEM((lc + ls, 4 * LANES), BF16), pltpu.VMEM((lc + ls, 2 * LANES), BF16)],
        compiler_params=_cparams(2),
        name=name,
    )(*args)


def _na_kernel(q_ref, k_ref, v_ref, kc_ref, vc_ref, bias_ref, o_ref):
    j = pl.program_id(0)
    start = pl.multiple_of((j >> 1) * 256, 256)
    kband = k_ref[pl.ds(start, NA_BAND), :].astype(BF16)
    vband = v_ref[pl.ds(start, NA_BAND), :].astype(BF16)
    kc = kc_ref[...].astype(BF16)
    vc = vc_ref[...].astype(BF16)
    tq = q_ref.shape[0]
    lo = _lane((tq, LANES)) < HEAD_DIM

    def scores(hd):
        p, half = divmod(hd, 2)
        sl = slice(p * LANES, (p + 1) * LANES)
        qm = jnp.where(lo if half == 0 else jnp.logical_not(lo), q_ref[:, sl].astype(F32), 0.0).astype(BF16)
        s_loc = lax.dot_general(qm, kband[:, sl], _NT, preferred_element_type=F32) + bias_ref[hd]
        return [s_loc, _bdot(qm, kc[sl, :])]

    def attend(hd, es):
        sl = slice((hd // 2) * LANES, (hd // 2 + 1) * LANES)
        return (_bdot(es[0].astype(BF16), vband[:, sl])
                + lax.dot_general(es[1].astype(BF16), vc[sl, :], _NT, preferred_element_type=F32))

    outs = _pipelined_heads(NA_HEADS, scores, attend)
    for p in range(NA_HEADS // 2):
        o_ref[:, p * LANES:(p + 1) * LANES] = jnp.where(lo, outs[2 * p], outs[2 * p + 1]).astype(o_ref.dtype)


def _na_latent(q, k, v, kc_t, vc_t, layer, bias_blocks):
    b, n, w = q.shape
    lc = kc_t.shape[3]
    nq = n // Q_TILE
    full = pl.BlockSpec((None, n, w), lambda j, bi: (bi, 0, 0))
    cspec = pl.BlockSpec((None, None, w, lc), lambda j, bi: (bi, layer, 0, 0))
    return pl.pallas_call(
        _na_kernel,
        grid=(nq, b),
        in_specs=[pl.BlockSpec((None, Q_TILE, w), lambda j, bi: (bi, j, 0)), full, full, cspec, cspec,
                  pl.BlockSpec((None, None, NA_HEADS, Q_TILE, NA_BAND), lambda j, bi: (layer, j, 0, 0, 0))],
        out_specs=pl.BlockSpec((None, Q_TILE, w), lambda j, bi: (bi, j, 0)),
        out_shape=jax.ShapeDtypeStruct((b, n, w), BF16),
        compiler_params=_cparams(2),
        name="na_latent",
    )(q, k, v, kc_t, vc_t, bias_blocks)


NA_GRID_ROWS = 16
NA_BAND_ROW0 = (0, 0, 4, 4)


def _na_bias_kernel(b_ref, o_ref, tp_s):
    hd = pl.program_id(0)
    n_dr, n_dc = 2 * NA_KH - 1, 2 * NA_KW - 1
    shape = (GRID_W, LANES)
    c = lax.broadcasted_iota(jnp.int32, shape, 0)
    lane = _lane(shape)
    kc = lane & (GRID_W - 1)
    lo = lane < GRID_W
    diff = kc - c + (NA_KW - 1)
    c0 = jnp.clip(c - NA_KW // 2, 0, GRID_W - NA_KW)
    col_ok = (kc >= c0) & (kc < c0 + NA_KW)
    neg = jnp.full(shape, NEG_INF, F32)
    for dr0 in range(-1, n_dr):
        acc = neg
        for d in range(n_dc):
            v_lo = b_ref[hd * n_dr + dr0, d] if dr0 >= 0 else 0.0
            v_hi = b_ref[hd * n_dr + dr0 + 1, d] if dr0 + 1 < n_dr else 0.0
            acc = jnp.where(diff == d, jnp.where(lo, v_lo, v_hi), acc)
        tp_s[dr0 + 1] = jnp.where(col_ok, acc * LOG2E, NEG_INF)
    for j in range(NA_GRID_ROWS // 4):
        for ri in range(4):
            r = 4 * j + ri
            r0 = min(max(r - NA_KH // 2, 0), NA_GRID_ROWS - NA_KH)
            for kp in range(NA_BAND // LANES):
                kr = NA_BAND_ROW0[j] + 2 * kp
                ok_lo, ok_hi = r0 <= kr < r0 + NA_KH, r0 <= kr + 1 < r0 + NA_KH
                dr0 = kr - r + (NA_KH - 1)
                if ok_lo and ok_hi:
                    t = tp_s[dr0 + 1]
                elif ok_lo:
                    t = jnp.where(lo, tp_s[dr0 + 1], NEG_INF)
                elif ok_hi:
                    t = jnp.where(lo, NEG_INF, tp_s[dr0 + 1])
                else:
                    t = neg
                o_ref[j, ri * GRID_W:(ri + 1) * GRID_W, kp * LANES:(kp + 1) * LANES] = t


def _na_bias_blocks(bias):
    nq = NA_GRID_ROWS // 4
    return pl.pallas_call(
        _na_bias_kernel,
        grid=(DEPTH * NA_HEADS,),
        in_specs=[pl.BlockSpec(memory_space=pltpu.SMEM)],
        out_specs=pl.BlockSpec((None, nq, None, Q_TILE, NA_BAND),
                               lambda i: (i // NA_HEADS, 0, i % NA_HEADS, 0, 0)),
        out_shape=jax.ShapeDtypeStruct((DEPTH, nq, NA_HEADS, Q_TILE, NA_BAND), F32),
        scratch_shapes=[pltpu.VMEM((2 * NA_KH, GRID_W, LANES), F32)],
        compiler_params=_cparams(1),
        name="na_bias",
    )(bias.reshape(DEPTH * NA_HEADS * (2 * NA_KH - 1), 2 * NA_KW - 1))


def _widen(cols, n):
    blk = _lane((n, DN_HEADS * DN_DV)) >> 6
    return jnp.where(blk == 0, cols[0], jnp.where(blk == 1, cols[1], jnp.where(blk == 2, cols[2], cols[3])))


def _deltanet_kernel(seq, has_state, *refs):
    if has_state:
        (zc_ref, zm_ref, s0_ref, cw_ref, alog_ref, dtb_ref, og_ref, o_ref,
         q_s, k_s, v_s, b_s, g_s, o_s, c_s, mp_s) = refs
    else:
        (zc_ref, zm_ref, cw_ref, alog_ref, dtb_ref, og_ref, o_ref, sfin_ref,
         q_s, k_s, v_s, b_s, g_s, o_s, c_s, mp_s) = refs
    n_chunks = seq // DN_CHUNK
    wide = DN_HEADS * DN_DV

    a_off, b_off = 0, DN_GATES
    ri = lax.broadcasted_iota(jnp.int32, (DN_ROWS, DN_ROWS), 0)
    ci = lax.broadcasted_iota(jnp.int32, (DN_ROWS, DN_ROWS), 1)
    same = (ri >> 6) == (ci >> 6)
    tri = [jnp.tile(jnp.where(same & ((ci <= ri) if d == 0 else (ci >= ri)), 1.0, 0.0).astype(BF16), (1, 3))
           for d in range(2)]
    n_blocks = seq // DN_ROWS

    def split3(x):
        hi = x.astype(BF16)
        rest = x - hi.astype(F32)
        mid = rest.astype(BF16)
        return jnp.concatenate([hi, mid, (rest - mid.astype(F32)).astype(BF16)], axis=0)

    def preprocess(rb, carry):
        r0 = pl.multiple_of(rb * DN_ROWS, DN_ROWS)
        rows = pl.ds(r0, DN_ROWS)
        before = pl.ds(pl.multiple_of(jnp.maximum(r0 - 8, 0), 8), 8)
        after = pl.ds(pl.multiple_of(jnp.minimum(r0 + DN_ROWS, seq - 8), 8), 8)
        for part, dst in enumerate((q_s, k_s, v_s)):
            cs = slice(part * wide, (part + 1) * wide)
            head = jnp.where(rb > 0, zc_ref[before, cs], 0.0)
            tail = jnp.where(rb < n_blocks - 1, zc_ref[after, cs], 0.0)
            xe = jnp.concatenate([head, zc_ref[rows, cs], tail], axis=0)
            w = cw_ref[:, cs]
            y = (w[0:1] * xe[7:7 + DN_ROWS] + w[1:2] * xe[8:8 + DN_ROWS]
                 + w[2:3] * xe[9:9 + DN_ROWS] + w[3:4] * xe[10:10 + DN_ROWS])
            y = _silu(y)
            if part == 0:
                y = _head_l2(y) * (DN_DK ** -0.5)
            elif part == 1:
                y = _head_l2(y)
            dst[rows, :] = y
        zm = zm_ref[rows, :]
        xa = zm + dtb_ref[...]
        logd = -jnp.exp(alog_ref[...]) * (jnp.maximum(xa, 0.0) + jnp.log1p(jnp.exp(-jnp.abs(xa))))
        beta = 1.0 / (1.0 + jnp.exp(-zm))
        logd3 = split3(logd)
        for d in range(2):
            b_s[d, rows, :] = _widen([beta[:, b_off + 4 * d + hd:b_off + 4 * d + hd + 1]
                                      for hd in range(DN_HEADS)], DN_ROWS)
            g = _bdot(tri[d], logd3)
            g_s[d, rows, :] = _widen([g[:, a_off + 4 * d + hd:a_off + 4 * d + hd + 1]
                                      for hd in range(DN_HEADS)], DN_ROWS)
        return carry

    lax.fori_loop(0, n_blocks, preprocess, 0)

    ii = lax.broadcasted_iota(jnp.int32, (DN_CHUNK, wide), 0)
    jj = _lane((DN_CHUNK, wide)) & (DN_CHUNK - 1)
    blk = _lane((DN_CHUNK, wide)) >> 6
    diag = ii == jj
    eye = jnp.where(diag, 1.0, 0.0)
    head_mask = [jnp.where(blk == hd, 1.0, 0.0).astype(BF16) for hd in range(DN_HEADS)]

    def bd(z):
        zb = z.astype(BF16)
        return jnp.concatenate([zb * hm for hm in head_mask], axis=0)

    def fold(gram):
        out = jnp.where(blk == 0, gram[0:DN_CHUNK], 0.0)
        for hd in range(1, DN_HEADS):
            out = out + jnp.where(blk == hd, gram[hd * DN_CHUNK:(hd + 1) * DN_CHUNK], 0.0)
        return out

    tri_masks = []
    for d in range(2):
        incl = (jj <= ii) if d == 0 else (jj >= ii)
        strict = (jj < ii) if d == 0 else (jj > ii)
        pair = [((ii >> (lvl + 1)) == (jj >> (lvl + 1)))
                & (((ii >> lvl) & 1) == (1 - d)) & (((jj >> lvl) & 1) == d) for lvl in range(6)]
        tri_masks.append((incl, strict, pair))

    def prepare(step, carry):
        chunks = [step * PREP_CHUNKS + i for i in range(PREP_CHUNKS)]
        rows = [pl.ds(pl.multiple_of(c * DN_CHUNK, DN_CHUNK), DN_CHUNK) for c in chunks]
        qkv = [(q_s[r, :], k_s[r, :], v_s[r, :]) for r in rows]
        inst = [(ci, d) for ci in range(PREP_CHUNKS) for d in range(2)]
        beta = {(ci, d): b_s[d, rows[ci], :] for ci, d in inst}
        kb = {(ci, d): qkv[ci][1] * beta[ci, d] for ci, d in inst}
        r = [lax.dot_general(jnp.concatenate([kb[ci, 0], kb[ci, 1], qkv[ci][0]], axis=0).astype(BF16),
                             bd(qkv[ci][1]), _NT, preferred_element_type=F32)
             for ci in range(PREP_CHUNKS)]
        g, a, qk, eg, t = {}, {}, {}, {}, {}
        for ci, d in inst:
            incl, strict, pair = tri_masks[d]
            g[ci, d] = g_s[d, rows[ci], :]
            g_row = jnp.sum(jnp.where(diag, g[ci, d], 0.0), axis=0, keepdims=True)
            dm = jnp.where(incl, jnp.exp(jnp.where(incl, g[ci, d] - g_row, 0.0)), 0.0)
            a[ci, d] = jnp.where(strict, r[ci][d * DN_CHUNK:(d + 1) * DN_CHUNK] * dm, 0.0)
            qk[ci, d] = (r[ci][2 * DN_CHUNK:] * dm).astype(BF16)
            eg[ci, d] = jnp.exp(g[ci, d])
            t[ci, d] = eye - jnp.where(pair[0], a[ci, d], 0.0)
        for lvl in range(1, 6):
            te = {i: _bdot(t[i].astype(BF16), bd(jnp.where(tri_masks[i[1]][2][lvl], a[i], 0.0))) for i in inst}
            t = {i: t[i] - _bdot(te[i].astype(BF16), bd(t[i])) for i in inst}
        nb = {i: jnp.where(diag, 0.0, t[i]).astype(BF16) for i in inst}
        rhs_u = {(ci, d): qkv[ci][2] * beta[ci, d] for ci, d in inst}
        rhs_w = {i: kb[i] * eg[i] for i in inst}
        u = {i: rhs_u[i] + _bdot(nb[i], bd(rhs_u[i])) for i in inst}
        w = {i: rhs_w[i] + _bdot(nb[i], bd(rhs_w[i])) for i in inst}
        kd = {}
        for ci, d in inst:
            g_last = g[ci, d][DN_CHUNK - 1:DN_CHUNK] if d == 0 else g[ci, d][0:1]
            kd[ci, d] = (qkv[ci][1] * jnp.exp(g_last - g[ci, d])).astype(BF16)
        p = {(ci, d): qkv[ci][0] * eg[ci, d] - _bdot(qk[ci, d], bd(w[ci, d])) for ci, d in inst}
        o0 = {i: _bdot(qk[i], bd(u[i])) for i in inst}
        m = {i: fold(lax.dot_general(kd[i], w[i].astype(BF16), _TN, preferred_element_type=F32)) for i in inst}
        cc = {i: fold(lax.dot_general(kd[i], u[i].astype(BF16), _TN, preferred_element_type=F32)) for i in inst}
        for ci, d in inst:
            mrow = pl.ds(pl.multiple_of(chunks[ci] * (2 * DN_CHUNK), 2 * DN_CHUNK), 2 * DN_CHUNK)
            c_s[d, rows[ci], :] = cc[ci, d]
            mp_s[d, mrow, :] = jnp.concatenate([m[ci, d], p[ci, d]], axis=0).astype(BF16)
        for ci in range(PREP_CHUNKS):
            o_s[rows[ci], :] = o0[ci, 0] + o0[ci, 1]
        return carry

    lax.fori_loop(0, n_chunks // PREP_CHUNKS, prepare, 0)

    def scan(i, states):
        new = []
        for d in range(2):
            c = i if d == 0 else n_chunks - 1 - i
            r0 = pl.multiple_of(c * DN_CHUNK, DN_CHUNK)
            rows = pl.ds(r0, DN_CHUNK)
            mrow = pl.ds(pl.multiple_of(c * (2 * DN_CHUNK), 2 * DN_CHUNK), 2 * DN_CHUNK)
            edge = pl.ds(pl.multiple_of(r0 + (DN_CHUNK - 8 if d == 0 else 0), 8), 8)
            g_edge = g_s[d, edge, :]
            g_last = g_edge[7:8] if d == 0 else g_edge[0:1]
            res = _bdot(mp_s[d, mrow, :], bd(states[d]))
            o_s[rows, :] = o_s[rows, :] + res[DN_CHUNK:]
            new.append(states[d] * jnp.exp(g_last) - res[0:DN_CHUNK] + c_s[d, rows, :])
        return tuple(new)

    init = tuple(s0_ref[d] if has_state else jnp.zeros((DN_DK, wide), F32) for d in range(2))
    fin = lax.fori_loop(0, n_chunks, scan, init)
    if not has_state:
        sfin_ref[0] = fin[0]
        sfin_ref[1] = fin[1]

    o_ref[...] = (_head_rms(o_s[...], og_ref[...]) * _silu(zc_ref[:, 3 * wide:4 * wide])).astype(o_ref.dtype)


def _deltanet(zc, zm, lw, layer, state=None):
    b, seq, _ = zc.shape
    wide = DN_HEADS * DN_DV
    has_state = state is not None
    per_b = lambda w: pl.BlockSpec((None, seq, w), lambda bi: (bi, 0, 0))
    const = lambda shape: pl.BlockSpec((None,) + shape, lambda bi: (layer,) + (0,) * len(shape))
    st_spec = pl.BlockSpec((None, 2, DN_DK, wide), lambda bi: (bi, 0, 0, 0))
    in_specs = [per_b(4 * wide), per_b(LANES)]
    args = [zc, zm]
    if has_state:
        in_specs.append(pl.BlockSpec((None, None, 2, DN_DK, wide), lambda bi: (bi, layer, 0, 0, 0)))
        args.append(state)
    in_specs += [const((DN_CONV, DN_QKV)), const((1, LANES)), const((1, LANES)), const((1, wide))]
    args += [lw["dn_conv_w"], lw["dn_alog_row"], lw["dn_dtb_row"], lw["dn_out_g"]]
    out_specs = [per_b(wide)]
    out_shape = [jax.ShapeDtypeStruct((b, seq, wide), BF16)]
    if not has_state:
        out_specs.append(st_spec)
        out_shape.append(jax.ShapeDtypeStruct((b, 2, DN_DK, wide), F32))
    res = pl.pallas_call(
        functools.partial(_deltanet_kernel, seq, has_state),
        grid=(b,),
        in_specs=in_specs,
        out_specs=out_specs,
        out_shape=out_shape,
        scratch_shapes=[pltpu.VMEM((seq, wide), F32), pltpu.VMEM((seq, wide), F32),
                        pltpu.VMEM((seq, wide), F32), pltpu.VMEM((2, seq, wide), F32),
                        pltpu.VMEM((2, seq, wide), F32), pltpu.VMEM((seq, wide), F32),
                        pltpu.VMEM((2, seq, wide), F32), pltpu.VMEM((2, 2 * seq, wide), BF16)],
        compiler_params=_cparams(1),
        name="deltanet_lat" if has_state else "deltanet_ctx",
    )(*args)
    return (res[0], None) if has_state else (res[0], res[1])


def _outffn_kernel(final, oa_ref, ob_ref, oc_ref, od_ref, x_ref, mod_ref, g2_ref, wo_ref, wg_ref,
                   wu_ref, wd_ref, fg_ref, y_ref):
    tiles = range(TILES_PER_STEP)
    m = mod_ref[...]
    o = [jnp.concatenate([oa_ref[t], ob_ref[t], oc_ref[t], od_ref[t]], axis=-1).astype(BF16) for t in tiles]
    x1 = [x_ref[t] + m[2:3] * _bdot(o[t], wo_ref[...]) for t in tiles]
    h = [(_rms_full(x1[t], g2_ref[...]) * (1.0 + m[4:5]) + m[3:4]).astype(BF16) for t in tiles]
    gate = [_bdot(h[t], wg_ref[...]) for t in tiles]
    up = [_bdot(h[t], wu_ref[...]) for t in tiles]
    act = [(_silu(gate[t]) * up[t]).astype(BF16) for t in tiles]
    x2 = [x1[t] + m[5:6] * _bdot(act[t], wd_ref[...]) for t in tiles]
    for t in tiles:
        y_ref[t] = _rms_full(x2[t], fg_ref[...]) if final else x2[t]


def _outffn(outs, x, mods, per_batch_mods, lw, layer, final_g, final):
    b, l, _ = x.shape
    tm, ts = ROW_TILE, TILES_PER_STEP
    n_tiles = b * l // tm
    tiles_per_seq = l // tm
    assert not per_batch_mods or tiles_per_seq % ts == 0
    row = lambda w: pl.BlockSpec((ts, tm, w), lambda i: (i, 0, 0))
    const = lambda shape: pl.BlockSpec((None,) + shape, lambda i: (layer,) + (0,) * len(shape),
                                       pipeline_mode=pl.Buffered(1))
    mod_spec = pl.BlockSpec((None, None, MOD_CHUNKS, D_MODEL),
                            (lambda i: (layer, 1 + i * ts // tiles_per_seq, 0, 0)) if per_batch_mods
                            else (lambda i: (layer, 0, 0, 0)))
    tiled = lambda a: a.reshape(n_tiles, tm, a.shape[-1])
    y = pl.pallas_call(
        functools.partial(_outffn_kernel, final),
        grid=(n_tiles // ts,),
        in_specs=[row(256), row(256), row(256), row(256), row(D_MODEL), mod_spec, const((1, D_MODEL)),
                  const((D_MODEL, D_MODEL)), const((D_MODEL, D_FF)), const((D_MODEL, D_FF)),
                  const((D_FF, D_MODEL)), pl.BlockSpec((1, D_MODEL), lambda i: (0, 0))],
        out_specs=row(D_MODEL),
        out_shape=jax.ShapeDtypeStruct((n_tiles, tm, D_MODEL), F32),
        compiler_params=_cparams(1),
        name="outffn",
    )(*[tiled(a) for a in outs], tiled(x), mods, lw["norm2_g"], lw["w_out"], lw["w_gate"], lw["w_up"],
      lw["w_down"], final_g)
    return y.reshape(b, l, D_MODEL)


def _rope_tables(n):
    t = jnp.arange(n)

    def axis(pos, half):
        inv = ROPE_BASE ** (-jnp.arange(half, dtype=F32) / half)
        ang = pos.astype(F32)[:, None] * inv[None, :]
        c, s = jnp.cos(ang), jnp.sin(ang)
        return jnp.concatenate([c, c], -1), jnp.concatenate([-s, s], -1)

    cr, sr = axis(t // GRID_W, 16)
    cc, sc = axis(t % GRID_W, 16)
    cos64 = jnp.tile(jnp.concatenate([cr, cc], -1), (1, 2))
    sin64 = jnp.tile(jnp.concatenate([sr, sc], -1), (1, 2))
    cr, sr = axis(t // GRID_W, 8)
    cc, sc = axis(t % GRID_W, 8)
    cos32, sin32 = jnp.concatenate([cr, cc], -1), jnp.concatenate([sr, sc], -1)
    one, zero = jnp.ones((n, 1), F32), jnp.zeros((n, 1), F32)
    cosm = jnp.concatenate([jnp.tile(one, (1, 64)), cos32, jnp.tile(one, (1, 32))], -1)
    sinm = jnp.concatenate([jnp.tile(zero, (1, 64)), sin32, jnp.tile(zero, (1, 32))], -1)
    coskr = jnp.concatenate([cos32, jnp.tile(one, (1, 96))], -1)
    sinkr = jnp.concatenate([sin32, jnp.tile(zero, (1, 96))], -1)
    return cos64, sin64, cosm, sinm, coskr, sinkr


_QA_ORDER = ((0, 64), (128, 192), (64, 128), (192, 256))


def _stacked_weights(p):
    w_in = jnp.swapaxes(p["w_in"], 1, 2)
    w_in = jnp.pad(w_in, ((0, 0), (0, IN_PAD_COLS - w_in.shape[1]), (0, 0))).astype(BF16)
    w_out = p["w_out"]
    w_out = jnp.concatenate([w_out[:, a:b] for a, b in _QA_ORDER + ((256, w_out.shape[1]),)], axis=1).astype(BF16)
    wq = p["mla_wq_up"].reshape(DEPTH, MLA_Q_LORA, MLA_HEADS, MLA_NOPE + MLA_ROPE)
    wq = jnp.pad(wq, ((0, 0), (0, 0), (0, 0), (0, LANES - MLA_NOPE - MLA_ROPE))).reshape(DEPTH, MLA_Q_LORA, 4 * LANES)
    wkv = p["mla_wkv_up"].reshape(DEPTH, MLA_KV_LORA, MLA_HEADS, MLA_NOPE + MLA_V)
    wk_top = jnp.pad(wkv[..., :MLA_NOPE], ((0, 0), (0, 0), (0, 0), (0, LANES - MLA_NOPE)))
    place = jnp.pad(jnp.eye(MLA_ROPE, dtype=F32), ((0, LANES - MLA_ROPE), (MLA_NOPE, LANES - MLA_NOPE - MLA_ROPE)))
    wk_bot = jnp.broadcast_to(place[None, :, None, :], (DEPTH, LANES, MLA_HEADS, LANES))
    wk = jnp.concatenate([wk_top, wk_bot], axis=1).reshape(DEPTH, 2 * LANES, 4 * LANES)
    wv = wkv[..., MLA_NOPE:].reshape(DEPTH, MLA_KV_LORA, MLA_HEADS * MLA_V)
    gate_row = lambda v: jnp.pad(v.reshape(DEPTH, 1, DN_GATES), ((0, 0), (0, 0), (0, LANES - DN_GATES)))
    row = lambda v: v[:, None, :]
    return {
        "norm1_g": row(p["norm1_g"]), "norm2_g": row(p["norm2_g"]),
        "w_in": w_in, "w_out": w_out,
        "qn_g": row(jnp.tile(p["gqa_qn_g"], (1, 4))), "kn_g": row(jnp.tile(p["gqa_kn_g"], (1, 2))),
        "mla_qn_g": row(p["mla_qn_g"]), "mla_kvn_g": row(p["mla_kvn_g"]),
        "wq": wq.astype(BF16), "wk": wk.astype(BF16), "wv": wv.astype(BF16),
        "dn_conv_w": p["dn_conv_w"], "dn_alog_row": gate_row(p["dn_a_log"]),
        "dn_dtb_row": gate_row(p["dn_dt_bias"]), "dn_out_g": row(jnp.tile(p["dn_out_g"], (1, DN_HEADS))),
        "w_gate": p["ffn_w_gate"].astype(BF16), "w_up": p["ffn_w_up"].astype(BF16),
        "w_down": p["ffn_w_down"].astype(BF16),
    }


def _state_to_wide(s):
    b = s.shape[0]
    return s.transpose(0, 1, 2, 4, 3, 5).reshape(b, DEPTH, 2, DN_DK, DN_HEADS * DN_DV)


def _state_from_wide(s):
    b = s.shape[0]
    return s.reshape(b, 2, DN_DK, DN_HEADS, DN_DV).transpose(0, 1, 3, 2, 4)


def kernel(x_prompt, x_sample, cache_gqa_k, cache_gqa_v, cache_na_k, cache_na_v, state_dn,
           cache_mla_ckv, cache_mla_krope, c, c_ctx, norm1_g, norm2_g, w_mod, b_mod, w_in, w_out,
           gqa_qn_g, gqa_kn_g, na_bias, dn_conv_w, dn_a_log, dn_dt_bias, dn_out_g, mla_qn_g,
           mla_wq_up, mla_kvn_g, mla_wkv_up, ffn_w_gate, ffn_w_up, ffn_w_down, final_g):
    p = {"norm1_g": norm1_g, "norm2_g": norm2_g, "w_in": w_in, "w_out": w_out, "gqa_qn_g": gqa_qn_g,
         "gqa_kn_g": gqa_kn_g, "dn_conv_w": dn_conv_w, "dn_a_log": dn_a_log, "dn_dt_bias": dn_dt_bias,
         "dn_out_g": dn_out_g, "mla_qn_g": mla_qn_g, "mla_wq_up": mla_wq_up, "mla_kvn_g": mla_kvn_g,
         "mla_wkv_up": mla_wkv_up, "ffn_w_gate": ffn_w_gate, "ffn_w_up": ffn_w_up, "ffn_w_down": ffn_w_down}
    nb_ctx, seq_ctx, _ = x_prompt.shape
    nb_lat, seq_lat, _ = x_sample.shape
    past = cache_gqa_k.shape[2]
    fg = final_g[None]

    cond = jnp.concatenate([c_ctx[None], c, jnp.zeros((16 - 1 - nb_lat, D_MODEL), F32)], axis=0)
    mods = _modulation(cond, w_mod, b_mod).reshape(DEPTH, 16, MOD_CHUNKS, D_MODEL)
    lw = _stacked_weights(p)

    x = x_prompt
    ctx_out = []
    for l in range(DEPTH):
        o_a, ka, va, o_b, kb, vb, zc, o_d, ckv, zm, krr = _inproj(x, mods, False, lw, l, None)
        o_c, s_dn = _deltanet(zc, zm, lw, l)
        x = _outffn((o_a, o_b, o_c, o_d), x, mods, False, lw, l, fg, l == DEPTH - 1)
        ctx_out.append((ka, va, kb, vb, _state_from_wide(s_dn), ckv, krr[:, :, :MLA_ROPE]))
    y_prompt = x
    new = [jnp.stack([s[i] for s in ctx_out], axis=1) for i in range(7)]
    for i in range(4):
        t = new[i].reshape(nb_ctx, DEPTH, -1, HEAD_DIM, seq_ctx)
        new[i] = jnp.transpose(t, (0, 1, 4, 2, 3))

    ropes = _rope_tables(seq_lat)
    keys_t = lambda c: jnp.transpose(c, (0, 1, 3, 4, 2)).reshape(nb_lat, DEPTH, -1, past)
    ck_a, cv_a, ck_b, cv_b = keys_t(cache_gqa_k), keys_t(cache_gqa_v), keys_t(cache_na_k), keys_t(cache_na_v)
    c_kr = jnp.pad(cache_mla_krope, ((0, 0), (0, 0), (0, 0), (0, LANES - MLA_ROPE)))
    bias_blocks = _na_bias_blocks(na_bias)
    state = _state_to_wide(state_dn)
    x = x_sample
    for l in range(DEPTH):
        qa, ka, va, qb, kb, vb, zc, qd, ckv, zm, krr = _inproj(x, mods, True, lw, l, ropes)
        o_a = _attn_pair(qa, ka, va, (0, 0), ck_a, cv_a, l, name="gqa_lat")
        o_b = _na_latent(qb, kb, vb, ck_b, cv_b, l, bias_blocks)
        o_c, _ = _deltanet(zc, zm, lw, l, state=state)
        o_d = _mla(qd, ckv, krr, lw, l, cache=(cache_mla_ckv, c_kr), name="mla_lat")
        x = _outffn((o_a, o_b, o_c, o_d), x, mods, True, lw, l, fg, l == DEPTH - 1)
    y_sample = x

    return (y_prompt, y_sample, *new)
```

```python
import functools

import numpy as np
import jax
import jax.numpy as jnp
from jax import lax
from jax.experimental import pallas as pl
from jax.experimental.pallas import tpu as pltpu

F32 = jnp.float32
BF16 = jnp.bfloat16

D_MODEL = 1024
DEPTH = 2
GRID_W = 64
HEAD_DIM = 64
ROPE_BASE = 10000.0
NEG_INF = -1e30
MOD_CHUNKS = 6
GQA_HEADS, GQA_KV_HEADS = 4, 2
NA_HEADS, NA_KH, NA_KW = 4, 8, 16
DN_HEADS, DN_DK, DN_DV, DN_CONV, DN_CHUNK = 4, 64, 64, 4, 64
DN_QKV = DN_HEADS * (2 * DN_DK + DN_DV)
DN_GATES = 2 * DN_HEADS
MLA_HEADS, MLA_Q_LORA, MLA_KV_LORA, MLA_NOPE, MLA_ROPE, MLA_V = 4, 256, 128, 64, 32, 64
MLA_SCALE = (MLA_NOPE + MLA_ROPE) ** -0.5
D_FF = -(-8 * D_MODEL // (3 * 256)) * 256
EPS = 1e-6
LOG2E = 1.4426950408889634

LANES = 128
ROW_TILE = 256
Q_TILE = 256
NA_BAND = 768
TILES_PER_STEP = 2
ATTN_Q_TILE = 512
HEAD_LOOKAHEAD = 1
DN_ROWS = 256
PREP_CHUNKS = 4
IN_PAD_COLS = 2816
VMEM_LIMIT = 56 * 1024 * 1024

_NT = (((1,), (1,)), ((), ()))
_TN = (((0,), (0,)), ((), ()))


def _cparams(n_axes):
    return pltpu.CompilerParams(dimension_semantics=("arbitrary",) * n_axes,
                                vmem_limit_bytes=VMEM_LIMIT)


def _lane(shape):
    return lax.broadcasted_iota(jnp.int32, shape, len(shape) - 1)


def _silu(x):
    return x / (1.0 + jnp.exp(-x))


def _rms_full(x, g):
    return x * lax.rsqrt(jnp.mean(x * x, axis=-1, keepdims=True) + EPS) * g


def _seg64_sum(x):
    lo = _lane(x.shape) < HEAD_DIM
    s_lo = jnp.sum(jnp.where(lo, x, 0.0), axis=-1, keepdims=True)
    s_hi = jnp.sum(jnp.where(lo, 0.0, x), axis=-1, keepdims=True)
    return jnp.where(lo, s_lo, s_hi)


def _head_rms(x, g):
    parts = []
    for p in range(x.shape[-1] // LANES):
        xp = x[:, p * LANES:(p + 1) * LANES]
        ms = _seg64_sum(xp * xp) * (1.0 / HEAD_DIM)
        parts.append(xp * lax.rsqrt(ms + EPS))
    y = parts[0] if len(parts) == 1 else jnp.concatenate(parts, axis=-1)
    return y * g


def _head_l2(x):
    parts = []
    for p in range(x.shape[-1] // LANES):
        xp = x[:, p * LANES:(p + 1) * LANES]
        parts.append(xp * lax.rsqrt(_seg64_sum(xp * xp) + EPS))
    return parts[0] if len(parts) == 1 else jnp.concatenate(parts, axis=-1)


def _rope(x, cos, sin, half):
    first = (_lane(x.shape) & (2 * half - 1)) < half
    rot = jnp.where(first, pltpu.roll(x, LANES - half, 1), pltpu.roll(x, half, 1))
    return x * cos + rot * sin


def _softmax_parts(scores):
    m = jnp.max(scores[0], axis=-1, keepdims=True)
    for s in scores[1:]:
        m = jnp.maximum(m, jnp.max(s, axis=-1, keepdims=True))
    es = [jnp.exp2(s - m) for s in scores]
    l = jnp.sum(es[0], axis=-1, keepdims=True)
    for e in es[1:]:
        l = l + jnp.sum(e, axis=-1, keepdims=True)
    return es, 1.0 / l


def _bdot(a, b):
    return jnp.dot(a, b, preferred_element_type=F32)


def _pipelined_heads(n_heads, scores, attend):
    outs = []
    queue = [scores(hd) for hd in range(min(HEAD_LOOKAHEAD, n_heads))]
    for hd in range(n_heads):
        if hd + HEAD_LOOKAHEAD < n_heads:
            queue.append(scores(hd + HEAD_LOOKAHEAD))
        es, rl = _softmax_parts(queue.pop(0))
        outs.append(attend(hd, es) * rl)
    return outs


def _pair_attention(q_blocks, sources, qmap):
    lo = _lane(q_blocks[0].shape) < HEAD_DIM

    def scores(hd):
        p, half = divmod(hd, 2)
        qm = jnp.where(lo if half == 0 else jnp.logical_not(lo), q_blocks[p], 0.0).astype(BF16)
        return [_bdot(qm, k_block(qmap[p])) if transposed
                else lax.dot_general(qm, k_block(qmap[p]), _NT, preferred_element_type=F32)
                for k_block, _, transposed in sources]

    def attend(hd, es):
        kv = qmap[hd // 2]
        parts = [lax.dot_general(e.astype(BF16), v_block(kv), _NT, preferred_element_type=F32) if transposed
                 else _bdot(e.astype(BF16), v_block(kv)) for e, (_, v_block, transposed) in zip(es, sources)]
        return functools.reduce(lambda a, b: a + b, parts)

    outs = _pipelined_heads(2 * len(qmap), scores, attend)
    return [jnp.where(lo, outs[2 * p], outs[2 * p + 1]) for p in range(len(qmap))]


def _mla_attention(q_heads, k_head, v_block):
    lo = _lane(q_heads[0].shape) < MLA_V

    def scores(hd):
        return [lax.dot_general(q_heads[hd], k_head(hd), _NT, preferred_element_type=F32)]

    def attend(hd, es):
        return _bdot(es[0].astype(BF16), v_block(hd // 2))

    outs = _pipelined_heads(MLA_HEADS, scores, attend)
    return [jnp.where(lo, outs[2 * p], outs[2 * p + 1]) for p in range(MLA_HEADS // 2)]


def _mod_kernel(c_ref, w_ref, b_ref, o_ref):
    s = _silu(c_ref[...]).astype(BF16)
    o_ref[...] = _bdot(s, w_ref[...].astype(BF16)) + b_ref[...]


def _modulation(cond, w_mod, b_mod):
    n = MOD_CHUNKS * D_MODEL
    tn = 1536
    return pl.pallas_call(
        _mod_kernel,
        grid=(DEPTH, n // tn),
        in_specs=[pl.BlockSpec((16, D_MODEL), lambda l, j: (0, 0)),
                  pl.BlockSpec((None, D_MODEL, tn), lambda l, j: (l, 0, j)),
                  pl.BlockSpec((None, 1, tn), lambda l, j: (l, 0, j))],
        out_specs=pl.BlockSpec((None, 16, tn), lambda l, j: (l, 0, j)),
        out_shape=jax.ShapeDtypeStruct((DEPTH, 16, n), F32),
        compiler_params=_cparams(2),
        name="modulation",
    )(cond, w_mod, b_mod.reshape(DEPTH, 1, n))


_IN_OUT_WIDTHS = (256, 128, 128, 256, 256, 256, 1024, 512, 128, 128, 128)
_IN_OUT_WIDTHS_CTX = (256, 128, 128, 256, 256, 256, 1024, 256, 128, 128, 128)
_IN_OUT_DTYPES_LAT = (BF16, BF16, BF16, BF16, BF16, BF16, F32, BF16, BF16, F32, BF16)
_IN_OUT_DTYPES_CTX = (BF16, F32, F32, BF16, F32, F32, F32, BF16, F32, F32, F32)


def _inproj_kernel(positioned, *refs):
    (x_ref, mod_ref, g1_ref, w_ref, qng_ref, kng_ref, mqg_ref, wq_ref, mkg_ref) = refs[:9]
    if positioned:
        cos64_ref, sin64_ref, cosm_ref, sinm_ref, coskr_ref, sinkr_ref = refs[9:15]
        n_in = 15
    else:
        wk_ref, wv_ref = refs[9:11]
        n_in = 11
    (qa_ref, ka_ref, va_ref, qb_ref, kb_ref, vb_ref, zc_ref, qd_ref, ckv_ref, zm_ref,
     krr_ref) = refs[n_in:]
    tiles = range(TILES_PER_STEP)
    tm = x_ref.shape[1]
    lane = _lane((tm, LANES))
    lo = lane < HEAD_DIM

    m = mod_ref[...]
    hb = jnp.concatenate([(_rms_full(x_ref[t], g1_ref[...]) * (1.0 + m[1:2]) + m[0:1]).astype(BF16)
                          for t in tiles], axis=0)

    def project(c0, c1):
        z = lax.dot_general(hb, w_ref[c0:c1, :], _NT, preferred_element_type=F32)
        return [z[t * tm:(t + 1) * tm] for t in tiles]

    za = project(0, 512)
    zb = project(512, 1280)

    for t in tiles:
        q = _head_rms(za[t][:, 0:256], qng_ref[...])
        k = _head_rms(za[t][:, 256:384], kng_ref[...])
        q0, q1 = q[:, 0:128], q[:, 128:256]
        q0, q1 = jnp.where(lo, q0, pltpu.roll(q1, HEAD_DIM, 1)), jnp.where(lo, pltpu.roll(q0, HEAD_DIM, 1), q1)
        if positioned:
            cos, sin = cos64_ref[t], sin64_ref[t]
            q0, q1 = _rope(q0, cos, sin, 16), _rope(q1, cos, sin, 16)
            k = _rope(k, cos, sin, 16)
        q0, q1 = q0 * (HEAD_DIM ** -0.5 * LOG2E), q1 * (HEAD_DIM ** -0.5 * LOG2E)
        v = za[t][:, 384:512]
        if not positioned:
            kb16, vb16 = k.astype(BF16), v.astype(BF16)
            q0, q1 = _pair_attention([q0, q1], [(lambda i: kb16, lambda i: vb16, False)], (0, 0))
        qa_ref[t] = jnp.concatenate([q0, q1], axis=-1).astype(qa_ref.dtype)
        ka_ref[t] = k.astype(ka_ref.dtype) if positioned else k.T
        va_ref[t] = v.astype(va_ref.dtype) if positioned else v.T

    zd = project(2304, IN_PAD_COLS)
    zc = project(1280, 2304)

    for t in tiles:
        q = zb[t][:, 0:256] * (HEAD_DIM ** -0.5 * LOG2E)
        k, v = zb[t][:, 256:512], zb[t][:, 512:768]
        if not positioned:
            kb16, vb16 = k.astype(BF16), v.astype(BF16)
            q = jnp.concatenate(_pair_attention(
                [q[:, 0:LANES], q[:, LANES:]],
                [(lambda i, kb16=kb16: kb16[:, i * LANES:(i + 1) * LANES],
                  lambda i, vb16=vb16: vb16[:, i * LANES:(i + 1) * LANES], False)], (0, 1)), axis=-1)
        qb_ref[t] = q.astype(qb_ref.dtype)
        kb_ref[t] = k.astype(kb_ref.dtype) if positioned else k.T
        vb_ref[t] = v.astype(vb_ref.dtype) if positioned else v.T

    shifted = []
    for t in tiles:
        rolled = [pltpu.roll(zd[t][:, j * LANES:(j + 1) * LANES], LANES - 2 * DN_GATES, 1) for j in range(4)]
        keep = lane < LANES - 2 * DN_GATES
        shifted.append([jnp.where(keep, rolled[j], rolled[(j + 1) % 4]) for j in range(4)])

    cq = jnp.concatenate([_rms_full(jnp.concatenate(shifted[t][0:2], axis=-1), mqg_ref[...]).astype(BF16)
                          for t in tiles], axis=0)
    qm = _bdot(cq, wq_ref[...])
    for t in tiles:
        zc_ref[t] = zc[t]
        zm_ref[t] = zd[t][:, 0:LANES]
        ckv = _rms_full(shifted[t][2], mkg_ref[...])
        ckv_ref[t] = ckv.astype(ckv_ref.dtype)
        kr = jnp.where(lane < MLA_ROPE, shifted[t][3], 0.0)
        q = qm[t * tm:(t + 1) * tm]
        if positioned:
            kr = _rope(kr, coskr_ref[t], sinkr_ref[t], 8)
            cm, sm = cosm_ref[t], sinm_ref[t]
            q = jnp.concatenate([_rope(q[:, i * LANES:(i + 1) * LANES], cm, sm, 8)
                                 for i in range(MLA_HEADS)], axis=-1)
        krr_ref[t] = kr.astype(krr_ref.dtype)
        q = q * (MLA_SCALE * LOG2E)
        if not positioned:
            c16 = ckv.astype(BF16)
            k16 = _bdot(jnp.concatenate([c16, kr.astype(BF16)], axis=-1), wk_ref[...]).astype(BF16)
            v16 = _bdot(c16, wv_ref[...]).astype(BF16)
            q = jnp.concatenate(_mla_attention(
                [q[:, i * LANES:(i + 1) * LANES].astype(BF16) for i in range(MLA_HEADS)],
                lambda i, k16=k16: k16[:, i * LANES:(i + 1) * LANES],
                lambda i, v16=v16: v16[:, i * LANES:(i + 1) * LANES]), axis=-1)
        qd_ref[t] = q.astype(qd_ref.dtype)


def _inproj(x, mods, per_batch_mods, lw, layer, ropes):
    b, l, _ = x.shape
    tm, ts = ROW_TILE, TILES_PER_STEP
    n_tiles = b * l // tm
    tiles_per_seq = l // tm
    positioned = ropes is not None
    row = lambda w: pl.BlockSpec((ts, tm, w), lambda i: (i, 0, 0))
    const = lambda shape: pl.BlockSpec((None,) + shape, lambda i: (layer,) + (0,) * len(shape))
    assert not per_batch_mods or tiles_per_seq % ts == 0
    mod_spec = pl.BlockSpec((None, None, MOD_CHUNKS, D_MODEL),
                            (lambda i: (layer, 1 + i * ts // tiles_per_seq, 0, 0)) if per_batch_mods
                            else (lambda i: (layer, 0, 0, 0)))
    in_specs = [row(D_MODEL), mod_spec, const((1, D_MODEL)), const((IN_PAD_COLS, D_MODEL)),
                const((1, 256)), const((1, 128)), const((1, MLA_Q_LORA)),
                const((MLA_Q_LORA, 4 * LANES)), const((1, MLA_KV_LORA))]
    args = [x.reshape(n_tiles, tm, D_MODEL), mods, lw["norm1_g"], lw["w_in"], lw["qn_g"], lw["kn_g"],
            lw["mla_qn_g"], lw["wq"], lw["mla_kvn_g"]]
    if positioned:
        steps_per_seq = tiles_per_seq // ts
        in_specs += [pl.BlockSpec((ts, tm, LANES), lambda i: (i % steps_per_seq, 0, 0))] * 6
        args += [r.reshape(tiles_per_seq, tm, LANES) for r in ropes]
        widths, dtypes = _IN_OUT_WIDTHS, _IN_OUT_DTYPES_LAT
    else:
        assert tiles_per_seq == 1
        in_specs += [const((2 * LANES, 4 * LANES)), const((LANES, 2 * LANES))]
        args += [lw["wk"], lw["wv"]]
        widths, dtypes = _IN_OUT_WIDTHS_CTX, _IN_OUT_DTYPES_CTX
    flipped = () if positioned else (1, 2, 4, 5)
    shapes = [(n_tiles, w, tm) if i in flipped else (n_tiles, tm, w) for i, w in enumerate(widths)]
    outs = pl.pallas_call(
        functools.partial(_inproj_kernel, positioned),
        grid=(n_tiles // ts,),
        in_specs=in_specs,
        out_specs=[pl.BlockSpec((ts,) + shp[1:], lambda i: (i, 0, 0)) for shp in shapes],
        out_shape=[jax.ShapeDtypeStruct(shp, dt) for shp, dt in zip(shapes, dtypes)],
        compiler_params=_cparams(1),
        name="inproj_lat" if positioned else "inproj_attn_ctx",
    )(*args)
    return [o if i in flipped else o.reshape(b, l, o.shape[-1]) for i, o in enumerate(outs)]


def _attn_pair_kernel(qmap, q_ref, kc_ref, vc_ref, k_ref, v_ref, o_ref, kcbuf, vcbuf, kbuf, vbuf):
    @pl.when(pl.program_id(1) == 0)
    def _():
        kcbuf[...] = kc_ref[...].astype(BF16)
        vcbuf[...] = vc_ref[...].astype(BF16)
        kbuf[...] = k_ref[...].astype(BF16)
        vbuf[...] = v_ref[...].astype(BF16)

    rows = lambda buf: (lambda i: buf[i * LANES:(i + 1) * LANES, :])
    cols = lambda buf: (lambda i: buf[:, i * LANES:(i + 1) * LANES])
    q_blocks = [q_ref[:, p * LANES:(p + 1) * LANES].astype(F32) for p in range(len(qmap))]
    outs = _pair_attention(q_blocks, [(rows(kcbuf), rows(vcbuf), True), (cols(kbuf), cols(vbuf), False)], qmap)
    for p, o in enumerate(outs):
        o_ref[:, p * LANES:(p + 1) * LANES] = o.astype(o_ref.dtype)


def _attn_pair(q, k, v, qmap, kc_t, vc_t, layer, name):
    b, lq, wq = q.shape
    ls, wk = k.shape[1], k.shape[2]
    lc = kc_t.shape[3]
    tq = min(ATTN_Q_TILE, lq)
    cspec = pl.BlockSpec((None, None, wk, lc), lambda bi, i: (bi, layer, 0, 0))
    sspec = pl.BlockSpec((None, ls, wk), lambda bi, i: (bi, 0, 0))
    return pl.pallas_call(
        functools.partial(_attn_pair_kernel, qmap),
        grid=(b, lq // tq),
        in_specs=[pl.BlockSpec((None, tq, wq), lambda bi, i: (bi, i, 0)), cspec, cspec, sspec, sspec],
        out_specs=pl.BlockSpec((None, tq, wq), lambda bi, i: (bi, i, 0)),
        out_shape=jax.ShapeDtypeStruct((b, lq, wq), BF16),
        scratch_shapes=[pltpu.VMEM((wk, lc), BF16), pltpu.VMEM((wk, lc), BF16),
                        pltpu.VMEM((ls, wk), BF16), pltpu.VMEM((ls, wk), BF16)],
        compiler_params=_cparams(2),
        name=name,
    )(q, kc_t, vc_t, k, v)


def _mla_kernel(has_cache, *refs):
    if has_cache:
        q_ref, ckvc_ref, krc_ref, ckv_ref, kr_ref, wk_ref, wv_ref, o_ref, kbuf, vbuf = refs
    else:
        q_ref, ckv_ref, kr_ref, wk_ref, wv_ref, o_ref, kbuf, vbuf = refs

    @pl.when(pl.program_id(1) == 0)
    def _():
        def expand(c_ref, r_ref, r0, r1):
            c = c_ref[...].astype(BF16)
            ckr = jnp.concatenate([c, r_ref[...].astype(BF16)], axis=-1)
            kbuf[r0:r1, :] = _bdot(ckr, wk_ref[...]).astype(BF16)
            vbuf[r0:r1, :] = _bdot(c, wv_ref[...]).astype(BF16)
        off = 0
        if has_cache:
            off = ckvc_ref.shape[0]
            expand(ckvc_ref, krc_ref, 0, off)
        expand(ckv_ref, kr_ref, off, kbuf.shape[0])

    outs = _mla_attention([q_ref[:, hd * LANES:(hd + 1) * LANES].astype(BF16) for hd in range(MLA_HEADS)],
                          lambda i: kbuf[:, i * LANES:(i + 1) * LANES],
                          lambda i: vbuf[:, i * LANES:(i + 1) * LANES])
    for p, o in enumerate(outs):
        o_ref[:, p * LANES:(p + 1) * LANES] = o.astype(o_ref.dtype)


def _mla(q, ckv, kr, lw, layer, cache=None, name="mla"):
    b, lq, wq = q.shape
    ls = ckv.shape[1]
    tq = min(ATTN_Q_TILE, lq)
    lc = 0 if cache is None else cache[0].shape[2]
    in_specs = [pl.BlockSpec((None, tq, wq), lambda bi, i: (bi, i, 0))]
    args = [q]
    if cache is not None:
        ckvc, krc = cache
        cspec = pl.BlockSpec((None, None, lc, LANES), lambda bi, i: (bi, layer, 0, 0))
        in_specs += [cspec, cspec]
        args += [ckvc, krc]
    sspec = pl.BlockSpec((None, ls, LANES), lambda bi, i: (bi, 0, 0))
    in_specs += [sspec, sspec,
                 pl.BlockSpec((None, 2 * LANES, 4 * LANES), lambda bi, i: (layer, 0, 0)),
                 pl.BlockSpec((None, LANES, 2 * LANES), lambda bi, i: (layer, 0, 0))]
    args += [ckv, kr, lw["wk"], lw["wv"]]
    return pl.pallas_call(
        functools.partial(_mla_kernel, cache is not None),
        grid=(b, lq // tq),
        in_specs=in_specs,
        out_specs=pl.BlockSpec((None, tq, 2 * LANES), lambda bi, i: (bi, i, 0)),
        out_shape=jax.ShapeDtypeStruct((b, lq, 2 * LANES), BF16),
        scratch_shapes=[pltpu.VMEM((lc + ls, 4 * LANES), BF16), pltpu.VMEM((lc + ls, 2 * LANES), BF16)],
        compiler_params=_cparams(2),
        name=name,
    )(*args)


def _na_kernel(q_ref, k_ref, v_ref, kc_ref, vc_ref, bias_ref, o_ref):
    j = pl.program_id(0)
    start = pl.multiple_of((j >> 1) * 256, 256)
    kband = k_ref[pl.ds(start, NA_BAND), :].astype(BF16)
    vband = v_ref[pl.ds(start, NA_BAND), :].astype(BF16)
    kc = kc_ref[...].astype(BF16)
    vc = vc_ref[...].astype(BF16)
    tq = q_ref.shape[0]
    lo = _lane((tq, LANES)) < HEAD_DIM

    def scores(hd):
        p, half = divmod(hd, 2)
        sl = slice(p * LANES, (p + 1) * LANES)
        qm = jnp.where(lo if half == 0 else jnp.logical_not(lo), q_ref[:, sl].astype(F32), 0.0).astype(BF16)
        s_loc = lax.dot_general(qm, kband[:, sl], _NT, preferred_element_type=F32) + bias_ref[hd]
        return [s_loc, _bdot(qm, kc[sl, :])]

    def attend(hd, es):
        sl = slice((hd // 2) * LANES, (hd // 2 + 1) * LANES)
        return (_bdot(es[0].astype(BF16), vband[:, sl])
                + lax.dot_general(es[1].astype(BF16), vc[sl, :], _NT, preferred_element_type=F32))

    outs = _pipelined_heads(NA_HEADS, scores, attend)
    for p in range(NA_HEADS // 2):
        o_ref[:, p * LANES:(p + 1) * LANES] = jnp.where(lo, outs[2 * p], outs[2 * p + 1]).astype(o_ref.dtype)


def _na_latent(q, k, v, kc_t, vc_t, layer, bias_blocks):
    b, n, w = q.shape
    lc = kc_t.shape[3]
    nq = n // Q_TILE
    full = pl.BlockSpec((None, n, w), lambda j, bi: (bi, 0, 0))
    cspec = pl.BlockSpec((None, None, w, lc), lambda j, bi: (bi, layer, 0, 0))
    return pl.pallas_call(
        _na_kernel,
        grid=(nq, b),
        in_specs=[pl.BlockSpec((None, Q_TILE, w), lambda j, bi: (bi, j, 0)), full, full, cspec, cspec,
                  pl.BlockSpec((None, None, NA_HEADS, Q_TILE, NA_BAND), lambda j, bi: (layer, j, 0, 0, 0))],
        out_specs=pl.BlockSpec((None, Q_TILE, w), lambda j, bi: (bi, j, 0)),
        out_shape=jax.ShapeDtypeStruct((b, n, w), BF16),
        compiler_params=_cparams(2),
        name="na_latent",
    )(q, k, v, kc_t, vc_t, bias_blocks)


NA_GRID_ROWS = 16
NA_BAND_ROW0 = (0, 0, 4, 4)


def _na_bias_kernel(b_ref, o_ref, tp_s):
    hd = pl.program_id(0)
    n_dr, n_dc = 2 * NA_KH - 1, 2 * NA_KW - 1
    shape = (GRID_W, LANES)
    c = lax.broadcasted_iota(jnp.int32, shape, 0)
    lane = _lane(shape)
    kc = lane & (GRID_W - 1)
    lo = lane < GRID_W
    diff = kc - c + (NA_KW - 1)
    c0 = jnp.clip(c - NA_KW // 2, 0, GRID_W - NA_KW)
    col_ok = (kc >= c0) & (kc < c0 + NA_KW)
    neg = jnp.full(shape, NEG_INF, F32)
    for dr0 in range(-1, n_dr):
        acc = neg
        for d in range(n_dc):
            v_lo = b_ref[hd * n_dr + dr0, d] if dr0 >= 0 else 0.0
            v_hi = b_ref[hd * n_dr + dr0 + 1, d] if dr0 + 1 < n_dr else 0.0
            acc = jnp.where(diff == d, jnp.where(lo, v_lo, v_hi), acc)
        tp_s[dr0 + 1] = jnp.where(col_ok, acc * LOG2E, NEG_INF)
    for j in range(NA_GRID_ROWS // 4):
        for ri in range(4):
            r = 4 * j + ri
            r0 = min(max(r - NA_KH // 2, 0), NA_GRID_ROWS - NA_KH)
            for kp in range(NA_BAND // LANES):
                kr = NA_BAND_ROW0[j] + 2 * kp
                ok_lo, ok_hi = r0 <= kr < r0 + NA_KH, r0 <= kr + 1 < r0 + NA_KH
                dr0 = kr - r + (NA_KH - 1)
                if ok_lo and ok_hi:
                    t = tp_s[dr0 + 1]
                elif ok_lo:
                    t = jnp.where(lo, tp_s[dr0 + 1], NEG_INF)
                elif ok_hi:
                    t = jnp.where(lo, NEG_INF, tp_s[dr0 + 1])
                else:
                    t = neg
                o_ref[j, ri * GRID_W:(ri + 1) * GRID_W, kp * LANES:(kp + 1) * LANES] = t


def _na_bias_blocks(bias):
    nq = NA_GRID_ROWS // 4
    return pl.pallas_call(
        _na_bias_kernel,
        grid=(DEPTH * NA_HEADS,),
        in_specs=[pl.BlockSpec(memory_space=pltpu.SMEM)],
        out_specs=pl.BlockSpec((None, nq, None, Q_TILE, NA_BAND),
                               lambda i: (i // NA_HEADS, 0, i % NA_HEADS, 0, 0)),
        out_shape=jax.ShapeDtypeStruct((DEPTH, nq, NA_HEADS, Q_TILE, NA_BAND), F32),
        scratch_shapes=[pltpu.VMEM((2 * NA_KH, GRID_W, LANES), F32)],
        compiler_params=_cparams(1),
        name="na_bias",
    )(bias.reshape(DEPTH * NA_HEADS * (2 * NA_KH - 1), 2 * NA_KW - 1))


def _widen(cols, n):
    blk = _lane((n, DN_HEADS * DN_DV)) >> 6
    return jnp.where(blk == 0, cols[0], jnp.where(blk == 1, cols[1], jnp.where(blk == 2, cols[2], cols[3])))


def _deltanet_kernel(seq, has_state, *refs):
    if has_state:
        (zc_ref, zm_ref, s0_ref, cw_ref, alog_ref, dtb_ref, og_ref, o_ref,
         q_s, k_s, v_s, b_s, g_s, o_s, c_s, mp_s) = refs
    else:
        (zc_ref, zm_ref, cw_ref, alog_ref, dtb_ref, og_ref, o_ref, sfin_ref,
         q_s, k_s, v_s, b_s, g_s, o_s, c_s, mp_s) = refs
    n_chunks = seq // DN_CHUNK
    wide = DN_HEADS * DN_DV

    a_off, b_off = 0, DN_GATES
    ri = lax.broadcasted_iota(jnp.int32, (DN_ROWS, DN_ROWS), 0)
    ci = lax.broadcasted_iota(jnp.int32, (DN_ROWS, DN_ROWS), 1)
    same = (ri >> 6) == (ci >> 6)
    tri = [jnp.tile(jnp.where(same & ((ci <= ri) if d == 0 else (ci >= ri)), 1.0, 0.0).astype(BF16), (1, 3))
           for d in range(2)]
    n_blocks = seq // DN_ROWS

    def terms3(x):
        hi = x.astype(BF16)
        rest = x - hi.astype(F32)
        mid = rest.astype(BF16)
        return hi, mid, (rest - mid.astype(F32)).astype(BF16)

    def split3(x):
        return jnp.concatenate(terms3(x), axis=0)

    def preprocess(rb, carry):
        r0 = pl.multiple_of(rb * DN_ROWS, DN_ROWS)
        rows = pl.ds(r0, DN_ROWS)
        before = pl.ds(pl.multiple_of(jnp.maximum(r0 - 8, 0), 8), 8)
        after = pl.ds(pl.multiple_of(jnp.minimum(r0 + DN_ROWS, seq - 8), 8), 8)
        for part, dst in enumerate((q_s, k_s, v_s)):
            cs = slice(part * wide, (part + 1) * wide)
            head = jnp.where(rb > 0, zc_ref[before, cs], 0.0)
            tail = jnp.where(rb < n_blocks - 1, zc_ref[after, cs], 0.0)
            xe = jnp.concatenate([head, zc_ref[rows, cs], tail], axis=0)
            w = cw_ref[:, cs]
            y = (w[0:1] * xe[7:7 + DN_ROWS] + w[1:2] * xe[8:8 + DN_ROWS]
                 + w[2:3] * xe[9:9 + DN_ROWS] + w[3:4] * xe[10:10 + DN_ROWS])
            y = _silu(y)
            if part == 0:
                y = _head_l2(y) * (DN_DK ** -0.5)
            elif part == 1:
                y = _head_l2(y)
            dst[rows, :] = y
        zm = zm_ref[rows, :]
        xa = zm + dtb_ref[...]
        logd = -jnp.exp(alog_ref[...]) * (jnp.maximum(xa, 0.0) + jnp.log1p(jnp.exp(-jnp.abs(xa))))
        beta = 1.0 / (1.0 + jnp.exp(-zm))
        logd3 = split3(logd)
        for d in range(2):
            b_s[d, rows, :] = _widen([beta[:, b_off + 4 * d + hd:b_off + 4 * d + hd + 1]
                                      for hd in range(DN_HEADS)], DN_ROWS)
            g = _bdot(tri[d], logd3)
            g_s[d, rows, :] = _widen([g[:, a_off + 4 * d + hd:a_off + 4 * d + hd + 1]
                                      for hd in range(DN_HEADS)], DN_ROWS)
        return carry

    lax.fori_loop(0, n_blocks, preprocess, 0)

    ii = lax.broadcasted_iota(jnp.int32, (DN_CHUNK, wide), 0)
    jj = _lane((DN_CHUNK, wide)) & (DN_CHUNK - 1)
    blk = _lane((DN_CHUNK, wide)) >> 6
    diag = ii == jj
    eye = jnp.where(diag, 1.0, 0.0)
    head_mask = [jnp.where(blk == hd, 1.0, 0.0).astype(BF16) for hd in range(DN_HEADS)]

    def bd(z):
        zb = z.astype(BF16)
        return jnp.concatenate([zb * hm for hm in head_mask], axis=0)

    def fold(gram):
        out = jnp.where(blk == 0, gram[0:DN_CHUNK], 0.0)
        for hd in range(1, DN_HEADS):
            out = out + jnp.where(blk == hd, gram[hd * DN_CHUNK:(hd + 1) * DN_CHUNK], 0.0)
        return out

    tri_masks = []
    for d in range(2):
        incl = (jj <= ii) if d == 0 else (jj >= ii)
        strict = (jj < ii) if d == 0 else (jj > ii)
        pair = [((ii >> (lvl + 1)) == (jj >> (lvl + 1)))
                & (((ii >> lvl) & 1) == (1 - d)) & (((jj >> lvl) & 1) == d) for lvl in range(6)]
        tri_masks.append((incl, strict, pair))

    def prepare(step, carry):
        chunks = [step * PREP_CHUNKS + i for i in range(PREP_CHUNKS)]
        rows = [pl.ds(pl.multiple_of(c * DN_CHUNK, DN_CHUNK), DN_CHUNK) for c in chunks]
        qkv = [(q_s[r, :], k_s[r, :], v_s[r, :]) for r in rows]
        inst = [(ci, d) for ci in range(PREP_CHUNKS) for d in range(2)]
        beta = {(ci, d): b_s[d, rows[ci], :] for ci, d in inst}
        kb = {(ci, d): qkv[ci][1] * beta[ci, d] for ci, d in inst}
        r = [lax.dot_general(jnp.concatenate([kb[ci, 0], kb[ci, 1], qkv[ci][0]], axis=0).astype(BF16),
                             bd(qkv[ci][1]), _NT, preferred_element_type=F32)
             for ci in range(PREP_CHUNKS)]
        g, a, qk, eg, t = {}, {}, {}, {}, {}
        for ci, d in inst:
            incl, strict, pair = tri_masks[d]
            g[ci, d] = g_s[d, rows[ci], :]
            g_row = jnp.sum(jnp.where(diag, g[ci, d], 0.0), axis=0, keepdims=True)
            dm = jnp.where(incl, jnp.exp(jnp.where(incl, g[ci, d] - g_row, 0.0)), 0.0)
            a[ci, d] = jnp.where(strict, r[ci][d * DN_CHUNK:(d + 1) * DN_CHUNK] * dm, 0.0)
            qk[ci, d] = r[ci][2 * DN_CHUNK:] * dm
            eg[ci, d] = jnp.exp(g[ci, d])
            t[ci, d] = eye - jnp.where(pair[0], a[ci, d], 0.0)
        for lvl in range(1, 6):
            te = {i: _bdot(t[i].astype(BF16), bd(jnp.where(tri_masks[i[1]][2][lvl], a[i], 0.0))) for i in inst}
            t = {i: t[i] - _bdot(te[i].astype(BF16), bd(t[i])) for i in inst}
        nb = {i: jnp.where(diag, 0.0, t[i]).astype(BF16) for i in inst}
        rhs_u = {(ci, d): qkv[ci][2] * beta[ci, d] for ci, d in inst}
        rhs_w = {i: kb[i] * eg[i] for i in inst}
        q2 = {i: qk[i] + _bdot(qk[i].astype(BF16), bd(nb[i])) for i in inst}
        both = {i: jnp.concatenate([nb[i], q2[i].astype(BF16)], axis=0) for i in inst}
        ru = {i: _bdot(both[i], bd(rhs_u[i])) for i in inst}
        rw = {i: _bdot(both[i], bd(rhs_w[i])) for i in inst}
        u = {i: rhs_u[i] + ru[i][0:DN_CHUNK] for i in inst}
        w = {i: rhs_w[i] + rw[i][0:DN_CHUNK] for i in inst}
        kd = {}
        for ci, d in inst:
            g_last = g[ci, d][DN_CHUNK - 1:DN_CHUNK] if d == 0 else g[ci, d][0:1]
            kd[ci, d] = (qkv[ci][1] * jnp.exp(g_last - g[ci, d])).astype(BF16)
        p = {(ci, d): qkv[ci][0] * eg[ci, d] - rw[ci, d][DN_CHUNK:] for ci, d in inst}
        o0 = {i: ru[i][DN_CHUNK:] for i in inst}
        m = {i: fold(lax.dot_general(kd[i], w[i].astype(BF16), _TN, preferred_element_type=F32)) for i in inst}
        cc = {i: fold(lax.dot_general(kd[i], u[i].astype(BF16), _TN, preferred_element_type=F32)) for i in inst}
        for ci, d in inst:
            mrow = pl.ds(pl.multiple_of(chunks[ci] * (2 * DN_CHUNK), 2 * DN_CHUNK), 2 * DN_CHUNK)
            c_s[d, rows[ci], :] = cc[ci, d]
            mp_s[d, mrow, :] = jnp.concatenate([m[ci, d], p[ci, d]], axis=0).astype(BF16)
        for ci in range(PREP_CHUNKS):
            o_s[rows[ci], :] = o0[ci, 0] + o0[ci, 1]
        return carry

    lax.fori_loop(0, n_chunks // PREP_CHUNKS, prepare, 0)

    def scan(i, states):
        new = []
        for d in range(2):
            c = i if d == 0 else n_chunks - 1 - i
            r0 = pl.multiple_of(c * DN_CHUNK, DN_CHUNK)
            rows = pl.ds(r0, DN_CHUNK)
            mrow = pl.ds(pl.multiple_of(c * (2 * DN_CHUNK), 2 * DN_CHUNK), 2 * DN_CHUNK)
            edge = pl.ds(pl.multiple_of(r0 + (DN_CHUNK - 8 if d == 0 else 0), 8), 8)
            g_edge = g_s[d, edge, :]
            g_last = g_edge[7:8] if d == 0 else g_edge[0:1]
            res = _bdot(mp_s[d, mrow, :], bd(states[d]))
            o_s[rows, :] = o_s[rows, :] + res[DN_CHUNK:]
            new.append(states[d] * jnp.exp(g_last) - res[0:DN_CHUNK] + c_s[d, rows, :])
        return tuple(new)

    place = [jnp.where(diag & (blk == hd), 1.0, 0.0).astype(BF16) for hd in range(DN_HEADS)]

    def to_wide(heads):
        parts = [_bdot(term, place[hd]) for hd, x in enumerate(heads) for term in terms3(x)]
        return functools.reduce(lambda a, b: a + b, parts)

    def head_of(s, hd):
        parts = [lax.dot_general(term, place[hd], _NT, preferred_element_type=F32) for term in terms3(s)]
        return functools.reduce(lambda a, b: a + b, parts)

    if has_state:
        init = tuple(to_wide([s0_ref[d, hd] for hd in range(DN_HEADS)]) for d in range(2))
    else:
        init = tuple(jnp.zeros((DN_DK, wide), F32) for d in range(2))
    fin = lax.fori_loop(0, n_chunks, scan, init)
    if not has_state:
        for d in range(2):
            for hd in range(DN_HEADS):
                sfin_ref[d, hd] = head_of(fin[d], hd)

    o_ref[...] = (_head_rms(o_s[...], og_ref[...]) * _silu(zc_ref[:, 3 * wide:4 * wide])).astype(o_ref.dtype)


def _deltanet(zc, zm, lw, layer, state=None):
    b, seq, _ = zc.shape
    wide = DN_HEADS * DN_DV
    has_state = state is not None
    per_b = lambda w: pl.BlockSpec((None, seq, w), lambda bi: (bi, 0, 0))
    const = lambda shape: pl.BlockSpec((None,) + shape, lambda bi: (layer,) + (0,) * len(shape))
    st_spec = pl.BlockSpec((None, 2, DN_HEADS, DN_DK, DN_DV), lambda bi: (bi, 0, 0, 0, 0))
    in_specs = [per_b(4 * wide), per_b(LANES)]
    args = [zc, zm]
    if has_state:
        in_specs.append(pl.BlockSpec((None, None, 2, DN_HEADS, DN_DK, DN_DV),
                                     lambda bi: (bi, layer, 0, 0, 0, 0)))
        args.append(state)
    in_specs += [const((DN_CONV, DN_QKV)), const((1, LANES)), const((1, LANES)), const((1, wide))]
    args += [lw["dn_conv_w"], lw["dn_alog_row"], lw["dn_dtb_row"], lw["dn_out_g"]]
    out_specs = [per_b(wide)]
    out_shape = [jax.ShapeDtypeStruct((b, seq, wide), BF16)]
    if not has_state:
        out_specs.append(st_spec)
        out_shape.append(jax.ShapeDtypeStruct((b, 2, DN_HEADS, DN_DK, DN_DV), F32))
    res = pl.pallas_call(
        functools.partial(_deltanet_kernel, seq, has_state),
        grid=(b,),
        in_specs=in_specs,
        out_specs=out_specs,
        out_shape=out_shape,
        scratch_shapes=[pltpu.VMEM((seq, wide), F32), pltpu.VMEM((seq, wide), F32),
                        pltpu.VMEM((seq, wide), F32), pltpu.VMEM((2, seq, wide), F32),
                        pltpu.VMEM((2, seq, wide), F32), pltpu.VMEM((seq, wide), F32),
                        pltpu.VMEM((2, seq, wide), F32), pltpu.VMEM((2, 2 * seq, wide), BF16)],
        compiler_params=_cparams(1),
        name="deltanet_lat" if has_state else "deltanet_ctx",
    )(*args)
    return (res[0], None) if has_state else (res[0], res[1])


def _outffn_kernel(final, oa_ref, ob_ref, oc_ref, od_ref, x_ref, mod_ref, g2_ref, wo_ref, wg_ref,
                   wu_ref, wd_ref, fg_ref, y_ref):
    tiles = range(TILES_PER_STEP)
    m = mod_ref[...]
    o = [jnp.concatenate([oa_ref[t], ob_ref[t], oc_ref[t], od_ref[t]], axis=-1).astype(BF16) for t in tiles]
    x1 = [x_ref[t] + m[2:3] * _bdot(o[t], wo_ref[...]) for t in tiles]
    h = [(_rms_full(x1[t], g2_ref[...]) * (1.0 + m[4:5]) + m[3:4]).astype(BF16) for t in tiles]
    gate = [_bdot(h[t], wg_ref[...]) for t in tiles]
    up = [_bdot(h[t], wu_ref[...]) for t in tiles]
    act = [(_silu(gate[t]) * up[t]).astype(BF16) for t in tiles]
    x2 = [x1[t] + m[5:6] * _bdot(act[t], wd_ref[...]) for t in tiles]
    for t in tiles:
        y_ref[t] = _rms_full(x2[t], fg_ref[...]) if final else x2[t]


def _outffn(outs, x, mods, per_batch_mods, lw, layer, final_g, final):
    b, l, _ = x.shape
    tm, ts = ROW_TILE, TILES_PER_STEP
    n_tiles = b * l // tm
    tiles_per_seq = l // tm
    assert not per_batch_mods or tiles_per_seq % ts == 0
    row = lambda w: pl.BlockSpec((ts, tm, w), lambda i: (i, 0, 0))
    const = lambda shape: pl.BlockSpec((None,) + shape, lambda i: (layer,) + (0,) * len(shape),
                                       pipeline_mode=pl.Buffered(1))
    mod_spec = pl.BlockSpec((None, None, MOD_CHUNKS, D_MODEL),
                            (lambda i: (layer, 1 + i * ts // tiles_per_seq, 0, 0)) if per_batch_mods
                            else (lambda i: (layer, 0, 0, 0)))
    tiled = lambda a: a.reshape(n_tiles, tm, a.shape[-1])
    y = pl.pallas_call(
        functools.partial(_outffn_kernel, final),
        grid=(n_tiles // ts,),
        in_specs=[row(256), row(256), row(256), row(256), row(D_MODEL), mod_spec, const((1, D_MODEL)),
                  const((D_MODEL, D_MODEL)), const((D_MODEL, D_FF)), const((D_MODEL, D_FF)),
                  const((D_FF, D_MODEL)), pl.BlockSpec((1, D_MODEL), lambda i: (0, 0))],
        out_specs=row(D_MODEL),
        out_shape=jax.ShapeDtypeStruct((n_tiles, tm, D_MODEL), F32),
        compiler_params=_cparams(1),
        name="outffn",
    )(*[tiled(a) for a in outs], tiled(x), mods, lw["norm2_g"], lw["w_out"], lw["w_gate"], lw["w_up"],
      lw["w_down"], final_g)
    return y.reshape(b, l, D_MODEL)


def _rope_tables(n):
    t = jnp.arange(n)

    def axis(pos, half):
        inv = ROPE_BASE ** (-jnp.arange(half, dtype=F32) / half)
        ang = pos.astype(F32)[:, None] * inv[None, :]
        c, s = jnp.cos(ang), jnp.sin(ang)
        return jnp.concatenate([c, c], -1), jnp.concatenate([-s, s], -1)

    cr, sr = axis(t // GRID_W, 16)
    cc, sc = axis(t % GRID_W, 16)
    cos64 = jnp.tile(jnp.concatenate([cr, cc], -1), (1, 2))
    sin64 = jnp.tile(jnp.concatenate([sr, sc], -1), (1, 2))
    cr, sr = axis(t // GRID_W, 8)
    cc, sc = axis(t % GRID_W, 8)
    cos32, sin32 = jnp.concatenate([cr, cc], -1), jnp.concatenate([sr, sc], -1)
    one, zero = jnp.ones((n, 1), F32), jnp.zeros((n, 1), F32)
    cosm = jnp.concatenate([jnp.tile(one, (1, 64)), cos32, jnp.tile(one, (1, 32))], -1)
    sinm = jnp.concatenate([jnp.tile(zero, (1, 64)), sin32, jnp.tile(zero, (1, 32))], -1)
    coskr = jnp.concatenate([cos32, jnp.tile(one, (1, 96))], -1)
    sinkr = jnp.concatenate([sin32, jnp.tile(zero, (1, 96))], -1)
    return cos64, sin64, cosm, sinm, coskr, sinkr


_QA_ORDER = ((0, 64), (128, 192), (64, 128), (192, 256))


def _stacked_weights(p):
    w_in = jnp.swapaxes(p["w_in"], 1, 2)
    w_in = jnp.pad(w_in, ((0, 0), (0, IN_PAD_COLS - w_in.shape[1]), (0, 0))).astype(BF16)
    w_out = p["w_out"]
    w_out = jnp.concatenate([w_out[:, a:b] for a, b in _QA_ORDER + ((256, w_out.shape[1]),)], axis=1).astype(BF16)
    wq = p["mla_wq_up"].reshape(DEPTH, MLA_Q_LORA, MLA_HEADS, MLA_NOPE + MLA_ROPE)
    wq = jnp.pad(wq, ((0, 0), (0, 0), (0, 0), (0, LANES - MLA_NOPE - MLA_ROPE))).reshape(DEPTH, MLA_Q_LORA, 4 * LANES)
    wkv = p["mla_wkv_up"].reshape(DEPTH, MLA_KV_LORA, MLA_HEADS, MLA_NOPE + MLA_V)
    wk_top = jnp.pad(wkv[..., :MLA_NOPE], ((0, 0), (0, 0), (0, 0), (0, LANES - MLA_NOPE)))
    place = jnp.pad(jnp.eye(MLA_ROPE, dtype=F32), ((0, LANES - MLA_ROPE), (MLA_NOPE, LANES - MLA_NOPE - MLA_ROPE)))
    wk_bot = jnp.broadcast_to(place[None, :, None, :], (DEPTH, LANES, MLA_HEADS, LANES))
    wk = jnp.concatenate([wk_top, wk_bot], axis=1).reshape(DEPTH, 2 * LANES, 4 * LANES)
    wv = wkv[..., MLA_NOPE:].reshape(DEPTH, MLA_KV_LORA, MLA_HEADS * MLA_V)
    gate_row = lambda v: jnp.pad(v.reshape(DEPTH, 1, DN_GATES), ((0, 0), (0, 0), (0, LANES - DN_GATES)))
    row = lambda v: v[:, None, :]
    return {
        "norm1_g": row(p["norm1_g"]), "norm2_g": row(p["norm2_g"]),
        "w_in": w_in, "w_out": w_out,
        "qn_g": row(jnp.tile(p["gqa_qn_g"], (1, 4))), "kn_g": row(jnp.tile(p["gqa_kn_g"], (1, 2))),
        "mla_qn_g": row(p["mla_qn_g"]), "mla_kvn_g": row(p["mla_kvn_g"]),
        "wq": wq.astype(BF16), "wk": wk.astype(BF16), "wv": wv.astype(BF16),
        "dn_conv_w": p["dn_conv_w"], "dn_alog_row": gate_row(p["dn_a_log"]),
        "dn_dtb_row": gate_row(p["dn_dt_bias"]), "dn_out_g": row(jnp.tile(p["dn_out_g"], (1, DN_HEADS))),
        "w_gate": p["ffn_w_gate"].astype(BF16), "w_up": p["ffn_w_up"].astype(BF16),
        "w_down": p["ffn_w_down"].astype(BF16),
    }


def kernel(x_prompt, x_sample, cache_gqa_k, cache_gqa_v, cache_na_k, cache_na_v, state_dn,
           cache_mla_ckv, cache_mla_krope, c, c_ctx, norm1_g, norm2_g, w_mod, b_mod, w_in, w_out,
           gqa_qn_g, gqa_kn_g, na_bias, dn_conv_w, dn_a_log, dn_dt_bias, dn_out_g, mla_qn_g,
           mla_wq_up, mla_kvn_g, mla_wkv_up, ffn_w_gate, ffn_w_up, ffn_w_down, final_g):
    p = {"norm1_g": norm1_g, "norm2_g": norm2_g, "w_in": w_in, "w_out": w_out, "gqa_qn_g": gqa_qn_g,
         "gqa_kn_g": gqa_kn_g, "dn_conv_w": dn_conv_w, "dn_a_log": dn_a_log, "dn_dt_bias": dn_dt_bias,
         "dn_out_g": dn_out_g, "mla_qn_g": mla_qn_g, "mla_wq_up": mla_wq_up, "mla_kvn_g": mla_kvn_g,
         "mla_wkv_up": mla_wkv_up, "ffn_w_gate": ffn_w_gate, "ffn_w_up": ffn_w_up, "ffn_w_down": ffn_w_down}
    nb_ctx, seq_ctx, _ = x_prompt.shape
    nb_lat, seq_lat, _ = x_sample.shape
    past = cache_gqa_k.shape[2]
    fg = final_g[None]

    cond = jnp.concatenate([c_ctx[None], c, jnp.zeros((16 - 1 - nb_lat, D_MODEL), F32)], axis=0)
    mods = _modulation(cond, w_mod, b_mod).reshape(DEPTH, 16, MOD_CHUNKS, D_MODEL)
    lw = _stacked_weights(p)

    x = x_prompt
    ctx_out = []
    for l in range(DEPTH):
        o_a, ka, va, o_b, kb, vb, zc, o_d, ckv, zm, krr = _inproj(x, mods, False, lw, l, None)
        o_c, s_dn = _deltanet(zc, zm, lw, l)
        x = _outffn((o_a, o_b, o_c, o_d), x, mods, False, lw, l, fg, l == DEPTH - 1)
        ctx_out.append((ka, va, kb, vb, s_dn, ckv, krr[:, :, :MLA_ROPE]))
    y_prompt = x
    new = [jnp.stack([s[i] for s in ctx_out], axis=1) for i in range(7)]
    for i in range(4):
        t = new[i].reshape(nb_ctx, DEPTH, -1, HEAD_DIM, seq_ctx)
        new[i] = jnp.transpose(t, (0, 1, 4, 2, 3))

    ropes = _rope_tables(seq_lat)
    keys_t = lambda c: jnp.transpose(c, (0, 1, 3, 4, 2)).reshape(nb_lat, DEPTH, -1, past)
    ck_a, cv_a, ck_b, cv_b = keys_t(cache_gqa_k), keys_t(cache_gqa_v), keys_t(cache_na_k), keys_t(cache_na_v)
    c_kr = jnp.pad(cache_mla_krope, ((0, 0), (0, 0), (0, 0), (0, LANES - MLA_ROPE)))
    bias_blocks = _na_bias_blocks(na_bias)
    x = x_sample
    for l in range(DEPTH):
        qa, ka, va, qb, kb, vb, zc, qd, ckv, zm, krr = _inproj(x, mods, True, lw, l, ropes)
        o_a = _attn_pair(qa, ka, va, (0, 0), ck_a, cv_a, l, name="gqa_lat")
        o_b = _na_latent(qb, kb, vb, ck_b, cv_b, l, bias_blocks)
        o_c, _ = _deltanet(zc, zm, lw, l, state=state_dn)
        o_d = _mla(qd, ckv, krr, lw, l, cache=(cache_mla_ckv, c_kr), name="mla_lat")
        x = _outffn((o_a, o_b, o_c, o_d), x, mods, True, lw, l, fg, l == DEPTH - 1)
    y_sample = x

    return (y_prompt, y_sample, *new)
```

```python
import functools

import numpy as np
import jax
import jax.numpy as jnp
from jax import lax
from jax.experimental import pallas as pl
from jax.experimental.pallas import tpu as pltpu

F32 = jnp.float32
BF16 = jnp.bfloat16

D_MODEL = 1024
DEPTH = 2
GRID_W = 64
HEAD_DIM = 64
ROPE_BASE = 10000.0
NEG_INF = -1e30
MOD_CHUNKS = 6
GQA_HEADS, GQA_KV_HEADS = 4, 2
NA_HEADS, NA_KH, NA_KW = 4, 8, 16
DN_HEADS, DN_DK, DN_DV, DN_CONV, DN_CHUNK = 4, 64, 64, 4, 64
DN_QKV = DN_HEADS * (2 * DN_DK + DN_DV)
DN_GATES = 2 * DN_HEADS
MLA_HEADS, MLA_Q_LORA, MLA_KV_LORA, MLA_NOPE, MLA_ROPE, MLA_V = 4, 256, 128, 64, 32, 64
MLA_SCALE = (MLA_NOPE + MLA_ROPE) ** -0.5
D_FF = -(-8 * D_MODEL // (3 * 256)) * 256
EPS = 1e-6
LOG2E = 1.4426950408889634

LANES = 128
ROW_TILE = 256
Q_TILE = 256
NA_BAND = 768
TILES_PER_STEP = 2
ATTN_Q_TILE = 512
HEAD_LOOKAHEAD = 1
DN_SEQS = 2
DN_ROWS = 256
PREP_CHUNKS = 4
IN_PAD_COLS = 2816
VMEM_LIMIT = 56 * 1024 * 1024

_NT = (((1,), (1,)), ((), ()))
_TN = (((0,), (0,)), ((), ()))


def _cparams(n_axes):
    return pltpu.CompilerParams(dimension_semantics=("arbitrary",) * n_axes,
                                vmem_limit_bytes=VMEM_LIMIT)


def _lane(shape):
    return lax.broadcasted_iota(jnp.int32, shape, len(shape) - 1)


def _silu(x):
    return x / (1.0 + jnp.exp(-x))


def _rms_full(x, g):
    return x * lax.rsqrt(jnp.mean(x * x, axis=-1, keepdims=True) + EPS) * g


def _seg64_sum(x):
    lo = _lane(x.shape) < HEAD_DIM
    s_lo = jnp.sum(jnp.where(lo, x, 0.0), axis=-1, keepdims=True)
    s_hi = jnp.sum(jnp.where(lo, 0.0, x), axis=-1, keepdims=True)
    return jnp.where(lo, s_lo, s_hi)


def _head_rms(x, g):
    parts = []
    for p in range(x.shape[-1] // LANES):
        xp = x[:, p * LANES:(p + 1) * LANES]
        ms = _seg64_sum(xp * xp) * (1.0 / HEAD_DIM)
        parts.append(xp * lax.rsqrt(ms + EPS))
    y = parts[0] if len(parts) == 1 else jnp.concatenate(parts, axis=-1)
    return y * g


def _head_l2(x):
    parts = []
    for p in range(x.shape[-1] // LANES):
        xp = x[:, p * LANES:(p + 1) * LANES]
        parts.append(xp * lax.rsqrt(_seg64_sum(xp * xp) + EPS))
    return parts[0] if len(parts) == 1 else jnp.concatenate(parts, axis=-1)


def _rope(x, cos, sin, half):
    first = (_lane(x.shape) & (2 * half - 1)) < half
    rot = jnp.where(first, pltpu.roll(x, LANES - half, 1), pltpu.roll(x, half, 1))
    return x * cos + rot * sin


def _softmax_parts(scores):
    m = jnp.max(scores[0], axis=-1, keepdims=True)
    for s in scores[1:]:
        m = jnp.maximum(m, jnp.max(s, axis=-1, keepdims=True))
    es = [jnp.exp2(s - m) for s in scores]
    l = jnp.sum(es[0], axis=-1, keepdims=True)
    for e in es[1:]:
        l = l + jnp.sum(e, axis=-1, keepdims=True)
    return es, 1.0 / l


def _bdot(a, b):
    return jnp.dot(a, b, preferred_element_type=F32)


def _pipelined_heads(n_heads, scores, attend):
    outs = []
    queue = [scores(hd) for hd in range(min(HEAD_LOOKAHEAD, n_heads))]
    for hd in range(n_heads):
        if hd + HEAD_LOOKAHEAD < n_heads:
            queue.append(scores(hd + HEAD_LOOKAHEAD))
        es, rl = _softmax_parts(queue.pop(0))
        outs.append(attend(hd, es) * rl)
    return outs


def _pair_attention(q_blocks, sources, qmap):
    lo = _lane(q_blocks[0].shape) < HEAD_DIM

    def scores(hd):
        p, half = divmod(hd, 2)
        qm = jnp.where(lo if half == 0 else jnp.logical_not(lo), q_blocks[p], 0.0).astype(BF16)
        return [_bdot(qm, k_block(qmap[p])) if transposed
                else lax.dot_general(qm, k_block(qmap[p]), _NT, preferred_element_type=F32)
                for k_block, _, transposed in sources]

    def attend(hd, es):
        kv = qmap[hd // 2]
        parts = [lax.dot_general(e.astype(BF16), v_block(kv), _NT, preferred_element_type=F32) if transposed
                 else _bdot(e.astype(BF16), v_block(kv)) for e, (_, v_block, transposed) in zip(es, sources)]
        return functools.reduce(lambda a, b: a + b, parts)

    outs = _pipelined_heads(2 * len(qmap), scores, attend)
    return [jnp.where(lo, outs[2 * p], outs[2 * p + 1]) for p in range(len(qmap))]


def _mla_attention(q_heads, k_head, v_block):
    lo = _lane(q_heads[0].shape) < MLA_V

    def scores(hd):
        return [lax.dot_general(q_heads[hd], k_head(hd), _NT, preferred_element_type=F32)]

    def attend(hd, es):
        return _bdot(es[0].astype(BF16), v_block(hd // 2))

    outs = _pipelined_heads(MLA_HEADS, scores, attend)
    return [jnp.where(lo, outs[2 * p], outs[2 * p + 1]) for p in range(MLA_HEADS // 2)]


def _mod_kernel(c_ref, w_ref, b_ref, o_ref):
    s = _silu(c_ref[...]).astype(BF16)
    o_ref[...] = _bdot(s, w_ref[...].astype(BF16)) + b_ref[...]


def _modulation(cond, w_mod, b_mod):
    n = MOD_CHUNKS * D_MODEL
    tn = 1536
    return pl.pallas_call(
        _mod_kernel,
        grid=(DEPTH, n // tn),
        in_specs=[pl.BlockSpec((16, D_MODEL), lambda l, j: (0, 0)),
                  pl.BlockSpec((None, D_MODEL, tn), lambda l, j: (l, 0, j)),
                  pl.BlockSpec((None, 1, tn), lambda l, j: (l, 0, j))],
        out_specs=pl.BlockSpec((None, 16, tn), lambda l, j: (l, 0, j)),
        out_shape=jax.ShapeDtypeStruct((DEPTH, 16, n), F32),
        compiler_params=_cparams(2),
        name="modulation",
    )(cond, w_mod, b_mod.reshape(DEPTH, 1, n))


_IN_OUT_WIDTHS = (256, 128, 128, 256, 256, 256, 1024, 512, 128, 128, 128)
_IN_OUT_WIDTHS_CTX = (256, 128, 128, 256, 256, 256, 1024, 256, 128, 128, 128)
_IN_OUT_DTYPES_LAT = (BF16, BF16, BF16, BF16, BF16, BF16, F32, BF16, BF16, F32, BF16)
_IN_OUT_DTYPES_CTX = (BF16, F32, F32, BF16, F32, F32, F32, BF16, F32, F32, F32)


def _inproj_kernel(positioned, *refs):
    (x_ref, mod_ref, g1_ref, w_ref, qng_ref, kng_ref, mqg_ref, wq_ref, mkg_ref) = refs[:9]
    if positioned:
        cos64_ref, sin64_ref, cosm_ref, sinm_ref, coskr_ref, sinkr_ref = refs[9:15]
        n_in = 15
    else:
        wk_ref, wv_ref = refs[9:11]
        n_in = 11
    (qa_ref, ka_ref, va_ref, qb_ref, kb_ref, vb_ref, zc_ref, qd_ref, ckv_ref, zm_ref,
     krr_ref) = refs[n_in:]
    tiles = range(TILES_PER_STEP)
    tm = x_ref.shape[1]
    lane = _lane((tm, LANES))
    lo = lane < HEAD_DIM

    m = mod_ref[...]
    hb = jnp.concatenate([(_rms_full(x_ref[t], g1_ref[...]) * (1.0 + m[1:2]) + m[0:1]).astype(BF16)
                          for t in tiles], axis=0)

    def project(c0, c1):
        z = lax.dot_general(hb, w_ref[c0:c1, :], _NT, preferred_element_type=F32)
        return [z[t * tm:(t + 1) * tm] for t in tiles]

    za = project(0, 512)
    zb = project(512, 1280)

    for t in tiles:
        q = _head_rms(za[t][:, 0:256], qng_ref[...])
        k = _head_rms(za[t][:, 256:384], kng_ref[...])
        q0, q1 = q[:, 0:128], q[:, 128:256]
        q0, q1 = jnp.where(lo, q0, pltpu.roll(q1, HEAD_DIM, 1)), jnp.where(lo, pltpu.roll(q0, HEAD_DIM, 1), q1)
        if positioned:
            cos, sin = cos64_ref[t], sin64_ref[t]
            q0, q1 = _rope(q0, cos, sin, 16), _rope(q1, cos, sin, 16)
            k = _rope(k, cos, sin, 16)
        q0, q1 = q0 * (HEAD_DIM ** -0.5 * LOG2E), q1 * (HEAD_DIM ** -0.5 * LOG2E)
        v = za[t][:, 384:512]
        if not positioned:
            kb16, vb16 = k.astype(BF16), v.astype(BF16)
            q0, q1 = _pair_attention([q0, q1], [(lambda i: kb16, lambda i: vb16, False)], (0, 0))
        qa_ref[t] = jnp.concatenate([q0, q1], axis=-1).astype(qa_ref.dtype)
        ka_ref[t] = k.astype(ka_ref.dtype) if positioned else k.T
        va_ref[t] = v.astype(va_ref.dtype) if positioned else v.T

    zd = project(2304, IN_PAD_COLS)
    zc = project(1280, 2304)

    for t in tiles:
        q = zb[t][:, 0:256] * (HEAD_DIM ** -0.5 * LOG2E)
        k, v = zb[t][:, 256:512], zb[t][:, 512:768]
        if not positioned:
            kb16, vb16 = k.astype(BF16), v.astype(BF16)
            q = jnp.concatenate(_pair_attention(
                [q[:, 0:LANES], q[:, LANES:]],
                [(lambda i, kb16=kb16: kb16[:, i * LANES:(i + 1) * LANES],
                  lambda i, vb16=vb16: vb16[:, i * LANES:(i + 1) * LANES], False)], (0, 1)), axis=-1)
        qb_ref[t] = q.astype(qb_ref.dtype)
        kb_ref[t] = k.astype(kb_ref.dtype) if positioned else k.T
        vb_ref[t] = v.astype(vb_ref.dtype) if positioned else v.T

    shifted = []
    for t in tiles:
        rolled = [pltpu.roll(zd[t][:, j * LANES:(j + 1) * LANES], LANES - 2 * DN_GATES, 1) for j in range(4)]
        keep = lane < LANES - 2 * DN_GATES
        shifted.append([jnp.where(keep, rolled[j], rolled[(j + 1) % 4]) for j in range(4)])

    cq = jnp.concatenate([_rms_full(jnp.concatenate(shifted[t][0:2], axis=-1), mqg_ref[...]).astype(BF16)
                          for t in tiles], axis=0)
    qm = _bdot(cq, wq_ref[...])
    for t in tiles:
        zc_ref[t] = zc[t]
        zm_ref[t] = zd[t][:, 0:LANES]
        ckv = _rms_full(shifted[t][2], mkg_ref[...])
        ckv_ref[t] = ckv.astype(ckv_ref.dtype)
        kr = jnp.where(lane < MLA_ROPE, shifted[t][3], 0.0)
        q = qm[t * tm:(t + 1) * tm]
        if positioned:
            kr = _rope(kr, coskr_ref[t], sinkr_ref[t], 8)
            cm, sm = cosm_ref[t], sinm_ref[t]
            q = jnp.concatenate([_rope(q[:, i * LANES:(i + 1) * LANES], cm, sm, 8)
                                 for i in range(MLA_HEADS)], axis=-1)
        krr_ref[t] = kr.astype(krr_ref.dtype)
        q = q * (MLA_SCALE * LOG2E)
        if not positioned:
            c16 = ckv.astype(BF16)
            k16 = _bdot(jnp.concatenate([c16, kr.astype(BF16)], axis=-1), wk_ref[...]).astype(BF16)
            v16 = _bdot(c16, wv_ref[...]).astype(BF16)
            q = jnp.concatenate(_mla_attention(
                [q[:, i * LANES:(i + 1) * LANES].astype(BF16) for i in range(MLA_HEADS)],
                lambda i, k16=k16: k16[:, i * LANES:(i + 1) * LANES],
                lambda i, v16=v16: v16[:, i * LANES:(i + 1) * LANES]), axis=-1)
        qd_ref[t] = q.astype(qd_ref.dtype)


def _inproj(x, mods, per_batch_mods, lw, layer, ropes):
    b, l, _ = x.shape
    tm, ts = ROW_TILE, TILES_PER_STEP
    n_tiles = b * l // tm
    tiles_per_seq = l // tm
    positioned = ropes is not None
    row = lambda w: pl.BlockSpec((ts, tm, w), lambda i: (i, 0, 0))
    const = lambda shape: pl.BlockSpec((None,) + shape, lambda i: (layer,) + (0,) * len(shape))
    assert not per_batch_mods or tiles_per_seq % ts == 0
    mod_spec = pl.BlockSpec((None, None, MOD_CHUNKS, D_MODEL),
                            (lambda i: (layer, 1 + i * ts // tiles_per_seq, 0, 0)) if per_batch_mods
                            else (lambda i: (layer, 0, 0, 0)))
    in_specs = [row(D_MODEL), mod_spec, const((1, D_MODEL)), const((IN_PAD_COLS, D_MODEL)),
                const((1, 256)), const((1, 128)), const((1, MLA_Q_LORA)),
                const((MLA_Q_LORA, 4 * LANES)), const((1, MLA_KV_LORA))]
    args = [x.reshape(n_tiles, tm, D_MODEL), mods, lw["norm1_g"], lw["w_in"], lw["qn_g"], lw["kn_g"],
            lw["mla_qn_g"], lw["wq"], lw["mla_kvn_g"]]
    if positioned:
        steps_per_seq = tiles_per_seq // ts
        in_specs += [pl.BlockSpec((ts, tm, LANES), lambda i: (i % steps_per_seq, 0, 0))] * 6
        args += [r.reshape(tiles_per_seq, tm, LANES) for r in ropes]
        widths, dtypes = _IN_OUT_WIDTHS, _IN_OUT_DTYPES_LAT
    else:
        assert tiles_per_seq == 1
        in_specs += [const((2 * LANES, 4 * LANES)), const((LANES, 2 * LANES))]
        args += [lw["wk"], lw["wv"]]
        widths, dtypes = _IN_OUT_WIDTHS_CTX, _IN_OUT_DTYPES_CTX
    flipped = () if positioned else (1, 2, 4, 5)
    shapes = [(n_tiles, w, tm) if i in flipped else (n_tiles, tm, w) for i, w in enumerate(widths)]
    outs = pl.pallas_call(
        functools.partial(_inproj_kernel, positioned),
        grid=(n_tiles // ts,),
        in_specs=in_specs,
        out_specs=[pl.BlockSpec((ts,) + shp[1:], lambda i: (i, 0, 0)) for shp in shapes],
        out_shape=[jax.ShapeDtypeStruct(shp, dt) for shp, dt in zip(shapes, dtypes)],
        compiler_params=_cparams(1),
        name="inproj_lat" if positioned else "inproj_attn_ctx",
    )(*args)
    return [o if i in flipped else o.reshape(b, l, o.shape[-1]) for i, o in enumerate(outs)]


def _attn_pair_kernel(qmap, q_ref, kc_ref, vc_ref, k_ref, v_ref, o_ref, kcbuf, vcbuf, kbuf, vbuf):
    @pl.when(pl.program_id(1) == 0)
    def _():
        kcbuf[...] = kc_ref[...].astype(BF16)
        vcbuf[...] = vc_ref[...].astype(BF16)
        kbuf[...] = k_ref[...].astype(BF16)
        vbuf[...] = v_ref[...].astype(BF16)

    rows = lambda buf: (lambda i: buf[i * LANES:(i + 1) * LANES, :])
    cols = lambda buf: (lambda i: buf[:, i * LANES:(i + 1) * LANES])
    q_blocks = [q_ref[:, p * LANES:(p + 1) * LANES].astype(F32) for p in range(len(qmap))]
    outs = _pair_attention(q_blocks, [(rows(kcbuf), rows(vcbuf), True), (cols(kbuf), cols(vbuf), False)], qmap)
    for p, o in enumerate(outs):
        o_ref[:, p * LANES:(p + 1) * LANES] = o.astype(o_ref.dtype)


def _attn_pair(q, k, v, qmap, kc_t, vc_t, layer, name):
    b, lq, wq = q.shape
    ls, wk = k.shape[1], k.shape[2]
    lc = kc_t.shape[3]
    tq = min(ATTN_Q_TILE, lq)
    cspec = pl.BlockSpec((None, None, wk, lc), lambda bi, i: (bi, layer, 0, 0))
    sspec = pl.BlockSpec((None, ls, wk), lambda bi, i: (bi, 0, 0))
    return pl.pallas_call(
        functools.partial(_attn_pair_kernel, qmap),
        grid=(b, lq // tq),
        in_specs=[pl.BlockSpec((None, tq, wq), lambda bi, i: (bi, i, 0)), cspec, cspec, sspec, sspec],
        out_specs=pl.BlockSpec((None, tq, wq), lambda bi, i: (bi, i, 0)),
        out_shape=jax.ShapeDtypeStruct((b, lq, wq), BF16),
        scratch_shapes=[pltpu.VMEM((wk, lc), BF16), pltpu.VMEM((wk, lc), BF16),
                        pltpu.VMEM((ls, wk), BF16), pltpu.VMEM((ls, wk), BF16)],
        compiler_params=_cparams(2),
        name=name,
    )(q, kc_t, vc_t, k, v)


def _mla_kernel(has_cache, *refs):
    if has_cache:
        q_ref, ckvc_ref, krc_ref, ckv_ref, kr_ref, wk_ref, wv_ref, o_ref, kbuf, vbuf = refs
    else:
        q_ref, ckv_ref, kr_ref, wk_ref, wv_ref, o_ref, kbuf, vbuf = refs

    @pl.when(pl.program_id(1) == 0)
    def _():
        def expand(c_ref, r_ref, r0, r1):
            c = c_ref[...].astype(BF16)
            ckr = jnp.concatenate([c, r_ref[...].astype(BF16)], axis=-1)
            kbuf[r0:r1, :] = _bdot(ckr, wk_ref[...]).astype(BF16)
            vbuf[r0:r1, :] = _bdot(c, wv_ref[...]).astype(BF16)
        off = 0
        if has_cache:
            off = ckvc_ref.shape[0]
            expand(ckvc_ref, krc_ref, 0, off)
        expand(ckv_ref, kr_ref, off, kbuf.shape[0])

    outs = _mla_attention([q_ref[:, hd * LANES:(hd + 1) * LANES].astype(BF16) for hd in range(MLA_HEADS)],
                          lambda i: kbuf[:, i * LANES:(i + 1) * LANES],
                          lambda i: vbuf[:, i * LANES:(i + 1) * LANES])
    for p, o in enumerate(outs):
        o_ref[:, p * LANES:(p + 1) * LANES] = o.astype(o_ref.dtype)


def _mla(q, ckv, kr, lw, layer, cache=None, name="mla"):
    b, lq, wq = q.shape
    ls = ckv.shape[1]
    tq = min(ATTN_Q_TILE, lq)
    lc = 0 if cache is None else cache[0].shape[2]
    in_specs = [pl.BlockSpec((None, tq, wq), lambda bi, i: (bi, i, 0))]
    args = [q]
    if cache is not None:
        ckvc, krc = cache
        cspec = pl.BlockSpec((None, None, lc, LANES), lambda bi, i: (bi, layer, 0, 0))
        in_specs += [cspec, cspec]
        args += [ckvc, krc]
    sspec = pl.BlockSpec((None, ls, LANES), lambda bi, i: (bi, 0, 0))
    in_specs += [sspec, sspec,
                 pl.BlockSpec((None, 2 * LANES, 4 * LANES), lambda bi, i: (layer, 0, 0)),
                 pl.BlockSpec((None, LANES, 2 * LANES), lambda bi, i: (layer, 0, 0))]
    args += [ckv, kr, lw["wk"], lw["wv"]]
    return pl.pallas_call(
        functools.partial(_mla_kernel, cache is not None),
        grid=(b, lq // tq),
        in_specs=in_specs,
        out_specs=pl.BlockSpec((None, tq, 2 * LANES), lambda bi, i: (bi, i, 0)),
        out_shape=jax.ShapeDtypeStruct((b, lq, 2 * LANES), BF16),
        scratch_shapes=[pltpu.VMEM((lc + ls, 4 * LANES), BF16), pltpu.VMEM((lc + ls, 2 * LANES), BF16)],
        compiler_params=_cparams(2),
        name=name,
    )(*args)


def _na_kernel(q_ref, k_ref, v_ref, kc_ref, vc_ref, bias_ref, o_ref):
    j = pl.program_id(0)
    start = pl.multiple_of((j >> 1) * 256, 256)
    kband = k_ref[pl.ds(start, NA_BAND), :].astype(BF16)
    vband = v_ref[pl.ds(start, NA_BAND), :].astype(BF16)
    kc = kc_ref[...].astype(BF16)
    vc = vc_ref[...].astype(BF16)
    tq = q_ref.shape[0]
    lo = _lane((tq, LANES)) < HEAD_DIM

    def scores(hd):
        p, half = divmod(hd, 2)
        sl = slice(p * LANES, (p + 1) * LANES)
        qm = jnp.where(lo if half == 0 else jnp.logical_not(lo), q_ref[:, sl].astype(F32), 0.0).astype(BF16)
        s_loc = lax.dot_general(qm, kband[:, sl], _NT, preferred_element_type=F32) + bias_ref[hd]
        return [s_loc, _bdot(qm, kc[sl, :])]

    def attend(hd, es):
        sl = slice((hd // 2) * LANES, (hd // 2 + 1) * LANES)
        return (_bdot(es[0].astype(BF16), vband[:, sl])
                + lax.dot_general(es[1].astype(BF16), vc[sl, :], _NT, preferred_element_type=F32))

    outs = _pipelined_heads(NA_HEADS, scores, attend)
    for p in range(NA_HEADS // 2):
        o_ref[:, p * LANES:(p + 1) * LANES] = jnp.where(lo, outs[2 * p], outs[2 * p + 1]).astype(o_ref.dtype)


def _na_latent(q, k, v, kc_t, vc_t, layer, bias_blocks):
    b, n, w = q.shape
    lc = kc_t.shape[3]
    nq = n // Q_TILE
    full = pl.BlockSpec((None, n, w), lambda j, bi: (bi, 0, 0))
    cspec = pl.BlockSpec((None, None, w, lc), lambda j, bi: (bi, layer, 0, 0))
    return pl.pallas_call(
        _na_kernel,
        grid=(nq, b),
        in_specs=[pl.BlockSpec((None, Q_TILE, w), lambda j, bi: (bi, j, 0)), full, full, cspec, cspec,
                  pl.BlockSpec((None, None, NA_HEADS, Q_TILE, NA_BAND), lambda j, bi: (layer, j, 0, 0, 0))],
        out_specs=pl.BlockSpec((None, Q_TILE, w), lambda j, bi: (bi, j, 0)),
        out_shape=jax.ShapeDtypeStruct((b, n, w), BF16),
        compiler_params=_cparams(2),
        name="na_latent",
    )(q, k, v, kc_t, vc_t, bias_blocks)


NA_GRID_ROWS = 16
NA_BAND_ROW0 = (0, 0, 4, 4)


def _na_bias_kernel(b_ref, o_ref, tp_s):
    hd = pl.program_id(0)
    n_dr, n_dc = 2 * NA_KH - 1, 2 * NA_KW - 1
    shape = (GRID_W, LANES)
    c = lax.broadcasted_iota(jnp.int32, shape, 0)
    lane = _lane(shape)
    kc = lane & (GRID_W - 1)
    lo = lane < GRID_W
    diff = kc - c + (NA_KW - 1)
    c0 = jnp.clip(c - NA_KW // 2, 0, GRID_W - NA_KW)
    col_ok = (kc >= c0) & (kc < c0 + NA_KW)
    neg = jnp.full(shape, NEG_INF, F32)
    for dr0 in range(-1, n_dr):
        acc = neg
        for d in range(n_dc):
            v_lo = b_ref[hd * n_dr + dr0, d] if dr0 >= 0 else 0.0
            v_hi = b_ref[hd * n_dr + dr0 + 1, d] if dr0 + 1 < n_dr else 0.0
            acc = jnp.where(diff == d, jnp.where(lo, v_lo, v_hi), acc)
        tp_s[dr0 + 1] = jnp.where(col_ok, acc * LOG2E, NEG_INF)
    for j in range(NA_GRID_ROWS // 4):
        for ri in range(4):
            r = 4 * j + ri
            r0 = min(max(r - NA_KH // 2, 0), NA_GRID_ROWS - NA_KH)
            for kp in range(NA_BAND // LANES):
                kr = NA_BAND_ROW0[j] + 2 * kp
                ok_lo, ok_hi = r0 <= kr < r0 + NA_KH, r0 <= kr + 1 < r0 + NA_KH
                dr0 = kr - r + (NA_KH - 1)
                if ok_lo and ok_hi:
                    t = tp_s[dr0 + 1]
                elif ok_lo:
                    t = jnp.where(lo, tp_s[dr0 + 1], NEG_INF)
                elif ok_hi:
                    t = jnp.where(lo, NEG_INF, tp_s[dr0 + 1])
                else:
                    t = neg
                o_ref[j, ri * GRID_W:(ri + 1) * GRID_W, kp * LANES:(kp + 1) * LANES] = t


def _na_bias_blocks(bias):
    nq = NA_GRID_ROWS // 4
    return pl.pallas_call(
        _na_bias_kernel,
        grid=(DEPTH * NA_HEADS,),
        in_specs=[pl.BlockSpec(memory_space=pltpu.SMEM)],
        out_specs=pl.BlockSpec((None, nq, None, Q_TILE, NA_BAND),
                               lambda i: (i // NA_HEADS, 0, i % NA_HEADS, 0, 0)),
        out_shape=jax.ShapeDtypeStruct((DEPTH, nq, NA_HEADS, Q_TILE, NA_BAND), F32),
        scratch_shapes=[pltpu.VMEM((2 * NA_KH, GRID_W, LANES), F32)],
        compiler_params=_cparams(1),
        name="na_bias",
    )(bias.reshape(DEPTH * NA_HEADS * (2 * NA_KH - 1), 2 * NA_KW - 1))


def _widen(cols, n):
    blk = _lane((n, DN_HEADS * DN_DV)) >> 6
    return jnp.where(blk == 0, cols[0], jnp.where(blk == 1, cols[1], jnp.where(blk == 2, cols[2], cols[3])))


def _deltanet_kernel(seq, has_state, *refs):
    if has_state:
        zc_all, zm_all, s0_ref, cw_ref, alog_ref, dtb_ref, og_ref, o_ref = refs[:8]
    else:
        zc_all, zm_all, cw_ref, alog_ref, dtb_ref, og_ref, o_ref, sfin_ref = refs[:8]
    scratch = refs[8:]
    g_all, o_all, c_all, mp_all = scratch[4:8]
    n_chunks = seq // DN_CHUNK
    wide = DN_HEADS * DN_DV

    a_off, b_off = 0, DN_GATES
    ri = lax.broadcasted_iota(jnp.int32, (DN_ROWS, DN_ROWS), 0)
    ci = lax.broadcasted_iota(jnp.int32, (DN_ROWS, DN_ROWS), 1)
    same = (ri >> 6) == (ci >> 6)
    tri = [jnp.tile(jnp.where(same & ((ci <= ri) if d == 0 else (ci >= ri)), 1.0, 0.0).astype(BF16), (1, 3))
           for d in range(2)]
    n_blocks = seq // DN_ROWS

    def terms3(x):
        hi = x.astype(BF16)
        rest = x - hi.astype(F32)
        mid = rest.astype(BF16)
        return hi, mid, (rest - mid.astype(F32)).astype(BF16)

    def split3(x):
        return jnp.concatenate(terms3(x), axis=0)

    def preprocess(views, rb, carry):
        zc_ref, zm_ref, q_s, k_s, v_s, b_s, g_s = views[:7]
        r0 = pl.multiple_of(rb * DN_ROWS, DN_ROWS)
        rows = pl.ds(r0, DN_ROWS)
        before = pl.ds(pl.multiple_of(jnp.maximum(r0 - 8, 0), 8), 8)
        after = pl.ds(pl.multiple_of(jnp.minimum(r0 + DN_ROWS, seq - 8), 8), 8)
        for part, dst in enumerate((q_s, k_s, v_s)):
            cs = slice(part * wide, (part + 1) * wide)
            head = jnp.where(rb > 0, zc_ref[before, cs], 0.0)
            tail = jnp.where(rb < n_blocks - 1, zc_ref[after, cs], 0.0)
            xe = jnp.concatenate([head, zc_ref[rows, cs], tail], axis=0)
            w = cw_ref[:, cs]
            y = (w[0:1] * xe[7:7 + DN_ROWS] + w[1:2] * xe[8:8 + DN_ROWS]
                 + w[2:3] * xe[9:9 + DN_ROWS] + w[3:4] * xe[10:10 + DN_ROWS])
            y = _silu(y)
            if part == 0:
                y = _head_l2(y) * (DN_DK ** -0.5)
            elif part == 1:
                y = _head_l2(y)
            dst[rows, :] = y
        zm = zm_ref[rows, :]
        xa = zm + dtb_ref[...]
        logd = -jnp.exp(alog_ref[...]) * (jnp.maximum(xa, 0.0) + jnp.log1p(jnp.exp(-jnp.abs(xa))))
        beta = 1.0 / (1.0 + jnp.exp(-zm))
        logd3 = split3(logd)
        for d in range(2):
            b_s[d, rows, :] = _widen([beta[:, b_off + 4 * d + hd:b_off + 4 * d + hd + 1]
                                      for hd in range(DN_HEADS)], DN_ROWS)
            g = _bdot(tri[d], logd3)
            g_s[d, rows, :] = _widen([g[:, a_off + 4 * d + hd:a_off + 4 * d + hd + 1]
                                      for hd in range(DN_HEADS)], DN_ROWS)
        return carry

    ii = lax.broadcasted_iota(jnp.int32, (DN_CHUNK, wide), 0)
    jj = _lane((DN_CHUNK, wide)) & (DN_CHUNK - 1)
    blk = _lane((DN_CHUNK, wide)) >> 6
    diag = ii == jj
    eye = jnp.where(diag, 1.0, 0.0)
    head_mask = [jnp.where(blk == hd, 1.0, 0.0).astype(BF16) for hd in range(DN_HEADS)]

    def bd(z):
        zb = z.astype(BF16)
        return jnp.concatenate([zb * hm for hm in head_mask], axis=0)

    def fold(gram):
        out = jnp.where(blk == 0, gram[0:DN_CHUNK], 0.0)
        for hd in range(1, DN_HEADS):
            out = out + jnp.where(blk == hd, gram[hd * DN_CHUNK:(hd + 1) * DN_CHUNK], 0.0)
        return out

    tri_masks = []
    for d in range(2):
        incl = (jj <= ii) if d == 0 else (jj >= ii)
        strict = (jj < ii) if d == 0 else (jj > ii)
        pair = [((ii >> (lvl + 1)) == (jj >> (lvl + 1)))
                & (((ii >> lvl) & 1) == (1 - d)) & (((jj >> lvl) & 1) == d) for lvl in range(6)]
        tri_masks.append((incl, strict, pair))

    def prepare(views, step, carry):
        q_s, k_s, v_s, b_s, g_s, o_s, c_s, mp_s = views[2:]
        chunks = [step * PREP_CHUNKS + i for i in range(PREP_CHUNKS)]
        rows = [pl.ds(pl.multiple_of(c * DN_CHUNK, DN_CHUNK), DN_CHUNK) for c in chunks]
        qkv = [(q_s[r, :], k_s[r, :], v_s[r, :]) for r in rows]
        inst = [(ci, d) for ci in range(PREP_CHUNKS) for d in range(2)]
        beta = {(ci, d): b_s[d, rows[ci], :] for ci, d in inst}
        kb = {(ci, d): qkv[ci][1] * beta[ci, d] for ci, d in inst}
        r = [lax.dot_general(jnp.concatenate([kb[ci, 0], kb[ci, 1], qkv[ci][0]], axis=0).astype(BF16),
                             bd(qkv[ci][1]), _NT, preferred_element_type=F32)
             for ci in range(PREP_CHUNKS)]
        g, a, qk, eg, t = {}, {}, {}, {}, {}
        for ci, d in inst:
            incl, strict, pair = tri_masks[d]
            g[ci, d] = g_s[d, rows[ci], :]
            g_row = jnp.sum(jnp.where(diag, g[ci, d], 0.0), axis=0, keepdims=True)
            dm = jnp.where(incl, jnp.exp(jnp.where(incl, g[ci, d] - g_row, 0.0)), 0.0)
            a[ci, d] = jnp.where(strict, r[ci][d * DN_CHUNK:(d + 1) * DN_CHUNK] * dm, 0.0)
            qk[ci, d] = r[ci][2 * DN_CHUNK:] * dm
            eg[ci, d] = jnp.exp(g[ci, d])
            t[ci, d] = eye - jnp.where(pair[0], a[ci, d], 0.0)
        for lvl in range(1, 6):
            te = {i: _bdot(t[i].astype(BF16), bd(jnp.where(tri_masks[i[1]][2][lvl], a[i], 0.0))) for i in inst}
            t = {i: t[i] - _bdot(te[i].astype(BF16), bd(t[i])) for i in inst}
        nb = {i: jnp.where(diag, 0.0, t[i]).astype(BF16) for i in inst}
        rhs_u = {(ci, d): qkv[ci][2] * beta[ci, d] for ci, d in inst}
        rhs_w = {i: kb[i] * eg[i] for i in inst}
        q2 = {i: qk[i] + _bdot(qk[i].astype(BF16), bd(nb[i])) for i in inst}
        both = {i: jnp.concatenate([nb[i], q2[i].astype(BF16)], axis=0) for i in inst}
        ru = {i: _bdot(both[i], bd(rhs_u[i])) for i in inst}
        rw = {i: _bdot(both[i], bd(rhs_w[i])) for i in inst}
        u = {i: rhs_u[i] + ru[i][0:DN_CHUNK] for i in inst}
        w = {i: rhs_w[i] + rw[i][0:DN_CHUNK] for i in inst}
        kd = {}
        for ci, d in inst:
            g_last = g[ci, d][DN_CHUNK - 1:DN_CHUNK] if d == 0 else g[ci, d][0:1]
            kd[ci, d] = (qkv[ci][1] * jnp.exp(g_last - g[ci, d])).astype(BF16)
        p = {(ci, d): qkv[ci][0] * eg[ci, d] - rw[ci, d][DN_CHUNK:] for ci, d in inst}
        o0 = {i: ru[i][DN_CHUNK:] for i in inst}
        m = {i: fold(lax.dot_general(kd[i], w[i].astype(BF16), _TN, preferred_element_type=F32)) for i in inst}
        cc = {i: fold(lax.dot_general(kd[i], u[i].astype(BF16), _TN, preferred_element_type=F32)) for i in inst}
        for ci, d in inst:
            mrow = pl.ds(pl.multiple_of(chunks[ci] * (2 * DN_CHUNK), 2 * DN_CHUNK), 2 * DN_CHUNK)
            c_s[d, rows[ci], :] = cc[ci, d]
            mp_s[d, mrow, :] = jnp.concatenate([m[ci, d], p[ci, d]], axis=0).astype(BF16)
        for ci in range(PREP_CHUNKS):
            o_s[rows[ci], :] = o0[ci, 0] + o0[ci, 1]
        return carry

    for sq in range(DN_SEQS):
        views = (zc_all.at[sq], zm_all.at[sq]) + tuple(ref.at[sq] for ref in scratch)
        lax.fori_loop(0, n_blocks, functools.partial(preprocess, views), 0)
        lax.fori_loop(0, n_chunks // PREP_CHUNKS, functools.partial(prepare, views), 0)

    chains = [(sq, d) for sq in range(DN_SEQS) for d in range(2)]

    def scan(i, states):
        new = []
        for (sq, d), state in zip(chains, states):
            c = i if d == 0 else n_chunks - 1 - i
            r0 = pl.multiple_of(c * DN_CHUNK, DN_CHUNK)
            rows = pl.ds(r0, DN_CHUNK)
            mrow = pl.ds(pl.multiple_of(c * (2 * DN_CHUNK), 2 * DN_CHUNK), 2 * DN_CHUNK)
            edge = pl.ds(pl.multiple_of(r0 + (DN_CHUNK - 8 if d == 0 else 0), 8), 8)
            g_edge = g_all[sq, d, edge, :]
            g_last = g_edge[7:8] if d == 0 else g_edge[0:1]
            res = _bdot(mp_all[sq, d, mrow, :], bd(state))
            o_all[sq, rows, :] = o_all[sq, rows, :] + res[DN_CHUNK:]
            new.append(state * jnp.exp(g_last) - res[0:DN_CHUNK] + c_all[sq, d, rows, :])
        return tuple(new)

    place = [jnp.where(diag & (blk == hd), 1.0, 0.0).astype(BF16) for hd in range(DN_HEADS)]

    def to_wide(heads):
        parts = [_bdot(term, place[hd]) for hd, x in enumerate(heads) for term in terms3(x)]
        return functools.reduce(lambda a, b: a + b, parts)

    def head_of(s, hd):
        parts = [lax.dot_general(term, place[hd], _NT, preferred_element_type=F32) for term in terms3(s)]
        return functools.reduce(lambda a, b: a + b, parts)

    if has_state:
        init = tuple(to_wide([s0_ref[sq, d, hd] for hd in range(DN_HEADS)]) for sq, d in chains)
    else:
        init = tuple(jnp.zeros((DN_DK, wide), F32) for _ in chains)
    fin = lax.fori_loop(0, n_chunks, scan, init)
    if not has_state:
        for (sq, d), state in zip(chains, fin):
            for hd in range(DN_HEADS):
                sfin_ref[sq, d, hd] = head_of(state, hd)

    for sq in range(DN_SEQS):
        gate = _silu(zc_all[sq, :, 3 * wide:4 * wide])
        o_ref[sq] = (_head_rms(o_all[sq], og_ref[...]) * gate).astype(o_ref.dtype)


def _deltanet(zc, zm, lw, layer, state=None):
    b, seq, _ = zc.shape
    wide = DN_HEADS * DN_DV
    has_state = state is not None
    ns = DN_SEQS
    per_b = lambda w: pl.BlockSpec((ns, seq, w), lambda bi: (bi, 0, 0))
    const = lambda shape: pl.BlockSpec((None,) + shape, lambda bi: (layer,) + (0,) * len(shape))
    st_spec = pl.BlockSpec((ns, 2, DN_HEADS, DN_DK, DN_DV), lambda bi: (bi, 0, 0, 0, 0))
    in_specs = [per_b(4 * wide), per_b(LANES)]
    args = [zc, zm]
    if has_state:
        in_specs.append(pl.BlockSpec((ns, None, 2, DN_HEADS, DN_DK, DN_DV),
                                     lambda bi: (bi, layer, 0, 0, 0, 0)))
        args.append(state)
    in_specs += [const((DN_CONV, DN_QKV)), const((1, LANES)), const((1, LANES)), const((1, wide))]
    args += [lw["dn_conv_w"], lw["dn_alog_row"], lw["dn_dtb_row"], lw["dn_out_g"]]
    out_specs = [per_b(wide)]
    out_shape = [jax.ShapeDtypeStruct((b, seq, wide), BF16)]
    if not has_state:
        out_specs.append(st_spec)
        out_shape.append(jax.ShapeDtypeStruct((b, 2, DN_HEADS, DN_DK, DN_DV), F32))
    res = pl.pallas_call(
        functools.partial(_deltanet_kernel, seq, has_state),
        grid=(b // ns,),
        in_specs=in_specs,
        out_specs=out_specs,
        out_shape=out_shape,
        scratch_shapes=[pltpu.VMEM((ns, seq, wide), F32), pltpu.VMEM((ns, seq, wide), F32),
                        pltpu.VMEM((ns, seq, wide), F32), pltpu.VMEM((ns, 2, seq, wide), F32),
                        pltpu.VMEM((ns, 2, seq, wide), F32), pltpu.VMEM((ns, seq, wide), F32),
                        pltpu.VMEM((ns, 2, seq, wide), F32), pltpu.VMEM((ns, 2, 2 * seq, wide), BF16)],
        compiler_params=_cparams(1),
        name="deltanet_lat" if has_state else "deltanet_ctx",
    )(*args)
    return (res[0], None) if has_state else (res[0], res[1])


def _outffn_kernel(final, oa_ref, ob_ref, oc_ref, od_ref, x_ref, mod_ref, g2_ref, wo_ref, wg_ref,
                   wu_ref, wd_ref, fg_ref, y_ref):
    tiles = range(TILES_PER_STEP)
    m = mod_ref[...]
    o = [jnp.concatenate([oa_ref[t], ob_ref[t], oc_ref[t], od_ref[t]], axis=-1).astype(BF16) for t in tiles]
    x1 = [x_ref[t] + m[2:3] * _bdot(o[t], wo_ref[...]) for t in tiles]
    h = [(_rms_full(x1[t], g2_ref[...]) * (1.0 + m[4:5]) + m[3:4]).astype(BF16) for t in tiles]
    gate = [_bdot(h[t], wg_ref[...]) for t in tiles]
    up = [_bdot(h[t], wu_ref[...]) for t in tiles]
    act = [(_silu(gate[t]) * up[t]).astype(BF16) for t in tiles]
    x2 = [x1[t] + m[5:6] * _bdot(act[t], wd_ref[...]) for t in tiles]
    for t in tiles:
        y_ref[t] = _rms_full(x2[t], fg_ref[...]) if final else x2[t]


def _outffn(outs, x, mods, per_batch_mods, lw, layer, final_g, final):
    b, l, _ = x.shape
    tm, ts = ROW_TILE, TILES_PER_STEP
    n_tiles = b * l // tm
    tiles_per_seq = l // tm
    assert not per_batch_mods or tiles_per_seq % ts == 0
    row = lambda w: pl.BlockSpec((ts, tm, w), lambda i: (i, 0, 0))
    const = lambda shape: pl.BlockSpec((None,) + shape, lambda i: (layer,) + (0,) * len(shape),
                                       pipeline_mode=pl.Buffered(1))
    mod_spec = pl.BlockSpec((None, None, MOD_CHUNKS, D_MODEL),
                            (lambda i: (layer, 1 + i * ts // tiles_per_seq, 0, 0)) if per_batch_mods
                            else (lambda i: (layer, 0, 0, 0)))
    tiled = lambda a: a.reshape(n_tiles, tm, a.shape[-1])
    y = pl.pallas_call(
        functools.partial(_outffn_kernel, final),
        grid=(n_tiles // ts,),
        in_specs=[row(256), row(256), row(256), row(256), row(D_MODEL), mod_spec, const((1, D_MODEL)),
                  const((D_MODEL, D_MODEL)), const((D_MODEL, D_FF)), const((D_MODEL, D_FF)),
                  const((D_FF, D_MODEL)), pl.BlockSpec((1, D_MODEL), lambda i: (0, 0))],
        out_specs=row(D_MODEL),
        out_shape=jax.ShapeDtypeStruct((n_tiles, tm, D_MODEL), F32),
        compiler_params=_cparams(1),
        name="outffn",
    )(*[tiled(a) for a in outs], tiled(x), mods, lw["norm2_g"], lw["w_out"], lw["w_gate"], lw["w_up"],
      lw["w_down"], final_g)
    return y.reshape(b, l, D_MODEL)


def _rope_tables(n):
    t = jnp.arange(n)

    def axis(pos, half):
        inv = ROPE_BASE ** (-jnp.arange(half, dtype=F32) / half)
        ang = pos.astype(F32)[:, None] * inv[None, :]
        c, s = jnp.cos(ang), jnp.sin(ang)
        return jnp.concatenate([c, c], -1), jnp.concatenate([-s, s], -1)

    cr, sr = axis(t // GRID_W, 16)
    cc, sc = axis(t % GRID_W, 16)
    cos64 = jnp.tile(jnp.concatenate([cr, cc], -1), (1, 2))
    sin64 = jnp.tile(jnp.concatenate([sr, sc], -1), (1, 2))
    cr, sr = axis(t // GRID_W, 8)
    cc, sc = axis(t % GRID_W, 8)
    cos32, sin32 = jnp.concatenate([cr, cc], -1), jnp.concatenate([sr, sc], -1)
    one, zero = jnp.ones((n, 1), F32), jnp.zeros((n, 1), F32)
    cosm = jnp.concatenate([jnp.tile(one, (1, 64)), cos32, jnp.tile(one, (1, 32))], -1)
    sinm = jnp.concatenate([jnp.tile(zero, (1, 64)), sin32, jnp.tile(zero, (1, 32))], -1)
    coskr = jnp.concatenate([cos32, jnp.tile(one, (1, 96))], -1)
    sinkr = jnp.concatenate([sin32, jnp.tile(zero, (1, 96))], -1)
    return cos64, sin64, cosm, sinm, coskr, sinkr


_QA_ORDER = ((0, 64), (128, 192), (64, 128), (192, 256))


def _stacked_weights(p):
    w_in = jnp.swapaxes(p["w_in"], 1, 2)
    w_in = jnp.pad(w_in, ((0, 0), (0, IN_PAD_COLS - w_in.shape[1]), (0, 0))).astype(BF16)
    w_out = p["w_out"]
    w_out = jnp.concatenate([w_out[:, a:b] for a, b in _QA_ORDER + ((256, w_out.shape[1]),)], axis=1).astype(BF16)
    wq = p["mla_wq_up"].reshape(DEPTH, MLA_Q_LORA, MLA_HEADS, MLA_NOPE + MLA_ROPE)
    wq = jnp.pad(wq, ((0, 0), (0, 0), (0, 0), (0, LANES - MLA_NOPE - MLA_ROPE))).reshape(DEPTH, MLA_Q_LORA, 4 * LANES)
    wkv = p["mla_wkv_up"].reshape(DEPTH, MLA_KV_LORA, MLA_HEADS, MLA_NOPE + MLA_V)
    wk_top = jnp.pad(wkv[..., :MLA_NOPE], ((0, 0), (0, 0), (0, 0), (0, LANES - MLA_NOPE)))
    place = jnp.pad(jnp.eye(MLA_ROPE, dtype=F32), ((0, LANES - MLA_ROPE), (MLA_NOPE, LANES - MLA_NOPE - MLA_ROPE)))
    wk_bot = jnp.broadcast_to(place[None, :, None, :], (DEPTH, LANES, MLA_HEADS, LANES))
    wk = jnp.concatenate([wk_top, wk_bot], axis=1).reshape(DEPTH, 2 * LANES, 4 * LANES)
    wv = wkv[..., MLA_NOPE:].reshape(DEPTH, MLA_KV_LORA, MLA_HEADS * MLA_V)
    gate_row = lambda v: jnp.pad(v.reshape(DEPTH, 1, DN_GATES), ((0, 0), (0, 0), (0, LANES - DN_GATES)))
    row = lambda v: v[:, None, :]
    return {
        "norm1_g": row(p["norm1_g"]), "norm2_g": row(p["norm2_g"]),
        "w_in": w_in, "w_out": w_out,
        "qn_g": row(jnp.tile(p["gqa_qn_g"], (1, 4))), "kn_g": row(jnp.tile(p["gqa_kn_g"], (1, 2))),
        "mla_qn_g": row(p["mla_qn_g"]), "mla_kvn_g": row(p["mla_kvn_g"]),
        "wq": wq.astype(BF16), "wk": wk.astype(BF16), "wv": wv.astype(BF16),
        "dn_conv_w": p["dn_conv_w"], "dn_alog_row": gate_row(p["dn_a_log"]),
        "dn_dtb_row": gate_row(p["dn_dt_bias"]), "dn_out_g": row(jnp.tile(p["dn_out_g"], (1, DN_HEADS))),
        "w_gate": p["ffn_w_gate"].astype(BF16), "w_up": p["ffn_w_up"].astype(BF16),
        "w_down": p["ffn_w_down"].astype(BF16),
    }


def kernel(x_prompt, x_sample, cache_gqa_k, cache_gqa_v, cache_na_k, cache_na_v, state_dn,
           cache_mla_ckv, cache_mla_krope, c, c_ctx, norm1_g, norm2_g, w_mod, b_mod, w_in, w_out,
           gqa_qn_g, gqa_kn_g, na_bias, dn_conv_w, dn_a_log, dn_dt_bias, dn_out_g, mla_qn_g,
           mla_wq_up, mla_kvn_g, mla_wkv_up, ffn_w_gate, ffn_w_up, ffn_w_down, final_g):
    p = {"norm1_g": norm1_g, "norm2_g": norm2_g, "w_in": w_in, "w_out": w_out, "gqa_qn_g": gqa_qn_g,
         "gqa_kn_g": gqa_kn_g, "dn_conv_w": dn_conv_w, "dn_a_log": dn_a_log, "dn_dt_bias": dn_dt_bias,
         "dn_out_g": dn_out_g, "mla_qn_g": mla_qn_g, "mla_wq_up": mla_wq_up, "mla_kvn_g": mla_kvn_g,
         "mla_wkv_up": mla_wkv_up, "ffn_w_gate": ffn_w_gate, "ffn_w_up": ffn_w_up, "ffn_w_down": ffn_w_down}
    nb_ctx, seq_ctx, _ = x_prompt.shape
    nb_lat, seq_lat, _ = x_sample.shape
    past = cache_gqa_k.shape[2]
    fg = final_g[None]

    cond = jnp.concatenate([c_ctx[None], c, jnp.zeros((16 - 1 - nb_lat, D_MODEL), F32)], axis=0)
    mods = _modulation(cond, w_mod, b_mod).reshape(DEPTH, 16, MOD_CHUNKS, D_MODEL)
    lw = _stacked_weights(p)

    x = x_prompt
    ctx_out = []
    for l in range(DEPTH):
        o_a, ka, va, o_b, kb, vb, zc, o_d, ckv, zm, krr = _inproj(x, mods, False, lw, l, None)
        o_c, s_dn = _deltanet(zc, zm, lw, l)
        x = _outffn((o_a, o_b, o_c, o_d), x, mods, False, lw, l, fg, l == DEPTH - 1)
        ctx_out.append((ka, va, kb, vb, s_dn, ckv, krr[:, :, :MLA_ROPE]))
    y_prompt = x
    new = [jnp.stack([s[i] for s in ctx_out], axis=1) for i in range(7)]
    for i in range(4):
        t = new[i].reshape(nb_ctx, DEPTH, -1, HEAD_DIM, seq_ctx)
        new[i] = jnp.transpose(t, (0, 1, 4, 2, 3))

    ropes = _rope_tables(seq_lat)
    keys_t = lambda c: jnp.transpose(c, (0, 1, 3, 4, 2)).reshape(nb_lat, DEPTH, -1, past)
    ck_a, cv_a, ck_b, cv_b = keys_t(cache_gqa_k), keys_t(cache_gqa_v), keys_t(cache_na_k), keys_t(cache_na_v)
    c_kr = jnp.pad(cache_mla_krope, ((0, 0), (0, 0), (0, 0), (0, LANES - MLA_ROPE)))
    bias_blocks = _na_bias_blocks(na_bias)
    x = x_sample
    for l in range(DEPTH):
        qa, ka, va, qb, kb, vb, zc, qd, ckv, zm, krr = _inproj(x, mods, True, lw, l, ropes)
        o_a = _attn_pair(qa, ka, va, (0, 0), ck_a, cv_a, l, name="gqa_lat")
        o_b = _na_latent(qb, kb, vb, ck_b, cv_b, l, bias_blocks)
        o_c, _ = _deltanet(zc, zm, lw, l, state=state_dn)
        o_d = _mla(qd, ckv, krr, lw, l, cache=(cache_mla_ckv, c_kr), name="mla_lat")
        x = _outffn((o_a, o_b, o_c, o_d), x, mods, True, lw, l, fg, l == DEPTH - 1)
    y_sample = x

    return (y_prompt, y_sample, *new)
```

```python
import functools

import numpy as np
import jax
import jax.numpy as jnp
from jax import lax
from jax.experimental import pallas as pl
from jax.experimental.pallas import tpu as pltpu

F32 = jnp.float32
BF16 = jnp.bfloat16

D_MODEL = 1024
DEPTH = 2
GRID_W = 64
HEAD_DIM = 64
ROPE_BASE = 10000.0
NEG_INF = -1e30
MOD_CHUNKS = 6
GQA_HEADS, GQA_KV_HEADS = 4, 2
NA_HEADS, NA_KH, NA_KW = 4, 8, 16
DN_HEADS, DN_DK, DN_DV, DN_CONV, DN_CHUNK = 4, 64, 64, 4, 64
DN_QKV = DN_HEADS * (2 * DN_DK + DN_DV)
DN_GATES = 2 * DN_HEADS
MLA_HEADS, MLA_Q_LORA, MLA_KV_LORA, MLA_NOPE, MLA_ROPE, MLA_V = 4, 256, 128, 64, 32, 64
MLA_SCALE = (MLA_NOPE + MLA_ROPE) ** -0.5
D_FF = -(-8 * D_MODEL // (3 * 256)) * 256
EPS = 1e-6
LOG2E = 1.4426950408889634

LANES = 128
ROW_TILE = 256
Q_TILE = 256
NA_BAND = 768
TILES_PER_STEP = 2
ATTN_Q_TILE = 512
HEAD_LOOKAHEAD = 1
DN_VMEM_BYTES_PER_ROW = 12288 + 9216
DN_VMEM_BUDGET = 46 * 1024 * 1024
DN_ROWS = 256
PREP_CHUNKS = 4
IN_PAD_COLS = 2816
VMEM_LIMIT = 56 * 1024 * 1024

_NT = (((1,), (1,)), ((), ()))
_TN = (((0,), (0,)), ((), ()))


def _cparams(n_axes):
    return pltpu.CompilerParams(dimension_semantics=("arbitrary",) * n_axes,
                                vmem_limit_bytes=VMEM_LIMIT)


def _lane(shape):
    return lax.broadcasted_iota(jnp.int32, shape, len(shape) - 1)


def _silu(x):
    return x / (1.0 + jnp.exp(-x))


def _rms_full(x, g):
    return x * lax.rsqrt(jnp.mean(x * x, axis=-1, keepdims=True) + EPS) * g


def _seg64_sum(x):
    lo = _lane(x.shape) < HEAD_DIM
    s_lo = jnp.sum(jnp.where(lo, x, 0.0), axis=-1, keepdims=True)
    s_hi = jnp.sum(jnp.where(lo, 0.0, x), axis=-1, keepdims=True)
    return jnp.where(lo, s_lo, s_hi)


def _head_rms(x, g):
    parts = []
    for p in range(x.shape[-1] // LANES):
        xp = x[:, p * LANES:(p + 1) * LANES]
        ms = _seg64_sum(xp * xp) * (1.0 / HEAD_DIM)
        parts.append(xp * lax.rsqrt(ms + EPS))
    y = parts[0] if len(parts) == 1 else jnp.concatenate(parts, axis=-1)
    return y * g


def _head_l2(x):
    parts = []
    for p in range(x.shape[-1] // LANES):
        xp = x[:, p * LANES:(p + 1) * LANES]
        parts.append(xp * lax.rsqrt(_seg64_sum(xp * xp) + EPS))
    return parts[0] if len(parts) == 1 else jnp.concatenate(parts, axis=-1)


def _rope(x, cos, sin, half):
    first = (_lane(x.shape) & (2 * half - 1)) < half
    rot = jnp.where(first, pltpu.roll(x, LANES - half, 1), pltpu.roll(x, half, 1))
    return x * cos + rot * sin


def _softmax_parts(scores):
    m = jnp.max(scores[0], axis=-1, keepdims=True)
    for s in scores[1:]:
        m = jnp.maximum(m, jnp.max(s, axis=-1, keepdims=True))
    es = [jnp.exp2(s - m) for s in scores]
    l = jnp.sum(es[0], axis=-1, keepdims=True)
    for e in es[1:]:
        l = l + jnp.sum(e, axis=-1, keepdims=True)
    return es, 1.0 / l


def _bdot(a, b):
    return jnp.dot(a, b, preferred_element_type=F32)


def _pipelined_heads(n_heads, scores, attend):
    outs = []
    queue = [scores(hd) for hd in range(min(HEAD_LOOKAHEAD, n_heads))]
    for hd in range(n_heads):
        if hd + HEAD_LOOKAHEAD < n_heads:
            queue.append(scores(hd + HEAD_LOOKAHEAD))
        es, rl = _softmax_parts(queue.pop(0))
        outs.append(attend(hd, es) * rl)
    return outs


def _pair_attention(q_blocks, sources, qmap):
    lo = _lane(q_blocks[0].shape) < HEAD_DIM

    def scores(hd):
        p, half = divmod(hd, 2)
        qm = jnp.where(lo if half == 0 else jnp.logical_not(lo), q_blocks[p], 0.0).astype(BF16)
        return [_bdot(qm, k_block(qmap[p])) if transposed
                else lax.dot_general(qm, k_block(qmap[p]), _NT, preferred_element_type=F32)
                for k_block, _, transposed in sources]

    def attend(hd, es):
        kv = qmap[hd // 2]
        parts = [lax.dot_general(e.astype(BF16), v_block(kv), _NT, preferred_element_type=F32) if transposed
                 else _bdot(e.astype(BF16), v_block(kv)) for e, (_, v_block, transposed) in zip(es, sources)]
        return functools.reduce(lambda a, b: a + b, parts)

    outs = _pipelined_heads(2 * len(qmap), scores, attend)
    return [jnp.where(lo, outs[2 * p], outs[2 * p + 1]) for p in range(len(qmap))]


def _mla_attention(q_heads, k_head, v_block):
    lo = _lane(q_heads[0].shape) < MLA_V

    def scores(hd):
        return [lax.dot_general(q_heads[hd], k_head(hd), _NT, preferred_element_type=F32)]

    def attend(hd, es):
        return _bdot(es[0].astype(BF16), v_block(hd // 2))

    outs = _pipelined_heads(MLA_HEADS, scores, attend)
    return [jnp.where(lo, outs[2 * p], outs[2 * p + 1]) for p in range(MLA_HEADS // 2)]


def _mod_kernel(c_ref, w_ref, b_ref, o_ref):
    s = _silu(c_ref[...]).astype(BF16)
    o_ref[...] = _bdot(s, w_ref[...].astype(BF16)) + b_ref[...]


def _modulation(cond, w_mod, b_mod):
    n = MOD_CHUNKS * D_MODEL
    tn = 1536
    return pl.pallas_call(
        _mod_kernel,
        grid=(DEPTH, n // tn),
        in_specs=[pl.BlockSpec((16, D_MODEL), lambda l, j: (0, 0)),
                  pl.BlockSpec((None, D_MODEL, tn), lambda l, j: (l, 0, j)),
                  pl.BlockSpec((None, 1, tn), lambda l, j: (l, 0, j))],
        out_specs=pl.BlockSpec((None, 16, tn), lambda l, j: (l, 0, j)),
        out_shape=jax.ShapeDtypeStruct((DEPTH, 16, n), F32),
        compiler_params=_cparams(2),
        name="modulation",
    )(cond, w_mod, b_mod.reshape(DEPTH, 1, n))


_IN_OUT_WIDTHS = (256, 128, 128, 256, 256, 256, 1024, 512, 128, 128, 128)
_IN_OUT_WIDTHS_CTX = (256, 128, 128, 256, 256, 256, 1024, 256, 128, 128, 128)
_IN_OUT_DTYPES_LAT = (BF16, BF16, BF16, BF16, BF16, BF16, F32, BF16, BF16, F32, BF16)
_IN_OUT_DTYPES_CTX = (BF16, F32, F32, BF16, F32, F32, F32, BF16, F32, F32, F32)


def _inproj_kernel(positioned, *refs):
    (x_ref, mod_ref, g1_ref, w_ref, qng_ref, kng_ref, mqg_ref, wq_ref, mkg_ref) = refs[:9]
    if positioned:
        cos64_ref, sin64_ref, cosm_ref, sinm_ref, coskr_ref, sinkr_ref = refs[9:15]
        n_in = 15
    else:
        wk_ref, wv_ref = refs[9:11]
        n_in = 11
    (qa_ref, ka_ref, va_ref, qb_ref, kb_ref, vb_ref, zc_ref, qd_ref, ckv_ref, zm_ref,
     krr_ref) = refs[n_in:]
    tiles = range(TILES_PER_STEP)
    tm = x_ref.shape[1]
    lane = _lane((tm, LANES))
    lo = lane < HEAD_DIM

    m = mod_ref[...]
    hb = jnp.concatenate([(_rms_full(x_ref[t], g1_ref[...]) * (1.0 + m[1:2]) + m[0:1]).astype(BF16)
                          for t in tiles], axis=0)

    def project(c0, c1):
        z = lax.dot_general(hb, w_ref[c0:c1, :], _NT, preferred_element_type=F32)
        return [z[t * tm:(t + 1) * tm] for t in tiles]

    za = project(0, 512)
    zb = project(512, 1280)

    for t in tiles:
        q = _head_rms(za[t][:, 0:256], qng_ref[...])
        k = _head_rms(za[t][:, 256:384], kng_ref[...])
        q0, q1 = q[:, 0:128], q[:, 128:256]
        q0, q1 = jnp.where(lo, q0, pltpu.roll(q1, HEAD_DIM, 1)), jnp.where(lo, pltpu.roll(q0, HEAD_DIM, 1), q1)
        if positioned:
            cos, sin = cos64_ref[t], sin64_ref[t]
            q0, q1 = _rope(q0, cos, sin, 16), _rope(q1, cos, sin, 16)
            k = _rope(k, cos, sin, 16)
        q0, q1 = q0 * (HEAD_DIM ** -0.5 * LOG2E), q1 * (HEAD_DIM ** -0.5 * LOG2E)
        v = za[t][:, 384:512]
        if not positioned:
            kb16, vb16 = k.astype(BF16), v.astype(BF16)
            q0, q1 = _pair_attention([q0, q1], [(lambda i: kb16, lambda i: vb16, False)], (0, 0))
        qa_ref[t] = jnp.concatenate([q0, q1], axis=-1).astype(qa_ref.dtype)
        ka_ref[t] = k.astype(ka_ref.dtype) if positioned else k.T
        va_ref[t] = v.astype(va_ref.dtype) if positioned else v.T

    zd = project(2304, IN_PAD_COLS)
    zc = project(1280, 2304)

    for t in tiles:
        q = zb[t][:, 0:256] * (HEAD_DIM ** -0.5 * LOG2E)
        k, v = zb[t][:, 256:512], zb[t][:, 512:768]
        if not positioned:
            kb16, vb16 = k.astype(BF16), v.astype(BF16)
            q = jnp.concatenate(_pair_attention(
                [q[:, 0:LANES], q[:, LANES:]],
                [(lambda i, kb16=kb16: kb16[:, i * LANES:(i + 1) * LANES],
                  lambda i, vb16=vb16: vb16[:, i * LANES:(i + 1) * LANES], False)], (0, 1)), axis=-1)
        qb_ref[t] = q.astype(qb_ref.dtype)
        kb_ref[t] = k.astype(kb_ref.dtype) if positioned else k.T
        vb_ref[t] = v.astype(vb_ref.dtype) if positioned else v.T

    shifted = []
    for t in tiles:
        rolled = [pltpu.roll(zd[t][:, j * LANES:(j + 1) * LANES], LANES - 2 * DN_GATES, 1) for j in range(4)]
        keep = lane < LANES - 2 * DN_GATES
        shifted.append([jnp.where(keep, rolled[j], rolled[(j + 1) % 4]) for j in range(4)])

    cq = jnp.concatenate([_rms_full(jnp.concatenate(shifted[t][0:2], axis=-1), mqg_ref[...]).astype(BF16)
                          for t in tiles], axis=0)
    qm = _bdot(cq, wq_ref[...])
    for t in tiles:
        zc_ref[t] = zc[t]
        zm_ref[t] = zd[t][:, 0:LANES]
        ckv = _rms_full(shifted[t][2], mkg_ref[...])
        ckv_ref[t] = ckv.astype(ckv_ref.dtype)
        kr = jnp.where(lane < MLA_ROPE, shifted[t][3], 0.0)
        q = qm[t * tm:(t + 1) * tm]
        if positioned:
            kr = _rope(kr, coskr_ref[t], sinkr_ref[t], 8)
            cm, sm = cosm_ref[t], sinm_ref[t]
            q = jnp.concatenate([_rope(q[:, i * LANES:(i + 1) * LANES], cm, sm, 8)
                                 for i in range(MLA_HEADS)], axis=-1)
        krr_ref[t] = kr.astype(krr_ref.dtype)
        q = q * (MLA_SCALE * LOG2E)
        if not positioned:
            c16 = ckv.astype(BF16)
            k16 = _bdot(jnp.concatenate([c16, kr.astype(BF16)], axis=-1), wk_ref[...]).astype(BF16)
            v16 = _bdot(c16, wv_ref[...]).astype(BF16)
            q = jnp.concatenate(_mla_attention(
                [q[:, i * LANES:(i + 1) * LANES].astype(BF16) for i in range(MLA_HEADS)],
                lambda i, k16=k16: k16[:, i * LANES:(i + 1) * LANES],
                lambda i, v16=v16: v16[:, i * LANES:(i + 1) * LANES]), axis=-1)
        qd_ref[t] = q.astype(qd_ref.dtype)


def _inproj(x, mods, per_batch_mods, lw, layer, ropes):
    b, l, _ = x.shape
    tm, ts = ROW_TILE, TILES_PER_STEP
    n_tiles = b * l // tm
    tiles_per_seq = l // tm
    positioned = ropes is not None
    row = lambda w: pl.BlockSpec((ts, tm, w), lambda i: (i, 0, 0))
    const = lambda shape: pl.BlockSpec((None,) + shape, lambda i: (layer,) + (0,) * len(shape))
    assert not per_batch_mods or tiles_per_seq % ts == 0
    mod_spec = pl.BlockSpec((None, None, MOD_CHUNKS, D_MODEL),
                            (lambda i: (layer, 1 + i * ts // tiles_per_seq, 0, 0)) if per_batch_mods
                            else (lambda i: (layer, 0, 0, 0)))
    in_specs = [row(D_MODEL), mod_spec, const((1, D_MODEL)), const((IN_PAD_COLS, D_MODEL)),
                const((1, 256)), const((1, 128)), const((1, MLA_Q_LORA)),
                const((MLA_Q_LORA, 4 * LANES)), const((1, MLA_KV_LORA))]
    args = [x.reshape(n_tiles, tm, D_MODEL), mods, lw["norm1_g"], lw["w_in"], lw["qn_g"], lw["kn_g"],
            lw["mla_qn_g"], lw["wq"], lw["mla_kvn_g"]]
    if positioned:
        steps_per_seq = tiles_per_seq // ts
        in_specs += [pl.BlockSpec((ts, tm, LANES), lambda i: (i % steps_per_seq, 0, 0))] * 6
        args += [r.reshape(tiles_per_seq, tm, LANES) for r in ropes]
        widths, dtypes = _IN_OUT_WIDTHS, _IN_OUT_DTYPES_LAT
    else:
        assert tiles_per_seq == 1
        in_specs += [const((2 * LANES, 4 * LANES)), const((LANES, 2 * LANES))]
        args += [lw["wk"], lw["wv"]]
        widths, dtypes = _IN_OUT_WIDTHS_CTX, _IN_OUT_DTYPES_CTX
    flipped = () if positioned else (1, 2, 4, 5)
    shapes = [(n_tiles, w, tm) if i in flipped else (n_tiles, tm, w) for i, w in enumerate(widths)]
    outs = pl.pallas_call(
        functools.partial(_inproj_kernel, positioned),
        grid=(n_tiles // ts,),
        in_specs=in_specs,
        out_specs=[pl.BlockSpec((ts,) + shp[1:], lambda i: (i, 0, 0)) for shp in shapes],
        out_shape=[jax.ShapeDtypeStruct(shp, dt) for shp, dt in zip(shapes, dtypes)],
        compiler_params=_cparams(1),
        name="inproj_lat" if positioned else "inproj_attn_ctx",
    )(*args)
    return [o if i in flipped else o.reshape(b, l, o.shape[-1]) for i, o in enumerate(outs)]


def _attn_pair_kernel(qmap, q_ref, kc_ref, vc_ref, k_ref, v_ref, o_ref, kcbuf, vcbuf, kbuf, vbuf):
    @pl.when(pl.program_id(1) == 0)
    def _():
        kcbuf[...] = kc_ref[...].astype(BF16)
        vcbuf[...] = vc_ref[...].astype(BF16)
        kbuf[...] = k_ref[...].astype(BF16)
        vbuf[...] = v_ref[...].astype(BF16)

    rows = lambda buf: (lambda i: buf[i * LANES:(i + 1) * LANES, :])
    cols = lambda buf: (lambda i: buf[:, i * LANES:(i + 1) * LANES])
    q_blocks = [q_ref[:, p * LANES:(p + 1) * LANES].astype(F32) for p in range(len(qmap))]
    outs = _pair_attention(q_blocks, [(rows(kcbuf), rows(vcbuf), True), (cols(kbuf), cols(vbuf), False)], qmap)
    for p, o in enumerate(outs):
        o_ref[:, p * LANES:(p + 1) * LANES] = o.astype(o_ref.dtype)


def _attn_pair(q, k, v, qmap, kc_t, vc_t, layer, name):
    b, lq, wq = q.shape
    ls, wk = k.shape[1], k.shape[2]
    lc = kc_t.shape[3]
    tq = min(ATTN_Q_TILE, lq)
    cspec = pl.BlockSpec((None, None, wk, lc), lambda bi, i: (bi, layer, 0, 0))
    sspec = pl.BlockSpec((None, ls, wk), lambda bi, i: (bi, 0, 0))
    return pl.pallas_call(
        functools.partial(_attn_pair_kernel, qmap),
        grid=(b, lq // tq),
        in_specs=[pl.BlockSpec((None, tq, wq), lambda bi, i: (bi, i, 0)), cspec, cspec, sspec, sspec],
        out_specs=pl.BlockSpec((None, tq, wq), lambda bi, i: (bi, i, 0)),
        out_shape=jax.ShapeDtypeStruct((b, lq, wq), BF16),
        scratch_shapes=[pltpu.VMEM((wk, lc), BF16), pltpu.VMEM((wk, lc), BF16),
                        pltpu.VMEM((ls, wk), BF16), pltpu.VMEM((ls, wk), BF16)],
        compiler_params=_cparams(2),
        name=name,
    )(q, kc_t, vc_t, k, v)


def _mla_kernel(has_cache, *refs):
    if has_cache:
        q_ref, ckvc_ref, krc_ref, ckv_ref, kr_ref, wk_ref, wv_ref, o_ref, kbuf, vbuf = refs
    else:
        q_ref, ckv_ref, kr_ref, wk_ref, wv_ref, o_ref, kbuf, vbuf = refs

    @pl.when(pl.program_id(1) == 0)
    def _():
        def expand(c_ref, r_ref, r0, r1):
            c = c_ref[...].astype(BF16)
            ckr = jnp.concatenate([c, r_ref[...].astype(BF16)], axis=-1)
            kbuf[r0:r1, :] = _bdot(ckr, wk_ref[...]).astype(BF16)
            vbuf[r0:r1, :] = _bdot(c, wv_ref[...]).astype(BF16)
        off = 0
        if has_cache:
            off = ckvc_ref.shape[0]
            expand(ckvc_ref, krc_ref, 0, off)
        expand(ckv_ref, kr_ref, off, kbuf.shape[0])

    outs = _mla_attention([q_ref[:, hd * LANES:(hd + 1) * LANES].astype(BF16) for hd in range(MLA_HEADS)],
                          lambda i: kbuf[:, i * LANES:(i + 1) * LANES],
                          lambda i: vbuf[:, i * LANES:(i + 1) * LANES])
    for p, o in enumerate(outs):
        o_ref[:, p * LANES:(p + 1) * LANES] = o.astype(o_ref.dtype)


def _mla(q, ckv, kr, lw, layer, cache=None, name="mla"):
    b, lq, wq = q.shape
    ls = ckv.shape[1]
    tq = min(ATTN_Q_TILE, lq)
    lc = 0 if cache is None else cache[0].shape[2]
    in_specs = [pl.BlockSpec((None, tq, wq), lambda bi, i: (bi, i, 0))]
    args = [q]
    if cache is not None:
        ckvc, krc = cache
        cspec = pl.BlockSpec((None, None, lc, LANES), lambda bi, i: (bi, layer, 0, 0))
        in_specs += [cspec, cspec]
        args += [ckvc, krc]
    sspec = pl.BlockSpec((None, ls, LANES), lambda bi, i: (bi, 0, 0))
    in_specs += [sspec, sspec,
                 pl.BlockSpec((None, 2 * LANES, 4 * LANES), lambda bi, i: (layer, 0, 0)),
                 pl.BlockSpec((None, LANES, 2 * LANES), lambda bi, i: (layer, 0, 0))]
    args += [ckv, kr, lw["wk"], lw["wv"]]
    return pl.pallas_call(
        functools.partial(_mla_kernel, cache is not None),
        grid=(b, lq // tq),
        in_specs=in_specs,
        out_specs=pl.BlockSpec((None, tq, 2 * LANES), lambda bi, i: (bi, i, 0)),
        out_shape=jax.ShapeDtypeStruct((b, lq, 2 * LANES), BF16),
        scratch_shapes=[pltpu.VMEM((lc + ls, 4 * LANES), BF16), pltpu.VMEM((lc + ls, 2 * LANES), BF16)],
        compiler_params=_cparams(2),
        name=name,
    )(*args)


def _na_kernel(q_ref, k_ref, v_ref, kc_ref, vc_ref, bias_ref, o_ref):
    j = pl.program_id(0)
    start = pl.multiple_of((j >> 1) * 256, 256)
    kband = k_ref[pl.ds(start, NA_BAND), :].astype(BF16)
    vband = v_ref[pl.ds(start, NA_BAND), :].astype(BF16)
    kc = kc_ref[...].astype(BF16)
    vc = vc_ref[...].astype(BF16)
    tq = q_ref.shape[0]
    lo = _lane((tq, LANES)) < HEAD_DIM

    def scores(hd):
        p, half = divmod(hd, 2)
        sl = slice(p * LANES, (p + 1) * LANES)
        qm = jnp.where(lo if half == 0 else jnp.logical_not(lo), q_ref[:, sl].astype(F32), 0.0).astype(BF16)
        s_loc = lax.dot_general(qm, kband[:, sl], _NT, preferred_element_type=F32) + bias_ref[hd]
        return [s_loc, _bdot(qm, kc[sl, :])]

    def attend(hd, es):
        sl = slice((hd // 2) * LANES, (hd // 2 + 1) * LANES)
        return (_bdot(es[0].astype(BF16), vband[:, sl])
                + lax.dot_general(es[1].astype(BF16), vc[sl, :], _NT, preferred_element_type=F32))

    outs = _pipelined_heads(NA_HEADS, scores, attend)
    for p in range(NA_HEADS // 2):
        o_ref[:, p * LANES:(p + 1) * LANES] = jnp.where(lo, outs[2 * p], outs[2 * p + 1]).astype(o_ref.dtype)


def _na_latent(q, k, v, kc_t, vc_t, layer, bias_blocks):
    b, n, w = q.shape
    lc = kc_t.shape[3]
    nq = n // Q_TILE
    full = pl.BlockSpec((None, n, w), lambda j, bi: (bi, 0, 0))
    cspec = pl.BlockSpec((None, None, w, lc), lambda j, bi: (bi, layer, 0, 0))
    return pl.pallas_call(
        _na_kernel,
        grid=(nq, b),
        in_specs=[pl.BlockSpec((None, Q_TILE, w), lambda j, bi: (bi, j, 0)), full, full, cspec, cspec,
                  pl.BlockSpec((None, None, NA_HEADS, Q_TILE, NA_BAND), lambda j, bi: (layer, j, 0, 0, 0))],
        out_specs=pl.BlockSpec((None, Q_TILE, w), lambda j, bi: (bi, j, 0)),
        out_shape=jax.ShapeDtypeStruct((b, n, w), BF16),
        compiler_params=_cparams(2),
        name="na_latent",
    )(q, k, v, kc_t, vc_t, bias_blocks)


NA_GRID_ROWS = 16
NA_BAND_ROW0 = (0, 0, 4, 4)


def _na_bias_kernel(b_ref, o_ref, tp_s):
    hd = pl.program_id(0)
    n_dr, n_dc = 2 * NA_KH - 1, 2 * NA_KW - 1
    shape = (GRID_W, LANES)
    c = lax.broadcasted_iota(jnp.int32, shape, 0)
    lane = _lane(shape)
    kc = lane & (GRID_W - 1)
    lo = lane < GRID_W
    diff = kc - c + (NA_KW - 1)
    c0 = jnp.clip(c - NA_KW // 2, 0, GRID_W - NA_KW)
    col_ok = (kc >= c0) & (kc < c0 + NA_KW)
    neg = jnp.full(shape, NEG_INF, F32)
    for dr0 in range(-1, n_dr):
        acc = neg
        for d in range(n_dc):
            v_lo = b_ref[hd * n_dr + dr0, d] if dr0 >= 0 else 0.0
            v_hi = b_ref[hd * n_dr + dr0 + 1, d] if dr0 + 1 < n_dr else 0.0
            acc = jnp.where(diff == d, jnp.where(lo, v_lo, v_hi), acc)
        tp_s[dr0 + 1] = jnp.where(col_ok, acc * LOG2E, NEG_INF)
    for j in range(NA_GRID_ROWS // 4):
        for ri in range(4):
            r = 4 * j + ri
            r0 = min(max(r - NA_KH // 2, 0), NA_GRID_ROWS - NA_KH)
            for kp in range(NA_BAND // LANES):
                kr = NA_BAND_ROW0[j] + 2 * kp
                ok_lo, ok_hi = r0 <= kr < r0 + NA_KH, r0 <= kr + 1 < r0 + NA_KH
                dr0 = kr - r + (NA_KH - 1)
                if ok_lo and ok_hi:
                    t = tp_s[dr0 + 1]
                elif ok_lo:
                    t = jnp.where(lo, tp_s[dr0 + 1], NEG_INF)
                elif ok_hi:
                    t = jnp.where(lo, NEG_INF, tp_s[dr0 + 1])
                else:
                    t = neg
                o_ref[j, ri * GRID_W:(ri + 1) * GRID_W, kp * LANES:(kp + 1) * LANES] = t


def _na_bias_blocks(bias):
    nq = NA_GRID_ROWS // 4
    return pl.pallas_call(
        _na_bias_kernel,
        grid=(DEPTH * NA_HEADS,),
        in_specs=[pl.BlockSpec(memory_space=pltpu.SMEM)],
        out_specs=pl.BlockSpec((None, nq, None, Q_TILE, NA_BAND),
                               lambda i: (i // NA_HEADS, 0, i % NA_HEADS, 0, 0)),
        out_shape=jax.ShapeDtypeStruct((DEPTH, nq, NA_HEADS, Q_TILE, NA_BAND), F32),
        scratch_shapes=[pltpu.VMEM((2 * NA_KH, GRID_W, LANES), F32)],
        compiler_params=_cparams(1),
        name="na_bias",
    )(bias.reshape(DEPTH * NA_HEADS * (2 * NA_KH - 1), 2 * NA_KW - 1))


def _widen(cols, n):
    blk = _lane((n, DN_HEADS * DN_DV)) >> 6
    return jnp.where(blk == 0, cols[0], jnp.where(blk == 1, cols[1], jnp.where(blk == 2, cols[2], cols[3])))


def _deltanet_kernel(seq, has_state, *refs):
    if has_state:
        zc_all, zm_all, s0_ref, cw_ref, alog_ref, dtb_ref, og_ref, o_ref = refs[:8]
    else:
        zc_all, zm_all, cw_ref, alog_ref, dtb_ref, og_ref, o_ref, sfin_ref = refs[:8]
    scratch = refs[8:]
    n_seqs = zc_all.shape[0]
    g_all, o_all, c_all, mp_all = scratch[4:8]
    n_chunks = seq // DN_CHUNK
    wide = DN_HEADS * DN_DV

    a_off, b_off = 0, DN_GATES
    ri = lax.broadcasted_iota(jnp.int32, (DN_ROWS, DN_ROWS), 0)
    ci = lax.broadcasted_iota(jnp.int32, (DN_ROWS, DN_ROWS), 1)
    same = (ri >> 6) == (ci >> 6)
    tri = [jnp.tile(jnp.where(same & ((ci <= ri) if d == 0 else (ci >= ri)), 1.0, 0.0).astype(BF16), (1, 3))
           for d in range(2)]
    n_blocks = seq // DN_ROWS

    def terms3(x):
        hi = x.astype(BF16)
        rest = x - hi.astype(F32)
        mid = rest.astype(BF16)
        return hi, mid, (rest - mid.astype(F32)).astype(BF16)

    def split3(x):
        return jnp.concatenate(terms3(x), axis=0)

    def preprocess(views, rb, carry):
        zc_ref, zm_ref, q_s, k_s, v_s, b_s, g_s = views[:7]
        r0 = pl.multiple_of(rb * DN_ROWS, DN_ROWS)
        rows = pl.ds(r0, DN_ROWS)
        before = pl.ds(pl.multiple_of(jnp.maximum(r0 - 8, 0), 8), 8)
        after = pl.ds(pl.multiple_of(jnp.minimum(r0 + DN_ROWS, seq - 8), 8), 8)
        for part, dst in enumerate((q_s, k_s, v_s)):
            cs = slice(part * wide, (part + 1) * wide)
            head = jnp.where(rb > 0, zc_ref[before, cs], 0.0)
            tail = jnp.where(rb < n_blocks - 1, zc_ref[after, cs], 0.0)
            xe = jnp.concatenate([head, zc_ref[rows, cs], tail], axis=0)
            w = cw_ref[:, cs]
            y = (w[0:1] * xe[7:7 + DN_ROWS] + w[1:2] * xe[8:8 + DN_ROWS]
                 + w[2:3] * xe[9:9 + DN_ROWS] + w[3:4] * xe[10:10 + DN_ROWS])
            y = _silu(y)
            if part == 0:
                y = _head_l2(y) * (DN_DK ** -0.5)
            elif part == 1:
                y = _head_l2(y)
            dst[rows, :] = y
        zm = zm_ref[rows, :]
        xa = zm + dtb_ref[...]
        logd = -jnp.exp(alog_ref[...]) * (jnp.maximum(xa, 0.0) + jnp.log1p(jnp.exp(-jnp.abs(xa))))
        beta = 1.0 / (1.0 + jnp.exp(-zm))
        logd3 = split3(logd)
        for d in range(2):
            b_s[d, rows, :] = _widen([beta[:, b_off + 4 * d + hd:b_off + 4 * d + hd + 1]
                                      for hd in range(DN_HEADS)], DN_ROWS)
            g = _bdot(tri[d], logd3)
            g_s[d, rows, :] = _widen([g[:, a_off + 4 * d + hd:a_off + 4 * d + hd + 1]
                                      for hd in range(DN_HEADS)], DN_ROWS)
        return carry

    ii = lax.broadcasted_iota(jnp.int32, (DN_CHUNK, wide), 0)
    jj = _lane((DN_CHUNK, wide)) & (DN_CHUNK - 1)
    blk = _lane((DN_CHUNK, wide)) >> 6
    diag = ii == jj
    eye = jnp.where(diag, 1.0, 0.0)
    head_mask = [jnp.where(blk == hd, 1.0, 0.0).astype(BF16) for hd in range(DN_HEADS)]

    def bd(z):
        zb = z.astype(BF16)
        return jnp.concatenate([zb * hm for hm in head_mask], axis=0)

    def fold(gram):
        out = jnp.where(blk == 0, gram[0:DN_CHUNK], 0.0)
        for hd in range(1, DN_HEADS):
            out = out + jnp.where(blk == hd, gram[hd * DN_CHUNK:(hd + 1) * DN_CHUNK], 0.0)
        return out

    tri_masks = []
    for d in range(2):
        incl = (jj <= ii) if d == 0 else (jj >= ii)
        strict = (jj < ii) if d == 0 else (jj > ii)
        pair = [((ii >> (lvl + 1)) == (jj >> (lvl + 1)))
                & (((ii >> lvl) & 1) == (1 - d)) & (((jj >> lvl) & 1) == d) for lvl in range(6)]
        tri_masks.append((incl, strict, pair))

    def prepare(views, step, carry):
        q_s, k_s, v_s, b_s, g_s, o_s, c_s, mp_s = views[2:]
        chunks = [step * PREP_CHUNKS + i for i in range(PREP_CHUNKS)]
        rows = [pl.ds(pl.multiple_of(c * DN_CHUNK, DN_CHUNK), DN_CHUNK) for c in chunks]
        qkv = [(q_s[r, :], k_s[r, :], v_s[r, :]) for r in rows]
        inst = [(ci, d) for ci in range(PREP_CHUNKS) for d in range(2)]
        beta = {(ci, d): b_s[d, rows[ci], :] for ci, d in inst}
        kb = {(ci, d): qkv[ci][1] * beta[ci, d] for ci, d in inst}
        r = [lax.dot_general(jnp.concatenate([kb[ci, 0], kb[ci, 1], qkv[ci][0]], axis=0).astype(BF16),
                             bd(qkv[ci][1]), _NT, preferred_element_type=F32)
             for ci in range(PREP_CHUNKS)]
        g, a, qk, eg, t = {}, {}, {}, {}, {}
        for ci, d in inst:
            incl, strict, pair = tri_masks[d]
            g[ci, d] = g_s[d, rows[ci], :]
            g_row = jnp.sum(jnp.where(diag, g[ci, d], 0.0), axis=0, keepdims=True)
            dm = jnp.where(incl, jnp.exp(jnp.where(incl, g[ci, d] - g_row, 0.0)), 0.0)
            a[ci, d] = jnp.where(strict, r[ci][d * DN_CHUNK:(d + 1) * DN_CHUNK] * dm, 0.0)
            qk[ci, d] = r[ci][2 * DN_CHUNK:] * dm
            eg[ci, d] = jnp.exp(g[ci, d])
            t[ci, d] = eye - jnp.where(pair[0], a[ci, d], 0.0)
        for lvl in range(1, 6):
            te = {i: _bdot(t[i].astype(BF16), bd(jnp.where(tri_masks[i[1]][2][lvl], a[i], 0.0))) for i in inst}
            t = {i: t[i] - _bdot(te[i].astype(BF16), bd(t[i])) for i in inst}
        nb = {i: jnp.where(diag, 0.0, t[i]).astype(BF16) for i in inst}
        rhs_u = {(ci, d): qkv[ci][2] * beta[ci, d] for ci, d in inst}
        rhs_w = {i: kb[i] * eg[i] for i in inst}
        q2 = {i: qk[i] + _bdot(qk[i].astype(BF16), bd(nb[i])) for i in inst}
        both = {i: jnp.concatenate([nb[i], q2[i].astype(BF16)], axis=0) for i in inst}
        ru = {i: _bdot(both[i], bd(rhs_u[i])) for i in inst}
        rw = {i: _bdot(both[i], bd(rhs_w[i])) for i in inst}
        u = {i: rhs_u[i] + ru[i][0:DN_CHUNK] for i in inst}
        w = {i: rhs_w[i] + rw[i][0:DN_CHUNK] for i in inst}
        kd = {}
        for ci, d in inst:
            g_last = g[ci, d][DN_CHUNK - 1:DN_CHUNK] if d == 0 else g[ci, d][0:1]
            kd[ci, d] = (qkv[ci][1] * jnp.exp(g_last - g[ci, d])).astype(BF16)
        p = {(ci, d): qkv[ci][0] * eg[ci, d] - rw[ci, d][DN_CHUNK:] for ci, d in inst}
        o0 = {i: ru[i][DN_CHUNK:] for i in inst}
        m = {i: fold(lax.dot_general(kd[i], w[i].astype(BF16), _TN, preferred_element_type=F32)) for i in inst}
        cc = {i: fold(lax.dot_general(kd[i], u[i].astype(BF16), _TN, preferred_element_type=F32)) for i in inst}
        for ci, d in inst:
            mrow = pl.ds(pl.multiple_of(chunks[ci] * (2 * DN_CHUNK), 2 * DN_CHUNK), 2 * DN_CHUNK)
            c_s[d, rows[ci], :] = cc[ci, d]
            mp_s[d, mrow, :] = jnp.concatenate([m[ci, d], p[ci, d]], axis=0).astype(BF16)
        for ci in range(PREP_CHUNKS):
            o_s[rows[ci], :] = o0[ci, 0] + o0[ci, 1]
        return carry

    for sq in range(n_seqs):
        views = (zc_all.at[sq], zm_all.at[sq]) + tuple(ref.at[sq] for ref in scratch)
        lax.fori_loop(0, n_blocks, functools.partial(preprocess, views), 0)
        lax.fori_loop(0, n_chunks // PREP_CHUNKS, functools.partial(prepare, views), 0)

    chains = [(sq, d) for sq in range(n_seqs) for d in range(2)]

    def scan(i, states):
        new = []
        for (sq, d), state in zip(chains, states):
            c = i if d == 0 else n_chunks - 1 - i
            r0 = pl.multiple_of(c * DN_CHUNK, DN_CHUNK)
            rows = pl.ds(r0, DN_CHUNK)
            mrow = pl.ds(pl.multiple_of(c * (2 * DN_CHUNK), 2 * DN_CHUNK), 2 * DN_CHUNK)
            edge = pl.ds(pl.multiple_of(r0 + (DN_CHUNK - 8 if d == 0 else 0), 8), 8)
            g_edge = g_all[sq, d, edge, :]
            g_last = g_edge[7:8] if d == 0 else g_edge[0:1]
            res = _bdot(mp_all[sq, d, mrow, :], bd(state))
            o_all[sq, rows, :] = o_all[sq, rows, :] + res[DN_CHUNK:]
            new.append(state * jnp.exp(g_last) - res[0:DN_CHUNK] + c_all[sq, d, rows, :])
        return tuple(new)

    place = [jnp.where(diag & (blk == hd), 1.0, 0.0).astype(BF16) for hd in range(DN_HEADS)]

    def to_wide(heads):
        parts = [_bdot(term, place[hd]) for hd, x in enumerate(heads) for term in terms3(x)]
        return functools.reduce(lambda a, b: a + b, parts)

    def head_of(s, hd):
        parts = [lax.dot_general(term, place[hd], _NT, preferred_element_type=F32) for term in terms3(s)]
        return functools.reduce(lambda a, b: a + b, parts)

    if has_state:
        init = tuple(to_wide([s0_ref[sq, d, hd] for hd in range(DN_HEADS)]) for sq, d in chains)
    else:
        init = tuple(jnp.zeros((DN_DK, wide), F32) for _ in chains)
    fin = lax.fori_loop(0, n_chunks, scan, init)
    if not has_state:
        for (sq, d), state in zip(chains, fin):
            for hd in range(DN_HEADS):
                sfin_ref[sq, d, hd] = head_of(state, hd)

    for sq in range(n_seqs):
        gate = _silu(zc_all[sq, :, 3 * wide:4 * wide])
        o_ref[sq] = (_head_rms(o_all[sq], og_ref[...]) * gate).astype(o_ref.dtype)


def _deltanet(zc, zm, lw, layer, state=None):
    b, seq, _ = zc.shape
    wide = DN_HEADS * DN_DV
    has_state = state is not None
    ns = max(n for n in (1, 2, 4) if b % n == 0 and n * seq * DN_VMEM_BYTES_PER_ROW <= DN_VMEM_BUDGET)
    per_b = lambda w: pl.BlockSpec((ns, seq, w), lambda bi: (bi, 0, 0))
    const = lambda shape: pl.BlockSpec((None,) + shape, lambda bi: (layer,) + (0,) * len(shape))
    st_spec = pl.BlockSpec((ns, 2, DN_HEADS, DN_DK, DN_DV), lambda bi: (bi, 0, 0, 0, 0))
    in_specs = [per_b(4 * wide), per_b(LANES)]
    args = [zc, zm]
    if has_state:
        in_specs.append(pl.BlockSpec((ns, None, 2, DN_HEADS, DN_DK, DN_DV),
                                     lambda bi: (bi, layer, 0, 0, 0, 0)))
        args.append(state)
    in_specs += [const((DN_CONV, DN_QKV)), const((1, LANES)), const((1, LANES)), const((1, wide))]
    args += [lw["dn_conv_w"], lw["dn_alog_row"], lw["dn_dtb_row"], lw["dn_out_g"]]
    out_specs = [per_b(wide)]
    out_shape = [jax.ShapeDtypeStruct((b, seq, wide), BF16)]
    if not has_state:
        out_specs.append(st_spec)
        out_shape.append(jax.ShapeDtypeStruct((b, 2, DN_HEADS, DN_DK, DN_DV), F32))
    res = pl.pallas_call(
        functools.partial(_deltanet_kernel, seq, has_state),
        grid=(b // ns,),
        in_specs=in_specs,
        out_specs=out_specs,
        out_shape=out_shape,
        scratch_shapes=[pltpu.VMEM((ns, seq, wide), F32), pltpu.VMEM((ns, seq, wide), F32),
                        pltpu.VMEM((ns, seq, wide), F32), pltpu.VMEM((ns, 2, seq, wide), F32),
                        pltpu.VMEM((ns, 2, seq, wide), F32), pltpu.VMEM((ns, seq, wide), F32),
                        pltpu.VMEM((ns, 2, seq, wide), F32), pltpu.VMEM((ns, 2, 2 * seq, wide), BF16)],
        compiler_params=_cparams(1),
        name="deltanet_lat" if has_state else "deltanet_ctx",
    )(*args)
    return (res[0], None) if has_state else (res[0], res[1])


def _outffn_kernel(final, oa_ref, ob_ref, oc_ref, od_ref, x_ref, mod_ref, g2_ref, wo_ref, wg_ref,
                   wu_ref, wd_ref, fg_ref, y_ref):
    tiles = range(TILES_PER_STEP)
    m = mod_ref[...]
    o = [jnp.concatenate([oa_ref[t], ob_ref[t], oc_ref[t], od_ref[t]], axis=-1).astype(BF16) for t in tiles]
    x1 = [x_ref[t] + m[2:3] * _bdot(o[t], wo_ref[...]) for t in tiles]
    h = [(_rms_full(x1[t], g2_ref[...]) * (1.0 + m[4:5]) + m[3:4]).astype(BF16) for t in tiles]
    gate = [_bdot(h[t], wg_ref[...]) for t in tiles]
    up = [_bdot(h[t], wu_ref[...]) for t in tiles]
    act = [(_silu(gate[t]) * up[t]).astype(BF16) for t in tiles]
    x2 = [x1[t] + m[5:6] * _bdot(act[t], wd_ref[...]) for t in tiles]
    for t in tiles:
        y_ref[t] = _rms_full(x2[t], fg_ref[...]) if final else x2[t]


def _outffn(outs, x, mods, per_batch_mods, lw, layer, final_g, final):
    b, l, _ = x.shape
    tm, ts = ROW_TILE, TILES_PER_STEP
    n_tiles = b * l // tm
    tiles_per_seq = l // tm
    assert not per_batch_mods or tiles_per_seq % ts == 0
    row = lambda w: pl.BlockSpec((ts, tm, w), lambda i: (i, 0, 0))
    const = lambda shape: pl.BlockSpec((None,) + shape, lambda i: (layer,) + (0,) * len(shape),
                                       pipeline_mode=pl.Buffered(1))
    mod_spec = pl.BlockSpec((None, None, MOD_CHUNKS, D_MODEL),
                            (lambda i: (layer, 1 + i * ts // tiles_per_seq, 0, 0)) if per_batch_mods
                            else (lambda i: (layer, 0, 0, 0)))
    tiled = lambda a: a.reshape(n_tiles, tm, a.shape[-1])
    y = pl.pallas_call(
        functools.partial(_outffn_kernel, final),
        grid=(n_tiles // ts,),
        in_specs=[row(256), row(256), row(256), row(256), row(D_MODEL), mod_spec, const((1, D_MODEL)),
                  const((D_MODEL, D_MODEL)), const((D_MODEL, D_FF)), const((D_MODEL, D_FF)),
                  const((D_FF, D_MODEL)), pl.BlockSpec((1, D_MODEL), lambda i: (0, 0))],
        out_specs=row(D_MODEL),
        out_shape=jax.ShapeDtypeStruct((n_tiles, tm, D_MODEL), F32),
        compiler_params=_cparams(1),
        name="outffn",
    )(*[tiled(a) for a in outs], tiled(x), mods, lw["norm2_g"], lw["w_out"], lw["w_gate"], lw["w_up"],
      lw["w_down"], final_g)
    return y.reshape(b, l, D_MODEL)


def _rope_tables(n):
    t = jnp.arange(n)

    def axis(pos, half):
        inv = ROPE_BASE ** (-jnp.arange(half, dtype=F32) / half)
        ang = pos.astype(F32)[:, None] * inv[None, :]
        c, s = jnp.cos(ang), jnp.sin(ang)
        return jnp.concatenate([c, c], -1), jnp.concatenate([-s, s], -1)

    cr, sr = axis(t // GRID_W, 16)
    cc, sc = axis(t % GRID_W, 16)
    cos64 = jnp.tile(jnp.concatenate([cr, cc], -1), (1, 2))
    sin64 = jnp.tile(jnp.concatenate([sr, sc], -1), (1, 2))
    cr, sr = axis(t // GRID_W, 8)
    cc, sc = axis(t % GRID_W, 8)
    cos32, sin32 = jnp.concatenate([cr, cc], -1), jnp.concatenate([sr, sc], -1)
    one, zero = jnp.ones((n, 1), F32), jnp.zeros((n, 1), F32)
    cosm = jnp.concatenate([jnp.tile(one, (1, 64)), cos32, jnp.tile(one, (1, 32))], -1)
    sinm = jnp.concatenate([jnp.tile(zero, (1, 64)), sin32, jnp.tile(zero, (1, 32))], -1)
    coskr = jnp.concatenate([cos32, jnp.tile(one, (1, 96))], -1)
    sinkr = jnp.concatenate([sin32, jnp.tile(zero, (1, 96))], -1)
    return cos64, sin64, cosm, sinm, coskr, sinkr


_QA_ORDER = ((0, 64), (128, 192), (64, 128), (192, 256))


def _stacked_weights(p):
    w_in = jnp.swapaxes(p["w_in"], 1, 2)
    w_in = jnp.pad(w_in, ((0, 0), (0, IN_PAD_COLS - w_in.shape[1]), (0, 0))).astype(BF16)
    w_out = p["w_out"]
    w_out = jnp.concatenate([w_out[:, a:b] for a, b in _QA_ORDER + ((256, w_out.shape[1]),)], axis=1).astype(BF16)
    wq = p["mla_wq_up"].reshape(DEPTH, MLA_Q_LORA, MLA_HEADS, MLA_NOPE + MLA_ROPE)
    wq = jnp.pad(wq, ((0, 0), (0, 0), (0, 0), (0, LANES - MLA_NOPE - MLA_ROPE))).reshape(DEPTH, MLA_Q_LORA, 4 * LANES)
    wkv = p["mla_wkv_up"].reshape(DEPTH, MLA_KV_LORA, MLA_HEADS, MLA_NOPE + MLA_V)
    wk_top = jnp.pad(wkv[..., :MLA_NOPE], ((0, 0), (0, 0), (0, 0), (0, LANES - MLA_NOPE)))
    place = jnp.pad(jnp.eye(MLA_ROPE, dtype=F32), ((0, LANES - MLA_ROPE), (MLA_NOPE, LANES - MLA_NOPE - MLA_ROPE)))
    wk_bot = jnp.broadcast_to(place[None, :, None, :], (DEPTH, LANES, MLA_HEADS, LANES))
    wk = jnp.concatenate([wk_top, wk_bot], axis=1).reshape(DEPTH, 2 * LANES, 4 * LANES)
    wv = wkv[..., MLA_NOPE:].reshape(DEPTH, MLA_KV_LORA, MLA_HEADS * MLA_V)
    gate_row = lambda v: jnp.pad(v.reshape(DEPTH, 1, DN_GATES), ((0, 0), (0, 0), (0, LANES - DN_GATES)))
    row = lambda v: v[:, None, :]
    return {
        "norm1_g": row(p["norm1_g"]), "norm2_g": row(p["norm2_g"]),
        "w_in": w_in, "w_out": w_out,
        "qn_g": row(jnp.tile(p["gqa_qn_g"], (1, 4))), "kn_g": row(jnp.tile(p["gqa_kn_g"], (1, 2))),
        "mla_qn_g": row(p["mla_qn_g"]), "mla_kvn_g": row(p["mla_kvn_g"]),
        "wq": wq.astype(BF16), "wk": wk.astype(BF16), "wv": wv.astype(BF16),
        "dn_conv_w": p["dn_conv_w"], "dn_alog_row": gate_row(p["dn_a_log"]),
        "dn_dtb_row": gate_row(p["dn_dt_bias"]), "dn_out_g": row(jnp.tile(p["dn_out_g"], (1, DN_HEADS))),
        "w_gate": p["ffn_w_gate"].astype(BF16), "w_up": p["ffn_w_up"].astype(BF16),
        "w_down": p["ffn_w_down"].astype(BF16),
    }


def kernel(x_prompt, x_sample, cache_gqa_k, cache_gqa_v, cache_na_k, cache_na_v, state_dn,
           cache_mla_ckv, cache_mla_krope, c, c_ctx, norm1_g, norm2_g, w_mod, b_mod, w_in, w_out,
           gqa_qn_g, gqa_kn_g, na_bias, dn_conv_w, dn_a_log, dn_dt_bias, dn_out_g, mla_qn_g,
           mla_wq_up, mla_kvn_g, mla_wkv_up, ffn_w_gate, ffn_w_up, ffn_w_down, final_g):
    p = {"norm1_g": norm1_g, "norm2_g": norm2_g, "w_in": w_in, "w_out": w_out, "gqa_qn_g": gqa_qn_g,
         "gqa_kn_g": gqa_kn_g, "dn_conv_w": dn_conv_w, "dn_a_log": dn_a_log, "dn_dt_bias": dn_dt_bias,
         "dn_out_g": dn_out_g, "mla_qn_g": mla_qn_g, "mla_wq_up": mla_wq_up, "mla_kvn_g": mla_kvn_g,
         "mla_wkv_up": mla_wkv_up, "ffn_w_gate": ffn_w_gate, "ffn_w_up": ffn_w_up, "ffn_w_down": ffn_w_down}
    nb_ctx, seq_ctx, _ = x_prompt.shape
    nb_lat, seq_lat, _ = x_sample.shape
    past = cache_gqa_k.shape[2]
    fg = final_g[None]

    cond = jnp.concatenate([c_ctx[None], c, jnp.zeros((16 - 1 - nb_lat, D_MODEL), F32)], axis=0)
    mods = _modulation(cond, w_mod, b_mod).reshape(DEPTH, 16, MOD_CHUNKS, D_MODEL)
    lw = _stacked_weights(p)

    x = x_prompt
    ctx_out = []
    for l in range(DEPTH):
        o_a, ka, va, o_b, kb, vb, zc, o_d, ckv, zm, krr = _inproj(x, mods, False, lw, l, None)
        o_c, s_dn = _deltanet(zc, zm, lw, l)
        x = _outffn((o_a, o_b, o_c, o_d), x, mods, False, lw, l, fg, l == DEPTH - 1)
        ctx_out.append((ka, va, kb, vb, s_dn, ckv, krr[:, :, :MLA_ROPE]))
    y_prompt = x
    new = [jnp.stack([s[i] for s in ctx_out], axis=1) for i in range(7)]
    for i in range(4):
        t = new[i].reshape(nb_ctx, DEPTH, -1, HEAD_DIM, seq_ctx)
        new[i] = jnp.transpose(t, (0, 1, 4, 2, 3))

    ropes = _rope_tables(seq_lat)
    keys_t = lambda c: jnp.transpose(c, (0, 1, 3, 4, 2)).reshape(nb_lat, DEPTH, -1, past)
    ck_a, cv_a, ck_b, cv_b = keys_t(cache_gqa_k), keys_t(cache_gqa_v), keys_t(cache_na_k), keys_t(cache_na_v)
    c_kr = jnp.pad(cache_mla_krope, ((0, 0), (0, 0), (0, 0), (0, LANES - MLA_ROPE)))
    bias_blocks = _na_bias_blocks(na_bias)
    x = x_sample
    for l in range(DEPTH):
        qa, ka, va, qb, kb, vb, zc, qd, ckv, zm, krr = _inproj(x, mods, True, lw, l, ropes)
        o_a = _attn_pair(qa, ka, va, (0, 0), ck_a, cv_a, l, name="gqa_lat")
        o_b = _na_latent(qb, kb, vb, ck_b, cv_b, l, bias_blocks)
        o_c, _ = _deltanet(zc, zm, lw, l, state=state_dn)
        o_d = _mla(qd, ckv, krr, lw, l, cache=(cache_mla_ckv, c_kr), name="mla_lat")
        x = _outffn((o_a, o_b, o_c, o_d), x, mods, True, lw, l, fg, l == DEPTH - 1)
    y_sample = x

    return (y_prompt, y_sample, *new)
```

```python
import functools

import numpy as np
import jax
import jax.numpy as jnp
from jax import lax
from jax.experimental import pallas as pl
from jax.experimental.pallas import tpu as pltpu

F32 = jnp.float32
BF16 = jnp.bfloat16

D_MODEL = 1024
DEPTH = 2
GRID_W = 64
HEAD_DIM = 64
ROPE_BASE = 10000.0
NEG_INF = -1e30
MOD_CHUNKS = 6
GQA_HEADS, GQA_KV_HEADS = 4, 2
NA_HEADS, NA_KH, NA_KW = 4, 8, 16
DN_HEADS, DN_DK, DN_DV, DN_CONV, DN_CHUNK = 4, 64, 64, 4, 64
DN_QKV = DN_HEADS * (2 * DN_DK + DN_DV)
DN_GATES = 2 * DN_HEADS
MLA_HEADS, MLA_Q_LORA, MLA_KV_LORA, MLA_NOPE, MLA_ROPE, MLA_V = 4, 256, 128, 64, 32, 64
MLA_SCALE = (MLA_NOPE + MLA_ROPE) ** -0.5
D_FF = -(-8 * D_MODEL // (3 * 256)) * 256
EPS = 1e-6
LOG2E = 1.4426950408889634

LANES = 128
ROW_TILE = 256
Q_TILE = 256
NA_BAND = 768
TILES_PER_STEP = 2
ATTN_Q_TILE = 512
HEAD_LOOKAHEAD = 1
DN_VMEM_BYTES_PER_ROW = 12288 + 9216
DN_VMEM_BUDGET = 46 * 1024 * 1024
DN_ROWS = 256
PREP_CHUNKS = 4
IN_PAD_COLS = 2816
VMEM_LIMIT = 56 * 1024 * 1024

_NT = (((1,), (1,)), ((), ()))
_TN = (((0,), (0,)), ((), ()))


def _cparams(n_axes):
    return pltpu.CompilerParams(dimension_semantics=("arbitrary",) * n_axes,
                                vmem_limit_bytes=VMEM_LIMIT)


def _lane(shape):
    return lax.broadcasted_iota(jnp.int32, shape, len(shape) - 1)


def _silu(x):
    return x / (1.0 + jnp.exp(-x))


def _rms_full(x, g):
    return x * lax.rsqrt(jnp.mean(x * x, axis=-1, keepdims=True) + EPS) * g


def _seg64_sum(x):
    lo = _lane(x.shape) < HEAD_DIM
    s_lo = jnp.sum(jnp.where(lo, x, 0.0), axis=-1, keepdims=True)
    s_hi = jnp.sum(jnp.where(lo, 0.0, x), axis=-1, keepdims=True)
    return jnp.where(lo, s_lo, s_hi)


def _head_rms(x, g):
    parts = []
    for p in range(x.shape[-1] // LANES):
        xp = x[:, p * LANES:(p + 1) * LANES]
        ms = _seg64_sum(xp * xp) * (1.0 / HEAD_DIM)
        parts.append(xp * lax.rsqrt(ms + EPS))
    y = parts[0] if len(parts) == 1 else jnp.concatenate(parts, axis=-1)
    return y * g


def _head_l2(x):
    parts = []
    for p in range(x.shape[-1] // LANES):
        xp = x[:, p * LANES:(p + 1) * LANES]
        parts.append(xp * lax.rsqrt(_seg64_sum(xp * xp) + EPS))
    return parts[0] if len(parts) == 1 else jnp.concatenate(parts, axis=-1)


def _rope(x, cos, sin, half):
    first = (_lane(x.shape) & (2 * half - 1)) < half
    rot = jnp.where(first, pltpu.roll(x, LANES - half, 1), pltpu.roll(x, half, 1))
    return x * cos + rot * sin


def _softmax_parts(scores):
    m = jnp.max(scores[0], axis=-1, keepdims=True)
    for s in scores[1:]:
        m = jnp.maximum(m, jnp.max(s, axis=-1, keepdims=True))
    es = [jnp.exp2(s - m) for s in scores]
    l = jnp.sum(es[0], axis=-1, keepdims=True)
    for e in es[1:]:
        l = l + jnp.sum(e, axis=-1, keepdims=True)
    return es, 1.0 / l


def _bdot(a, b):
    return jnp.dot(a, b, preferred_element_type=F32)


def _pipelined_heads(n_heads, scores, attend):
    outs = []
    queue = [scores(hd) for hd in range(min(HEAD_LOOKAHEAD, n_heads))]
    for hd in range(n_heads):
        if hd + HEAD_LOOKAHEAD < n_heads:
            queue.append(scores(hd + HEAD_LOOKAHEAD))
        es, rl = _softmax_parts(queue.pop(0))
        outs.append(attend(hd, es) * rl)
    return outs


def _pair_attention(q_blocks, sources, qmap):
    lo = _lane(q_blocks[0].shape) < HEAD_DIM

    def scores(hd):
        p, half = divmod(hd, 2)
        qm = jnp.where(lo if half == 0 else jnp.logical_not(lo), q_blocks[p], 0.0).astype(BF16)
        return [_bdot(qm, k_block(qmap[p])) if transposed
                else lax.dot_general(qm, k_block(qmap[p]), _NT, preferred_element_type=F32)
                for k_block, _, transposed in sources]

    def attend(hd, es):
        kv = qmap[hd // 2]
        parts = [lax.dot_general(e.astype(BF16), v_block(kv), _NT, preferred_element_type=F32) if transposed
                 else _bdot(e.astype(BF16), v_block(kv)) for e, (_, v_block, transposed) in zip(es, sources)]
        return functools.reduce(lambda a, b: a + b, parts)

    outs = _pipelined_heads(2 * len(qmap), scores, attend)
    return [jnp.where(lo, outs[2 * p], outs[2 * p + 1]) for p in range(len(qmap))]


def _mla_attention(q_heads, k_head, v_block):
    lo = _lane(q_heads[0].shape) < MLA_V

    def scores(hd):
        return [lax.dot_general(q_heads[hd], k_head(hd), _NT, preferred_element_type=F32)]

    def attend(hd, es):
        return _bdot(es[0].astype(BF16), v_block(hd // 2))

    outs = _pipelined_heads(MLA_HEADS, scores, attend)
    return [jnp.where(lo, outs[2 * p], outs[2 * p + 1]) for p in range(MLA_HEADS // 2)]


def _mod_kernel(c_ref, w_ref, b_ref, o_ref):
    s = _silu(c_ref[...]).astype(BF16)
    o_ref[...] = _bdot(s, w_ref[...].astype(BF16)) + b_ref[...]


def _modulation(cond, w_mod, b_mod):
    n = MOD_CHUNKS * D_MODEL
    tn = 1536
    return pl.pallas_call(
        _mod_kernel,
        grid=(DEPTH, n // tn),
        in_specs=[pl.BlockSpec((16, D_MODEL), lambda l, j: (0, 0)),
                  pl.BlockSpec((None, D_MODEL, tn), lambda l, j: (l, 0, j)),
                  pl.BlockSpec((None, 1, tn), lambda l, j: (l, 0, j))],
        out_specs=pl.BlockSpec((None, 16, tn), lambda l, j: (l, 0, j)),
        out_shape=jax.ShapeDtypeStruct((DEPTH, 16, n), F32),
        compiler_params=_cparams(2),
        name="modulation",
    )(cond, w_mod, b_mod.reshape(DEPTH, 1, n))


_IN_OUT_WIDTHS = (256, 128, 128, 256, 256, 256, 1024, 512, 128, 128, 128)
_IN_OUT_WIDTHS_CTX = (256, 128, 128, 256, 256, 256, 1024, 256, 128, 128, 128)
_IN_OUT_DTYPES_LAT = (BF16, BF16, BF16, BF16, BF16, BF16, F32, BF16, BF16, F32, BF16)
_IN_OUT_DTYPES_CTX = (BF16, F32, F32, BF16, F32, F32, F32, BF16, F32, F32, F32)


def _inproj_kernel(positioned, *refs):
    (x_ref, mod_ref, g1_ref, w_ref, qng_ref, kng_ref, mqg_ref, wq_ref, mkg_ref) = refs[:9]
    if positioned:
        cos64_ref, sin64_ref, cosm_ref, sinm_ref, coskr_ref, sinkr_ref = refs[9:15]
        n_in = 15
    else:
        wk_ref, wv_ref = refs[9:11]
        n_in = 11
    (qa_ref, ka_ref, va_ref, qb_ref, kb_ref, vb_ref, zc_ref, qd_ref, ckv_ref, zm_ref,
     krr_ref) = refs[n_in:]
    tiles = range(TILES_PER_STEP)
    tm = x_ref.shape[1]
    lane = _lane((tm, LANES))
    lo = lane < HEAD_DIM

    m = mod_ref[...]
    hb = jnp.concatenate([(_rms_full(x_ref[t], g1_ref[...]) * (1.0 + m[1:2]) + m[0:1]).astype(BF16)
                          for t in tiles], axis=0)

    def project(c0, c1):
        z = lax.dot_general(hb, w_ref[c0:c1, :], _NT, preferred_element_type=F32)
        return [z[t * tm:(t + 1) * tm] for t in tiles]

    za = project(0, 512)
    zb = project(512, 1280)

    for t in tiles:
        q = _head_rms(za[t][:, 0:256], qng_ref[...])
        k = _head_rms(za[t][:, 256:384], kng_ref[...])
        q0, q1 = q[:, 0:128], q[:, 128:256]
        q0, q1 = jnp.where(lo, q0, pltpu.roll(q1, HEAD_DIM, 1)), jnp.where(lo, pltpu.roll(q0, HEAD_DIM, 1), q1)
        if positioned:
            cos, sin = cos64_ref[t], sin64_ref[t]
            q0, q1 = _rope(q0, cos, sin, 16), _rope(q1, cos, sin, 16)
            k = _rope(k, cos, sin, 16)
        q0, q1 = q0 * (HEAD_DIM ** -0.5 * LOG2E), q1 * (HEAD_DIM ** -0.5 * LOG2E)
        v = za[t][:, 384:512]
        if not positioned:
            kb16, vb16 = k.astype(BF16), v.astype(BF16)
            q0, q1 = _pair_attention([q0, q1], [(lambda i: kb16, lambda i: vb16, False)], (0, 0))
        qa_ref[t] = jnp.concatenate([q0, q1], axis=-1).astype(qa_ref.dtype)
        ka_ref[t] = k.astype(ka_ref.dtype) if positioned else k.T
        va_ref[t] = v.astype(va_ref.dtype) if positioned else v.T

    zd = project(2304, IN_PAD_COLS)
    zc = project(1280, 2304)

    for t in tiles:
        q = zb[t][:, 0:256] * (HEAD_DIM ** -0.5 * LOG2E)
        k, v = zb[t][:, 256:512], zb[t][:, 512:768]
        if not positioned:
            kb16, vb16 = k.astype(BF16), v.astype(BF16)
            q = jnp.concatenate(_pair_attention(
                [q[:, 0:LANES], q[:, LANES:]],
                [(lambda i, kb16=kb16: kb16[:, i * LANES:(i + 1) * LANES],
                  lambda i, vb16=vb16: vb16[:, i * LANES:(i + 1) * LANES], False)], (0, 1)), axis=-1)
        qb_ref[t] = q.astype(qb_ref.dtype)
        kb_ref[t] = k.astype(kb_ref.dtype) if positioned else k.T
        vb_ref[t] = v.astype(vb_ref.dtype) if positioned else v.T

    shifted = []
    for t in tiles:
        rolled = [pltpu.roll(zd[t][:, j * LANES:(j + 1) * LANES], LANES - 2 * DN_GATES, 1) for j in range(4)]
        keep = lane < LANES - 2 * DN_GATES
        shifted.append([jnp.where(keep, rolled[j], rolled[(j + 1) % 4]) for j in range(4)])

    cq = jnp.concatenate([_rms_full(jnp.concatenate(shifted[t][0:2], axis=-1), mqg_ref[...]).astype(BF16)
                          for t in tiles], axis=0)
    qm = _bdot(cq, wq_ref[...])
    for t in tiles:
        zc_ref[t] = zc[t]
        zm_ref[t] = zd[t][:, 0:LANES]
        ckv = _rms_full(shifted[t][2], mkg_ref[...])
        ckv_ref[t] = ckv.astype(ckv_ref.dtype)
        kr = jnp.where(lane < MLA_ROPE, shifted[t][3], 0.0)
        q = qm[t * tm:(t + 1) * tm]
        if positioned:
            kr = _rope(kr, coskr_ref[t], sinkr_ref[t], 8)
            cm, sm = cosm_ref[t], sinm_ref[t]
            q = jnp.concatenate([_rope(q[:, i * LANES:(i + 1) * LANES], cm, sm, 8)
                                 for i in range(MLA_HEADS)], axis=-1)
        krr_ref[t] = kr.astype(krr_ref.dtype)
        q = q * (MLA_SCALE * LOG2E)
        if not positioned:
            c16 = ckv.astype(BF16)
            k16 = _bdot(jnp.concatenate([c16, kr.astype(BF16)], axis=-1), wk_ref[...]).astype(BF16)
            v16 = _bdot(c16, wv_ref[...]).astype(BF16)
            q = jnp.concatenate(_mla_attention(
                [q[:, i * LANES:(i + 1) * LANES].astype(BF16) for i in range(MLA_HEADS)],
                lambda i, k16=k16: k16[:, i * LANES:(i + 1) * LANES],
                lambda i, v16=v16: v16[:, i * LANES:(i + 1) * LANES]), axis=-1)
        qd_ref[t] = q.astype(qd_ref.dtype)


def _inproj(x, mods, per_batch_mods, lw, layer, ropes):
    b, l, _ = x.shape
    tm, ts = ROW_TILE, TILES_PER_STEP
    n_tiles = b * l // tm
    tiles_per_seq = l // tm
    positioned = ropes is not None
    row = lambda w: pl.BlockSpec((ts, tm, w), lambda i: (i, 0, 0))
    const = lambda shape: pl.BlockSpec((None,) + shape, lambda i: (layer,) + (0,) * len(shape))
    assert not per_batch_mods or tiles_per_seq % ts == 0
    mod_spec = pl.BlockSpec((None, None, MOD_CHUNKS, D_MODEL),
                            (lambda i: (layer, 1 + i * ts // tiles_per_seq, 0, 0)) if per_batch_mods
                            else (lambda i: (layer, 0, 0, 0)))
    in_specs = [row(D_MODEL), mod_spec, const((1, D_MODEL)), const((IN_PAD_COLS, D_MODEL)),
                const((1, 256)), const((1, 128)), const((1, MLA_Q_LORA)),
                const((MLA_Q_LORA, 4 * LANES)), const((1, MLA_KV_LORA))]
    args = [x.reshape(n_tiles, tm, D_MODEL), mods, lw["norm1_g"], lw["w_in"], lw["qn_g"], lw["kn_g"],
            lw["mla_qn_g"], lw["wq"], lw["mla_kvn_g"]]
    if positioned:
        steps_per_seq = tiles_per_seq // ts
        in_specs += [pl.BlockSpec((ts, tm, LANES), lambda i: (i % steps_per_seq, 0, 0))] * 6
        args += [r.reshape(tiles_per_seq, tm, LANES) for r in ropes]
        widths, dtypes = _IN_OUT_WIDTHS, _IN_OUT_DTYPES_LAT
    else:
        assert tiles_per_seq == 1
        in_specs += [const((2 * LANES, 4 * LANES)), const((LANES, 2 * LANES))]
        args += [lw["wk"], lw["wv"]]
        widths, dtypes = _IN_OUT_WIDTHS_CTX, _IN_OUT_DTYPES_CTX
    flipped = () if positioned else (1, 2, 4, 5)
    shapes = [(n_tiles, w, tm) if i in flipped else (n_tiles, tm, w) for i, w in enumerate(widths)]
    outs = pl.pallas_call(
        functools.partial(_inproj_kernel, positioned),
        grid=(n_tiles // ts,),
        in_specs=in_specs,
        out_specs=[pl.BlockSpec((ts,) + shp[1:], lambda i: (i, 0, 0)) for shp in shapes],
        out_shape=[jax.ShapeDtypeStruct(shp, dt) for shp, dt in zip(shapes, dtypes)],
        compiler_params=_cparams(1),
        name="inproj_lat" if positioned else "inproj_attn_ctx",
    )(*args)
    return [o if i in flipped else o.reshape(b, l, o.shape[-1]) for i, o in enumerate(outs)]


def _attn_pair_kernel(qmap, q_ref, kc_ref, vc_ref, k_ref, v_ref, o_ref, kcbuf, vcbuf, kbuf, vbuf):
    @pl.when(pl.program_id(1) == 0)
    def _():
        kcbuf[...] = kc_ref[...].astype(BF16)
        vcbuf[...] = vc_ref[...].astype(BF16)
        kbuf[...] = k_ref[...].astype(BF16)
        vbuf[...] = v_ref[...].astype(BF16)

    rows = lambda buf: (lambda i: buf[i * LANES:(i + 1) * LANES, :])
    cols = lambda buf: (lambda i: buf[:, i * LANES:(i + 1) * LANES])
    q_blocks = [q_ref[:, p * LANES:(p + 1) * LANES].astype(F32) for p in range(len(qmap))]
    outs = _pair_attention(q_blocks, [(rows(kcbuf), rows(vcbuf), True), (cols(kbuf), cols(vbuf), False)], qmap)
    for p, o in enumerate(outs):
        o_ref[:, p * LANES:(p + 1) * LANES] = o.astype(o_ref.dtype)


def _attn_pair(q, k, v, qmap, kc_t, vc_t, layer, name):
    b, lq, wq = q.shape
    ls, wk = k.shape[1], k.shape[2]
    lc = kc_t.shape[3]
    tq = min(ATTN_Q_TILE, lq)
    cspec = pl.BlockSpec((None, None, wk, lc), lambda bi, i: (bi, layer, 0, 0))
    sspec = pl.BlockSpec((None, ls, wk), lambda bi, i: (bi, 0, 0))
    return pl.pallas_call(
        functools.partial(_attn_pair_kernel, qmap),
        grid=(b, lq // tq),
        in_specs=[pl.BlockSpec((None, tq, wq), lambda bi, i: (bi, i, 0)), cspec, cspec, sspec, sspec],
        out_specs=pl.BlockSpec((None, tq, wq), lambda bi, i: (bi, i, 0)),
        out_shape=jax.ShapeDtypeStruct((b, lq, wq), BF16),
        scratch_shapes=[pltpu.VMEM((wk, lc), BF16), pltpu.VMEM((wk, lc), BF16),
                        pltpu.VMEM((ls, wk), BF16), pltpu.VMEM((ls, wk), BF16)],
        compiler_params=_cparams(2),
        name=name,
    )(q, kc_t, vc_t, k, v)


def _mla_kernel(has_cache, *refs):
    if has_cache:
        q_ref, ckvc_ref, krc_ref, ckv_ref, kr_ref, wk_ref, wv_ref, o_ref, kbuf, vbuf = refs
    else:
        q_ref, ckv_ref, kr_ref, wk_ref, wv_ref, o_ref, kbuf, vbuf = refs

    @pl.when(pl.program_id(1) == 0)
    def _():
        def expand(c_ref, r_ref, r0, r1):
            c = c_ref[...].astype(BF16)
            ckr = jnp.concatenate([c, r_ref[...].astype(BF16)], axis=-1)
            kbuf[r0:r1, :] = _bdot(ckr, wk_ref[...]).astype(BF16)
            vbuf[r0:r1, :] = _bdot(c, wv_ref[...]).astype(BF16)
        off = 0
        if has_cache:
            off = ckvc_ref.shape[0]
            expand(ckvc_ref, krc_ref, 0, off)
        expand(ckv_ref, kr_ref, off, kbuf.shape[0])

    outs = _mla_attention([q_ref[:, hd * LANES:(hd + 1) * LANES].astype(BF16) for hd in range(MLA_HEADS)],
                          lambda i: kbuf[:, i * LANES:(i + 1) * LANES],
                          lambda i: vbuf[:, i * LANES:(i + 1) * LANES])
    for p, o in enumerate(outs):
        o_ref[:, p * LANES:(p + 1) * LANES] = o.astype(o_ref.dtype)


def _mla(q, ckv, kr, lw, layer, cache=None, name="mla"):
    b, lq, wq = q.shape
    ls = ckv.shape[1]
    tq = min(ATTN_Q_TILE, lq)
    lc = 0 if cache is None else cache[0].shape[2]
    in_specs = [pl.BlockSpec((None, tq, wq), lambda bi, i: (bi, i, 0))]
    args = [q]
    if cache is not None:
        ckvc, krc = cache
        cspec = pl.BlockSpec((None, None, lc, LANES), lambda bi, i: (bi, layer, 0, 0))
        in_specs += [cspec, cspec]
        args += [ckvc, krc]
    sspec = pl.BlockSpec((None, ls, LANES), lambda bi, i: (bi, 0, 0))
    in_specs += [sspec, sspec,
                 pl.BlockSpec((None, 2 * LANES, 4 * LANES), lambda bi, i: (layer, 0, 0)),
                 pl.BlockSpec((None, LANES, 2 * LANES), lambda bi, i: (layer, 0, 0))]
    args += [ckv, kr, lw["wk"], lw["wv"]]
    return pl.pallas_call(
        functools.partial(_mla_kernel, cache is not None),
        grid=(b, lq // tq),
        in_specs=in_specs,
        out_specs=pl.BlockSpec((None, tq, 2 * LANES), lambda bi, i: (bi, i, 0)),
        out_shape=jax.ShapeDtypeStruct((b, lq, 2 * LANES), BF16),
        scratch_shapes=[pltpu.VMEM((lc + ls, 4 * LANES), BF16), pltpu.VMEM((lc + ls, 2 * LANES), BF16)],
        compiler_params=_cparams(2),
        name=name,
    )(*args)


def _na_kernel(q_ref, k_ref, v_ref, kc_ref, vc_ref, bias_ref, o_ref):
    start = pl.multiple_of(pl.program_id(0) * 256, 256)
    kband = k_ref[pl.ds(start, NA_BAND), :].astype(BF16)
    vband = v_ref[pl.ds(start, NA_BAND), :].astype(BF16)
    kc = kc_ref[...].astype(BF16)
    vc = vc_ref[...].astype(BF16)
    tq = q_ref.shape[0]
    lo = _lane((tq, LANES)) < HEAD_DIM

    def scores(hd):
        p, half = divmod(hd, 2)
        sl = slice(p * LANES, (p + 1) * LANES)
        qm = jnp.where(lo if half == 0 else jnp.logical_not(lo), q_ref[:, sl].astype(F32), 0.0).astype(BF16)
        bias = jnp.concatenate([bias_ref[0, hd], bias_ref[1, hd]], axis=0)
        s_loc = lax.dot_general(qm, kband[:, sl], _NT, preferred_element_type=F32) + bias
        return [s_loc, _bdot(qm, kc[sl, :])]

    def attend(hd, es):
        sl = slice((hd // 2) * LANES, (hd // 2 + 1) * LANES)
        return (_bdot(es[0].astype(BF16), vband[:, sl])
                + lax.dot_general(es[1].astype(BF16), vc[sl, :], _NT, preferred_element_type=F32))

    outs = _pipelined_heads(NA_HEADS, scores, attend)
    for p in range(NA_HEADS // 2):
        o_ref[:, p * LANES:(p + 1) * LANES] = jnp.where(lo, outs[2 * p], outs[2 * p + 1]).astype(o_ref.dtype)


def _na_latent(q, k, v, kc_t, vc_t, layer, bias_blocks):
    b, n, w = q.shape
    lc = kc_t.shape[3]
    nq = n // (2 * Q_TILE)
    full = pl.BlockSpec((None, n, w), lambda j, bi: (bi, 0, 0))
    cspec = pl.BlockSpec((None, None, w, lc), lambda j, bi: (bi, layer, 0, 0))
    return pl.pallas_call(
        _na_kernel,
        grid=(nq, b),
        in_specs=[pl.BlockSpec((None, 2 * Q_TILE, w), lambda j, bi: (bi, j, 0)), full, full, cspec, cspec,
                  pl.BlockSpec((None, 2, NA_HEADS, Q_TILE, NA_BAND), lambda j, bi: (layer, j, 0, 0, 0))],
        out_specs=pl.BlockSpec((None, 2 * Q_TILE, w), lambda j, bi: (bi, j, 0)),
        out_shape=jax.ShapeDtypeStruct((b, n, w), BF16),
        compiler_params=_cparams(2),
        name="na_latent",
    )(q, k, v, kc_t, vc_t, bias_blocks)


NA_GRID_ROWS = 16
NA_BAND_ROW0 = (0, 0, 4, 4)


def _na_bias_kernel(b_ref, o_ref, tp_s):
    hd = pl.program_id(0)
    n_dr, n_dc = 2 * NA_KH - 1, 2 * NA_KW - 1
    shape = (GRID_W, LANES)
    c = lax.broadcasted_iota(jnp.int32, shape, 0)
    lane = _lane(shape)
    kc = lane & (GRID_W - 1)
    lo = lane < GRID_W
    diff = kc - c + (NA_KW - 1)
    c0 = jnp.clip(c - NA_KW // 2, 0, GRID_W - NA_KW)
    col_ok = (kc >= c0) & (kc < c0 + NA_KW)
    neg = jnp.full(shape, NEG_INF, F32)
    for dr0 in range(-1, n_dr):
        acc = neg
        for d in range(n_dc):
            v_lo = b_ref[hd * n_dr + dr0, d] if dr0 >= 0 else 0.0
            v_hi = b_ref[hd * n_dr + dr0 + 1, d] if dr0 + 1 < n_dr else 0.0
            acc = jnp.where(diff == d, jnp.where(lo, v_lo, v_hi), acc)
        tp_s[dr0 + 1] = jnp.where(col_ok, acc * LOG2E, NEG_INF)
    for j in range(NA_GRID_ROWS // 4):
        for ri in range(4):
            r = 4 * j + ri
            r0 = min(max(r - NA_KH // 2, 0), NA_GRID_ROWS - NA_KH)
            for kp in range(NA_BAND // LANES):
                kr = NA_BAND_ROW0[j] + 2 * kp
                ok_lo, ok_hi = r0 <= kr < r0 + NA_KH, r0 <= kr + 1 < r0 + NA_KH
                dr0 = kr - r + (NA_KH - 1)
                if ok_lo and ok_hi:
                    t = tp_s[dr0 + 1]
                elif ok_lo:
                    t = jnp.where(lo, tp_s[dr0 + 1], NEG_INF)
                elif ok_hi:
                    t = jnp.where(lo, NEG_INF, tp_s[dr0 + 1])
                else:
                    t = neg
                o_ref[j, ri * GRID_W:(ri + 1) * GRID_W, kp * LANES:(kp + 1) * LANES] = t


def _na_bias_blocks(bias):
    nq = NA_GRID_ROWS // 4
    return pl.pallas_call(
        _na_bias_kernel,
        grid=(DEPTH * NA_HEADS,),
        in_specs=[pl.BlockSpec(memory_space=pltpu.SMEM)],
        out_specs=pl.BlockSpec((None, nq, None, Q_TILE, NA_BAND),
                               lambda i: (i // NA_HEADS, 0, i % NA_HEADS, 0, 0)),
        out_shape=jax.ShapeDtypeStruct((DEPTH, nq, NA_HEADS, Q_TILE, NA_BAND), F32),
        scratch_shapes=[pltpu.VMEM((2 * NA_KH, GRID_W, LANES), F32)],
        compiler_params=_cparams(1),
        name="na_bias",
    )(bias.reshape(DEPTH * NA_HEADS * (2 * NA_KH - 1), 2 * NA_KW - 1))


def _widen(cols, n):
    blk = _lane((n, DN_HEADS * DN_DV)) >> 6
    return jnp.where(blk == 0, cols[0], jnp.where(blk == 1, cols[1], jnp.where(blk == 2, cols[2], cols[3])))


def _deltanet_kernel(seq, has_state, *refs):
    if has_state:
        zc_all, zm_all, s0_ref, cw_ref, alog_ref, dtb_ref, og_ref, o_ref = refs[:8]
    else:
        zc_all, zm_all, cw_ref, alog_ref, dtb_ref, og_ref, o_ref, sfin_ref = refs[:8]
    scratch = refs[8:]
    n_seqs = zc_all.shape[0]
    g_all, o_all, c_all, mp_all = scratch[4:8]
    n_chunks = seq // DN_CHUNK
    wide = DN_HEADS * DN_DV

    a_off, b_off = 0, DN_GATES
    ri = lax.broadcasted_iota(jnp.int32, (DN_ROWS, DN_ROWS), 0)
    ci = lax.broadcasted_iota(jnp.int32, (DN_ROWS, DN_ROWS), 1)
    same = (ri >> 6) == (ci >> 6)
    tri = [jnp.tile(jnp.where(same & ((ci <= ri) if d == 0 else (ci >= ri)), 1.0, 0.0).astype(BF16), (1, 3))
           for d in range(2)]
    n_blocks = seq // DN_ROWS

    def terms3(x):
        hi = x.astype(BF16)
        rest = x - hi.astype(F32)
        mid = rest.astype(BF16)
        return hi, mid, (rest - mid.astype(F32)).astype(BF16)

    def split3(x):
        return jnp.concatenate(terms3(x), axis=0)

    def preprocess(views, rb, carry):
        zc_ref, zm_ref, q_s, k_s, v_s, b_s, g_s = views[:7]
        r0 = pl.multiple_of(rb * DN_ROWS, DN_ROWS)
        rows = pl.ds(r0, DN_ROWS)
        before = pl.ds(pl.multiple_of(jnp.maximum(r0 - 8, 0), 8), 8)
        after = pl.ds(pl.multiple_of(jnp.minimum(r0 + DN_ROWS, seq - 8), 8), 8)
        for part, dst in enumerate((q_s, k_s, v_s)):
            cs = slice(part * wide, (part + 1) * wide)
            head = jnp.where(rb > 0, zc_ref[before, cs], 0.0)
            tail = jnp.where(rb < n_blocks - 1, zc_ref[after, cs], 0.0)
            xe = jnp.concatenate([head, zc_ref[rows, cs], tail], axis=0)
            w = cw_ref[:, cs]
            y = (w[0:1] * xe[7:7 + DN_ROWS] + w[1:2] * xe[8:8 + DN_ROWS]
                 + w[2:3] * xe[9:9 + DN_ROWS] + w[3:4] * xe[10:10 + DN_ROWS])
            y = _silu(y)
            if part == 0:
                y = _head_l2(y) * (DN_DK ** -0.5)
            elif part == 1:
                y = _head_l2(y)
            dst[rows, :] = y
        zm = zm_ref[rows, :]
        xa = zm + dtb_ref[...]
        logd = -jnp.exp(alog_ref[...]) * (jnp.maximum(xa, 0.0) + jnp.log1p(jnp.exp(-jnp.abs(xa))))
        beta = 1.0 / (1.0 + jnp.exp(-zm))
        logd3 = split3(logd)
        for d in range(2):
            b_s[d, rows, :] = _widen([beta[:, b_off + 4 * d + hd:b_off + 4 * d + hd + 1]
                                      for hd in range(DN_HEADS)], DN_ROWS)
            g = _bdot(tri[d], logd3)
            g_s[d, rows, :] = _widen([g[:, a_off + 4 * d + hd:a_off + 4 * d + hd + 1]
                                      for hd in range(DN_HEADS)], DN_ROWS)
        return carry

    ii = lax.broadcasted_iota(jnp.int32, (DN_CHUNK, wide), 0)
    jj = _lane((DN_CHUNK, wide)) & (DN_CHUNK - 1)
    blk = _lane((DN_CHUNK, wide)) >> 6
    diag = ii == jj
    eye = jnp.where(diag, 1.0, 0.0)
    half_mask = [jnp.where((_lane((DN_CHUNK, LANES)) >> 6) == half, 1.0, 0.0).astype(BF16) for half in range(2)]
    zero_block = jnp.zeros((DN_CHUNK, LANES), BF16)

    def bd(z):
        zb = z.astype(BF16)
        rows = []
        for hd in range(DN_HEADS):
            col, half = divmod(hd, 2)
            kept = zb[:, col * LANES:(col + 1) * LANES] * half_mask[half]
            rows.append(jnp.concatenate([kept, zero_block] if col == 0 else [zero_block, kept], axis=1))
        return jnp.concatenate(rows, axis=0)

    def fold(gram):
        out = jnp.where(blk == 0, gram[0:DN_CHUNK], 0.0)
        for hd in range(1, DN_HEADS):
            out = out + jnp.where(blk == hd, gram[hd * DN_CHUNK:(hd + 1) * DN_CHUNK], 0.0)
        return out

    tri_masks = []
    for d in range(2):
        incl = (jj <= ii) if d == 0 else (jj >= ii)
        strict = (jj < ii) if d == 0 else (jj > ii)
        pair = [((ii >> (lvl + 1)) == (jj >> (lvl + 1)))
                & (((ii >> lvl) & 1) == (1 - d)) & (((jj >> lvl) & 1) == d) for lvl in range(6)]
        tri_masks.append((incl, strict, pair))

    def prepare(views, step, carry):
        q_s, k_s, v_s, b_s, g_s, o_s, c_s, mp_s = views[2:]
        chunks = [step * PREP_CHUNKS + i for i in range(PREP_CHUNKS)]
        rows = [pl.ds(pl.multiple_of(c * DN_CHUNK, DN_CHUNK), DN_CHUNK) for c in chunks]
        qkv = [(q_s[r, :], k_s[r, :], v_s[r, :]) for r in rows]
        inst = [(ci, d) for ci in range(PREP_CHUNKS) for d in range(2)]
        beta = {(ci, d): b_s[d, rows[ci], :] for ci, d in inst}
        kb = {(ci, d): qkv[ci][1] * beta[ci, d] for ci, d in inst}
        r = [lax.dot_general(jnp.concatenate([kb[ci, 0], kb[ci, 1], qkv[ci][0]], axis=0).astype(BF16),
                             bd(qkv[ci][1]), _NT, preferred_element_type=F32)
             for ci in range(PREP_CHUNKS)]
        g, a, qk, eg, t = {}, {}, {}, {}, {}
        for ci, d in inst:
            incl, strict, pair = tri_masks[d]
            g[ci, d] = g_s[d, rows[ci], :]
            g_row = jnp.sum(jnp.where(diag, g[ci, d], 0.0), axis=0, keepdims=True)
            dm = jnp.where(incl, jnp.exp(jnp.where(incl, g[ci, d] - g_row, 0.0)), 0.0)
            a[ci, d] = jnp.where(strict, r[ci][d * DN_CHUNK:(d + 1) * DN_CHUNK] * dm, 0.0)
            qk[ci, d] = r[ci][2 * DN_CHUNK:] * dm
            eg[ci, d] = jnp.exp(g[ci, d])
            t[ci, d] = eye - jnp.where(pair[0], a[ci, d], 0.0)
        for lvl in range(1, 6):
            te = {i: _bdot(t[i].astype(BF16), bd(jnp.where(tri_masks[i[1]][2][lvl], a[i], 0.0))) for i in inst}
            t = {i: t[i] - _bdot(te[i].astype(BF16), bd(t[i])) for i in inst}
        nb = {i: jnp.where(diag, 0.0, t[i]).astype(BF16) for i in inst}
        rhs_u = {(ci, d): qkv[ci][2] * beta[ci, d] for ci, d in inst}
        rhs_w = {i: kb[i] * eg[i] for i in inst}
        q2 = {i: qk[i] + _bdot(qk[i].astype(BF16), bd(nb[i])) for i in inst}
        both = {i: jnp.concatenate([nb[i], q2[i].astype(BF16)], axis=0) for i in inst}
        ru = {i: _bdot(both[i], bd(rhs_u[i])) for i in inst}
        rw = {i: _bdot(both[i], bd(rhs_w[i])) for i in inst}
        u = {i: rhs_u[i] + ru[i][0:DN_CHUNK] for i in inst}
        w = {i: rhs_w[i] + rw[i][0:DN_CHUNK] for i in inst}
        kd = {}
        for ci, d in inst:
            g_last = g[ci, d][DN_CHUNK - 1:DN_CHUNK] if d == 0 else g[ci, d][0:1]
            kd[ci, d] = (qkv[ci][1] * jnp.exp(g_last - g[ci, d])).astype(BF16)
        p = {(ci, d): qkv[ci][0] * eg[ci, d] - rw[ci, d][DN_CHUNK:] for ci, d in inst}
        o0 = {i: ru[i][DN_CHUNK:] for i in inst}
        m = {i: fold(lax.dot_general(kd[i], w[i].astype(BF16), _TN, preferred_element_type=F32)) for i in inst}
        cc = {i: fold(lax.dot_general(kd[i], u[i].astype(BF16), _TN, preferred_element_type=F32)) for i in inst}
        for ci, d in inst:
            mrow = pl.ds(pl.multiple_of(chunks[ci] * (2 * DN_CHUNK), 2 * DN_CHUNK), 2 * DN_CHUNK)
            c_s[d, rows[ci], :] = cc[ci, d]
            mp_s[d, mrow, :] = jnp.concatenate([m[ci, d], p[ci, d]], axis=0).astype(BF16)
        for ci in range(PREP_CHUNKS):
            o_s[rows[ci], :] = o0[ci, 0] + o0[ci, 1]
        return carry

    for sq in range(n_seqs):
        views = (zc_all.at[sq], zm_all.at[sq]) + tuple(ref.at[sq] for ref in scratch)
        lax.fori_loop(0, n_blocks, functools.partial(preprocess, views), 0)
        lax.fori_loop(0, n_chunks // PREP_CHUNKS, functools.partial(prepare, views), 0)

    chains = [(sq, d) for sq in range(n_seqs) for d in range(2)]

    def scan(i, states):
        new = []
        for (sq, d), state in zip(chains, states):
            c = i if d == 0 else n_chunks - 1 - i
            r0 = pl.multiple_of(c * DN_CHUNK, DN_CHUNK)
            rows = pl.ds(r0, DN_CHUNK)
            mrow = pl.ds(pl.multiple_of(c * (2 * DN_CHUNK), 2 * DN_CHUNK), 2 * DN_CHUNK)
            edge = pl.ds(pl.multiple_of(r0 + (DN_CHUNK - 8 if d == 0 else 0), 8), 8)
            g_edge = g_all[sq, d, edge, :]
            g_last = g_edge[7:8] if d == 0 else g_edge[0:1]
            res = _bdot(mp_all[sq, d, mrow, :], bd(state))
            o_all[sq, rows, :] = o_all[sq, rows, :] + res[DN_CHUNK:]
            new.append(state * jnp.exp(g_last) - res[0:DN_CHUNK] + c_all[sq, d, rows, :])
        return tuple(new)

    place = [jnp.where(diag & (blk == hd), 1.0, 0.0).astype(BF16) for hd in range(DN_HEADS)]

    def to_wide(heads):
        parts = [_bdot(term, place[hd]) for hd, x in enumerate(heads) for term in terms3(x)]
        return functools.reduce(lambda a, b: a + b, parts)

    def head_of(s, hd):
        parts = [lax.dot_general(term, place[hd], _NT, preferred_element_type=F32) for term in terms3(s)]
        return functools.reduce(lambda a, b: a + b, parts)

    if has_state:
        init = tuple(to_wide([s0_ref[sq, d, hd] for hd in range(DN_HEADS)]) for sq, d in chains)
    else:
        init = tuple(jnp.zeros((DN_DK, wide), F32) for _ in chains)
    fin = lax.fori_loop(0, n_chunks, scan, init)
    if not has_state:
        for (sq, d), state in zip(chains, fin):
            for hd in range(DN_HEADS):
                sfin_ref[sq, d, hd] = head_of(state, hd)

    for sq in range(n_seqs):
        gate = _silu(zc_all[sq, :, 3 * wide:4 * wide])
        o_ref[sq] = (_head_rms(o_all[sq], og_ref[...]) * gate).astype(o_ref.dtype)


def _deltanet(zc, zm, lw, layer, state=None):
    b, seq, _ = zc.shape
    wide = DN_HEADS * DN_DV
    has_state = state is not None
    ns = max(n for n in (1, 2, 4) if b % n == 0 and n * seq * DN_VMEM_BYTES_PER_ROW <= DN_VMEM_BUDGET)
    per_b = lambda w: pl.BlockSpec((ns, seq, w), lambda bi: (bi, 0, 0))
    const = lambda shape: pl.BlockSpec((None,) + shape, lambda bi: (layer,) + (0,) * len(shape))
    st_spec = pl.BlockSpec((ns, 2, DN_HEADS, DN_DK, DN_DV), lambda bi: (bi, 0, 0, 0, 0))
    in_specs = [per_b(4 * wide), per_b(LANES)]
    args = [zc, zm]
    if has_state:
        in_specs.append(pl.BlockSpec((ns, None, 2, DN_HEADS, DN_DK, DN_DV),
                                     lambda bi: (bi, layer, 0, 0, 0, 0)))
        args.append(state)
    in_specs += [const((DN_CONV, DN_QKV)), const((1, LANES)), const((1, LANES)), const((1, wide))]
    args += [lw["dn_conv_w"], lw["dn_alog_row"], lw["dn_dtb_row"], lw["dn_out_g"]]
    out_specs = [per_b(wide)]
    out_shape = [jax.ShapeDtypeStruct((b, seq, wide), BF16)]
    if not has_state:
        out_specs.append(st_spec)
        out_shape.append(jax.ShapeDtypeStruct((b, 2, DN_HEADS, DN_DK, DN_DV), F32))
    res = pl.pallas_call(
        functools.partial(_deltanet_kernel, seq, has_state),
        grid=(b // ns,),
        in_specs=in_specs,
        out_specs=out_specs,
        out_shape=out_shape,
        scratch_shapes=[pltpu.VMEM((ns, seq, wide), F32), pltpu.VMEM((ns, seq, wide), F32),
                        pltpu.VMEM((ns, seq, wide), F32), pltpu.VMEM((ns, 2, seq, wide), F32),
                        pltpu.VMEM((ns, 2, seq, wide), F32), pltpu.VMEM((ns, seq, wide), F32),
                        pltpu.VMEM((ns, 2, seq, wide), F32), pltpu.VMEM((ns, 2, 2 * seq, wide), BF16)],
        compiler_params=_cparams(1),
        name="deltanet_lat" if has_state else "deltanet_ctx",
    )(*args)
    return (res[0], None) if has_state else (res[0], res[1])


def _outffn_kernel(final, oa_ref, ob_ref, oc_ref, od_ref, x_ref, mod_ref, g2_ref, wo_ref, wg_ref,
                   wu_ref, wd_ref, fg_ref, y_ref):
    tiles = range(TILES_PER_STEP)
    m = mod_ref[...]
    o = [jnp.concatenate([oa_ref[t], ob_ref[t], oc_ref[t], od_ref[t]], axis=-1).astype(BF16) for t in tiles]
    x1 = [x_ref[t] + m[2:3] * _bdot(o[t], wo_ref[...]) for t in tiles]
    h = [(_rms_full(x1[t], g2_ref[...]) * (1.0 + m[4:5]) + m[3:4]).astype(BF16) for t in tiles]
    gate = [_bdot(h[t], wg_ref[...]) for t in tiles]
    up = [_bdot(h[t], wu_ref[...]) for t in tiles]
    act = [(_silu(gate[t]) * up[t]).astype(BF16) for t in tiles]
    x2 = [x1[t] + m[5:6] * _bdot(act[t], wd_ref[...]) for t in tiles]
    for t in tiles:
        y_ref[t] = _rms_full(x2[t], fg_ref[...]) if final else x2[t]


def _outffn(outs, x, mods, per_batch_mods, lw, layer, final_g, final):
    b, l, _ = x.shape
    tm, ts = ROW_TILE, TILES_PER_STEP
    n_tiles = b * l // tm
    tiles_per_seq = l // tm
    assert not per_batch_mods or tiles_per_seq % ts == 0
    row = lambda w: pl.BlockSpec((ts, tm, w), lambda i: (i, 0, 0))
    const = lambda shape: pl.BlockSpec((None,) + shape, lambda i: (layer,) + (0,) * len(shape),
                                       pipeline_mode=pl.Buffered(1))
    mod_spec = pl.BlockSpec((None, None, MOD_CHUNKS, D_MODEL),
                            (lambda i: (layer, 1 + i * ts // tiles_per_seq, 0, 0)) if per_batch_mods
                            else (lambda i: (layer, 0, 0, 0)))
    tiled = lambda a: a.reshape(n_tiles, tm, a.shape[-1])
    y = pl.pallas_call(
        functools.partial(_outffn_kernel, final),
        grid=(n_tiles // ts,),
        in_specs=[row(256), row(256), row(256), row(256), row(D_MODEL), mod_spec, const((1, D_MODEL)),
                  const((D_MODEL, D_MODEL)), const((D_MODEL, D_FF)), const((D_MODEL, D_FF)),
                  const((D_FF, D_MODEL)), pl.BlockSpec((1, D_MODEL), lambda i: (0, 0))],
        out_specs=row(D_MODEL),
        out_shape=jax.ShapeDtypeStruct((n_tiles, tm, D_MODEL), F32),
        compiler_params=_cparams(1),
        name="outffn",
    )(*[tiled(a) for a in outs], tiled(x), mods, lw["norm2_g"], lw["w_out"], lw["w_gate"], lw["w_up"],
      lw["w_down"], final_g)
    return y.reshape(b, l, D_MODEL)


def _rope_tables(n):
    t = jnp.arange(n)

    def axis(pos, half):
        inv = ROPE_BASE ** (-jnp.arange(half, dtype=F32) / half)
        ang = pos.astype(F32)[:, None] * inv[None, :]
        c, s = jnp.cos(ang), jnp.sin(ang)
        return jnp.concatenate([c, c], -1), jnp.concatenate([-s, s], -1)

    cr, sr = axis(t // GRID_W, 16)
    cc, sc = axis(t % GRID_W, 16)
    cos64 = jnp.tile(jnp.concatenate([cr, cc], -1), (1, 2))
    sin64 = jnp.tile(jnp.concatenate([sr, sc], -1), (1, 2))
    cr, sr = axis(t // GRID_W, 8)
    cc, sc = axis(t % GRID_W, 8)
    cos32, sin32 = jnp.concatenate([cr, cc], -1), jnp.concatenate([sr, sc], -1)
    one, zero = jnp.ones((n, 1), F32), jnp.zeros((n, 1), F32)
    cosm = jnp.concatenate([jnp.tile(one, (1, 64)), cos32, jnp.tile(one, (1, 32))], -1)
    sinm = jnp.concatenate([jnp.tile(zero, (1, 64)), sin32, jnp.tile(zero, (1, 32))], -1)
    coskr = jnp.concatenate([cos32, jnp.tile(one, (1, 96))], -1)
    sinkr = jnp.concatenate([sin32, jnp.tile(zero, (1, 96))], -1)
    return cos64, sin64, cosm, sinm, coskr, sinkr


_QA_ORDER = ((0, 64), (128, 192), (64, 128), (192, 256))


def _stacked_weights(p):
    w_in = jnp.swapaxes(p["w_in"], 1, 2)
    w_in = jnp.pad(w_in, ((0, 0), (0, IN_PAD_COLS - w_in.shape[1]), (0, 0))).astype(BF16)
    w_out = p["w_out"]
    w_out = jnp.concatenate([w_out[:, a:b] for a, b in _QA_ORDER + ((256, w_out.shape[1]),)], axis=1).astype(BF16)
    wq = p["mla_wq_up"].reshape(DEPTH, MLA_Q_LORA, MLA_HEADS, MLA_NOPE + MLA_ROPE)
    wq = jnp.pad(wq, ((0, 0), (0, 0), (0, 0), (0, LANES - MLA_NOPE - MLA_ROPE))).reshape(DEPTH, MLA_Q_LORA, 4 * LANES)
    wkv = p["mla_wkv_up"].reshape(DEPTH, MLA_KV_LORA, MLA_HEADS, MLA_NOPE + MLA_V)
    wk_top = jnp.pad(wkv[..., :MLA_NOPE], ((0, 0), (0, 0), (0, 0), (0, LANES - MLA_NOPE)))
    place = jnp.pad(jnp.eye(MLA_ROPE, dtype=F32), ((0, LANES - MLA_ROPE), (MLA_NOPE, LANES - MLA_NOPE - MLA_ROPE)))
    wk_bot = jnp.broadcast_to(place[None, :, None, :], (DEPTH, LANES, MLA_HEADS, LANES))
    wk = jnp.concatenate([wk_top, wk_bot], axis=1).reshape(DEPTH, 2 * LANES, 4 * LANES)
    wv = wkv[..., MLA_NOPE:].reshape(DEPTH, MLA_KV_LORA, MLA_HEADS * MLA_V)
    gate_row = lambda v: jnp.pad(v.reshape(DEPTH, 1, DN_GATES), ((0, 0), (0, 0), (0, LANES - DN_GATES)))
    row = lambda v: v[:, None, :]
    return {
        "norm1_g": row(p["norm1_g"]), "norm2_g": row(p["norm2_g"]),
        "w_in": w_in, "w_out": w_out,
        "qn_g": row(jnp.tile(p["gqa_qn_g"], (1, 4))), "kn_g": row(jnp.tile(p["gqa_kn_g"], (1, 2))),
        "mla_qn_g": row(p["mla_qn_g"]), "mla_kvn_g": row(p["mla_kvn_g"]),
        "wq": wq.astype(BF16), "wk": wk.astype(BF16), "wv": wv.astype(BF16),
        "dn_conv_w": p["dn_conv_w"], "dn_alog_row": gate_row(p["dn_a_log"]),
        "dn_dtb_row": gate_row(p["dn_dt_bias"]), "dn_out_g": row(jnp.tile(p["dn_out_g"], (1, DN_HEADS))),
        "w_gate": p["ffn_w_gate"].astype(BF16), "w_up": p["ffn_w_up"].astype(BF16),
        "w_down": p["ffn_w_down"].astype(BF16),
    }


def kernel(x_prompt, x_sample, cache_gqa_k, cache_gqa_v, cache_na_k, cache_na_v, state_dn,
           cache_mla_ckv, cache_mla_krope, c, c_ctx, norm1_g, norm2_g, w_mod, b_mod, w_in, w_out,
           gqa_qn_g, gqa_kn_g, na_bias, dn_conv_w, dn_a_log, dn_dt_bias, dn_out_g, mla_qn_g,
           mla_wq_up, mla_kvn_g, mla_wkv_up, ffn_w_gate, ffn_w_up, ffn_w_down, final_g):
    p = {"norm1_g": norm1_g, "norm2_g": norm2_g, "w_in": w_in, "w_out": w_out, "gqa_qn_g": gqa_qn_g,
         "gqa_kn_g": gqa_kn_g, "dn_conv_w": dn_conv_w, "dn_a_log": dn_a_log, "dn_dt_bias": dn_dt_bias,
         "dn_out_g": dn_out_g, "mla_qn_g": mla_qn_g, "mla_wq_up": mla_wq_up, "mla_kvn_g": mla_kvn_g,
         "mla_wkv_up": mla_wkv_up, "ffn_w_gate": ffn_w_gate, "ffn_w_up": ffn_w_up, "ffn_w_down": ffn_w_down}
    nb_ctx, seq_ctx, _ = x_prompt.shape
    nb_lat, seq_lat, _ = x_sample.shape
    past = cache_gqa_k.shape[2]
    fg = final_g[None]

    cond = jnp.concatenate([c_ctx[None], c, jnp.zeros((16 - 1 - nb_lat, D_MODEL), F32)], axis=0)
    mods = _modulation(cond, w_mod, b_mod).reshape(DEPTH, 16, MOD_CHUNKS, D_MODEL)
    lw = _stacked_weights(p)

    x = x_prompt
    ctx_out = []
    for l in range(DEPTH):
        o_a, ka, va, o_b, kb, vb, zc, o_d, ckv, zm, krr = _inproj(x, mods, False, lw, l, None)
        o_c, s_dn = _deltanet(zc, zm, lw, l)
        x = _outffn((o_a, o_b, o_c, o_d), x, mods, False, lw, l, fg, l == DEPTH - 1)
        ctx_out.append((ka, va, kb, vb, s_dn, ckv, krr[:, :, :MLA_ROPE]))
    y_prompt = x
    new = [jnp.stack([s[i] for s in ctx_out], axis=1) for i in range(7)]
    for i in range(4):
        t = new[i].reshape(nb_ctx, DEPTH, -1, HEAD_DIM, seq_ctx)
        new[i] = jnp.transpose(t, (0, 1, 4, 2, 3))

    ropes = _rope_tables(seq_lat)
    keys_t = lambda c: jnp.transpose(c, (0, 1, 3, 4, 2)).reshape(nb_lat, DEPTH, -1, past)
    ck_a, cv_a, ck_b, cv_b = keys_t(cache_gqa_k), keys_t(cache_gqa_v), keys_t(cache_na_k), keys_t(cache_na_v)
    c_kr = jnp.pad(cache_mla_krope, ((0, 0), (0, 0), (0, 0), (0, LANES - MLA_ROPE)))
    bias_blocks = _na_bias_blocks(na_bias)
    x = x_sample
    for l in range(DEPTH):
        qa, ka, va, qb, kb, vb, zc, qd, ckv, zm, krr = _inproj(x, mods, True, lw, l, ropes)
        o_a = _attn_pair(qa, ka, va, (0, 0), ck_a, cv_a, l, name="gqa_lat")
        o_b = _na_latent(qb, kb, vb, ck_b, cv_b, l, bias_blocks)
        o_c, _ = _deltanet(zc, zm, lw, l, state=state_dn)
        o_d = _mla(qd, ckv, krr, lw, l, cache=(cache_mla_ckv, c_kr), name="mla_lat")
        x = _outffn((o_a, o_b, o_c, o_d), x, mods, True, lw, l, fg, l == DEPTH - 1)
    y_sample = x

    return (y_prompt, y_sample, *new)
```

```python
import functools

import numpy as np
import jax
import jax.numpy as jnp
from jax import lax
from jax.experimental import pallas as pl
from jax.experimental.pallas import tpu as pltpu

F32 = jnp.float32
BF16 = jnp.bfloat16

D_MODEL = 1024
DEPTH = 2
GRID_W = 64
HEAD_DIM = 64
ROPE_BASE = 10000.0
NEG_INF = -1e30
MOD_CHUNKS = 6
GQA_HEADS, GQA_KV_HEADS = 4, 2
NA_HEADS, NA_KH, NA_KW = 4, 8, 16
DN_HEADS, DN_DK, DN_DV, DN_CONV, DN_CHUNK = 4, 64, 64, 4, 64
DN_QKV = DN_HEADS * (2 * DN_DK + DN_DV)
DN_GATES = 2 * DN_HEADS
MLA_HEADS, MLA_Q_LORA, MLA_KV_LORA, MLA_NOPE, MLA_ROPE, MLA_V = 4, 256, 128, 64, 32, 64
MLA_SCALE = (MLA_NOPE + MLA_ROPE) ** -0.5
D_FF = -(-8 * D_MODEL // (3 * 256)) * 256
EPS = 1e-6
LOG2E = 1.4426950408889634

LANES = 128
ROW_TILE = 256
Q_TILE = 256
NA_BAND = 768
TILES_PER_STEP = 2
ATTN_Q_TILE = 512
HEAD_LOOKAHEAD = 1
DN_VMEM_BYTES_PER_ROW = 12288 + 9216
DN_VMEM_BUDGET = 46 * 1024 * 1024
DN_ROWS = 256
PREP_CHUNKS = 4
IN_COLS = 2736
IN_TAIL_SKIP = 512 - (2 * 8 + 256 + 128 + 32)
VMEM_LIMIT = 56 * 1024 * 1024

_NT = (((1,), (1,)), ((), ()))
_TN = (((0,), (0,)), ((), ()))


def _cparams(n_axes):
    return pltpu.CompilerParams(dimension_semantics=("arbitrary",) * n_axes,
                                vmem_limit_bytes=VMEM_LIMIT)


def _lane(shape):
    return lax.broadcasted_iota(jnp.int32, shape, len(shape) - 1)


def _silu(x):
    return x / (1.0 + jnp.exp(-x))


def _rms_full(x, g):
    return x * lax.rsqrt(jnp.mean(x * x, axis=-1, keepdims=True) + EPS) * g


def _seg64_sum(x):
    lo = _lane(x.shape) < HEAD_DIM
    s_lo = jnp.sum(jnp.where(lo, x, 0.0), axis=-1, keepdims=True)
    s_hi = jnp.sum(jnp.where(lo, 0.0, x), axis=-1, keepdims=True)
    return jnp.where(lo, s_lo, s_hi)


def _head_rms(x, g):
    parts = []
    for p in range(x.shape[-1] // LANES):
        xp = x[:, p * LANES:(p + 1) * LANES]
        ms = _seg64_sum(xp * xp) * (1.0 / HEAD_DIM)
        parts.append(xp * lax.rsqrt(ms + EPS))
    y = parts[0] if len(parts) == 1 else jnp.concatenate(parts, axis=-1)
    return y * g


def _head_l2(x):
    parts = []
    for p in range(x.shape[-1] // LANES):
        xp = x[:, p * LANES:(p + 1) * LANES]
        parts.append(xp * lax.rsqrt(_seg64_sum(xp * xp) + EPS))
    return parts[0] if len(parts) == 1 else jnp.concatenate(parts, axis=-1)


def _rope(x, cos, sin, half):
    first = (_lane(x.shape) & (2 * half - 1)) < half
    rot = jnp.where(first, pltpu.roll(x, LANES - half, 1), pltpu.roll(x, half, 1))
    return x * cos + rot * sin


def _softmax_parts(scores):
    m = jnp.max(scores[0], axis=-1, keepdims=True)
    for s in scores[1:]:
        m = jnp.maximum(m, jnp.max(s, axis=-1, keepdims=True))
    es = [jnp.exp2(s - m) for s in scores]
    l = jnp.sum(es[0], axis=-1, keepdims=True)
    for e in es[1:]:
        l = l + jnp.sum(e, axis=-1, keepdims=True)
    return es, 1.0 / l


def _bdot(a, b):
    return jnp.dot(a, b, preferred_element_type=F32)


def _pipelined_heads(n_heads, scores, attend):
    outs = []
    queue = [scores(hd) for hd in range(min(HEAD_LOOKAHEAD, n_heads))]
    for hd in range(n_heads):
        if hd + HEAD_LOOKAHEAD < n_heads:
            queue.append(scores(hd + HEAD_LOOKAHEAD))
        es, rl = _softmax_parts(queue.pop(0))
        outs.append(attend(hd, es) * rl)
    return outs


def _pair_attention(q_blocks, sources, qmap):
    lo = _lane(q_blocks[0].shape) < HEAD_DIM

    def scores(hd):
        p, half = divmod(hd, 2)
        qm = jnp.where(lo if half == 0 else jnp.logical_not(lo), q_blocks[p], 0.0).astype(BF16)
        return [_bdot(qm, k_block(qmap[p])) if transposed
                else lax.dot_general(qm, k_block(qmap[p]), _NT, preferred_element_type=F32)
                for k_block, _, transposed in sources]

    def attend(hd, es):
        kv = qmap[hd // 2]
        parts = [lax.dot_general(e.astype(BF16), v_block(kv), _NT, preferred_element_type=F32) if transposed
                 else _bdot(e.astype(BF16), v_block(kv)) for e, (_, v_block, transposed) in zip(es, sources)]
        return functools.reduce(lambda a, b: a + b, parts)

    outs = _pipelined_heads(2 * len(qmap), scores, attend)
    return [jnp.where(lo, outs[2 * p], outs[2 * p + 1]) for p in range(len(qmap))]


def _mla_attention(q_heads, k_head, v_block):
    lo = _lane(q_heads[0].shape) < MLA_V

    def scores(hd):
        return [lax.dot_general(q_heads[hd], k_head(hd), _NT, preferred_element_type=F32)]

    def attend(hd, es):
        return _bdot(es[0].astype(BF16), v_block(hd // 2))

    outs = _pipelined_heads(MLA_HEADS, scores, attend)
    return [jnp.where(lo, outs[2 * p], outs[2 * p + 1]) for p in range(MLA_HEADS // 2)]


def _mod_kernel(c_ref, w_ref, b_ref, o_ref):
    s = _silu(c_ref[...]).astype(BF16)
    o_ref[...] = _bdot(s, w_ref[...].astype(BF16)) + b_ref[...]


def _modulation(cond, w_mod, b_mod):
    n = MOD_CHUNKS * D_MODEL
    tn = 1536
    return pl.pallas_call(
        _mod_kernel,
        grid=(DEPTH, n // tn),
        in_specs=[pl.BlockSpec((16, D_MODEL), lambda l, j: (0, 0)),
                  pl.BlockSpec((None, D_MODEL, tn), lambda l, j: (l, 0, j)),
                  pl.BlockSpec((None, 1, tn), lambda l, j: (l, 0, j))],
        out_specs=pl.BlockSpec((None, 16, tn), lambda l, j: (l, 0, j)),
        out_shape=jax.ShapeDtypeStruct((DEPTH, 16, n), F32),
        compiler_params=_cparams(2),
        name="modulation",
    )(cond, w_mod, b_mod.reshape(DEPTH, 1, n))


_IN_OUT_WIDTHS = (256, 128, 128, 256, 256, 256, 1024, 512, 128, 128, 128)
_IN_OUT_WIDTHS_CTX = (256, 128, 128, 256, 256, 256, 1024, 256, 128, 128, 128)
_IN_OUT_DTYPES_LAT = (BF16, BF16, BF16, BF16, BF16, BF16, F32, BF16, BF16, F32, BF16)
_IN_OUT_DTYPES_CTX = (BF16, F32, F32, BF16, F32, F32, F32, BF16, F32, F32, F32)


def _inproj_kernel(positioned, *refs):
    (x_ref, mod_ref, g1_ref, w_ref, qng_ref, kng_ref, mqg_ref, wq_ref, mkg_ref) = refs[:9]
    if positioned:
        cos64_ref, sin64_ref, cosm_ref, sinm_ref, coskr_ref, sinkr_ref = refs[9:15]
        n_in = 15
    else:
        wk_ref, wv_ref = refs[9:11]
        n_in = 11
    (qa_ref, ka_ref, va_ref, qb_ref, kb_ref, vb_ref, zc_ref, qd_ref, ckv_ref, zm_ref,
     krr_ref) = refs[n_in:]
    tiles = range(TILES_PER_STEP)
    tm = x_ref.shape[1]
    lane = _lane((tm, LANES))
    lo = lane < HEAD_DIM

    m = mod_ref[...]
    hb = jnp.concatenate([(_rms_full(x_ref[t], g1_ref[...]) * (1.0 + m[1:2]) + m[0:1]).astype(BF16)
                          for t in tiles], axis=0)

    def project(c0, c1):
        z = lax.dot_general(hb, w_ref[c0:c1, :], _NT, preferred_element_type=F32)
        return [z[t * tm:(t + 1) * tm] for t in tiles]

    za = project(0, 512)
    zb = project(512, 1280)

    for t in tiles:
        q = _head_rms(za[t][:, 0:256], qng_ref[...])
        k = _head_rms(za[t][:, 256:384], kng_ref[...])
        q0, q1 = q[:, 0:128], q[:, 128:256]
        q0, q1 = jnp.where(lo, q0, pltpu.roll(q1, HEAD_DIM, 1)), jnp.where(lo, pltpu.roll(q0, HEAD_DIM, 1), q1)
        if positioned:
            cos, sin = cos64_ref[t], sin64_ref[t]
            q0, q1 = _rope(q0, cos, sin, 16), _rope(q1, cos, sin, 16)
            k = _rope(k, cos, sin, 16)
        q0, q1 = q0 * (HEAD_DIM ** -0.5 * LOG2E), q1 * (HEAD_DIM ** -0.5 * LOG2E)
        v = za[t][:, 384:512]
        if not positioned:
            kb16, vb16 = k.astype(BF16), v.astype(BF16)
            q0, q1 = _pair_attention([q0, q1], [(lambda i: kb16, lambda i: vb16, False)], (0, 0))
        qa_ref[t] = jnp.concatenate([q0, q1], axis=-1).astype(qa_ref.dtype)
        ka_ref[t] = k.astype(ka_ref.dtype) if positioned else k.T
        va_ref[t] = v.astype(va_ref.dtype) if positioned else v.T

    zd = project(IN_COLS - 4 * LANES, IN_COLS)
    zc = project(1280, 2304)

    for t in tiles:
        q = zb[t][:, 0:256] * (HEAD_DIM ** -0.5 * LOG2E)
        k, v = zb[t][:, 256:512], zb[t][:, 512:768]
        if not positioned:
            kb16, vb16 = k.astype(BF16), v.astype(BF16)
            q = jnp.concatenate(_pair_attention(
                [q[:, 0:LANES], q[:, LANES:]],
                [(lambda i, kb16=kb16: kb16[:, i * LANES:(i + 1) * LANES],
                  lambda i, vb16=vb16: vb16[:, i * LANES:(i + 1) * LANES], False)], (0, 1)), axis=-1)
        qb_ref[t] = q.astype(qb_ref.dtype)
        kb_ref[t] = k.astype(kb_ref.dtype) if positioned else k.T
        vb_ref[t] = v.astype(vb_ref.dtype) if positioned else v.T

    lead = IN_TAIL_SKIP + 2 * DN_GATES
    shifted = []
    for t in tiles:
        rolled = [pltpu.roll(zd[t][:, j * LANES:(j + 1) * LANES], LANES - lead, 1) for j in range(4)]
        keep = lane < LANES - lead
        shifted.append([jnp.where(keep, rolled[j], rolled[(j + 1) % 4]) for j in range(4)])

    cq = jnp.concatenate([_rms_full(jnp.concatenate(shifted[t][0:2], axis=-1), mqg_ref[...]).astype(BF16)
                          for t in tiles], axis=0)
    qm = _bdot(cq, wq_ref[...])
    for t in tiles:
        zc_ref[t] = zc[t]
        zm_ref[t] = zd[t][:, 0:LANES]
        ckv = _rms_full(shifted[t][2], mkg_ref[...])
        ckv_ref[t] = ckv.astype(ckv_ref.dtype)
        kr = jnp.where(lane < MLA_ROPE, shifted[t][3], 0.0)
        q = qm[t * tm:(t + 1) * tm]
        if positioned:
            kr = _rope(kr, coskr_ref[t], sinkr_ref[t], 8)
            cm, sm = cosm_ref[t], sinm_ref[t]
            q = jnp.concatenate([_rope(q[:, i * LANES:(i + 1) * LANES], cm, sm, 8)
                                 for i in range(MLA_HEADS)], axis=-1)
        krr_ref[t] = kr.astype(krr_ref.dtype)
        q = q * (MLA_SCALE * LOG2E)
        if not positioned:
            c16 = ckv.astype(BF16)
            k16 = _bdot(jnp.concatenate([c16, kr.astype(BF16)], axis=-1), wk_ref[...]).astype(BF16)
            v16 = _bdot(c16, wv_ref[...]).astype(BF16)
            q = jnp.concatenate(_mla_attention(
                [q[:, i * LANES:(i + 1) * LANES].astype(BF16) for i in range(MLA_HEADS)],
                lambda i, k16=k16: k16[:, i * LANES:(i + 1) * LANES],
                lambda i, v16=v16: v16[:, i * LANES:(i + 1) * LANES]), axis=-1)
        qd_ref[t] = q.astype(qd_ref.dtype)


def _inproj(x, mods, per_batch_mods, lw, layer, ropes):
    b, l, _ = x.shape
    tm, ts = ROW_TILE, TILES_PER_STEP
    n_tiles = b * l // tm
    tiles_per_seq = l // tm
    positioned = ropes is not None
    row = lambda w: pl.BlockSpec((ts, tm, w), lambda i: (i, 0, 0))
    const = lambda shape: pl.BlockSpec((None,) + shape, lambda i: (layer,) + (0,) * len(shape))
    assert not per_batch_mods or tiles_per_seq % ts == 0
    mod_spec = pl.BlockSpec((None, None, MOD_CHUNKS, D_MODEL),
                            (lambda i: (layer, 1 + i * ts // tiles_per_seq, 0, 0)) if per_batch_mods
                            else (lambda i: (layer, 0, 0, 0)))
    in_specs = [row(D_MODEL), mod_spec, const((1, D_MODEL)), const((IN_COLS, D_MODEL)),
                const((1, 256)), const((1, 128)), const((1, MLA_Q_LORA)),
                const((MLA_Q_LORA, 4 * LANES)), const((1, MLA_KV_LORA))]
    args = [x.reshape(n_tiles, tm, D_MODEL), mods, lw["norm1_g"], lw["w_in"], lw["qn_g"], lw["kn_g"],
            lw["mla_qn_g"], lw["wq"], lw["mla_kvn_g"]]
    if positioned:
        steps_per_seq = tiles_per_seq // ts
        in_specs += [pl.BlockSpec((ts, tm, LANES), lambda i: (i % steps_per_seq, 0, 0))] * 6
        args += [r.reshape(tiles_per_seq, tm, LANES) for r in ropes]
        widths, dtypes = _IN_OUT_WIDTHS, _IN_OUT_DTYPES_LAT
    else:
        assert tiles_per_seq == 1
        in_specs += [const((2 * LANES, 4 * LANES)), const((LANES, 2 * LANES))]
        args += [lw["wk"], lw["wv"]]
        widths, dtypes = _IN_OUT_WIDTHS_CTX, _IN_OUT_DTYPES_CTX
    flipped = () if positioned else (1, 2, 4, 5)
    shapes = [(n_tiles, w, tm) if i in flipped else (n_tiles, tm, w) for i, w in enumerate(widths)]
    outs = pl.pallas_call(
        functools.partial(_inproj_kernel, positioned),
        grid=(n_tiles // ts,),
        in_specs=in_specs,
        out_specs=[pl.BlockSpec((ts,) + shp[1:], lambda i: (i, 0, 0)) for shp in shapes],
        out_shape=[jax.ShapeDtypeStruct(shp, dt) for shp, dt in zip(shapes, dtypes)],
        compiler_params=_cparams(1),
        name="inproj_lat" if positioned else "inproj_attn_ctx",
    )(*args)
    return [o if i in flipped else o.reshape(b, l, o.shape[-1]) for i, o in enumerate(outs)]


def _attn_pair_kernel(qmap, q_ref, kc_ref, vc_ref, k_ref, v_ref, o_ref, kcbuf, vcbuf, kbuf, vbuf):
    @pl.when(pl.program_id(1) == 0)
    def _():
        kcbuf[...] = kc_ref[...].astype(BF16)
        vcbuf[...] = vc_ref[...].astype(BF16)
        kbuf[...] = k_ref[...].astype(BF16)
        vbuf[...] = v_ref[...].astype(BF16)

    rows = lambda buf: (lambda i: buf[i * LANES:(i + 1) * LANES, :])
    cols = lambda buf: (lambda i: buf[:, i * LANES:(i + 1) * LANES])
    q_blocks = [q_ref[:, p * LANES:(p + 1) * LANES].astype(F32) for p in range(len(qmap))]
    outs = _pair_attention(q_blocks, [(rows(kcbuf), rows(vcbuf), True), (cols(kbuf), cols(vbuf), False)], qmap)
    for p, o in enumerate(outs):
        o_ref[:, p * LANES:(p + 1) * LANES] = o.astype(o_ref.dtype)


def _attn_pair(q, k, v, qmap, kc_t, vc_t, layer, name):
    b, lq, wq = q.shape
    ls, wk = k.shape[1], k.shape[2]
    lc = kc_t.shape[3]
    tq = min(ATTN_Q_TILE, lq)
    cspec = pl.BlockSpec((None, None, wk, lc), lambda bi, i: (bi, layer, 0, 0))
    sspec = pl.BlockSpec((None, ls, wk), lambda bi, i: (bi, 0, 0))
    return pl.pallas_call(
        functools.partial(_attn_pair_kernel, qmap),
        grid=(b, lq // tq),
        in_specs=[pl.BlockSpec((None, tq, wq), lambda bi, i: (bi, i, 0)), cspec, cspec, sspec, sspec],
        out_specs=pl.BlockSpec((None, tq, wq), lambda bi, i: (bi, i, 0)),
        out_shape=jax.ShapeDtypeStruct((b, lq, wq), BF16),
        scratch_shapes=[pltpu.VMEM((wk, lc), BF16), pltpu.VMEM((wk, lc), BF16),
                        pltpu.VMEM((ls, wk), BF16), pltpu.VMEM((ls, wk), BF16)],
        compiler_params=_cparams(2),
        name=name,
    )(q, kc_t, vc_t, k, v)


def _mla_kernel(q_ref, ckvc_ref, krc_ref, ckv_ref, kr_ref, wk_ref, wv_ref, o_ref, kbuf, vbuf):
    @pl.when(pl.program_id(1) == 0)
    def _():
        def expand(c_ref, r_ref, r0, r1):
            c = c_ref[...].astype(BF16)
            ckr = jnp.concatenate([c, r_ref[...].astype(BF16)], axis=-1)
            kbuf[r0:r1, :] = _bdot(ckr, wk_ref[...]).astype(BF16)
            vbuf[r0:r1, :] = _bdot(c, wv_ref[...]).astype(BF16)
        off = ckvc_ref.shape[0]
        expand(ckvc_ref, krc_ref, 0, off)
        expand(ckv_ref, kr_ref, off, kbuf.shape[0])

    outs = _mla_attention([q_ref[:, hd * LANES:(hd + 1) * LANES].astype(BF16) for hd in range(MLA_HEADS)],
                          lambda i: kbuf[:, i * LANES:(i + 1) * LANES],
                          lambda i: vbuf[:, i * LANES:(i + 1) * LANES])
    for p, o in enumerate(outs):
        o_ref[:, p * LANES:(p + 1) * LANES] = o.astype(o_ref.dtype)


def _mla(q, ckv, kr, lw, layer, ckv_cache, kr_cache):
    b, lq, wq = q.shape
    ls, lc = ckv.shape[1], ckv_cache.shape[2]
    tq = min(ATTN_Q_TILE, lq)
    cspec = pl.BlockSpec((None, None, lc, LANES), lambda bi, i: (bi, layer, 0, 0))
    sspec = pl.BlockSpec((None, ls, LANES), lambda bi, i: (bi, 0, 0))
    return pl.pallas_call(
        _mla_kernel,
        grid=(b, lq // tq),
        in_specs=[pl.BlockSpec((None, tq, wq), lambda bi, i: (bi, i, 0)), cspec, cspec, sspec, sspec,
                  pl.BlockSpec((None, 2 * LANES, 4 * LANES), lambda bi, i: (layer, 0, 0)),
                  pl.BlockSpec((None, LANES, 2 * LANES), lambda bi, i: (layer, 0, 0))],
        out_specs=pl.BlockSpec((None, tq, 2 * LANES), lambda bi, i: (bi, i, 0)),
        out_shape=jax.ShapeDtypeStruct((b, lq, 2 * LANES), BF16),
        scratch_shapes=[pltpu.VMEM((lc + ls, 4 * LANES), BF16), pltpu.VMEM((lc + ls, 2 * LANES), BF16)],
        compiler_params=_cparams(2),
        name="mla_lat",
    )(q, ckv_cache, kr_cache, ckv, kr, lw["wk"], lw["wv"])


def _na_kernel(q_ref, k_ref, v_ref, kc_ref, vc_ref, bias_ref, o_ref):
    start = pl.multiple_of(pl.program_id(0) * 256, 256)
    kband = k_ref[pl.ds(start, NA_BAND), :].astype(BF16)
    vband = v_ref[pl.ds(start, NA_BAND), :].astype(BF16)
    kc = kc_ref[...].astype(BF16)
    vc = vc_ref[...].astype(BF16)
    tq = q_ref.shape[0]
    lo = _lane((tq, LANES)) < HEAD_DIM

    def scores(hd):
        p, half = divmod(hd, 2)
        sl = slice(p * LANES, (p + 1) * LANES)
        qm = jnp.where(lo if half == 0 else jnp.logical_not(lo), q_ref[:, sl].astype(F32), 0.0).astype(BF16)
        bias = jnp.concatenate([bias_ref[0, hd], bias_ref[1, hd]], axis=0)
        s_loc = lax.dot_general(qm, kband[:, sl], _NT, preferred_element_type=F32) + bias
        return [s_loc, _bdot(qm, kc[sl, :])]

    def attend(hd, es):
        sl = slice((hd // 2) * LANES, (hd // 2 + 1) * LANES)
        return (_bdot(es[0].astype(BF16), vband[:, sl])
                + lax.dot_general(es[1].astype(BF16), vc[sl, :], _NT, preferred_element_type=F32))

    outs = _pipelined_heads(NA_HEADS, scores, attend)
    for p in range(NA_HEADS // 2):
        o_ref[:, p * LANES:(p + 1) * LANES] = jnp.where(lo, outs[2 * p], outs[2 * p + 1]).astype(o_ref.dtype)


def _na_latent(q, k, v, kc_t, vc_t, layer, bias_blocks):
    b, n, w = q.shape
    lc = kc_t.shape[3]
    nq = n // (2 * Q_TILE)
    full = pl.BlockSpec((None, n, w), lambda j, bi: (bi, 0, 0))
    cspec = pl.BlockSpec((None, None, w, lc), lambda j, bi: (bi, layer, 0, 0))
    return pl.pallas_call(
        _na_kernel,
        grid=(nq, b),
        in_specs=[pl.BlockSpec((None, 2 * Q_TILE, w), lambda j, bi: (bi, j, 0)), full, full, cspec, cspec,
                  pl.BlockSpec((None, 2, NA_HEADS, Q_TILE, NA_BAND), lambda j, bi: (layer, j, 0, 0, 0))],
        out_specs=pl.BlockSpec((None, 2 * Q_TILE, w), lambda j, bi: (bi, j, 0)),
        out_shape=jax.ShapeDtypeStruct((b, n, w), BF16),
        compiler_params=_cparams(2),
        name="na_latent",
    )(q, k, v, kc_t, vc_t, bias_blocks)


NA_GRID_ROWS = 16
NA_BAND_ROW0 = (0, 0, 4, 4)


def _na_bias_kernel(b_ref, o_ref, tp_s):
    hd = pl.program_id(0)
    n_dr, n_dc = 2 * NA_KH - 1, 2 * NA_KW - 1
    shape = (GRID_W, LANES)
    c = lax.broadcasted_iota(jnp.int32, shape, 0)
    lane = _lane(shape)
    kc = lane & (GRID_W - 1)
    lo = lane < GRID_W
    diff = kc - c + (NA_KW - 1)
    c0 = jnp.clip(c - NA_KW // 2, 0, GRID_W - NA_KW)
    col_ok = (kc >= c0) & (kc < c0 + NA_KW)
    neg = jnp.full(shape, NEG_INF, F32)
    for dr0 in range(-1, n_dr):
        acc = neg
        for d in range(n_dc):
            v_lo = b_ref[hd * n_dr + dr0, d] if dr0 >= 0 else 0.0
            v_hi = b_ref[hd * n_dr + dr0 + 1, d] if dr0 + 1 < n_dr else 0.0
            acc = jnp.where(diff == d, jnp.where(lo, v_lo, v_hi), acc)
        tp_s[dr0 + 1] = jnp.where(col_ok, acc * LOG2E, NEG_INF)
    for j in range(NA_GRID_ROWS // 4):
        for ri in range(4):
            r = 4 * j + ri
            r0 = min(max(r - NA_KH // 2, 0), NA_GRID_ROWS - NA_KH)
            for kp in range(NA_BAND // LANES):
                kr = NA_BAND_ROW0[j] + 2 * kp
                ok_lo, ok_hi = r0 <= kr < r0 + NA_KH, r0 <= kr + 1 < r0 + NA_KH
                dr0 = kr - r + (NA_KH - 1)
                if ok_lo and ok_hi:
                    t = tp_s[dr0 + 1]
                elif ok_lo:
                    t = jnp.where(lo, tp_s[dr0 + 1], NEG_INF)
                elif ok_hi:
                    t = jnp.where(lo, NEG_INF, tp_s[dr0 + 1])
                else:
                    t = neg
                o_ref[j, ri * GRID_W:(ri + 1) * GRID_W, kp * LANES:(kp + 1) * LANES] = t


def _na_bias_blocks(bias):
    nq = NA_GRID_ROWS // 4
    return pl.pallas_call(
        _na_bias_kernel,
        grid=(DEPTH * NA_HEADS,),
        in_specs=[pl.BlockSpec(memory_space=pltpu.SMEM)],
        out_specs=pl.BlockSpec((None, nq, None, Q_TILE, NA_BAND),
                               lambda i: (i // NA_HEADS, 0, i % NA_HEADS, 0, 0)),
        out_shape=jax.ShapeDtypeStruct((DEPTH, nq, NA_HEADS, Q_TILE, NA_BAND), F32),
        scratch_shapes=[pltpu.VMEM((2 * NA_KH, GRID_W, LANES), F32)],
        compiler_params=_cparams(1),
        name="na_bias",
    )(bias.reshape(DEPTH * NA_HEADS * (2 * NA_KH - 1), 2 * NA_KW - 1))


def _widen(cols, n):
    blk = _lane((n, DN_HEADS * DN_DV)) >> 6
    return jnp.where(blk == 0, cols[0], jnp.where(blk == 1, cols[1], jnp.where(blk == 2, cols[2], cols[3])))


def _deltanet_kernel(seq, has_state, *refs):
    if has_state:
        zc_all, zm_all, s0_ref, cw_ref, alog_ref, dtb_ref, og_ref, o_ref = refs[:8]
    else:
        zc_all, zm_all, cw_ref, alog_ref, dtb_ref, og_ref, o_ref, sfin_ref = refs[:8]
    scratch = refs[8:]
    n_seqs = zc_all.shape[0]
    g_all, o_all, c_all, mp_all = scratch[4:8]
    n_chunks = seq // DN_CHUNK
    wide = DN_HEADS * DN_DV

    a_off, b_off = IN_TAIL_SKIP, IN_TAIL_SKIP + DN_GATES
    ri = lax.broadcasted_iota(jnp.int32, (DN_ROWS, DN_ROWS), 0)
    ci = lax.broadcasted_iota(jnp.int32, (DN_ROWS, DN_ROWS), 1)
    same = (ri >> 6) == (ci >> 6)
    tri = [jnp.tile(jnp.where(same & ((ci <= ri) if d == 0 else (ci >= ri)), 1.0, 0.0).astype(BF16), (1, 3))
           for d in range(2)]
    n_blocks = seq // DN_ROWS

    def terms3(x):
        hi = x.astype(BF16)
        rest = x - hi.astype(F32)
        mid = rest.astype(BF16)
        return hi, mid, (rest - mid.astype(F32)).astype(BF16)

    def split3(x):
        return jnp.concatenate(terms3(x), axis=0)

    def preprocess(views, rb, carry):
        zc_ref, zm_ref, q_s, k_s, v_s, b_s, g_s = views[:7]
        r0 = pl.multiple_of(rb * DN_ROWS, DN_ROWS)
        rows = pl.ds(r0, DN_ROWS)
        before = pl.ds(pl.multiple_of(jnp.maximum(r0 - 8, 0), 8), 8)
        after = pl.ds(pl.multiple_of(jnp.minimum(r0 + DN_ROWS, seq - 8), 8), 8)
        for part, dst in enumerate((q_s, k_s, v_s)):
            cs = slice(part * wide, (part + 1) * wide)
            head = jnp.where(rb > 0, zc_ref[before, cs], 0.0)
            tail = jnp.where(rb < n_blocks - 1, zc_ref[after, cs], 0.0)
            xe = jnp.concatenate([head, zc_ref[rows, cs], tail], axis=0)
            w = cw_ref[:, cs]
            y = (w[0:1] * xe[7:7 + DN_ROWS] + w[1:2] * xe[8:8 + DN_ROWS]
                 + w[2:3] * xe[9:9 + DN_ROWS] + w[3:4] * xe[10:10 + DN_ROWS])
            y = _silu(y)
            if part == 0:
                y = _head_l2(y) * (DN_DK ** -0.5)
            elif part == 1:
                y = _head_l2(y)
            dst[rows, :] = y
        zm = zm_ref[rows, :]
        xa = zm + dtb_ref[...]
        logd = -jnp.exp(alog_ref[...]) * (jnp.maximum(xa, 0.0) + jnp.log1p(jnp.exp(-jnp.abs(xa))))
        beta = 1.0 / (1.0 + jnp.exp(-zm))
        logd3 = split3(logd)
        for d in range(2):
            b_s[d, rows, :] = _widen([beta[:, b_off + 4 * d + hd:b_off + 4 * d + hd + 1]
                                      for hd in range(DN_HEADS)], DN_ROWS)
            g = _bdot(tri[d], logd3)
            g_s[d, rows, :] = _widen([g[:, a_off + 4 * d + hd:a_off + 4 * d + hd + 1]
                                      for hd in range(DN_HEADS)], DN_ROWS)
        return carry

    ii = lax.broadcasted_iota(jnp.int32, (DN_CHUNK, wide), 0)
    jj = _lane((DN_CHUNK, wide)) & (DN_CHUNK - 1)
    blk = _lane((DN_CHUNK, wide)) >> 6
    diag = ii == jj
    eye = jnp.where(diag, 1.0, 0.0)
    half_mask = [jnp.where((_lane((DN_CHUNK, LANES)) >> 6) == half, 1.0, 0.0).astype(BF16) for half in range(2)]
    zero_block = jnp.zeros((DN_CHUNK, LANES), BF16)

    def bd(z):
        zb = z.astype(BF16)
        rows = []
        for hd in range(DN_HEADS):
            col, half = divmod(hd, 2)
            kept = zb[:, col * LANES:(col + 1) * LANES] * half_mask[half]
            rows.append(jnp.concatenate([kept, zero_block] if col == 0 else [zero_block, kept], axis=1))
        return jnp.concatenate(rows, axis=0)

    def fold(gram):
        out = jnp.where(blk == 0, gram[0:DN_CHUNK], 0.0)
        for hd in range(1, DN_HEADS):
            out = out + jnp.where(blk == hd, gram[hd * DN_CHUNK:(hd + 1) * DN_CHUNK], 0.0)
        return out

    tri_masks = []
    for d in range(2):
        incl = (jj <= ii) if d == 0 else (jj >= ii)
        strict = (jj < ii) if d == 0 else (jj > ii)
        pair = [((ii >> (lvl + 1)) == (jj >> (lvl + 1)))
                & (((ii >> lvl) & 1) == (1 - d)) & (((jj >> lvl) & 1) == d) for lvl in range(6)]
        tri_masks.append((incl, strict, pair))

    def prepare(views, step, carry):
        q_s, k_s, v_s, b_s, g_s, o_s, c_s, mp_s = views[2:]
        chunks = [step * PREP_CHUNKS + i for i in range(PREP_CHUNKS)]
        rows = [pl.ds(pl.multiple_of(c * DN_CHUNK, DN_CHUNK), DN_CHUNK) for c in chunks]
        qkv = [(q_s[r, :], k_s[r, :], v_s[r, :]) for r in rows]
        inst = [(ci, d) for ci in range(PREP_CHUNKS) for d in range(2)]
        beta = {(ci, d): b_s[d, rows[ci], :] for ci, d in inst}
        kb = {(ci, d): qkv[ci][1] * beta[ci, d] for ci, d in inst}
        r = [lax.dot_general(jnp.concatenate([kb[ci, 0], kb[ci, 1], qkv[ci][0]], axis=0).astype(BF16),
                             bd(qkv[ci][1]), _NT, preferred_element_type=F32)
             for ci in range(PREP_CHUNKS)]
        g, a, qk, eg, t = {}, {}, {}, {}, {}
        for ci, d in inst:
            incl, strict, pair = tri_masks[d]
            g[ci, d] = g_s[d, rows[ci], :]
            g_row = jnp.sum(jnp.where(diag, g[ci, d], 0.0), axis=0, keepdims=True)
            dm = jnp.where(incl, jnp.exp(jnp.where(incl, g[ci, d] - g_row, 0.0)), 0.0)
            a[ci, d] = jnp.where(strict, r[ci][d * DN_CHUNK:(d + 1) * DN_CHUNK] * dm, 0.0)
            qk[ci, d] = r[ci][2 * DN_CHUNK:] * dm
            eg[ci, d] = jnp.exp(g[ci, d])
            t[ci, d] = eye - jnp.where(pair[0], a[ci, d], 0.0)
        for lvl in range(1, 6):
            te = {i: _bdot(t[i].astype(BF16), bd(jnp.where(tri_masks[i[1]][2][lvl], a[i], 0.0))) for i in inst}
            t = {i: t[i] - _bdot(te[i].astype(BF16), bd(t[i])) for i in inst}
        nb = {i: jnp.where(diag, 0.0, t[i]).astype(BF16) for i in inst}
        rhs_u = {(ci, d): qkv[ci][2] * beta[ci, d] for ci, d in inst}
        rhs_w = {i: kb[i] * eg[i] for i in inst}
        q2 = {i: qk[i] + _bdot(qk[i].astype(BF16), bd(nb[i])) for i in inst}
        both = {i: jnp.concatenate([nb[i], q2[i].astype(BF16)], axis=0) for i in inst}
        ru = {i: _bdot(both[i], bd(rhs_u[i])) for i in inst}
        rw = {i: _bdot(both[i], bd(rhs_w[i])) for i in inst}
        u = {i: rhs_u[i] + ru[i][0:DN_CHUNK] for i in inst}
        w = {i: rhs_w[i] + rw[i][0:DN_CHUNK] for i in inst}
        kd = {}
        for ci, d in inst:
            g_last = g[ci, d][DN_CHUNK - 1:DN_CHUNK] if d == 0 else g[ci, d][0:1]
            kd[ci, d] = (qkv[ci][1] * jnp.exp(g_last - g[ci, d])).astype(BF16)
        p = {(ci, d): qkv[ci][0] * eg[ci, d] - rw[ci, d][DN_CHUNK:] for ci, d in inst}
        o0 = {i: ru[i][DN_CHUNK:] for i in inst}
        m = {i: fold(lax.dot_general(kd[i], w[i].astype(BF16), _TN, preferred_element_type=F32)) for i in inst}
        cc = {i: fold(lax.dot_general(kd[i], u[i].astype(BF16), _TN, preferred_element_type=F32)) for i in inst}
        for ci, d in inst:
            mrow = pl.ds(pl.multiple_of(chunks[ci] * (2 * DN_CHUNK), 2 * DN_CHUNK), 2 * DN_CHUNK)
            c_s[d, rows[ci], :] = cc[ci, d]
            mp_s[d, mrow, :] = jnp.concatenate([m[ci, d], p[ci, d]], axis=0).astype(BF16)
        for ci in range(PREP_CHUNKS):
            o_s[rows[ci], :] = o0[ci, 0] + o0[ci, 1]
        return carry

    for sq in range(n_seqs):
        views = (zc_all.at[sq], zm_all.at[sq]) + tuple(ref.at[sq] for ref in scratch)
        lax.fori_loop(0, n_blocks, functools.partial(preprocess, views), 0)
        lax.fori_loop(0, n_chunks // PREP_CHUNKS, functools.partial(prepare, views), 0)

    chains = [(sq, d) for sq in range(n_seqs) for d in range(2)]

    def scan(i, states):
        new = []
        for (sq, d), state in zip(chains, states):
            c = i if d == 0 else n_chunks - 1 - i
            r0 = pl.multiple_of(c * DN_CHUNK, DN_CHUNK)
            rows = pl.ds(r0, DN_CHUNK)
            mrow = pl.ds(pl.multiple_of(c * (2 * DN_CHUNK), 2 * DN_CHUNK), 2 * DN_CHUNK)
            edge = pl.ds(pl.multiple_of(r0 + (DN_CHUNK - 8 if d == 0 else 0), 8), 8)
            g_edge = g_all[sq, d, edge, :]
            g_last = g_edge[7:8] if d == 0 else g_edge[0:1]
            res = _bdot(mp_all[sq, d, mrow, :], bd(state))
            o_all[sq, rows, :] = o_all[sq, rows, :] + res[DN_CHUNK:]
            new.append(state * jnp.exp(g_last) - res[0:DN_CHUNK] + c_all[sq, d, rows, :])
        return tuple(new)

    place = [jnp.where(diag & (blk == hd), 1.0, 0.0).astype(BF16) for hd in range(DN_HEADS)]

    def to_wide(heads):
        parts = [_bdot(term, place[hd]) for hd, x in enumerate(heads) for term in terms3(x)]
        return functools.reduce(lambda a, b: a + b, parts)

    def head_of(s, hd):
        parts = [lax.dot_general(term, place[hd], _NT, preferred_element_type=F32) for term in terms3(s)]
        return functools.reduce(lambda a, b: a + b, parts)

    if has_state:
        init = tuple(to_wide([s0_ref[sq, d, hd] for hd in range(DN_HEADS)]) for sq, d in chains)
    else:
        init = tuple(jnp.zeros((DN_DK, wide), F32) for _ in chains)
    fin = lax.fori_loop(0, n_chunks, scan, init)
    if not has_state:
        for (sq, d), state in zip(chains, fin):
            for hd in range(DN_HEADS):
                sfin_ref[sq, d, hd] = head_of(state, hd)

    for sq in range(n_seqs):
        gate = _silu(zc_all[sq, :, 3 * wide:4 * wide])
        o_ref[sq] = (_head_rms(o_all[sq], og_ref[...]) * gate).astype(o_ref.dtype)


def _deltanet(zc, zm, lw, layer, state=None):
    b, seq, _ = zc.shape
    wide = DN_HEADS * DN_DV
    has_state = state is not None
    ns = max(n for n in (1, 2, 4) if b % n == 0 and n * seq * DN_VMEM_BYTES_PER_ROW <= DN_VMEM_BUDGET)
    per_b = lambda w: pl.BlockSpec((ns, seq, w), lambda bi: (bi, 0, 0))
    const = lambda shape: pl.BlockSpec((None,) + shape, lambda bi: (layer,) + (0,) * len(shape))
    st_spec = pl.BlockSpec((ns, 2, DN_HEADS, DN_DK, DN_DV), lambda bi: (bi, 0, 0, 0, 0))
    in_specs = [per_b(4 * wide), per_b(LANES)]
    args = [zc, zm]
    if has_state:
        in_specs.append(pl.BlockSpec((ns, None, 2, DN_HEADS, DN_DK, DN_DV),
                                     lambda bi: (bi, layer, 0, 0, 0, 0)))
        args.append(state)
    in_specs += [const((DN_CONV, DN_QKV)), const((1, LANES)), const((1, LANES)), const((1, wide))]
    args += [lw["dn_conv_w"], lw["dn_alog_row"], lw["dn_dtb_row"], lw["dn_out_g"]]
    out_specs = [per_b(wide)]
    out_shape = [jax.ShapeDtypeStruct((b, seq, wide), BF16)]
    if not has_state:
        out_specs.append(st_spec)
        out_shape.append(jax.ShapeDtypeStruct((b, 2, DN_HEADS, DN_DK, DN_DV), F32))
    res = pl.pallas_call(
        functools.partial(_deltanet_kernel, seq, has_state),
        grid=(b // ns,),
        in_specs=in_specs,
        out_specs=out_specs,
        out_shape=out_shape,
        scratch_shapes=[pltpu.VMEM((ns, seq, wide), F32), pltpu.VMEM((ns, seq, wide), F32),
                        pltpu.VMEM((ns, seq, wide), F32), pltpu.VMEM((ns, 2, seq, wide), F32),
                        pltpu.VMEM((ns, 2, seq, wide), F32), pltpu.VMEM((ns, seq, wide), F32),
                        pltpu.VMEM((ns, 2, seq, wide), F32), pltpu.VMEM((ns, 2, 2 * seq, wide), BF16)],
        compiler_params=_cparams(1),
        name="deltanet_lat" if has_state else "deltanet_ctx",
    )(*args)
    return (res[0], None) if has_state else (res[0], res[1])


def _outffn_kernel(final, oa_ref, ob_ref, oc_ref, od_ref, x_ref, mod_ref, g2_ref, wo_ref, wg_ref,
                   wu_ref, wd_ref, fg_ref, y_ref):
    tiles = range(TILES_PER_STEP)
    m = mod_ref[...]
    o = [jnp.concatenate([oa_ref[t], ob_ref[t], oc_ref[t], od_ref[t]], axis=-1).astype(BF16) for t in tiles]
    x1 = [x_ref[t] + m[2:3] * _bdot(o[t], wo_ref[...]) for t in tiles]
    h = [(_rms_full(x1[t], g2_ref[...]) * (1.0 + m[4:5]) + m[3:4]).astype(BF16) for t in tiles]
    gate = [_bdot(h[t], wg_ref[...]) for t in tiles]
    up = [_bdot(h[t], wu_ref[...]) for t in tiles]
    act = [(_silu(gate[t]) * up[t]).astype(BF16) for t in tiles]
    x2 = [x1[t] + m[5:6] * _bdot(act[t], wd_ref[...]) for t in tiles]
    for t in tiles:
        y_ref[t] = _rms_full(x2[t], fg_ref[...]) if final else x2[t]


def _outffn(outs, x, mods, per_batch_mods, lw, layer, final_g, final):
    b, l, _ = x.shape
    tm, ts = ROW_TILE, TILES_PER_STEP
    n_tiles = b * l // tm
    tiles_per_seq = l // tm
    assert not per_batch_mods or tiles_per_seq % ts == 0
    row = lambda w: pl.BlockSpec((ts, tm, w), lambda i: (i, 0, 0))
    const = lambda shape: pl.BlockSpec((None,) + shape, lambda i: (layer,) + (0,) * len(shape),
                                       pipeline_mode=pl.Buffered(1))
    mod_spec = pl.BlockSpec((None, None, MOD_CHUNKS, D_MODEL),
                            (lambda i: (layer, 1 + i * ts // tiles_per_seq, 0, 0)) if per_batch_mods
                            else (lambda i: (layer, 0, 0, 0)))
    tiled = lambda a: a.reshape(n_tiles, tm, a.shape[-1])
    y = pl.pallas_call(
        functools.partial(_outffn_kernel, final),
        grid=(n_tiles // ts,),
        in_specs=[row(256), row(256), row(256), row(256), row(D_MODEL), mod_spec, const((1, D_MODEL)),
                  const((D_MODEL, D_MODEL)), const((D_MODEL, D_FF)), const((D_MODEL, D_FF)),
                  const((D_FF, D_MODEL)), pl.BlockSpec((1, D_MODEL), lambda i: (0, 0))],
        out_specs=row(D_MODEL),
        out_shape=jax.ShapeDtypeStruct((n_tiles, tm, D_MODEL), F32),
        compiler_params=_cparams(1),
        name="outffn",
    )(*[tiled(a) for a in outs], tiled(x), mods, lw["norm2_g"], lw["w_out"], lw["w_gate"], lw["w_up"],
      lw["w_down"], final_g)
    return y.reshape(b, l, D_MODEL)


def _rope_tables(n):
    t = jnp.arange(n)

    def axis(pos, half):
        inv = ROPE_BASE ** (-jnp.arange(half, dtype=F32) / half)
        ang = pos.astype(F32)[:, None] * inv[None, :]
        c, s = jnp.cos(ang), jnp.sin(ang)
        return jnp.concatenate([c, c], -1), jnp.concatenate([-s, s], -1)

    cr, sr = axis(t // GRID_W, 16)
    cc, sc = axis(t % GRID_W, 16)
    cos64 = jnp.tile(jnp.concatenate([cr, cc], -1), (1, 2))
    sin64 = jnp.tile(jnp.concatenate([sr, sc], -1), (1, 2))
    cr, sr = axis(t // GRID_W, 8)
    cc, sc = axis(t % GRID_W, 8)
    cos32, sin32 = jnp.concatenate([cr, cc], -1), jnp.concatenate([sr, sc], -1)
    one, zero = jnp.ones((n, 1), F32), jnp.zeros((n, 1), F32)
    cosm = jnp.concatenate([jnp.tile(one, (1, 64)), cos32, jnp.tile(one, (1, 32))], -1)
    sinm = jnp.concatenate([jnp.tile(zero, (1, 64)), sin32, jnp.tile(zero, (1, 32))], -1)
    coskr = jnp.concatenate([cos32, jnp.tile(one, (1, 96))], -1)
    sinkr = jnp.concatenate([sin32, jnp.tile(zero, (1, 96))], -1)
    return cos64, sin64, cosm, sinm, coskr, sinkr


_QA_ORDER = ((0, 64), (128, 192), (64, 128), (192, 256))


def _stacked_weights(p):
    w_in = jnp.swapaxes(p["w_in"], 1, 2).astype(BF16)
    w_out = p["w_out"]
    w_out = jnp.concatenate([w_out[:, a:b] for a, b in _QA_ORDER + ((256, w_out.shape[1]),)], axis=1).astype(BF16)
    wq = p["mla_wq_up"].reshape(DEPTH, MLA_Q_LORA, MLA_HEADS, MLA_NOPE + MLA_ROPE)
    wq = jnp.pad(wq, ((0, 0), (0, 0), (0, 0), (0, LANES - MLA_NOPE - MLA_ROPE))).reshape(DEPTH, MLA_Q_LORA, 4 * LANES)
    wkv = p["mla_wkv_up"].reshape(DEPTH, MLA_KV_LORA, MLA_HEADS, MLA_NOPE + MLA_V)
    wk_top = jnp.pad(wkv[..., :MLA_NOPE], ((0, 0), (0, 0), (0, 0), (0, LANES - MLA_NOPE)))
    place = jnp.pad(jnp.eye(MLA_ROPE, dtype=F32), ((0, LANES - MLA_ROPE), (MLA_NOPE, LANES - MLA_NOPE - MLA_ROPE)))
    wk_bot = jnp.broadcast_to(place[None, :, None, :], (DEPTH, LANES, MLA_HEADS, LANES))
    wk = jnp.concatenate([wk_top, wk_bot], axis=1).reshape(DEPTH, 2 * LANES, 4 * LANES)
    wv = wkv[..., MLA_NOPE:].reshape(DEPTH, MLA_KV_LORA, MLA_HEADS * MLA_V)
    gate_row = lambda v: jnp.pad(v.reshape(DEPTH, 1, DN_GATES),
                                 ((0, 0), (0, 0), (IN_TAIL_SKIP, LANES - IN_TAIL_SKIP - DN_GATES)))
    row = lambda v: v[:, None, :]
    return {
        "norm1_g": row(p["norm1_g"]), "norm2_g": row(p["norm2_g"]),
        "w_in": w_in, "w_out": w_out,
        "qn_g": row(jnp.tile(p["gqa_qn_g"], (1, 4))), "kn_g": row(jnp.tile(p["gqa_kn_g"], (1, 2))),
        "mla_qn_g": row(p["mla_qn_g"]), "mla_kvn_g": row(p["mla_kvn_g"]),
        "wq": wq.astype(BF16), "wk": wk.astype(BF16), "wv": wv.astype(BF16),
        "dn_conv_w": p["dn_conv_w"], "dn_alog_row": gate_row(p["dn_a_log"]),
        "dn_dtb_row": gate_row(p["dn_dt_bias"]), "dn_out_g": row(jnp.tile(p["dn_out_g"], (1, DN_HEADS))),
        "w_gate": p["ffn_w_gate"].astype(BF16), "w_up": p["ffn_w_up"].astype(BF16),
        "w_down": p["ffn_w_down"].astype(BF16),
    }


def kernel(x_prompt, x_sample, cache_gqa_k, cache_gqa_v, cache_na_k, cache_na_v, state_dn,
           cache_mla_ckv, cache_mla_krope, c, c_ctx, norm1_g, norm2_g, w_mod, b_mod, w_in, w_out,
           gqa_qn_g, gqa_kn_g, na_bias, dn_conv_w, dn_a_log, dn_dt_bias, dn_out_g, mla_qn_g,
           mla_wq_up, mla_kvn_g, mla_wkv_up, ffn_w_gate, ffn_w_up, ffn_w_down, final_g):
    p = {"norm1_g": norm1_g, "norm2_g": norm2_g, "w_in": w_in, "w_out": w_out, "gqa_qn_g": gqa_qn_g,
         "gqa_kn_g": gqa_kn_g, "dn_conv_w": dn_conv_w, "dn_a_log": dn_a_log, "dn_dt_bias": dn_dt_bias,
         "dn_out_g": dn_out_g, "mla_qn_g": mla_qn_g, "mla_wq_up": mla_wq_up, "mla_kvn_g": mla_kvn_g,
         "mla_wkv_up": mla_wkv_up, "ffn_w_gate": ffn_w_gate, "ffn_w_up": ffn_w_up, "ffn_w_down": ffn_w_down}
    nb_ctx, seq_ctx, _ = x_prompt.shape
    nb_lat, seq_lat, _ = x_sample.shape
    past = cache_gqa_k.shape[2]
    fg = final_g[None]

    cond = jnp.concatenate([c_ctx[None], c, jnp.zeros((16 - 1 - nb_lat, D_MODEL), F32)], axis=0)
    mods = _modulation(cond, w_mod, b_mod).reshape(DEPTH, 16, MOD_CHUNKS, D_MODEL)
    lw = _stacked_weights(p)

    x = x_prompt
    ctx_out = []
    for l in range(DEPTH):
        o_a, ka, va, o_b, kb, vb, zc, o_d, ckv, zm, krr = _inproj(x, mods, False, lw, l, None)
        o_c, s_dn = _deltanet(zc, zm, lw, l)
        x = _outffn((o_a, o_b, o_c, o_d), x, mods, False, lw, l, fg, l == DEPTH - 1)
        ctx_out.append((ka, va, kb, vb, s_dn, ckv, krr[:, :, :MLA_ROPE]))
    y_prompt = x
    new = [jnp.stack([s[i] for s in ctx_out], axis=1) for i in range(7)]
    for i in range(4):
        t = new[i].reshape(nb_ctx, DEPTH, -1, HEAD_DIM, seq_ctx)
        new[i] = jnp.transpose(t, (0, 1, 4, 2, 3))

    ropes = _rope_tables(seq_lat)
    keys_t = lambda c: jnp.transpose(c, (0, 1, 3, 4, 2)).reshape(nb_lat, DEPTH, -1, past)
    ck_a, cv_a, ck_b, cv_b = keys_t(cache_gqa_k), keys_t(cache_gqa_v), keys_t(cache_na_k), keys_t(cache_na_v)
    c_kr = jnp.pad(cache_mla_krope, ((0, 0), (0, 0), (0, 0), (0, LANES - MLA_ROPE)))
    bias_blocks = _na_bias_blocks(na_bias)
    x = x_sample
    for l in range(DEPTH):
        qa, ka, va, qb, kb, vb, zc, qd, ckv, zm, krr = _inproj(x, mods, True, lw, l, ropes)
        o_a = _attn_pair(qa, ka, va, (0, 0), ck_a, cv_a, l, name="gqa_lat")
        o_b = _na_latent(qb, kb, vb, ck_b, cv_b, l, bias_blocks)
        o_c, _ = _deltanet(zc, zm, lw, l, state=state_dn)
        o_d = _mla(qd, ckv, krr, lw, l, cache_mla_ckv, c_kr)
        x = _outffn((o_a, o_b, o_c, o_d), x, mods, True, lw, l, fg, l == DEPTH - 1)
    y_sample = x

    return (y_prompt, y_sample, *new)
```

```python
import functools

import numpy as np
import jax
import jax.numpy as jnp
from jax import lax
from jax.experimental import pallas as pl
from jax.experimental.pallas import tpu as pltpu

F32 = jnp.float32
BF16 = jnp.bfloat16

D_MODEL = 1024
DEPTH = 2
GRID_W = 64
HEAD_DIM = 64
ROPE_BASE = 10000.0
NEG_INF = -1e30
MOD_CHUNKS = 6
GQA_HEADS, GQA_KV_HEADS = 4, 2
NA_HEADS, NA_KH, NA_KW = 4, 8, 16
DN_HEADS, DN_DK, DN_DV, DN_CONV, DN_CHUNK = 4, 64, 64, 4, 64
DN_QKV = DN_HEADS * (2 * DN_DK + DN_DV)
DN_GATES = 2 * DN_HEADS
MLA_HEADS, MLA_Q_LORA, MLA_KV_LORA, MLA_NOPE, MLA_ROPE, MLA_V = 4, 256, 128, 64, 32, 64
MLA_SCALE = (MLA_NOPE + MLA_ROPE) ** -0.5
D_FF = -(-8 * D_MODEL // (3 * 256)) * 256
EPS = 1e-6
LOG2E = 1.4426950408889634

LANES = 128
ROW_TILE = 256
Q_TILE = 256
NA_BAND = 768
TILES_PER_STEP = 2
ATTN_Q_TILE = 512
HEAD_LOOKAHEAD = 1
DN_VMEM_BYTES_PER_ROW = 12288 + 9216
DN_VMEM_BUDGET = 46 * 1024 * 1024
DN_ROWS = 256
PREP_CHUNKS = 4
IN_COLS = 2736
IN_TAIL_SKIP = 512 - (2 * 8 + 256 + 128 + 32)
VMEM_LIMIT = 56 * 1024 * 1024

_NT = (((1,), (1,)), ((), ()))
_TN = (((0,), (0,)), ((), ()))


def _cparams(n_axes):
    return pltpu.CompilerParams(dimension_semantics=("arbitrary",) * n_axes,
                                vmem_limit_bytes=VMEM_LIMIT)


def _lane(shape):
    return lax.broadcasted_iota(jnp.int32, shape, len(shape) - 1)


def _silu(x):
    return x / (1.0 + jnp.exp(-x))


def _rms_full(x, g):
    return x * lax.rsqrt(jnp.mean(x * x, axis=-1, keepdims=True) + EPS) * g


def _seg64_sum(x):
    lo = _lane(x.shape) < HEAD_DIM
    s_lo = jnp.sum(jnp.where(lo, x, 0.0), axis=-1, keepdims=True)
    s_hi = jnp.sum(jnp.where(lo, 0.0, x), axis=-1, keepdims=True)
    return jnp.where(lo, s_lo, s_hi)


def _head_rms(x, g):
    parts = []
    for p in range(x.shape[-1] // LANES):
        xp = x[:, p * LANES:(p + 1) * LANES]
        ms = _seg64_sum(xp * xp) * (1.0 / HEAD_DIM)
        parts.append(xp * lax.rsqrt(ms + EPS))
    y = parts[0] if len(parts) == 1 else jnp.concatenate(parts, axis=-1)
    return y * g


def _head_l2(x):
    parts = []
    for p in range(x.shape[-1] // LANES):
        xp = x[:, p * LANES:(p + 1) * LANES]
        parts.append(xp * lax.rsqrt(_seg64_sum(xp * xp) + EPS))
    return parts[0] if len(parts) == 1 else jnp.concatenate(parts, axis=-1)


def _rope(x, cos, sin, half):
    first = (_lane(x.shape) & (2 * half - 1)) < half
    rot = jnp.where(first, pltpu.roll(x, LANES - half, 1), pltpu.roll(x, half, 1))
    return x * cos + rot * sin


def _softmax_parts(scores):
    m = jnp.max(scores[0], axis=-1, keepdims=True)
    for s in scores[1:]:
        m = jnp.maximum(m, jnp.max(s, axis=-1, keepdims=True))
    es = [jnp.exp2(s - m) for s in scores]
    l = jnp.sum(es[0], axis=-1, keepdims=True)
    for e in es[1:]:
        l = l + jnp.sum(e, axis=-1, keepdims=True)
    return es, 1.0 / l


def _bdot(a, b):
    return jnp.dot(a, b, preferred_element_type=F32)


def _pipelined_heads(n_heads, scores, attend):
    outs = []
    queue = [scores(hd) for hd in range(min(HEAD_LOOKAHEAD, n_heads))]
    for hd in range(n_heads):
        if hd + HEAD_LOOKAHEAD < n_heads:
            queue.append(scores(hd + HEAD_LOOKAHEAD))
        es, rl = _softmax_parts(queue.pop(0))
        outs.append(attend(hd, es) * rl)
    return outs


def _pair_attention(q_blocks, sources, qmap):
    lo = _lane(q_blocks[0].shape) < HEAD_DIM

    def scores(hd):
        p, half = divmod(hd, 2)
        qm = jnp.where(lo if half == 0 else jnp.logical_not(lo), q_blocks[p], 0.0).astype(BF16)
        return [_bdot(qm, k_block(qmap[p])) if transposed
                else lax.dot_general(qm, k_block(qmap[p]), _NT, preferred_element_type=F32)
                for k_block, _, transposed in sources]

    def attend(hd, es):
        kv = qmap[hd // 2]
        parts = [lax.dot_general(e.astype(BF16), v_block(kv), _NT, preferred_element_type=F32) if transposed
                 else _bdot(e.astype(BF16), v_block(kv)) for e, (_, v_block, transposed) in zip(es, sources)]
        return functools.reduce(lambda a, b: a + b, parts)

    outs = _pipelined_heads(2 * len(qmap), scores, attend)
    return [jnp.where(lo, outs[2 * p], outs[2 * p + 1]) for p in range(len(qmap))]


def _mla_attention(q_heads, k_head, v_block):
    lo = _lane(q_heads[0].shape) < MLA_V

    def scores(hd):
        return [lax.dot_general(q_heads[hd], k_head(hd), _NT, preferred_element_type=F32)]

    def attend(hd, es):
        return _bdot(es[0].astype(BF16), v_block(hd // 2))

    outs = _pipelined_heads(MLA_HEADS, scores, attend)
    return [jnp.where(lo, outs[2 * p], outs[2 * p + 1]) for p in range(MLA_HEADS // 2)]


def _mod_kernel(c_ref, w_ref, b_ref, o_ref):
    s = _silu(c_ref[...]).astype(BF16)
    o_ref[...] = _bdot(s, w_ref[...].astype(BF16)) + b_ref[...]


def _modulation(cond, w_mod, b_mod):
    n = MOD_CHUNKS * D_MODEL
    tn = 1536
    return pl.pallas_call(
        _mod_kernel,
        grid=(DEPTH, n // tn),
        in_specs=[pl.BlockSpec((16, D_MODEL), lambda l, j: (0, 0)),
                  pl.BlockSpec((None, D_MODEL, tn), lambda l, j: (l, 0, j)),
                  pl.BlockSpec((None, 1, tn), lambda l, j: (l, 0, j))],
        out_specs=pl.BlockSpec((None, 16, tn), lambda l, j: (l, 0, j)),
        out_shape=jax.ShapeDtypeStruct((DEPTH, 16, n), F32),
        compiler_params=_cparams(2),
        name="modulation",
    )(cond, w_mod, b_mod.reshape(DEPTH, 1, n))


_IN_OUT_WIDTHS = (256, 128, 128, 256, 256, 256, 1024, 512, 128, 128, 128)
_IN_OUT_WIDTHS_CTX = (256, 128, 128, 256, 256, 256, 1024, 256, 128, 128, 128)
_IN_OUT_DTYPES_LAT = (BF16, BF16, BF16, BF16, BF16, BF16, F32, BF16, BF16, F32, BF16)
_IN_OUT_DTYPES_CTX = (BF16, F32, F32, BF16, F32, F32, F32, BF16, F32, F32, F32)


def _inproj_kernel(positioned, *refs):
    (x_ref, mod_ref, g1_ref, w_ref, qng_ref, kng_ref, mqg_ref, wq_ref, mkg_ref) = refs[:9]
    if positioned:
        cos64_ref, sin64_ref, cosm_ref, sinm_ref, coskr_ref, sinkr_ref = refs[9:15]
        n_in = 15
    else:
        wk_ref, wv_ref = refs[9:11]
        n_in = 11
    (qa_ref, ka_ref, va_ref, qb_ref, kb_ref, vb_ref, zc_ref, qd_ref, ckv_ref, zm_ref,
     krr_ref) = refs[n_in:]
    tiles = range(TILES_PER_STEP)
    tm = x_ref.shape[1]
    lane = _lane((tm, LANES))
    lo = lane < HEAD_DIM

    m = mod_ref[...]
    hb = jnp.concatenate([(_rms_full(x_ref[t], g1_ref[...]) * (1.0 + m[1:2]) + m[0:1]).astype(BF16)
                          for t in tiles], axis=0)

    def project(c0, c1):
        z = lax.dot_general(hb, w_ref[c0:c1, :], _NT, preferred_element_type=F32)
        return [z[t * tm:(t + 1) * tm] for t in tiles]

    za = project(0, 512)
    zb = project(512, 1280)

    for t in tiles:
        q = _head_rms(za[t][:, 0:256], qng_ref[...])
        k = _head_rms(za[t][:, 256:384], kng_ref[...])
        q0, q1 = q[:, 0:128], q[:, 128:256]
        q0, q1 = jnp.where(lo, q0, pltpu.roll(q1, HEAD_DIM, 1)), jnp.where(lo, pltpu.roll(q0, HEAD_DIM, 1), q1)
        if positioned:
            cos, sin = cos64_ref[t], sin64_ref[t]
            q0, q1 = _rope(q0, cos, sin, 16), _rope(q1, cos, sin, 16)
            k = _rope(k, cos, sin, 16)
        q0, q1 = q0 * (HEAD_DIM ** -0.5 * LOG2E), q1 * (HEAD_DIM ** -0.5 * LOG2E)
        v = za[t][:, 384:512]
        if not positioned:
            kb16, vb16 = k.astype(BF16), v.astype(BF16)
            q0, q1 = _pair_attention([q0, q1], [(lambda i: kb16, lambda i: vb16, False)], (0, 0))
        qa_ref[t] = jnp.concatenate([q0, q1], axis=-1).astype(qa_ref.dtype)
        ka_ref[t] = k.astype(ka_ref.dtype) if positioned else k.T
        va_ref[t] = v.astype(va_ref.dtype) if positioned else v.T

    zd = project(IN_COLS - 4 * LANES, IN_COLS)
    zc = project(1280, 2304)

    for t in tiles:
        q = zb[t][:, 0:256] * (HEAD_DIM ** -0.5 * LOG2E)
        k, v = zb[t][:, 256:512], zb[t][:, 512:768]
        if not positioned:
            kb16, vb16 = k.astype(BF16), v.astype(BF16)
            q = jnp.concatenate(_pair_attention(
                [q[:, 0:LANES], q[:, LANES:]],
                [(lambda i, kb16=kb16: kb16[:, i * LANES:(i + 1) * LANES],
                  lambda i, vb16=vb16: vb16[:, i * LANES:(i + 1) * LANES], False)], (0, 1)), axis=-1)
        qb_ref[t] = q.astype(qb_ref.dtype)
        kb_ref[t] = k.astype(kb_ref.dtype) if positioned else k.T
        vb_ref[t] = v.astype(vb_ref.dtype) if positioned else v.T

    lead = IN_TAIL_SKIP + 2 * DN_GATES
    shifted = []
    for t in tiles:
        rolled = [pltpu.roll(zd[t][:, j * LANES:(j + 1) * LANES], LANES - lead, 1) for j in range(4)]
        keep = lane < LANES - lead
        shifted.append([jnp.where(keep, rolled[j], rolled[(j + 1) % 4]) for j in range(4)])

    cq = jnp.concatenate([_rms_full(jnp.concatenate(shifted[t][0:2], axis=-1), mqg_ref[...]).astype(BF16)
                          for t in tiles], axis=0)
    qm = _bdot(cq, wq_ref[...])
    for t in tiles:
        zc_ref[t] = zc[t]
        zm_ref[t] = zd[t][:, 0:LANES]
        ckv = _rms_full(shifted[t][2], mkg_ref[...])
        ckv_ref[t] = ckv.astype(ckv_ref.dtype)
        kr = jnp.where(lane < MLA_ROPE, shifted[t][3], 0.0)
        q = qm[t * tm:(t + 1) * tm]
        if positioned:
            kr = _rope(kr, coskr_ref[t], sinkr_ref[t], 8)
            cm, sm = cosm_ref[t], sinm_ref[t]
            q = jnp.concatenate([_rope(q[:, i * LANES:(i + 1) * LANES], cm, sm, 8)
                                 for i in range(MLA_HEADS)], axis=-1)
        krr_ref[t] = kr.astype(krr_ref.dtype)
        q = q * (MLA_SCALE * LOG2E)
        if not positioned:
            c16 = ckv.astype(BF16)
            k16 = _bdot(jnp.concatenate([c16, kr.astype(BF16)], axis=-1), wk_ref[...]).astype(BF16)
            v16 = _bdot(c16, wv_ref[...]).astype(BF16)
            q = jnp.concatenate(_mla_attention(
                [q[:, i * LANES:(i + 1) * LANES].astype(BF16) for i in range(MLA_HEADS)],
                lambda i, k16=k16: k16[:, i * LANES:(i + 1) * LANES],
                lambda i, v16=v16: v16[:, i * LANES:(i + 1) * LANES]), axis=-1)
        qd_ref[t] = q.astype(qd_ref.dtype)


def _inproj(x, mods, per_batch_mods, lw, layer, ropes):
    b, l, _ = x.shape
    tm, ts = ROW_TILE, TILES_PER_STEP
    n_tiles = b * l // tm
    tiles_per_seq = l // tm
    positioned = ropes is not None
    row = lambda w: pl.BlockSpec((ts, tm, w), lambda i: (i, 0, 0))
    const = lambda shape: pl.BlockSpec((None,) + shape, lambda i: (layer,) + (0,) * len(shape))
    assert not per_batch_mods or tiles_per_seq % ts == 0
    mod_spec = pl.BlockSpec((None, None, MOD_CHUNKS, D_MODEL),
                            (lambda i: (layer, 1 + i * ts // tiles_per_seq, 0, 0)) if per_batch_mods
                            else (lambda i: (layer, 0, 0, 0)))
    in_specs = [row(D_MODEL), mod_spec, const((1, D_MODEL)), const((IN_COLS, D_MODEL)),
                const((1, 256)), const((1, 128)), const((1, MLA_Q_LORA)),
                const((MLA_Q_LORA, 4 * LANES)), const((1, MLA_KV_LORA))]
    args = [x.reshape(n_tiles, tm, D_MODEL), mods, lw["norm1_g"], lw["w_in"], lw["qn_g"], lw["kn_g"],
            lw["mla_qn_g"], lw["wq"], lw["mla_kvn_g"]]
    if positioned:
        steps_per_seq = tiles_per_seq // ts
        in_specs += [pl.BlockSpec((ts, tm, LANES), lambda i: (i % steps_per_seq, 0, 0))] * 6
        args += [r.reshape(tiles_per_seq, tm, LANES) for r in ropes]
        widths, dtypes = _IN_OUT_WIDTHS, _IN_OUT_DTYPES_LAT
    else:
        assert tiles_per_seq == 1
        in_specs += [const((2 * LANES, 4 * LANES)), const((LANES, 2 * LANES))]
        args += [lw["wk"], lw["wv"]]
        widths, dtypes = _IN_OUT_WIDTHS_CTX, _IN_OUT_DTYPES_CTX
    flipped = () if positioned else (1, 2, 4, 5)
    shapes = [(n_tiles, w, tm) if i in flipped else (n_tiles, tm, w) for i, w in enumerate(widths)]
    outs = pl.pallas_call(
        functools.partial(_inproj_kernel, positioned),
        grid=(n_tiles // ts,),
        in_specs=in_specs,
        out_specs=[pl.BlockSpec((ts,) + shp[1:], lambda i: (i, 0, 0)) for shp in shapes],
        out_shape=[jax.ShapeDtypeStruct(shp, dt) for shp, dt in zip(shapes, dtypes)],
        compiler_params=_cparams(1),
        name="inproj_lat" if positioned else "inproj_attn_ctx",
    )(*args)
    return [o if i in flipped else o.reshape(b, l, o.shape[-1]) for i, o in enumerate(outs)]


def _attn_pair_kernel(qmap, q_ref, kc_ref, vc_ref, k_ref, v_ref, o_ref, kcbuf, vcbuf, kbuf, vbuf):
    @pl.when(pl.program_id(1) == 0)
    def _():
        kcbuf[...] = kc_ref[...].astype(BF16)
        vcbuf[...] = vc_ref[...].astype(BF16)
        kbuf[...] = k_ref[...].astype(BF16)
        vbuf[...] = v_ref[...].astype(BF16)

    rows = lambda buf: (lambda i: buf[i * LANES:(i + 1) * LANES, :])
    cols = lambda buf: (lambda i: buf[:, i * LANES:(i + 1) * LANES])
    q_blocks = [q_ref[:, p * LANES:(p + 1) * LANES].astype(F32) for p in range(len(qmap))]
    outs = _pair_attention(q_blocks, [(rows(kcbuf), rows(vcbuf), True), (cols(kbuf), cols(vbuf), False)], qmap)
    for p, o in enumerate(outs):
        o_ref[:, p * LANES:(p + 1) * LANES] = o.astype(o_ref.dtype)


def _attn_pair(q, k, v, qmap, kc_t, vc_t, layer, name):
    b, lq, wq = q.shape
    ls, wk = k.shape[1], k.shape[2]
    lc = kc_t.shape[3]
    tq = min(ATTN_Q_TILE, lq)
    cspec = pl.BlockSpec((None, None, wk, lc), lambda bi, i: (bi, layer, 0, 0))
    sspec = pl.BlockSpec((None, ls, wk), lambda bi, i: (bi, 0, 0))
    return pl.pallas_call(
        functools.partial(_attn_pair_kernel, qmap),
        grid=(b, lq // tq),
        in_specs=[pl.BlockSpec((None, tq, wq), lambda bi, i: (bi, i, 0)), cspec, cspec, sspec, sspec],
        out_specs=pl.BlockSpec((None, tq, wq), lambda bi, i: (bi, i, 0)),
        out_shape=jax.ShapeDtypeStruct((b, lq, wq), BF16),
        scratch_shapes=[pltpu.VMEM((wk, lc), BF16), pltpu.VMEM((wk, lc), BF16),
                        pltpu.VMEM((ls, wk), BF16), pltpu.VMEM((ls, wk), BF16)],
        compiler_params=_cparams(2),
        name=name,
    )(q, kc_t, vc_t, k, v)


def _mla_kernel(q_ref, ckvc_ref, krc_ref, ckv_ref, kr_ref, wk_ref, wv_ref, o_ref, kbuf, vbuf):
    @pl.when(pl.program_id(1) == 0)
    def _():
        def expand(c_ref, r_ref, r0, r1):
            c = c_ref[...].astype(BF16)
            ckr = jnp.concatenate([c, r_ref[...].astype(BF16)], axis=-1)
            kbuf[r0:r1, :] = _bdot(ckr, wk_ref[...]).astype(BF16)
            vbuf[r0:r1, :] = _bdot(c, wv_ref[...]).astype(BF16)
        off = ckvc_ref.shape[0]
        expand(ckvc_ref, krc_ref, 0, off)
        expand(ckv_ref, kr_ref, off, kbuf.shape[0])

    outs = _mla_attention([q_ref[:, hd * LANES:(hd + 1) * LANES].astype(BF16) for hd in range(MLA_HEADS)],
                          lambda i: kbuf[:, i * LANES:(i + 1) * LANES],
                          lambda i: vbuf[:, i * LANES:(i + 1) * LANES])
    for p, o in enumerate(outs):
        o_ref[:, p * LANES:(p + 1) * LANES] = o.astype(o_ref.dtype)


def _mla(q, ckv, kr, lw, layer, ckv_cache, kr_cache):
    b, lq, wq = q.shape
    ls, lc = ckv.shape[1], ckv_cache.shape[2]
    tq = min(ATTN_Q_TILE, lq)
    cspec = pl.BlockSpec((None, None, lc, LANES), lambda bi, i: (bi, layer, 0, 0))
    sspec = pl.BlockSpec((None, ls, LANES), lambda bi, i: (bi, 0, 0))
    return pl.pallas_call(
        _mla_kernel,
        grid=(b, lq // tq),
        in_specs=[pl.BlockSpec((None, tq, wq), lambda bi, i: (bi, i, 0)), cspec, cspec, sspec, sspec,
                  pl.BlockSpec((None, 2 * LANES, 4 * LANES), lambda bi, i: (layer, 0, 0)),
                  pl.BlockSpec((None, LANES, 2 * LANES), lambda bi, i: (layer, 0, 0))],
        out_specs=pl.BlockSpec((None, tq, 2 * LANES), lambda bi, i: (bi, i, 0)),
        out_shape=jax.ShapeDtypeStruct((b, lq, 2 * LANES), BF16),
        scratch_shapes=[pltpu.VMEM((lc + ls, 4 * LANES), BF16), pltpu.VMEM((lc + ls, 2 * LANES), BF16)],
        compiler_params=_cparams(2),
        name="mla_lat",
    )(q, ckv_cache, kr_cache, ckv, kr, lw["wk"], lw["wv"])


def _na_kernel(q_ref, k_ref, v_ref, kc_ref, vc_ref, bias_ref, o_ref):
    start = pl.multiple_of(pl.program_id(0) * 256, 256)
    kband = k_ref[pl.ds(start, NA_BAND), :].astype(BF16)
    vband = v_ref[pl.ds(start, NA_BAND), :].astype(BF16)
    kc = kc_ref[...].astype(BF16)
    vc = vc_ref[...].astype(BF16)
    tq = q_ref.shape[0]
    lo = _lane((tq, LANES)) < HEAD_DIM

    def scores(hd):
        p, half = divmod(hd, 2)
        sl = slice(p * LANES, (p + 1) * LANES)
        qm = jnp.where(lo if half == 0 else jnp.logical_not(lo), q_ref[:, sl].astype(F32), 0.0).astype(BF16)
        bias = jnp.concatenate([bias_ref[0, hd], bias_ref[1, hd]], axis=0)
        s_loc = lax.dot_general(qm, kband[:, sl], _NT, preferred_element_type=F32) + bias
        return [s_loc, _bdot(qm, kc[sl, :])]

    def attend(hd, es):
        sl = slice((hd // 2) * LANES, (hd // 2 + 1) * LANES)
        return (_bdot(es[0].astype(BF16), vband[:, sl])
                + lax.dot_general(es[1].astype(BF16), vc[sl, :], _NT, preferred_element_type=F32))

    outs = _pipelined_heads(NA_HEADS, scores, attend)
    for p in range(NA_HEADS // 2):
        o_ref[:, p * LANES:(p + 1) * LANES] = jnp.where(lo, outs[2 * p], outs[2 * p + 1]).astype(o_ref.dtype)


def _na_latent(q, k, v, kc_t, vc_t, layer, bias_blocks):
    b, n, w = q.shape
    lc = kc_t.shape[3]
    nq = n // (2 * Q_TILE)
    full = pl.BlockSpec((None, n, w), lambda j, bi: (bi, 0, 0))
    cspec = pl.BlockSpec((None, None, w, lc), lambda j, bi: (bi, layer, 0, 0))
    return pl.pallas_call(
        _na_kernel,
        grid=(nq, b),
        in_specs=[pl.BlockSpec((None, 2 * Q_TILE, w), lambda j, bi: (bi, j, 0)), full, full, cspec, cspec,
                  pl.BlockSpec((None, 2, NA_HEADS, Q_TILE, NA_BAND), lambda j, bi: (layer, j, 0, 0, 0))],
        out_specs=pl.BlockSpec((None, 2 * Q_TILE, w), lambda j, bi: (bi, j, 0)),
        out_shape=jax.ShapeDtypeStruct((b, n, w), BF16),
        compiler_params=_cparams(2),
        name="na_latent",
    )(q, k, v, kc_t, vc_t, bias_blocks)


NA_GRID_ROWS = 16
NA_BAND_ROW0 = (0, 0, 4, 4)


def _na_bias_kernel(b_ref, o_ref, tp_s):
    hd = pl.program_id(0)
    n_dr, n_dc = 2 * NA_KH - 1, 2 * NA_KW - 1
    shape = (GRID_W, LANES)
    c = lax.broadcasted_iota(jnp.int32, shape, 0)
    lane = _lane(shape)
    kc = lane & (GRID_W - 1)
    lo = lane < GRID_W
    diff = kc - c + (NA_KW - 1)
    c0 = jnp.clip(c - NA_KW // 2, 0, GRID_W - NA_KW)
    col_ok = (kc >= c0) & (kc < c0 + NA_KW)
    neg = jnp.full(shape, NEG_INF, F32)
    for dr0 in range(-1, n_dr):
        acc = neg
        for d in range(n_dc):
            v_lo = b_ref[hd * n_dr + dr0, d] if dr0 >= 0 else 0.0
            v_hi = b_ref[hd * n_dr + dr0 + 1, d] if dr0 + 1 < n_dr else 0.0
            acc = jnp.where(diff == d, jnp.where(lo, v_lo, v_hi), acc)
        tp_s[dr0 + 1] = jnp.where(col_ok, acc * LOG2E, NEG_INF)
    for j in range(NA_GRID_ROWS // 4):
        for ri in range(4):
            r = 4 * j + ri
            r0 = min(max(r - NA_KH // 2, 0), NA_GRID_ROWS - NA_KH)
            for kp in range(NA_BAND // LANES):
                kr = NA_BAND_ROW0[j] + 2 * kp
                ok_lo, ok_hi = r0 <= kr < r0 + NA_KH, r0 <= kr + 1 < r0 + NA_KH
                dr0 = kr - r + (NA_KH - 1)
                if ok_lo and ok_hi:
                    t = tp_s[dr0 + 1]
                elif ok_lo:
                    t = jnp.where(lo, tp_s[dr0 + 1], NEG_INF)
                elif ok_hi:
                    t = jnp.where(lo, NEG_INF, tp_s[dr0 + 1])
                else:
                    t = neg
                o_ref[j, ri * GRID_W:(ri + 1) * GRID_W, kp * LANES:(kp + 1) * LANES] = t


def _na_bias_blocks(bias):
    nq = NA_GRID_ROWS // 4
    return pl.pallas_call(
        _na_bias_kernel,
        grid=(DEPTH * NA_HEADS,),
        in_specs=[pl.BlockSpec(memory_space=pltpu.SMEM)],
        out_specs=pl.BlockSpec((None, nq, None, Q_TILE, NA_BAND),
                               lambda i: (i // NA_HEADS, 0, i % NA_HEADS, 0, 0)),
        out_shape=jax.ShapeDtypeStruct((DEPTH, nq, NA_HEADS, Q_TILE, NA_BAND), F32),
        scratch_shapes=[pltpu.VMEM((2 * NA_KH, GRID_W, LANES), F32)],
        compiler_params=_cparams(1),
        name="na_bias",
    )(bias.reshape(DEPTH * NA_HEADS * (2 * NA_KH - 1), 2 * NA_KW - 1))


def _widen(cols, n):
    blk = _lane((n, DN_HEADS * DN_DV)) >> 6
    return jnp.where(blk == 0, cols[0], jnp.where(blk == 1, cols[1], jnp.where(blk == 2, cols[2], cols[3])))


def _deltanet_kernel(seq, has_state, *refs):
    if has_state:
        zc_all, zm_all, s0_ref, cw_ref, alog_ref, dtb_ref, og_ref, o_ref = refs[:8]
    else:
        zc_all, zm_all, cw_ref, alog_ref, dtb_ref, og_ref, o_ref, sfin_ref = refs[:8]
    scratch = refs[8:]
    n_seqs = zc_all.shape[0]
    g_all, o_all, c_all, mp_all = scratch[4:8]
    n_chunks = seq // DN_CHUNK
    wide = DN_HEADS * DN_DV

    a_off, b_off = IN_TAIL_SKIP, IN_TAIL_SKIP + DN_GATES
    ri = lax.broadcasted_iota(jnp.int32, (DN_ROWS, DN_ROWS), 0)
    ci = lax.broadcasted_iota(jnp.int32, (DN_ROWS, DN_ROWS), 1)
    same = (ri >> 6) == (ci >> 6)
    tri = [jnp.tile(jnp.where(same & ((ci <= ri) if d == 0 else (ci >= ri)), 1.0, 0.0).astype(BF16), (1, 3))
           for d in range(2)]
    n_blocks = seq // DN_ROWS

    def terms3(x):
        hi = x.astype(BF16)
        rest = x - hi.astype(F32)
        mid = rest.astype(BF16)
        return hi, mid, (rest - mid.astype(F32)).astype(BF16)

    def split3(x):
        return jnp.concatenate(terms3(x), axis=0)

    def preprocess(views, rb, carry):
        zc_ref, zm_ref, q_s, k_s, v_s, b_s, g_s = views[:7]
        r0 = pl.multiple_of(rb * DN_ROWS, DN_ROWS)
        rows = pl.ds(r0, DN_ROWS)
        before = pl.ds(pl.multiple_of(jnp.maximum(r0 - 8, 0), 8), 8)
        after = pl.ds(pl.multiple_of(jnp.minimum(r0 + DN_ROWS, seq - 8), 8), 8)
        for part, dst in enumerate((q_s, k_s, v_s)):
            cs = slice(part * wide, (part + 1) * wide)
            head = jnp.where(rb > 0, zc_ref[before, cs], 0.0)
            tail = jnp.where(rb < n_blocks - 1, zc_ref[after, cs], 0.0)
            xe = jnp.concatenate([head, zc_ref[rows, cs], tail], axis=0)
            w = cw_ref[:, cs]
            y = (w[0:1] * xe[7:7 + DN_ROWS] + w[1:2] * xe[8:8 + DN_ROWS]
                 + w[2:3] * xe[9:9 + DN_ROWS] + w[3:4] * xe[10:10 + DN_ROWS])
            y = _silu(y)
            if part == 0:
                y = _head_l2(y) * (DN_DK ** -0.5)
            elif part == 1:
                y = _head_l2(y)
            dst[rows, :] = y
        zm = zm_ref[rows, :]
        xa = zm + dtb_ref[...]
        logd = -jnp.exp(alog_ref[...]) * (jnp.maximum(xa, 0.0) + jnp.log1p(jnp.exp(-jnp.abs(xa))))
        beta = 1.0 / (1.0 + jnp.exp(-zm))
        logd3 = split3(logd)
        for d in range(2):
            b_s[d, rows, :] = _widen([beta[:, b_off + 4 * d + hd:b_off + 4 * d + hd + 1]
                                      for hd in range(DN_HEADS)], DN_ROWS)
            g = _bdot(tri[d], logd3)
            g_s[d, rows, :] = _widen([g[:, a_off + 4 * d + hd:a_off + 4 * d + hd + 1]
                                      for hd in range(DN_HEADS)], DN_ROWS)
        return carry

    ii = lax.broadcasted_iota(jnp.int32, (DN_CHUNK, wide), 0)
    jj = _lane((DN_CHUNK, wide)) & (DN_CHUNK - 1)
    blk = _lane((DN_CHUNK, wide)) >> 6
    diag = ii == jj
    eye = jnp.where(diag, 1.0, 0.0)
    half_mask = [jnp.where((_lane((DN_CHUNK, LANES)) >> 6) == half, 1.0, 0.0).astype(BF16) for half in range(2)]
    zero_block = jnp.zeros((DN_CHUNK, LANES), BF16)

    def bd(z):
        zb = z.astype(BF16)
        rows = []
        for hd in range(DN_HEADS):
            col, half = divmod(hd, 2)
            kept = zb[:, col * LANES:(col + 1) * LANES] * half_mask[half]
            rows.append(jnp.concatenate([kept, zero_block] if col == 0 else [zero_block, kept], axis=1))
        return jnp.concatenate(rows, axis=0)

    def fold(gram):
        out = jnp.where(blk == 0, gram[0:DN_CHUNK], 0.0)
        for hd in range(1, DN_HEADS):
            out = out + jnp.where(blk == hd, gram[hd * DN_CHUNK:(hd + 1) * DN_CHUNK], 0.0)
        return out

    tri_masks = []
    for d in range(2):
        incl = (jj <= ii) if d == 0 else (jj >= ii)
        strict = (jj < ii) if d == 0 else (jj > ii)
        pair = [((ii >> (lvl + 1)) == (jj >> (lvl + 1)))
                & (((ii >> lvl) & 1) == (1 - d)) & (((jj >> lvl) & 1) == d) for lvl in range(6)]
        tri_masks.append((incl, strict, pair))

    def prepare(views, step, carry):
        q_s, k_s, v_s, b_s, g_s, o_s, c_s, mp_s = views[2:]
        chunks = [step * PREP_CHUNKS + i for i in range(PREP_CHUNKS)]
        rows = [pl.ds(pl.multiple_of(c * DN_CHUNK, DN_CHUNK), DN_CHUNK) for c in chunks]
        qkv = [(q_s[r, :], k_s[r, :], v_s[r, :]) for r in rows]
        inst = [(ci, d) for ci in range(PREP_CHUNKS) for d in range(2)]
        beta = {(ci, d): b_s[d, rows[ci], :] for ci, d in inst}
        kb = {(ci, d): qkv[ci][1] * beta[ci, d] for ci, d in inst}
        r = [lax.dot_general(jnp.concatenate([kb[ci, 0], kb[ci, 1], qkv[ci][0]], axis=0).astype(BF16),
                             bd(qkv[ci][1]), _NT, preferred_element_type=F32)
             for ci in range(PREP_CHUNKS)]
        g, a, qk, eg, t = {}, {}, {}, {}, {}
        for ci, d in inst:
            incl, strict, pair = tri_masks[d]
            g[ci, d] = g_s[d, rows[ci], :]
            g_row = jnp.sum(jnp.where(diag, g[ci, d], 0.0), axis=0, keepdims=True)
            dm = jnp.where(incl, jnp.exp(jnp.where(incl, g[ci, d] - g_row, 0.0)), 0.0)
            a[ci, d] = jnp.where(strict, r[ci][d * DN_CHUNK:(d + 1) * DN_CHUNK] * dm, 0.0)
            qk[ci, d] = r[ci][2 * DN_CHUNK:] * dm
            eg[ci, d] = jnp.exp(g[ci, d])
            t[ci, d] = eye - jnp.where(pair[0], a[ci, d], 0.0)
        for lvl in range(1, 6):
            te = {i: _bdot(t[i].astype(BF16), bd(jnp.where(tri_masks[i[1]][2][lvl], a[i], 0.0))) for i in inst}
            t = {i: t[i] - _bdot(te[i].astype(BF16), bd(t[i])) for i in inst}
        nb = {i: jnp.where(diag, 0.0, t[i]).astype(BF16) for i in inst}
        rhs_u = {(ci, d): qkv[ci][2] * beta[ci, d] for ci, d in inst}
        rhs_w = {i: kb[i] * eg[i] for i in inst}
        q2 = {i: qk[i] + _bdot(qk[i].astype(BF16), bd(nb[i])) for i in inst}
        both = {i: jnp.concatenate([nb[i], q2[i].astype(BF16)], axis=0) for i in inst}
        ru = {i: _bdot(both[i], bd(rhs_u[i])) for i in inst}
        rw = {i: _bdot(both[i], bd(rhs_w[i])) for i in inst}
        u = {i: rhs_u[i] + ru[i][0:DN_CHUNK] for i in inst}
        w = {i: rhs_w[i] + rw[i][0:DN_CHUNK] for i in inst}
        kd = {}
        for ci, d in inst:
            g_last = g[ci, d][DN_CHUNK - 1:DN_CHUNK] if d == 0 else g[ci, d][0:1]
            kd[ci, d] = (qkv[ci][1] * jnp.exp(g_last - g[ci, d])).astype(BF16)
        p = {(ci, d): qkv[ci][0] * eg[ci, d] - rw[ci, d][DN_CHUNK:] for ci, d in inst}
        o0 = {i: ru[i][DN_CHUNK:] for i in inst}
        m = {i: fold(lax.dot_general(kd[i], w[i].astype(BF16), _TN, preferred_element_type=F32)) for i in inst}
        cc = {i: fold(lax.dot_general(kd[i], u[i].astype(BF16), _TN, preferred_element_type=F32)) for i in inst}
        for ci, d in inst:
            mrow = pl.ds(pl.multiple_of(chunks[ci] * (2 * DN_CHUNK), 2 * DN_CHUNK), 2 * DN_CHUNK)
            c_s[d, rows[ci], :] = cc[ci, d]
            mp_s[d, mrow, :] = jnp.concatenate([m[ci, d], p[ci, d]], axis=0).astype(BF16)
        for ci in range(PREP_CHUNKS):
            o_s[rows[ci], :] = o0[ci, 0] + o0[ci, 1]
        return carry

    for sq in range(n_seqs):
        views = (zc_all.at[sq], zm_all.at[sq]) + tuple(ref.at[sq] for ref in scratch)
        lax.fori_loop(0, n_blocks, functools.partial(preprocess, views), 0)
        lax.fori_loop(0, n_chunks // PREP_CHUNKS, functools.partial(prepare, views), 0)

    chains = [(sq, d) for sq in range(n_seqs) for d in range(2)]

    def scan(i, states):
        new = []
        for (sq, d), state in zip(chains, states):
            c = i if d == 0 else n_chunks - 1 - i
            r0 = pl.multiple_of(c * DN_CHUNK, DN_CHUNK)
            rows = pl.ds(r0, DN_CHUNK)
            mrow = pl.ds(pl.multiple_of(c * (2 * DN_CHUNK), 2 * DN_CHUNK), 2 * DN_CHUNK)
            edge = pl.ds(pl.multiple_of(r0 + (DN_CHUNK - 8 if d == 0 else 0), 8), 8)
            g_edge = g_all[sq, d, edge, :]
            g_last = g_edge[7:8] if d == 0 else g_edge[0:1]
            res = _bdot(mp_all[sq, d, mrow, :], bd(state))
            o_all[sq, rows, :] = o_all[sq, rows, :] + res[DN_CHUNK:]
            new.append(state * jnp.exp(g_last) - res[0:DN_CHUNK] + c_all[sq, d, rows, :])
        return tuple(new)

    place = [jnp.where(diag & (blk == hd), 1.0, 0.0).astype(BF16) for hd in range(DN_HEADS)]

    def to_wide(heads):
        parts = [_bdot(term, place[hd]) for hd, x in enumerate(heads) for term in terms3(x)]
        return functools.reduce(lambda a, b: a + b, parts)

    def head_of(s, hd):
        parts = [lax.dot_general(term, place[hd], _NT, preferred_element_type=F32) for term in terms3(s)]
        return functools.reduce(lambda a, b: a + b, parts)

    if has_state:
        init = tuple(to_wide([s0_ref[sq, d, hd] for hd in range(DN_HEADS)]) for sq, d in chains)
    else:
        init = tuple(jnp.zeros((DN_DK, wide), F32) for _ in chains)
    fin = lax.fori_loop(0, n_chunks, scan, init)
    if not has_state:
        for (sq, d), state in zip(chains, fin):
            for hd in range(DN_HEADS):
                sfin_ref[sq, d, hd] = head_of(state, hd)

    for sq in range(n_seqs):
        gate = _silu(zc_all[sq, :, 3 * wide:4 * wide])
        o_ref[sq] = (_head_rms(o_all[sq], og_ref[...]) * gate).astype(o_ref.dtype)


def _deltanet(zc, zm, lw, layer, state=None):
    b, seq, _ = zc.shape
    wide = DN_HEADS * DN_DV
    has_state = state is not None
    ns = max(n for n in (1, 2, 4) if b % n == 0 and n * seq * DN_VMEM_BYTES_PER_ROW <= DN_VMEM_BUDGET)
    per_b = lambda w: pl.BlockSpec((ns, seq, w), lambda bi: (bi, 0, 0))
    const = lambda shape: pl.BlockSpec((None,) + shape, lambda bi: (layer,) + (0,) * len(shape))
    st_spec = pl.BlockSpec((ns, 2, DN_HEADS, DN_DK, DN_DV), lambda bi: (bi, 0, 0, 0, 0))
    in_specs = [per_b(4 * wide), per_b(LANES)]
    args = [zc, zm]
    if has_state:
        in_specs.append(pl.BlockSpec((ns, None, 2, DN_HEADS, DN_DK, DN_DV),
                                     lambda bi: (bi, layer, 0, 0, 0, 0)))
        args.append(state)
    in_specs += [const((DN_CONV, DN_QKV)), const((1, LANES)), const((1, LANES)), const((1, wide))]
    args += [lw["dn_conv_w"], lw["dn_alog_row"], lw["dn_dtb_row"], lw["dn_out_g"]]
    out_specs = [per_b(wide)]
    out_shape = [jax.ShapeDtypeStruct((b, seq, wide), BF16)]
    if not has_state:
        out_specs.append(st_spec)
        out_shape.append(jax.ShapeDtypeStruct((b, 2, DN_HEADS, DN_DK, DN_DV), F32))
    res = pl.pallas_call(
        functools.partial(_deltanet_kernel, seq, has_state),
        grid=(b // ns,),
        in_specs=in_specs,
        out_specs=out_specs,
        out_shape=out_shape,
        scratch_shapes=[pltpu.VMEM((ns, seq, wide), F32), pltpu.VMEM((ns, seq, wide), F32),
                        pltpu.VMEM((ns, seq, wide), F32), pltpu.VMEM((ns, 2, seq, wide), F32),
                        pltpu.VMEM((ns, 2, seq, wide), F32), pltpu.VMEM((ns, seq, wide), F32),
                        pltpu.VMEM((ns, 2, seq, wide), F32), pltpu.VMEM((ns, 2, 2 * seq, wide), BF16)],
        compiler_params=_cparams(1),
        name="deltanet_lat" if has_state else "deltanet_ctx",
    )(*args)
    return (res[0], None) if has_state else (res[0], res[1])


def _outffn_kernel(final, oa_ref, ob_ref, oc_ref, od_ref, x_ref, mod_ref, g2_ref, wo_ref, wg_ref,
                   wu_ref, wd_ref, fg_ref, y_ref):
    tiles = range(TILES_PER_STEP)
    m = mod_ref[...]
    o = [jnp.concatenate([oa_ref[t], ob_ref[t], oc_ref[t], od_ref[t]], axis=-1).astype(BF16) for t in tiles]
    x1 = [x_ref[t] + m[2:3] * _bdot(o[t], wo_ref[...]) for t in tiles]
    h = [(_rms_full(x1[t], g2_ref[...]) * (1.0 + m[4:5]) + m[3:4]).astype(BF16) for t in tiles]
    gate = [_bdot(h[t], wg_ref[...]) for t in tiles]
    up = [_bdot(h[t], wu_ref[...]) for t in tiles]
    act = [(_silu(gate[t]) * up[t]).astype(BF16) for t in tiles]
    x2 = [x1[t] + m[5:6] * _bdot(act[t], wd_ref[...]) for t in tiles]
    for t in tiles:
        y_ref[t] = _rms_full(x2[t], fg_ref[...]) if final else x2[t]


def _outffn(outs, x, mods, per_batch_mods, lw, layer, final_g, final):
    b, l, _ = x.shape
    tm, ts = ROW_TILE, TILES_PER_STEP
    n_tiles = b * l // tm
    tiles_per_seq = l // tm
    assert not per_batch_mods or tiles_per_seq % ts == 0
    row = lambda w: pl.BlockSpec((ts, tm, w), lambda i: (i, 0, 0))
    const = lambda shape: pl.BlockSpec((None,) + shape, lambda i: (layer,) + (0,) * len(shape),
                                       pipeline_mode=pl.Buffered(1))
    mod_spec = pl.BlockSpec((None, None, MOD_CHUNKS, D_MODEL),
                            (lambda i: (layer, 1 + i * ts // tiles_per_seq, 0, 0)) if per_batch_mods
                            else (lambda i: (layer, 0, 0, 0)))
    tiled = lambda a: a.reshape(n_tiles, tm, a.shape[-1])
    y = pl.pallas_call(
        functools.partial(_outffn_kernel, final),
        grid=(n_tiles // ts,),
        in_specs=[row(256), row(256), row(256), row(256), row(D_MODEL), mod_spec, const((1, D_MODEL)),
                  const((D_MODEL, D_MODEL)), const((D_MODEL, D_FF)), const((D_MODEL, D_FF)),
                  const((D_FF, D_MODEL)), pl.BlockSpec((1, D_MODEL), lambda i: (0, 0))],
        out_specs=row(D_MODEL),
        out_shape=jax.ShapeDtypeStruct((n_tiles, tm, D_MODEL), F32),
        compiler_params=_cparams(1),
        name="outffn",
    )(*[tiled(a) for a in outs], tiled(x), mods, lw["norm2_g"], lw["w_out"], lw["w_gate"], lw["w_up"],
      lw["w_down"], final_g)
    return y.reshape(b, l, D_MODEL)


def _rope_tables(n):
    t = np.arange(n)

    def axis(pos, half):
        inv = np.float32(ROPE_BASE) ** (-np.arange(half, dtype=np.float32) / np.float32(half))
        ang = pos.astype(np.float32)[:, None] * inv[None, :]
        c, s = np.cos(ang), np.sin(ang)
        return np.concatenate([c, c], -1), np.concatenate([-s, s], -1)

    cr, sr = axis(t // GRID_W, 16)
    cc, sc = axis(t % GRID_W, 16)
    cos64 = np.tile(np.concatenate([cr, cc], -1), (1, 2))
    sin64 = np.tile(np.concatenate([sr, sc], -1), (1, 2))
    cr, sr = axis(t // GRID_W, 8)
    cc, sc = axis(t % GRID_W, 8)
    cos32, sin32 = np.concatenate([cr, cc], -1), np.concatenate([sr, sc], -1)
    one, zero = np.ones((n, 1), np.float32), np.zeros((n, 1), np.float32)
    cosm = np.concatenate([np.tile(one, (1, 64)), cos32, np.tile(one, (1, 32))], -1)
    sinm = np.concatenate([np.tile(zero, (1, 64)), sin32, np.tile(zero, (1, 32))], -1)
    coskr = np.concatenate([cos32, np.tile(one, (1, 96))], -1)
    sinkr = np.concatenate([sin32, np.tile(zero, (1, 96))], -1)
    return tuple(jnp.asarray(a, F32) for a in (cos64, sin64, cosm, sinm, coskr, sinkr))


_QA_ORDER = ((0, 64), (128, 192), (64, 128), (192, 256))


def _stacked_weights(p):
    w_in = jnp.swapaxes(p["w_in"], 1, 2).astype(BF16)
    w_out = p["w_out"]
    w_out = jnp.concatenate([w_out[:, a:b] for a, b in _QA_ORDER + ((256, w_out.shape[1]),)], axis=1).astype(BF16)
    wq = p["mla_wq_up"].reshape(DEPTH, MLA_Q_LORA, MLA_HEADS, MLA_NOPE + MLA_ROPE)
    wq = jnp.pad(wq, ((0, 0), (0, 0), (0, 0), (0, LANES - MLA_NOPE - MLA_ROPE))).reshape(DEPTH, MLA_Q_LORA, 4 * LANES)
    wkv = p["mla_wkv_up"].reshape(DEPTH, MLA_KV_LORA, MLA_HEADS, MLA_NOPE + MLA_V)
    wk_top = jnp.pad(wkv[..., :MLA_NOPE], ((0, 0), (0, 0), (0, 0), (0, LANES - MLA_NOPE)))
    place = jnp.pad(jnp.eye(MLA_ROPE, dtype=F32), ((0, LANES - MLA_ROPE), (MLA_NOPE, LANES - MLA_NOPE - MLA_ROPE)))
    wk_bot = jnp.broadcast_to(place[None, :, None, :], (DEPTH, LANES, MLA_HEADS, LANES))
    wk = jnp.concatenate([wk_top, wk_bot], axis=1).reshape(DEPTH, 2 * LANES, 4 * LANES)
    wv = wkv[..., MLA_NOPE:].reshape(DEPTH, MLA_KV_LORA, MLA_HEADS * MLA_V)
    gate_row = lambda v: jnp.pad(v.reshape(DEPTH, 1, DN_GATES),
                                 ((0, 0), (0, 0), (IN_TAIL_SKIP, LANES - IN_TAIL_SKIP - DN_GATES)))
    row = lambda v: v[:, None, :]
    return {
        "norm1_g": row(p["norm1_g"]), "norm2_g": row(p["norm2_g"]),
        "w_in": w_in, "w_out": w_out,
        "qn_g": row(jnp.tile(p["gqa_qn_g"], (1, 4))), "kn_g": row(jnp.tile(p["gqa_kn_g"], (1, 2))),
        "mla_qn_g": row(p["mla_qn_g"]), "mla_kvn_g": row(p["mla_kvn_g"]),
        "wq": wq.astype(BF16), "wk": wk.astype(BF16), "wv": wv.astype(BF16),
        "dn_conv_w": p["dn_conv_w"], "dn_alog_row": gate_row(p["dn_a_log"]),
        "dn_dtb_row": gate_row(p["dn_dt_bias"]), "dn_out_g": row(jnp.tile(p["dn_out_g"], (1, DN_HEADS))),
        "w_gate": p["ffn_w_gate"].astype(BF16), "w_up": p["ffn_w_up"].astype(BF16),
        "w_down": p["ffn_w_down"].astype(BF16),
    }


def kernel(x_prompt, x_sample, cache_gqa_k, cache_gqa_v, cache_na_k, cache_na_v, state_dn,
           cache_mla_ckv, cache_mla_krope, c, c_ctx, norm1_g, norm2_g, w_mod, b_mod, w_in, w_out,
           gqa_qn_g, gqa_kn_g, na_bias, dn_conv_w, dn_a_log, dn_dt_bias, dn_out_g, mla_qn_g,
           mla_wq_up, mla_kvn_g, mla_wkv_up, ffn_w_gate, ffn_w_up, ffn_w_down, final_g):
    p = {"norm1_g": norm1_g, "norm2_g": norm2_g, "w_in": w_in, "w_out": w_out, "gqa_qn_g": gqa_qn_g,
         "gqa_kn_g": gqa_kn_g, "dn_conv_w": dn_conv_w, "dn_a_log": dn_a_log, "dn_dt_bias": dn_dt_bias,
         "dn_out_g": dn_out_g, "mla_qn_g": mla_qn_g, "mla_wq_up": mla_wq_up, "mla_kvn_g": mla_kvn_g,
         "mla_wkv_up": mla_wkv_up, "ffn_w_gate": ffn_w_gate, "ffn_w_up": ffn_w_up, "ffn_w_down": ffn_w_down}
    nb_ctx, seq_ctx, _ = x_prompt.shape
    nb_lat, seq_lat, _ = x_sample.shape
    past = cache_gqa_k.shape[2]
    fg = final_g[None]

    cond = jnp.concatenate([c_ctx[None], c, jnp.zeros((16 - 1 - nb_lat, D_MODEL), F32)], axis=0)
    mods = _modulation(cond, w_mod, b_mod).reshape(DEPTH, 16, MOD_CHUNKS, D_MODEL)
    lw = _stacked_weights(p)

    x = x_prompt
    ctx_out = []
    for l in range(DEPTH):
        o_a, ka, va, o_b, kb, vb, zc, o_d, ckv, zm, krr = _inproj(x, mods, False, lw, l, None)
        o_c, s_dn = _deltanet(zc, zm, lw, l)
        x = _outffn((o_a, o_b, o_c, o_d), x, mods, False, lw, l, fg, l == DEPTH - 1)
        ctx_out.append((ka, va, kb, vb, s_dn, ckv, krr[:, :, :MLA_ROPE]))
    y_prompt = x
    new = [jnp.stack([s[i] for s in ctx_out], axis=1) for i in range(7)]
    for i in range(4):
        t = new[i].reshape(nb_ctx, DEPTH, -1, HEAD_DIM, seq_ctx)
        new[i] = jnp.transpose(t, (0, 1, 4, 2, 3))

    ropes = _rope_tables(seq_lat)
    keys_t = lambda c: jnp.transpose(c, (0, 1, 3, 4, 2)).reshape(nb_lat, DEPTH, -1, past)
    ck_a, cv_a, ck_b, cv_b = keys_t(cache_gqa_k), keys_t(cache_gqa_v), keys_t(cache_na_k), keys_t(cache_na_v)
    c_kr = jnp.pad(cache_mla_krope, ((0, 0), (0, 0), (0, 0), (0, LANES - MLA_ROPE)))
    bias_blocks = _na_bias_blocks(na_bias)
    x = x_sample
    for l in range(DEPTH):
        qa, ka, va, qb, kb, vb, zc, qd, ckv, zm, krr = _inproj(x, mods, True, lw, l, ropes)
        o_a = _attn_pair(qa, ka, va, (0, 0), ck_a, cv_a, l, name="gqa_lat")
        o_b = _na_latent(qb, kb, vb, ck_b, cv_b, l, bias_blocks)
        o_c, _ = _deltanet(zc, zm, lw, l, state=state_dn)
        o_d = _mla(qd, ckv, krr, lw, l, cache_mla_ckv, c_kr)
        x = _outffn((o_a, o_b, o_c, o_d), x, mods, True, lw, l, fg, l == DEPTH - 1)
    y_sample = x

    return (y_prompt, y_sample, *new)
```

```python
import functools

import numpy as np
import jax
import jax.numpy as jnp
from jax import lax
from jax.experimental import pallas as pl
from jax.experimental.pallas import tpu as pltpu

F32 = jnp.float32
BF16 = jnp.bfloat16

D_MODEL = 1024
DEPTH = 2
GRID_W = 64
HEAD_DIM = 64
ROPE_BASE = 10000.0
NEG_INF = -1e30
MOD_CHUNKS = 6
GQA_HEADS, GQA_KV_HEADS = 4, 2
NA_HEADS, NA_KH, NA_KW = 4, 8, 16
DN_HEADS, DN_DK, DN_DV, DN_CONV, DN_CHUNK = 4, 64, 64, 4, 64
DN_QKV = DN_HEADS * (2 * DN_DK + DN_DV)
DN_GATES = 2 * DN_HEADS
MLA_HEADS, MLA_Q_LORA, MLA_KV_LORA, MLA_NOPE, MLA_ROPE, MLA_V = 4, 256, 128, 64, 32, 64
MLA_SCALE = (MLA_NOPE + MLA_ROPE) ** -0.5
D_FF = -(-8 * D_MODEL // (3 * 256)) * 256
EPS = 1e-6
LOG2E = 1.4426950408889634

LANES = 128
ROW_TILE = 256
Q_TILE = 256
NA_BAND = 768
TILES_PER_STEP = 2
LAT_TILES_PER_STEP = 4
ATTN_Q_TILE = 1024
HEAD_LOOKAHEAD = 1
DN_VMEM_BYTES_PER_ROW = 12288 + 9216
DN_VMEM_BUDGET = 46 * 1024 * 1024
DN_ROWS = 256
PREP_CHUNKS = 4
IN_COLS = 2736
IN_TAIL_SKIP = 512 - (2 * 8 + 256 + 128 + 32)
VMEM_LIMIT = 56 * 1024 * 1024

_NT = (((1,), (1,)), ((), ()))
_TN = (((0,), (0,)), ((), ()))


def _cparams(n_axes):
    return pltpu.CompilerParams(dimension_semantics=("arbitrary",) * n_axes,
                                vmem_limit_bytes=VMEM_LIMIT)


def _lane(shape):
    return lax.broadcasted_iota(jnp.int32, shape, len(shape) - 1)


def _silu(x):
    return x / (1.0 + jnp.exp(-x))


def _rms_full(x, g):
    return x * lax.rsqrt(jnp.mean(x * x, axis=-1, keepdims=True) + EPS) * g


def _seg64_sum(x):
    lo = _lane(x.shape) < HEAD_DIM
    s_lo = jnp.sum(jnp.where(lo, x, 0.0), axis=-1, keepdims=True)
    s_hi = jnp.sum(jnp.where(lo, 0.0, x), axis=-1, keepdims=True)
    return jnp.where(lo, s_lo, s_hi)


def _head_rms(x, g):
    parts = []
    for p in range(x.shape[-1] // LANES):
        xp = x[:, p * LANES:(p + 1) * LANES]
        ms = _seg64_sum(xp * xp) * (1.0 / HEAD_DIM)
        parts.append(xp * lax.rsqrt(ms + EPS))
    y = parts[0] if len(parts) == 1 else jnp.concatenate(parts, axis=-1)
    return y * g


def _head_l2(x):
    parts = []
    for p in range(x.shape[-1] // LANES):
        xp = x[:, p * LANES:(p + 1) * LANES]
        parts.append(xp * lax.rsqrt(_seg64_sum(xp * xp) + EPS))
    return parts[0] if len(parts) == 1 else jnp.concatenate(parts, axis=-1)


def _rope(x, cos, sin, half):
    first = (_lane(x.shape) & (2 * half - 1)) < half
    rot = jnp.where(first, pltpu.roll(x, LANES - half, 1), pltpu.roll(x, half, 1))
    return x * cos + rot * sin


def _softmax_parts(scores):
    m = jnp.max(scores[0], axis=-1, keepdims=True)
    for s in scores[1:]:
        m = jnp.maximum(m, jnp.max(s, axis=-1, keepdims=True))
    es = [jnp.exp2(s - m) for s in scores]
    l = jnp.sum(es[0], axis=-1, keepdims=True)
    for e in es[1:]:
        l = l + jnp.sum(e, axis=-1, keepdims=True)
    return es, 1.0 / l


def _bdot(a, b):
    return jnp.dot(a, b, preferred_element_type=F32)


def _pipelined_heads(n_heads, scores, attend):
    outs = []
    queue = [scores(hd) for hd in range(min(HEAD_LOOKAHEAD, n_heads))]
    for hd in range(n_heads):
        if hd + HEAD_LOOKAHEAD < n_heads:
            queue.append(scores(hd + HEAD_LOOKAHEAD))
        es, rl = _softmax_parts(queue.pop(0))
        outs.append(attend(hd, es) * rl)
    return outs


def _pair_attention(q_blocks, sources, qmap):
    lo = _lane(q_blocks[0].shape) < HEAD_DIM

    def scores(hd):
        p, half = divmod(hd, 2)
        qm = jnp.where(lo if half == 0 else jnp.logical_not(lo), q_blocks[p], 0.0).astype(BF16)
        return [_bdot(qm, k_block(qmap[p])) if transposed
                else lax.dot_general(qm, k_block(qmap[p]), _NT, preferred_element_type=F32)
                for k_block, _, transposed in sources]

    def attend(hd, es):
        kv = qmap[hd // 2]
        parts = [lax.dot_general(e.astype(BF16), v_block(kv), _NT, preferred_element_type=F32) if transposed
                 else _bdot(e.astype(BF16), v_block(kv)) for e, (_, v_block, transposed) in zip(es, sources)]
        return functools.reduce(lambda a, b: a + b, parts)

    outs = _pipelined_heads(2 * len(qmap), scores, attend)
    return [jnp.where(lo, outs[2 * p], outs[2 * p + 1]) for p in range(len(qmap))]


def _mla_attention(q_heads, k_head, v_block):
    lo = _lane(q_heads[0].shape) < MLA_V

    def scores(hd):
        return [lax.dot_general(q_heads[hd], k_head(hd), _NT, preferred_element_type=F32)]

    def attend(hd, es):
        return _bdot(es[0].astype(BF16), v_block(hd // 2))

    outs = _pipelined_heads(MLA_HEADS, scores, attend)
    return [jnp.where(lo, outs[2 * p], outs[2 * p + 1]) for p in range(MLA_HEADS // 2)]


def _mod_kernel(c_ref, w_ref, b_ref, o_ref):
    s = _silu(c_ref[...]).astype(BF16)
    o_ref[...] = _bdot(s, w_ref[...].astype(BF16)) + b_ref[...]


def _modulation(cond, w_mod, b_mod):
    n = MOD_CHUNKS * D_MODEL
    tn = 1536
    return pl.pallas_call(
        _mod_kernel,
        grid=(DEPTH, n // tn),
        in_specs=[pl.BlockSpec((16, D_MODEL), lambda l, j: (0, 0)),
                  pl.BlockSpec((None, D_MODEL, tn), lambda l, j: (l, 0, j)),
                  pl.BlockSpec((None, 1, tn), lambda l, j: (l, 0, j))],
        out_specs=pl.BlockSpec((None, 16, tn), lambda l, j: (l, 0, j)),
        out_shape=jax.ShapeDtypeStruct((DEPTH, 16, n), F32),
        compiler_params=_cparams(2),
        name="modulation",
    )(cond, w_mod, b_mod.reshape(DEPTH, 1, n))


_IN_OUT_WIDTHS = (256, 128, 128, 256, 256, 256, 1024, 512, 128, 128, 128)
_IN_OUT_WIDTHS_CTX = (256, 128, 128, 256, 256, 256, 1024, 256, 128, 128, 128)
_IN_OUT_DTYPES_LAT = (BF16, BF16, BF16, BF16, BF16, BF16, F32, BF16, BF16, F32, BF16)
_IN_OUT_DTYPES_CTX = (BF16, F32, F32, BF16, F32, F32, F32, BF16, F32, F32, F32)


def _inproj_kernel(positioned, *refs):
    (x_ref, mod_ref, g1_ref, w_ref, qng_ref, kng_ref, mqg_ref, wq_ref, mkg_ref) = refs[:9]
    if positioned:
        cos64_ref, sin64_ref, cosm_ref, sinm_ref, coskr_ref, sinkr_ref = refs[9:15]
        n_in = 15
    else:
        wk_ref, wv_ref = refs[9:11]
        n_in = 11
    (qa_ref, ka_ref, va_ref, qb_ref, kb_ref, vb_ref, zc_ref, qd_ref, ckv_ref, zm_ref,
     krr_ref) = refs[n_in:]
    tiles = range(x_ref.shape[0])
    tm = x_ref.shape[1]
    lane = _lane((tm, LANES))
    lo = lane < HEAD_DIM

    m = mod_ref[...]
    hb = jnp.concatenate([(_rms_full(x_ref[t], g1_ref[...]) * (1.0 + m[1:2]) + m[0:1]).astype(BF16)
                          for t in tiles], axis=0)

    def project(c0, c1):
        z = lax.dot_general(hb, w_ref[c0:c1, :], _NT, preferred_element_type=F32)
        return [z[t * tm:(t + 1) * tm] for t in tiles]

    za = project(0, 512)
    zb = project(512, 1280)

    for t in tiles:
        q = _head_rms(za[t][:, 0:256], qng_ref[...])
        k = _head_rms(za[t][:, 256:384], kng_ref[...])
        q0, q1 = q[:, 0:128], q[:, 128:256]
        q0, q1 = jnp.where(lo, q0, pltpu.roll(q1, HEAD_DIM, 1)), jnp.where(lo, pltpu.roll(q0, HEAD_DIM, 1), q1)
        if positioned:
            cos, sin = cos64_ref[t], sin64_ref[t]
            q0, q1 = _rope(q0, cos, sin, 16), _rope(q1, cos, sin, 16)
            k = _rope(k, cos, sin, 16)
        q0, q1 = q0 * (HEAD_DIM ** -0.5 * LOG2E), q1 * (HEAD_DIM ** -0.5 * LOG2E)
        v = za[t][:, 384:512]
        if not positioned:
            kb16, vb16 = k.astype(BF16), v.astype(BF16)
            q0, q1 = _pair_attention([q0, q1], [(lambda i: kb16, lambda i: vb16, False)], (0, 0))
        qa_ref[t] = jnp.concatenate([q0, q1], axis=-1).astype(qa_ref.dtype)
        ka_ref[t] = k.astype(ka_ref.dtype) if positioned else k.T
        va_ref[t] = v.astype(va_ref.dtype) if positioned else v.T

    zd = project(IN_COLS - 4 * LANES, IN_COLS)
    zc = project(1280, 2304)

    for t in tiles:
        q = zb[t][:, 0:256] * (HEAD_DIM ** -0.5 * LOG2E)
        k, v = zb[t][:, 256:512], zb[t][:, 512:768]
        if not positioned:
            kb16, vb16 = k.astype(BF16), v.astype(BF16)
            q = jnp.concatenate(_pair_attention(
                [q[:, 0:LANES], q[:, LANES:]],
                [(lambda i, kb16=kb16: kb16[:, i * LANES:(i + 1) * LANES],
                  lambda i, vb16=vb16: vb16[:, i * LANES:(i + 1) * LANES], False)], (0, 1)), axis=-1)
        qb_ref[t] = q.astype(qb_ref.dtype)
        kb_ref[t] = k.astype(kb_ref.dtype) if positioned else k.T
        vb_ref[t] = v.astype(vb_ref.dtype) if positioned else v.T

    lead = IN_TAIL_SKIP + 2 * DN_GATES
    shifted = []
    for t in tiles:
        rolled = [pltpu.roll(zd[t][:, j * LANES:(j + 1) * LANES], LANES - lead, 1) for j in range(4)]
        keep = lane < LANES - lead
        shifted.append([jnp.where(keep, rolled[j], rolled[(j + 1) % 4]) for j in range(4)])

    cq = jnp.concatenate([_rms_full(jnp.concatenate(shifted[t][0:2], axis=-1), mqg_ref[...]).astype(BF16)
                          for t in tiles], axis=0)
    qm = _bdot(cq, wq_ref[...])
    for t in tiles:
        zc_ref[t] = zc[t]
        zm_ref[t] = zd[t][:, 0:LANES]
        ckv = _rms_full(shifted[t][2], mkg_ref[...])
        ckv_ref[t] = ckv.astype(ckv_ref.dtype)
        kr = jnp.where(lane < MLA_ROPE, shifted[t][3], 0.0)
        q = qm[t * tm:(t + 1) * tm]
        if positioned:
            kr = _rope(kr, coskr_ref[t], sinkr_ref[t], 8)
            cm, sm = cosm_ref[t], sinm_ref[t]
            q = jnp.concatenate([_rope(q[:, i * LANES:(i + 1) * LANES], cm, sm, 8)
                                 for i in range(MLA_HEADS)], axis=-1)
        krr_ref[t] = kr.astype(krr_ref.dtype)
        q = q * (MLA_SCALE * LOG2E)
        if not positioned:
            c16 = ckv.astype(BF16)
            k16 = _bdot(jnp.concatenate([c16, kr.astype(BF16)], axis=-1), wk_ref[...]).astype(BF16)
            v16 = _bdot(c16, wv_ref[...]).astype(BF16)
            q = jnp.concatenate(_mla_attention(
                [q[:, i * LANES:(i + 1) * LANES].astype(BF16) for i in range(MLA_HEADS)],
                lambda i, k16=k16: k16[:, i * LANES:(i + 1) * LANES],
                lambda i, v16=v16: v16[:, i * LANES:(i + 1) * LANES]), axis=-1)
        qd_ref[t] = q.astype(qd_ref.dtype)


def _inproj(x, mods, per_batch_mods, lw, layer, ropes):
    b, l, _ = x.shape
    positioned = ropes is not None
    tm, ts = ROW_TILE, (LAT_TILES_PER_STEP if positioned else TILES_PER_STEP)
    n_tiles = b * l // tm
    tiles_per_seq = l // tm
    row = lambda w: pl.BlockSpec((ts, tm, w), lambda i: (i, 0, 0))
    const = lambda shape: pl.BlockSpec((None,) + shape, lambda i: (layer,) + (0,) * len(shape))
    assert not per_batch_mods or tiles_per_seq % ts == 0
    mod_spec = pl.BlockSpec((None, None, MOD_CHUNKS, D_MODEL),
                            (lambda i: (layer, 1 + i * ts // tiles_per_seq, 0, 0)) if per_batch_mods
                            else (lambda i: (layer, 0, 0, 0)))
    in_specs = [row(D_MODEL), mod_spec, const((1, D_MODEL)), const((IN_COLS, D_MODEL)),
                const((1, 256)), const((1, 128)), const((1, MLA_Q_LORA)),
                const((MLA_Q_LORA, 4 * LANES)), const((1, MLA_KV_LORA))]
    args = [x.reshape(n_tiles, tm, D_MODEL), mods, lw["norm1_g"], lw["w_in"], lw["qn_g"], lw["kn_g"],
            lw["mla_qn_g"], lw["wq"], lw["mla_kvn_g"]]
    if positioned:
        steps_per_seq = tiles_per_seq // ts
        in_specs += [pl.BlockSpec((ts, tm, LANES), lambda i: (i % steps_per_seq, 0, 0))] * 6
        args += [r.reshape(tiles_per_seq, tm, LANES) for r in ropes]
        widths, dtypes = _IN_OUT_WIDTHS, _IN_OUT_DTYPES_LAT
    else:
        assert tiles_per_seq == 1
        in_specs += [const((2 * LANES, 4 * LANES)), const((LANES, 2 * LANES))]
        args += [lw["wk"], lw["wv"]]
        widths, dtypes = _IN_OUT_WIDTHS_CTX, _IN_OUT_DTYPES_CTX
    flipped = () if positioned else (1, 2, 4, 5)
    shapes = [(n_tiles, w, tm) if i in flipped else (n_tiles, tm, w) for i, w in enumerate(widths)]
    outs = pl.pallas_call(
        functools.partial(_inproj_kernel, positioned),
        grid=(n_tiles // ts,),
        in_specs=in_specs,
        out_specs=[pl.BlockSpec((ts,) + shp[1:], lambda i: (i, 0, 0)) for shp in shapes],
        out_shape=[jax.ShapeDtypeStruct(shp, dt) for shp, dt in zip(shapes, dtypes)],
        compiler_params=_cparams(1),
        name="inproj_lat" if positioned else "inproj_attn_ctx",
    )(*args)
    return [o if i in flipped else o.reshape(b, l, o.shape[-1]) for i, o in enumerate(outs)]


def _attn_pair_kernel(qmap, q_ref, kc_ref, vc_ref, k_ref, v_ref, o_ref, kcbuf, vcbuf, kbuf, vbuf):
    @pl.when(pl.program_id(1) == 0)
    def _():
        kcbuf[...] = kc_ref[...].astype(BF16)
        vcbuf[...] = vc_ref[...].astype(BF16)
        kbuf[...] = k_ref[...].astype(BF16)
        vbuf[...] = v_ref[...].astype(BF16)

    rows = lambda buf: (lambda i: buf[i * LANES:(i + 1) * LANES, :])
    cols = lambda buf: (lambda i: buf[:, i * LANES:(i + 1) * LANES])
    q_blocks = [q_ref[:, p * LANES:(p + 1) * LANES].astype(F32) for p in range(len(qmap))]
    outs = _pair_attention(q_blocks, [(rows(kcbuf), rows(vcbuf), True), (cols(kbuf), cols(vbuf), False)], qmap)
    for p, o in enumerate(outs):
        o_ref[:, p * LANES:(p + 1) * LANES] = o.astype(o_ref.dtype)


def _attn_pair(q, k, v, qmap, kc_t, vc_t, layer, name):
    b, lq, wq = q.shape
    ls, wk = k.shape[1], k.shape[2]
    lc = kc_t.shape[3]
    tq = min(ATTN_Q_TILE, lq)
    cspec = pl.BlockSpec((None, None, wk, lc), lambda bi, i: (bi, layer, 0, 0))
    sspec = pl.BlockSpec((None, ls, wk), lambda bi, i: (bi, 0, 0))
    return pl.pallas_call(
        functools.partial(_attn_pair_kernel, qmap),
        grid=(b, lq // tq),
        in_specs=[pl.BlockSpec((None, tq, wq), lambda bi, i: (bi, i, 0)), cspec, cspec, sspec, sspec],
        out_specs=pl.BlockSpec((None, tq, wq), lambda bi, i: (bi, i, 0)),
        out_shape=jax.ShapeDtypeStruct((b, lq, wq), BF16),
        scratch_shapes=[pltpu.VMEM((wk, lc), BF16), pltpu.VMEM((wk, lc), BF16),
                        pltpu.VMEM((ls, wk), BF16), pltpu.VMEM((ls, wk), BF16)],
        compiler_params=_cparams(2),
        name=name,
    )(q, kc_t, vc_t, k, v)


def _mla_kernel(q_ref, ckvc_ref, krc_ref, ckv_ref, kr_ref, wk_ref, wv_ref, o_ref, kbuf, vbuf):
    @pl.when(pl.program_id(1) == 0)
    def _():
        def expand(c_ref, r_ref, r0, r1):
            c = c_ref[...].astype(BF16)
            ckr = jnp.concatenate([c, r_ref[...].astype(BF16)], axis=-1)
            kbuf[r0:r1, :] = _bdot(ckr, wk_ref[...]).astype(BF16)
            vbuf[r0:r1, :] = _bdot(c, wv_ref[...]).astype(BF16)
        off = ckvc_ref.shape[0]
        expand(ckvc_ref, krc_ref, 0, off)
        expand(ckv_ref, kr_ref, off, kbuf.shape[0])

    outs = _mla_attention([q_ref[:, hd * LANES:(hd + 1) * LANES].astype(BF16) for hd in range(MLA_HEADS)],
                          lambda i: kbuf[:, i * LANES:(i + 1) * LANES],
                          lambda i: vbuf[:, i * LANES:(i + 1) * LANES])
    for p, o in enumerate(outs):
        o_ref[:, p * LANES:(p + 1) * LANES] = o.astype(o_ref.dtype)


def _mla(q, ckv, kr, lw, layer, ckv_cache, kr_cache):
    b, lq, wq = q.shape
    ls, lc = ckv.shape[1], ckv_cache.shape[2]
    tq = min(ATTN_Q_TILE, lq)
    cspec = pl.BlockSpec((None, None, lc, LANES), lambda bi, i: (bi, layer, 0, 0))
    sspec = pl.BlockSpec((None, ls, LANES), lambda bi, i: (bi, 0, 0))
    return pl.pallas_call(
        _mla_kernel,
        grid=(b, lq // tq),
        in_specs=[pl.BlockSpec((None, tq, wq), lambda bi, i: (bi, i, 0)), cspec, cspec, sspec, sspec,
                  pl.BlockSpec((None, 2 * LANES, 4 * LANES), lambda bi, i: (layer, 0, 0)),
                  pl.BlockSpec((None, LANES, 2 * LANES), lambda bi, i: (layer, 0, 0))],
        out_specs=pl.BlockSpec((None, tq, 2 * LANES), lambda bi, i: (bi, i, 0)),
        out_shape=jax.ShapeDtypeStruct((b, lq, 2 * LANES), BF16),
        scratch_shapes=[pltpu.VMEM((lc + ls, 4 * LANES), BF16), pltpu.VMEM((lc + ls, 2 * LANES), BF16)],
        compiler_params=_cparams(2),
        name="mla_lat",
    )(q, ckv_cache, kr_cache, ckv, kr, lw["wk"], lw["wv"])


def _na_kernel(q_ref, k_ref, v_ref, kc_ref, vc_ref, bias_ref, o_ref):
    start = pl.multiple_of(pl.program_id(0) * 256, 256)
    kband = k_ref[pl.ds(start, NA_BAND), :].astype(BF16)
    vband = v_ref[pl.ds(start, NA_BAND), :].astype(BF16)
    kc = kc_ref[...].astype(BF16)
    vc = vc_ref[...].astype(BF16)
    tq = q_ref.shape[0]
    lo = _lane((tq, LANES)) < HEAD_DIM

    def scores(hd):
        p, half = divmod(hd, 2)
        sl = slice(p * LANES, (p + 1) * LANES)
        qm = jnp.where(lo if half == 0 else jnp.logical_not(lo), q_ref[:, sl].astype(F32), 0.0).astype(BF16)
        bias = jnp.concatenate([bias_ref[0, hd], bias_ref[1, hd]], axis=0)
        s_loc = lax.dot_general(qm, kband[:, sl], _NT, preferred_element_type=F32) + bias
        return [s_loc, _bdot(qm, kc[sl, :])]

    def attend(hd, es):
        sl = slice((hd // 2) * LANES, (hd // 2 + 1) * LANES)
        return (_bdot(es[0].astype(BF16), vband[:, sl])
                + lax.dot_general(es[1].astype(BF16), vc[sl, :], _NT, preferred_element_type=F32))

    outs = _pipelined_heads(NA_HEADS, scores, attend)
    for p in range(NA_HEADS // 2):
        o_ref[:, p * LANES:(p + 1) * LANES] = jnp.where(lo, outs[2 * p], outs[2 * p + 1]).astype(o_ref.dtype)


def _na_latent(q, k, v, kc_t, vc_t, layer, bias_blocks):
    b, n, w = q.shape
    lc = kc_t.shape[3]
    nq = n // (2 * Q_TILE)
    full = pl.BlockSpec((None, n, w), lambda j, bi: (bi, 0, 0))
    cspec = pl.BlockSpec((None, None, w, lc), lambda j, bi: (bi, layer, 0, 0))
    return pl.pallas_call(
        _na_kernel,
        grid=(nq, b),
        in_specs=[pl.BlockSpec((None, 2 * Q_TILE, w), lambda j, bi: (bi, j, 0)), full, full, cspec, cspec,
                  pl.BlockSpec((None, 2, NA_HEADS, Q_TILE, NA_BAND), lambda j, bi: (layer, j, 0, 0, 0))],
        out_specs=pl.BlockSpec((None, 2 * Q_TILE, w), lambda j, bi: (bi, j, 0)),
        out_shape=jax.ShapeDtypeStruct((b, n, w), BF16),
        compiler_params=_cparams(2),
        name="na_latent",
    )(q, k, v, kc_t, vc_t, bias_blocks)


NA_GRID_ROWS = 16
NA_BAND_ROW0 = (0, 0, 4, 4)


def _na_bias_kernel(b_ref, o_ref, tp_s):
    hd = pl.program_id(0)
    n_dr, n_dc = 2 * NA_KH - 1, 2 * NA_KW - 1
    shape = (GRID_W, LANES)
    c = lax.broadcasted_iota(jnp.int32, shape, 0)
    lane = _lane(shape)
    kc = lane & (GRID_W - 1)
    lo = lane < GRID_W
    diff = kc - c + (NA_KW - 1)
    c0 = jnp.clip(c - NA_KW // 2, 0, GRID_W - NA_KW)
    col_ok = (kc >= c0) & (kc < c0 + NA_KW)
    neg = jnp.full(shape, NEG_INF, F32)
    for dr0 in range(-1, n_dr):
        acc = neg
        for d in range(n_dc):
            v_lo = b_ref[hd * n_dr + dr0, d] if dr0 >= 0 else 0.0
            v_hi = b_ref[hd * n_dr + dr0 + 1, d] if dr0 + 1 < n_dr else 0.0
            acc = jnp.where(diff == d, jnp.where(lo, v_lo, v_hi), acc)
        tp_s[dr0 + 1] = jnp.where(col_ok, acc * LOG2E, NEG_INF)
    for j in range(NA_GRID_ROWS // 4):
        for ri in range(4):
            r = 4 * j + ri
            r0 = min(max(r - NA_KH // 2, 0), NA_GRID_ROWS - NA_KH)
            for kp in range(NA_BAND // LANES):
                kr = NA_BAND_ROW0[j] + 2 * kp
                ok_lo, ok_hi = r0 <= kr < r0 + NA_KH, r0 <= kr + 1 < r0 + NA_KH
                dr0 = kr - r + (NA_KH - 1)
                if ok_lo and ok_hi:
                    t = tp_s[dr0 + 1]
                elif ok_lo:
                    t = jnp.where(lo, tp_s[dr0 + 1], NEG_INF)
                elif ok_hi:
                    t = jnp.where(lo, NEG_INF, tp_s[dr0 + 1])
                else:
                    t = neg
                o_ref[j, ri * GRID_W:(ri + 1) * GRID_W, kp * LANES:(kp + 1) * LANES] = t


def _na_bias_blocks(bias):
    nq = NA_GRID_ROWS // 4
    return pl.pallas_call(
        _na_bias_kernel,
        grid=(DEPTH * NA_HEADS,),
        in_specs=[pl.BlockSpec(memory_space=pltpu.SMEM)],
        out_specs=pl.BlockSpec((None, nq, None, Q_TILE, NA_BAND),
                               lambda i: (i // NA_HEADS, 0, i % NA_HEADS, 0, 0)),
        out_shape=jax.ShapeDtypeStruct((DEPTH, nq, NA_HEADS, Q_TILE, NA_BAND), F32),
        scratch_shapes=[pltpu.VMEM((2 * NA_KH, GRID_W, LANES), F32)],
        compiler_params=_cparams(1),
        name="na_bias",
    )(bias.reshape(DEPTH * NA_HEADS * (2 * NA_KH - 1), 2 * NA_KW - 1))


def _widen(cols, n):
    blk = _lane((n, DN_HEADS * DN_DV)) >> 6
    return jnp.where(blk == 0, cols[0], jnp.where(blk == 1, cols[1], jnp.where(blk == 2, cols[2], cols[3])))


def _deltanet_kernel(seq, has_state, *refs):
    if has_state:
        zc_all, zm_all, s0_ref, cw_ref, alog_ref, dtb_ref, og_ref, o_ref = refs[:8]
    else:
        zc_all, zm_all, cw_ref, alog_ref, dtb_ref, og_ref, o_ref, sfin_ref = refs[:8]
    scratch = refs[8:]
    n_seqs = zc_all.shape[0]
    g_all, o_all, c_all, mp_all = scratch[4:8]
    n_chunks = seq // DN_CHUNK
    wide = DN_HEADS * DN_DV

    a_off, b_off = IN_TAIL_SKIP, IN_TAIL_SKIP + DN_GATES
    ri = lax.broadcasted_iota(jnp.int32, (DN_ROWS, DN_ROWS), 0)
    ci = lax.broadcasted_iota(jnp.int32, (DN_ROWS, DN_ROWS), 1)
    same = (ri >> 6) == (ci >> 6)
    tri = [jnp.tile(jnp.where(same & ((ci <= ri) if d == 0 else (ci >= ri)), 1.0, 0.0).astype(BF16), (1, 3))
           for d in range(2)]
    n_blocks = seq // DN_ROWS

    def terms3(x):
        hi = x.astype(BF16)
        rest = x - hi.astype(F32)
        mid = rest.astype(BF16)
        return hi, mid, (rest - mid.astype(F32)).astype(BF16)

    def split3(x):
        return jnp.concatenate(terms3(x), axis=0)

    def preprocess(views, rb, carry):
        zc_ref, zm_ref, q_s, k_s, v_s, b_s, g_s = views[:7]
        r0 = pl.multiple_of(rb * DN_ROWS, DN_ROWS)
        rows = pl.ds(r0, DN_ROWS)
        before = pl.ds(pl.multiple_of(jnp.maximum(r0 - 8, 0), 8), 8)
        after = pl.ds(pl.multiple_of(jnp.minimum(r0 + DN_ROWS, seq - 8), 8), 8)
        for part, dst in enumerate((q_s, k_s, v_s)):
            cs = slice(part * wide, (part + 1) * wide)
            head = jnp.where(rb > 0, zc_ref[before, cs], 0.0)
            tail = jnp.where(rb < n_blocks - 1, zc_ref[after, cs], 0.0)
            xe = jnp.concatenate([head, zc_ref[rows, cs], tail], axis=0)
            w = cw_ref[:, cs]
            y = (w[0:1] * xe[7:7 + DN_ROWS] + w[1:2] * xe[8:8 + DN_ROWS]
                 + w[2:3] * xe[9:9 + DN_ROWS] + w[3:4] * xe[10:10 + DN_ROWS])
            y = _silu(y)
            if part == 0:
                y = _head_l2(y) * (DN_DK ** -0.5)
            elif part == 1:
                y = _head_l2(y)
            dst[rows, :] = y
        zm = zm_ref[rows, :]
        xa = zm + dtb_ref[...]
        logd = -jnp.exp(alog_ref[...]) * (jnp.maximum(xa, 0.0) + jnp.log1p(jnp.exp(-jnp.abs(xa))))
        beta = 1.0 / (1.0 + jnp.exp(-zm))
        logd3 = split3(logd)
        for d in range(2):
            b_s[d, rows, :] = _widen([beta[:, b_off + 4 * d + hd:b_off + 4 * d + hd + 1]
                                      for hd in range(DN_HEADS)], DN_ROWS)
            g = _bdot(tri[d], logd3)
            g_s[d, rows, :] = _widen([g[:, a_off + 4 * d + hd:a_off + 4 * d + hd + 1]
                                      for hd in range(DN_HEADS)], DN_ROWS)
        return carry

    ii = lax.broadcasted_iota(jnp.int32, (DN_CHUNK, wide), 0)
    jj = _lane((DN_CHUNK, wide)) & (DN_CHUNK - 1)
    blk = _lane((DN_CHUNK, wide)) >> 6
    diag = ii == jj
    eye = jnp.where(diag, 1.0, 0.0)
    half_mask = [jnp.where((_lane((DN_CHUNK, LANES)) >> 6) == half, 1.0, 0.0).astype(BF16) for half in range(2)]
    zero_block = jnp.zeros((DN_CHUNK, LANES), BF16)

    def bd(z):
        zb = z.astype(BF16)
        rows = []
        for hd in range(DN_HEADS):
            col, half = divmod(hd, 2)
            kept = zb[:, col * LANES:(col + 1) * LANES] * half_mask[half]
            rows.append(jnp.concatenate([kept, zero_block] if col == 0 else [zero_block, kept], axis=1))
        return jnp.concatenate(rows, axis=0)

    def fold(gram):
        out = jnp.where(blk == 0, gram[0:DN_CHUNK], 0.0)
        for hd in range(1, DN_HEADS):
            out = out + jnp.where(blk == hd, gram[hd * DN_CHUNK:(hd + 1) * DN_CHUNK], 0.0)
        return out

    tri_masks = []
    for d in range(2):
        incl = (jj <= ii) if d == 0 else (jj >= ii)
        strict = (jj < ii) if d == 0 else (jj > ii)
        pair = [((ii >> (lvl + 1)) == (jj >> (lvl + 1)))
                & (((ii >> lvl) & 1) == (1 - d)) & (((jj >> lvl) & 1) == d) for lvl in range(6)]
        tri_masks.append((incl, strict, pair))

    def prepare(views, step, carry):
        q_s, k_s, v_s, b_s, g_s, o_s, c_s, mp_s = views[2:]
        chunks = [step * PREP_CHUNKS + i for i in range(PREP_CHUNKS)]
        rows = [pl.ds(pl.multiple_of(c * DN_CHUNK, DN_CHUNK), DN_CHUNK) for c in chunks]
        qkv = [(q_s[r, :], k_s[r, :], v_s[r, :]) for r in rows]
        inst = [(ci, d) for ci in range(PREP_CHUNKS) for d in range(2)]
        beta = {(ci, d): b_s[d, rows[ci], :] for ci, d in inst}
        kb = {(ci, d): qkv[ci][1] * beta[ci, d] for ci, d in inst}
        r = [lax.dot_general(jnp.concatenate([kb[ci, 0], kb[ci, 1], qkv[ci][0]], axis=0).astype(BF16),
                             bd(qkv[ci][1]), _NT, preferred_element_type=F32)
             for ci in range(PREP_CHUNKS)]
        g, a, qk, eg, t = {}, {}, {}, {}, {}
        for ci, d in inst:
            incl, strict, pair = tri_masks[d]
            g[ci, d] = g_s[d, rows[ci], :]
            g_row = jnp.sum(jnp.where(diag, g[ci, d], 0.0), axis=0, keepdims=True)
            dm = jnp.where(incl, jnp.exp(jnp.where(incl, g[ci, d] - g_row, 0.0)), 0.0)
            a[ci, d] = jnp.where(strict, r[ci][d * DN_CHUNK:(d + 1) * DN_CHUNK] * dm, 0.0)
            qk[ci, d] = r[ci][2 * DN_CHUNK:] * dm
            eg[ci, d] = jnp.exp(g[ci, d])
            t[ci, d] = eye - jnp.where(pair[0], a[ci, d], 0.0)
        for lvl in range(1, 6):
            te = {i: _bdot(t[i].astype(BF16), bd(jnp.where(tri_masks[i[1]][2][lvl], a[i], 0.0))) for i in inst}
            t = {i: t[i] - _bdot(te[i].astype(BF16), bd(t[i])) for i in inst}
        nb = {i: jnp.where(diag, 0.0, t[i]).astype(BF16) for i in inst}
        rhs_u = {(ci, d): qkv[ci][2] * beta[ci, d] for ci, d in inst}
        rhs_w = {i: kb[i] * eg[i] for i in inst}
        q2 = {i: qk[i] + _bdot(qk[i].astype(BF16), bd(nb[i])) for i in inst}
        both = {i: jnp.concatenate([nb[i], q2[i].astype(BF16)], axis=0) for i in inst}
        ru = {i: _bdot(both[i], bd(rhs_u[i])) for i in inst}
        rw = {i: _bdot(both[i], bd(rhs_w[i])) for i in inst}
        u = {i: rhs_u[i] + ru[i][0:DN_CHUNK] for i in inst}
        w = {i: rhs_w[i] + rw[i][0:DN_CHUNK] for i in inst}
        kd = {}
        for ci, d in inst:
            g_last = g[ci, d][DN_CHUNK - 1:DN_CHUNK] if d == 0 else g[ci, d][0:1]
            kd[ci, d] = (qkv[ci][1] * jnp.exp(g_last - g[ci, d])).astype(BF16)
        p = {(ci, d): qkv[ci][0] * eg[ci, d] - rw[ci, d][DN_CHUNK:] for ci, d in inst}
        o0 = {i: ru[i][DN_CHUNK:] for i in inst}
        m = {i: fold(lax.dot_general(kd[i], w[i].astype(BF16), _TN, preferred_element_type=F32)) for i in inst}
        cc = {i: fold(lax.dot_general(kd[i], u[i].astype(BF16), _TN, preferred_element_type=F32)) for i in inst}
        for ci, d in inst:
            mrow = pl.ds(pl.multiple_of(chunks[ci] * (2 * DN_CHUNK), 2 * DN_CHUNK), 2 * DN_CHUNK)
            c_s[d, rows[ci], :] = cc[ci, d]
            mp_s[d, mrow, :] = jnp.concatenate([m[ci, d], p[ci, d]], axis=0).astype(BF16)
        for ci in range(PREP_CHUNKS):
            o_s[rows[ci], :] = o0[ci, 0] + o0[ci, 1]
        return carry

    for sq in range(n_seqs):
        views = (zc_all.at[sq], zm_all.at[sq]) + tuple(ref.at[sq] for ref in scratch)
        lax.fori_loop(0, n_blocks, functools.partial(preprocess, views), 0)
        lax.fori_loop(0, n_chunks // PREP_CHUNKS, functools.partial(prepare, views), 0)

    chains = [(sq, d) for sq in range(n_seqs) for d in range(2)]

    def scan(i, states):
        new = []
        for (sq, d), state in zip(chains, states):
            c = i if d == 0 else n_chunks - 1 - i
            r0 = pl.multiple_of(c * DN_CHUNK, DN_CHUNK)
            rows = pl.ds(r0, DN_CHUNK)
            mrow = pl.ds(pl.multiple_of(c * (2 * DN_CHUNK), 2 * DN_CHUNK), 2 * DN_CHUNK)
            edge = pl.ds(pl.multiple_of(r0 + (DN_CHUNK - 8 if d == 0 else 0), 8), 8)
            g_edge = g_all[sq, d, edge, :]
            g_last = g_edge[7:8] if d == 0 else g_edge[0:1]
            res = _bdot(mp_all[sq, d, mrow, :], bd(state))
            o_all[sq, rows, :] = o_all[sq, rows, :] + res[DN_CHUNK:]
            new.append(state * jnp.exp(g_last) - res[0:DN_CHUNK] + c_all[sq, d, rows, :])
        return tuple(new)

    place = [jnp.where(diag & (blk == hd), 1.0, 0.0).astype(BF16) for hd in range(DN_HEADS)]

    def to_wide(heads):
        parts = [_bdot(term, place[hd]) for hd, x in enumerate(heads) for term in terms3(x)]
        return functools.reduce(lambda a, b: a + b, parts)

    def head_of(s, hd):
        parts = [lax.dot_general(term, place[hd], _NT, preferred_element_type=F32) for term in terms3(s)]
        return functools.reduce(lambda a, b: a + b, parts)

    if has_state:
        init = tuple(to_wide([s0_ref[sq, d, hd] for hd in range(DN_HEADS)]) for sq, d in chains)
    else:
        init = tuple(jnp.zeros((DN_DK, wide), F32) for _ in chains)
    fin = lax.fori_loop(0, n_chunks, scan, init)
    if not has_state:
        for (sq, d), state in zip(chains, fin):
            for hd in range(DN_HEADS):
                sfin_ref[sq, d, hd] = head_of(state, hd)

    for sq in range(n_seqs):
        gate = _silu(zc_all[sq, :, 3 * wide:4 * wide])
        o_ref[sq] = (_head_rms(o_all[sq], og_ref[...]) * gate).astype(o_ref.dtype)


def _deltanet(zc, zm, lw, layer, state=None):
    b, seq, _ = zc.shape
    wide = DN_HEADS * DN_DV
    has_state = state is not None
    ns = max(n for n in (1, 2, 4) if b % n == 0 and n * seq * DN_VMEM_BYTES_PER_ROW <= DN_VMEM_BUDGET)
    per_b = lambda w: pl.BlockSpec((ns, seq, w), lambda bi: (bi, 0, 0))
    const = lambda shape: pl.BlockSpec((None,) + shape, lambda bi: (layer,) + (0,) * len(shape))
    st_spec = pl.BlockSpec((ns, 2, DN_HEADS, DN_DK, DN_DV), lambda bi: (bi, 0, 0, 0, 0))
    in_specs = [per_b(4 * wide), per_b(LANES)]
    args = [zc, zm]
    if has_state:
        in_specs.append(pl.BlockSpec((ns, None, 2, DN_HEADS, DN_DK, DN_DV),
                                     lambda bi: (bi, layer, 0, 0, 0, 0)))
        args.append(state)
    in_specs += [const((DN_CONV, DN_QKV)), const((1, LANES)), const((1, LANES)), const((1, wide))]
    args += [lw["dn_conv_w"], lw["dn_alog_row"], lw["dn_dtb_row"], lw["dn_out_g"]]
    out_specs = [per_b(wide)]
    out_shape = [jax.ShapeDtypeStruct((b, seq, wide), BF16)]
    if not has_state:
        out_specs.append(st_spec)
        out_shape.append(jax.ShapeDtypeStruct((b, 2, DN_HEADS, DN_DK, DN_DV), F32))
    res = pl.pallas_call(
        functools.partial(_deltanet_kernel, seq, has_state),
        grid=(b // ns,),
        in_specs=in_specs,
        out_specs=out_specs,
        out_shape=out_shape,
        scratch_shapes=[pltpu.VMEM((ns, seq, wide), F32), pltpu.VMEM((ns, seq, wide), F32),
                        pltpu.VMEM((ns, seq, wide), F32), pltpu.VMEM((ns, 2, seq, wide), F32),
                        pltpu.VMEM((ns, 2, seq, wide), F32), pltpu.VMEM((ns, seq, wide), F32),
                        pltpu.VMEM((ns, 2, seq, wide), F32), pltpu.VMEM((ns, 2, 2 * seq, wide), BF16)],
        compiler_params=_cparams(1),
        name="deltanet_lat" if has_state else "deltanet_ctx",
    )(*args)
    return (res[0], None) if has_state else (res[0], res[1])


def _outffn_kernel(final, oa_ref, ob_ref, oc_ref, od_ref, x_ref, mod_ref, g2_ref, wo_ref, wg_ref,
                   wu_ref, wd_ref, fg_ref, y_ref):
    tiles = range(TILES_PER_STEP)
    m = mod_ref[...]
    o = [jnp.concatenate([oa_ref[t], ob_ref[t], oc_ref[t], od_ref[t]], axis=-1).astype(BF16) for t in tiles]
    x1 = [x_ref[t] + m[2:3] * _bdot(o[t], wo_ref[...]) for t in tiles]
    h = [(_rms_full(x1[t], g2_ref[...]) * (1.0 + m[4:5]) + m[3:4]).astype(BF16) for t in tiles]
    gate = [_bdot(h[t], wg_ref[...]) for t in tiles]
    up = [_bdot(h[t], wu_ref[...]) for t in tiles]
    act = [(_silu(gate[t]) * up[t]).astype(BF16) for t in tiles]
    x2 = [x1[t] + m[5:6] * _bdot(act[t], wd_ref[...]) for t in tiles]
    for t in tiles:
        y_ref[t] = _rms_full(x2[t], fg_ref[...]) if final else x2[t]


def _outffn(outs, x, mods, per_batch_mods, lw, layer, final_g, final):
    b, l, _ = x.shape
    tm, ts = ROW_TILE, TILES_PER_STEP
    n_tiles = b * l // tm
    tiles_per_seq = l // tm
    assert not per_batch_mods or tiles_per_seq % ts == 0
    row = lambda w: pl.BlockSpec((ts, tm, w), lambda i: (i, 0, 0))
    const = lambda shape: pl.BlockSpec((None,) + shape, lambda i: (layer,) + (0,) * len(shape),
                                       pipeline_mode=pl.Buffered(1))
    mod_spec = pl.BlockSpec((None, None, MOD_CHUNKS, D_MODEL),
                            (lambda i: (layer, 1 + i * ts // tiles_per_seq, 0, 0)) if per_batch_mods
                            else (lambda i: (layer, 0, 0, 0)))
    tiled = lambda a: a.reshape(n_tiles, tm, a.shape[-1])
    y = pl.pallas_call(
        functools.partial(_outffn_kernel, final),
        grid=(n_tiles // ts,),
        in_specs=[row(256), row(256), row(256), row(256), row(D_MODEL), mod_spec, const((1, D_MODEL)),
                  const((D_MODEL, D_MODEL)), const((D_MODEL, D_FF)), const((D_MODEL, D_FF)),
                  const((D_FF, D_MODEL)), pl.BlockSpec((1, D_MODEL), lambda i: (0, 0))],
        out_specs=row(D_MODEL),
        out_shape=jax.ShapeDtypeStruct((n_tiles, tm, D_MODEL), F32),
        compiler_params=_cparams(1),
        name="outffn",
    )(*[tiled(a) for a in outs], tiled(x), mods, lw["norm2_g"], lw["w_out"], lw["w_gate"], lw["w_up"],
      lw["w_down"], final_g)
    return y.reshape(b, l, D_MODEL)


def _rope_tables(n):
    t = np.arange(n)

    def axis(pos, half):
        inv = np.float32(ROPE_BASE) ** (-np.arange(half, dtype=np.float32) / np.float32(half))
        ang = pos.astype(np.float32)[:, None] * inv[None, :]
        c, s = np.cos(ang), np.sin(ang)
        return np.concatenate([c, c], -1), np.concatenate([-s, s], -1)

    cr, sr = axis(t // GRID_W, 16)
    cc, sc = axis(t % GRID_W, 16)
    cos64 = np.tile(np.concatenate([cr, cc], -1), (1, 2))
    sin64 = np.tile(np.concatenate([sr, sc], -1), (1, 2))
    cr, sr = axis(t // GRID_W, 8)
    cc, sc = axis(t % GRID_W, 8)
    cos32, sin32 = np.concatenate([cr, cc], -1), np.concatenate([sr, sc], -1)
    one, zero = np.ones((n, 1), np.float32), np.zeros((n, 1), np.float32)
    cosm = np.concatenate([np.tile(one, (1, 64)), cos32, np.tile(one, (1, 32))], -1)
    sinm = np.concatenate([np.tile(zero, (1, 64)), sin32, np.tile(zero, (1, 32))], -1)
    coskr = np.concatenate([cos32, np.tile(one, (1, 96))], -1)
    sinkr = np.concatenate([sin32, np.tile(zero, (1, 96))], -1)
    return tuple(jnp.asarray(a, F32) for a in (cos64, sin64, cosm, sinm, coskr, sinkr))


_QA_ORDER = ((0, 64), (128, 192), (64, 128), (192, 256))


def _stacked_weights(p):
    w_in = jnp.swapaxes(p["w_in"], 1, 2).astype(BF16)
    w_out = p["w_out"]
    w_out = jnp.concatenate([w_out[:, a:b] for a, b in _QA_ORDER + ((256, w_out.shape[1]),)], axis=1).astype(BF16)
    wq = p["mla_wq_up"].reshape(DEPTH, MLA_Q_LORA, MLA_HEADS, MLA_NOPE + MLA_ROPE)
    wq = jnp.pad(wq, ((0, 0), (0, 0), (0, 0), (0, LANES - MLA_NOPE - MLA_ROPE))).reshape(DEPTH, MLA_Q_LORA, 4 * LANES)
    wkv = p["mla_wkv_up"].reshape(DEPTH, MLA_KV_LORA, MLA_HEADS, MLA_NOPE + MLA_V)
    wk_top = jnp.pad(wkv[..., :MLA_NOPE], ((0, 0), (0, 0), (0, 0), (0, LANES - MLA_NOPE)))
    place = jnp.pad(jnp.eye(MLA_ROPE, dtype=F32), ((0, LANES - MLA_ROPE), (MLA_NOPE, LANES - MLA_NOPE - MLA_ROPE)))
    wk_bot = jnp.broadcast_to(place[None, :, None, :], (DEPTH, LANES, MLA_HEADS, LANES))
    wk = jnp.concatenate([wk_top, wk_bot], axis=1).reshape(DEPTH, 2 * LANES, 4 * LANES)
    wv = wkv[..., MLA_NOPE:].reshape(DEPTH, MLA_KV_LORA, MLA_HEADS * MLA_V)
    gate_row = lambda v: jnp.pad(v.reshape(DEPTH, 1, DN_GATES),
                                 ((0, 0), (0, 0), (IN_TAIL_SKIP, LANES - IN_TAIL_SKIP - DN_GATES)))
    row = lambda v: v[:, None, :]
    return {
        "norm1_g": row(p["norm1_g"]), "norm2_g": row(p["norm2_g"]),
        "w_in": w_in, "w_out": w_out,
        "qn_g": row(jnp.tile(p["gqa_qn_g"], (1, 4))), "kn_g": row(jnp.tile(p["gqa_kn_g"], (1, 2))),
        "mla_qn_g": row(p["mla_qn_g"]), "mla_kvn_g": row(p["mla_kvn_g"]),
        "wq": wq.astype(BF16), "wk": wk.astype(BF16), "wv": wv.astype(BF16),
        "dn_conv_w": p["dn_conv_w"], "dn_alog_row": gate_row(p["dn_a_log"]),
        "dn_dtb_row": gate_row(p["dn_dt_bias"]), "dn_out_g": row(jnp.tile(p["dn_out_g"], (1, DN_HEADS))),
        "w_gate": p["ffn_w_gate"].astype(BF16), "w_up": p["ffn_w_up"].astype(BF16),
        "w_down": p["ffn_w_down"].astype(BF16),
    }


def kernel(x_prompt, x_sample, cache_gqa_k, cache_gqa_v, cache_na_k, cache_na_v, state_dn,
           cache_mla_ckv, cache_mla_krope, c, c_ctx, norm1_g, norm2_g, w_mod, b_mod, w_in, w_out,
           gqa_qn_g, gqa_kn_g, na_bias, dn_conv_w, dn_a_log, dn_dt_bias, dn_out_g, mla_qn_g,
           mla_wq_up, mla_kvn_g, mla_wkv_up, ffn_w_gate, ffn_w_up, ffn_w_down, final_g):
    p = {"norm1_g": norm1_g, "norm2_g": norm2_g, "w_in": w_in, "w_out": w_out, "gqa_qn_g": gqa_qn_g,
         "gqa_kn_g": gqa_kn_g, "dn_conv_w": dn_conv_w, "dn_a_log": dn_a_log, "dn_dt_bias": dn_dt_bias,
         "dn_out_g": dn_out_g, "mla_qn_g": mla_qn_g, "mla_wq_up": mla_wq_up, "mla_kvn_g": mla_kvn_g,
         "mla_wkv_up": mla_wkv_up, "ffn_w_gate": ffn_w_gate, "ffn_w_up": ffn_w_up, "ffn_w_down": ffn_w_down}
    nb_ctx, seq_ctx, _ = x_prompt.shape
    nb_lat, seq_lat, _ = x_sample.shape
    past = cache_gqa_k.shape[2]
    fg = final_g[None]

    cond = jnp.concatenate([c_ctx[None], c, jnp.zeros((16 - 1 - nb_lat, D_MODEL), F32)], axis=0)
    mods = _modulation(cond, w_mod, b_mod).reshape(DEPTH, 16, MOD_CHUNKS, D_MODEL)
    lw = _stacked_weights(p)

    x = x_prompt
    ctx_out = []
    for l in range(DEPTH):
        o_a, ka, va, o_b, kb, vb, zc, o_d, ckv, zm, krr = _inproj(x, mods, False, lw, l, None)
        o_c, s_dn = _deltanet(zc, zm, lw, l)
        x = _outffn((o_a, o_b, o_c, o_d), x, mods, False, lw, l, fg, l == DEPTH - 1)
        ctx_out.append((ka, va, kb, vb, s_dn, ckv, krr[:, :, :MLA_ROPE]))
    y_prompt = x
    new = [jnp.stack([s[i] for s in ctx_out], axis=1) for i in range(7)]
    for i in range(4):
        t = new[i].reshape(nb_ctx, DEPTH, -1, HEAD_DIM, seq_ctx)
        new[i] = jnp.transpose(t, (0, 1, 4, 2, 3))

    ropes = _rope_tables(seq_lat)
    keys_t = lambda c: jnp.transpose(c, (0, 1, 3, 4, 2)).reshape(nb_lat, DEPTH, -1, past)
    ck_a, cv_a, ck_b, cv_b = keys_t(cache_gqa_k), keys_t(cache_gqa_v), keys_t(cache_na_k), keys_t(cache_na_v)
    c_kr = jnp.pad(cache_mla_krope, ((0, 0), (0, 0), (0, 0), (0, LANES - MLA_ROPE)))
    bias_blocks = _na_bias_blocks(na_bias)
    x = x_sample
    for l in range(DEPTH):
        qa, ka, va, qb, kb, vb, zc, qd, ckv, zm, krr = _inproj(x, mods, True, lw, l, ropes)
        o_a = _attn_pair(qa, ka, va, (0, 0), ck_a, cv_a, l, name="gqa_lat")
        o_b = _na_latent(qb, kb, vb, ck_b, cv_b, l, bias_blocks)
        o_c, _ = _deltanet(zc, zm, lw, l, state=state_dn)
        o_d = _mla(qd, ckv, krr, lw, l, cache_mla_ckv, c_kr)
        x = _outffn((o_a, o_b, o_c, o_d), x, mods, True, lw, l, fg, l == DEPTH - 1)
    y_sample = x

    return (y_prompt, y_sample, *new)
```

```python
import functools

import numpy as np
import jax
import jax.numpy as jnp
from jax import lax
from jax.experimental import pallas as pl
from jax.experimental.pallas import tpu as pltpu

F32 = jnp.float32
BF16 = jnp.bfloat16

D_MODEL = 1024
DEPTH = 2
GRID_W = 64
HEAD_DIM = 64
ROPE_BASE = 10000.0
NEG_INF = -1e30
MOD_CHUNKS = 6
GQA_HEADS, GQA_KV_HEADS = 4, 2
NA_HEADS, NA_KH, NA_KW = 4, 8, 16
DN_HEADS, DN_DK, DN_DV, DN_CONV, DN_CHUNK = 4, 64, 64, 4, 64
DN_QKV = DN_HEADS * (2 * DN_DK + DN_DV)
DN_GATES = 2 * DN_HEADS
MLA_HEADS, MLA_Q_LORA, MLA_KV_LORA, MLA_NOPE, MLA_ROPE, MLA_V = 4, 256, 128, 64, 32, 64
MLA_SCALE = (MLA_NOPE + MLA_ROPE) ** -0.5
D_FF = -(-8 * D_MODEL // (3 * 256)) * 256
EPS = 1e-6
LOG2E = 1.4426950408889634

LANES = 128
ROW_TILE = 256
Q_TILE = 256
NA_BAND = 768
TILES_PER_STEP = 2
LAT_TILES_PER_STEP = 4
ATTN_Q_TILE = 1024
HEAD_LOOKAHEAD = 1
DN_VMEM_BYTES_PER_ROW = 12288 + 9216
DN_VMEM_BUDGET = 46 * 1024 * 1024
DN_ROWS = 256
PREP_CHUNKS = 4
IN_COLS = 2736
IN_TAIL_SKIP = 512 - (2 * 8 + 256 + 128 + 32)
VMEM_LIMIT = 56 * 1024 * 1024

_NT = (((1,), (1,)), ((), ()))
_TN = (((0,), (0,)), ((), ()))


def _cparams(n_axes):
    return pltpu.CompilerParams(dimension_semantics=("arbitrary",) * n_axes,
                                vmem_limit_bytes=VMEM_LIMIT)


def _lane(shape):
    return lax.broadcasted_iota(jnp.int32, shape, len(shape) - 1)


def _silu(x):
    return x / (1.0 + jnp.exp(-x))


def _rms_full(x, g):
    return x * lax.rsqrt(jnp.mean(x * x, axis=-1, keepdims=True) + EPS) * g


def _seg64_sum(x):
    lo = _lane(x.shape) < HEAD_DIM
    s_lo = jnp.sum(jnp.where(lo, x, 0.0), axis=-1, keepdims=True)
    s_hi = jnp.sum(jnp.where(lo, 0.0, x), axis=-1, keepdims=True)
    return jnp.where(lo, s_lo, s_hi)


def _head_rms(x, g):
    parts = []
    for p in range(x.shape[-1] // LANES):
        xp = x[:, p * LANES:(p + 1) * LANES]
        ms = _seg64_sum(xp * xp) * (1.0 / HEAD_DIM)
        parts.append(xp * lax.rsqrt(ms + EPS))
    y = parts[0] if len(parts) == 1 else jnp.concatenate(parts, axis=-1)
    return y * g


def _head_l2(x):
    parts = []
    for p in range(x.shape[-1] // LANES):
        xp = x[:, p * LANES:(p + 1) * LANES]
        parts.append(xp * lax.rsqrt(_seg64_sum(xp * xp) + EPS))
    return parts[0] if len(parts) == 1 else jnp.concatenate(parts, axis=-1)


def _rope(x, cos, sin, half):
    first = (_lane(x.shape) & (2 * half - 1)) < half
    rot = jnp.where(first, pltpu.roll(x, LANES - half, 1), pltpu.roll(x, half, 1))
    return x * cos + rot * sin


def _softmax_parts(scores):
    m = jnp.max(scores[0], axis=-1, keepdims=True)
    for s in scores[1:]:
        m = jnp.maximum(m, jnp.max(s, axis=-1, keepdims=True))
    es = [jnp.exp2(s - m) for s in scores]
    l = jnp.sum(es[0], axis=-1, keepdims=True)
    for e in es[1:]:
        l = l + jnp.sum(e, axis=-1, keepdims=True)
    return es, 1.0 / l


def _bdot(a, b):
    return jnp.dot(a, b, preferred_element_type=F32)


def _pipelined_heads(n_heads, scores, attend):
    outs = []
    queue = [scores(hd) for hd in range(min(HEAD_LOOKAHEAD, n_heads))]
    for hd in range(n_heads):
        if hd + HEAD_LOOKAHEAD < n_heads:
            queue.append(scores(hd + HEAD_LOOKAHEAD))
        es, rl = _softmax_parts(queue.pop(0))
        outs.append(attend(hd, es) * rl)
    return outs


def _pair_attention(q_blocks, sources, qmap):
    lo = _lane(q_blocks[0].shape) < HEAD_DIM

    def scores(hd):
        p, half = divmod(hd, 2)
        qm = jnp.where(lo if half == 0 else jnp.logical_not(lo), q_blocks[p], 0.0).astype(BF16)
        return [_bdot(qm, k_block(qmap[p])) if transposed
                else lax.dot_general(qm, k_block(qmap[p]), _NT, preferred_element_type=F32)
                for k_block, _, transposed in sources]

    def attend(hd, es):
        kv = qmap[hd // 2]
        parts = [lax.dot_general(e.astype(BF16), v_block(kv), _NT, preferred_element_type=F32) if transposed
                 else _bdot(e.astype(BF16), v_block(kv)) for e, (_, v_block, transposed) in zip(es, sources)]
        return functools.reduce(lambda a, b: a + b, parts)

    outs = _pipelined_heads(2 * len(qmap), scores, attend)
    return [jnp.where(lo, outs[2 * p], outs[2 * p + 1]) for p in range(len(qmap))]


def _mla_attention(q_heads, k_head, v_block):
    lo = _lane(q_heads[0].shape) < MLA_V

    def scores(hd):
        return [lax.dot_general(q_heads[hd], k_head(hd), _NT, preferred_element_type=F32)]

    def attend(hd, es):
        return _bdot(es[0].astype(BF16), v_block(hd // 2))

    outs = _pipelined_heads(MLA_HEADS, scores, attend)
    return [jnp.where(lo, outs[2 * p], outs[2 * p + 1]) for p in range(MLA_HEADS // 2)]


def _mod_kernel(c_ref, w_ref, b_ref, o_ref):
    s = _silu(c_ref[...]).astype(BF16)
    o_ref[...] = _bdot(s, w_ref[...].astype(BF16)) + b_ref[...]


def _modulation(cond, w_mod, b_mod):
    n = MOD_CHUNKS * D_MODEL
    tn = 1536
    return pl.pallas_call(
        _mod_kernel,
        grid=(DEPTH, n // tn),
        in_specs=[pl.BlockSpec((16, D_MODEL), lambda l, j: (0, 0)),
                  pl.BlockSpec((None, D_MODEL, tn), lambda l, j: (l, 0, j)),
                  pl.BlockSpec((None, 1, tn), lambda l, j: (l, 0, j))],
        out_specs=pl.BlockSpec((None, 16, tn), lambda l, j: (l, 0, j)),
        out_shape=jax.ShapeDtypeStruct((DEPTH, 16, n), F32),
        compiler_params=_cparams(2),
        name="modulation",
    )(cond, w_mod, b_mod.reshape(DEPTH, 1, n))


_IN_OUT_WIDTHS = (256, 128, 128, 256, 256, 256, 1024, 512, 128, 128, 128)
_IN_OUT_WIDTHS_CTX = (256, 128, 128, 256, 256, 256, 1024, 256, 128, 128, 128)
_IN_OUT_DTYPES_LAT = (BF16, BF16, BF16, BF16, BF16, BF16, F32, BF16, BF16, F32, BF16)
_IN_OUT_DTYPES_CTX = (BF16, F32, F32, BF16, F32, F32, F32, BF16, F32, F32, F32)


def _inproj_kernel(positioned, *refs):
    (x_ref, mod_ref, g1_ref, w_ref, qng_ref, kng_ref, mqg_ref, wq_ref, mkg_ref) = refs[:9]
    if positioned:
        cos64_ref, sin64_ref, cosm_ref, sinm_ref, coskr_ref, sinkr_ref = refs[9:15]
        n_in = 15
    else:
        wk_ref, wv_ref = refs[9:11]
        n_in = 11
    (qa_ref, ka_ref, va_ref, qb_ref, kb_ref, vb_ref, zc_ref, qd_ref, ckv_ref, zm_ref,
     krr_ref) = refs[n_in:]
    tiles = range(x_ref.shape[0])
    tm = x_ref.shape[1]
    lane = _lane((tm, LANES))
    lo = lane < HEAD_DIM

    m = mod_ref[...]
    hb = jnp.concatenate([(_rms_full(x_ref[t], g1_ref[...]) * (1.0 + m[1:2]) + m[0:1]).astype(BF16)
                          for t in tiles], axis=0)

    def project(c0, c1):
        z = lax.dot_general(hb, w_ref[c0:c1, :], _NT, preferred_element_type=F32)
        return [z[t * tm:(t + 1) * tm] for t in tiles]

    za = project(0, 512)
    zb = project(512, 1280)

    for t in tiles:
        q = _head_rms(za[t][:, 0:256], qng_ref[...])
        k = _head_rms(za[t][:, 256:384], kng_ref[...])
        q0, q1 = q[:, 0:128], q[:, 128:256]
        q0, q1 = jnp.where(lo, q0, pltpu.roll(q1, HEAD_DIM, 1)), jnp.where(lo, pltpu.roll(q0, HEAD_DIM, 1), q1)
        if positioned:
            cos, sin = cos64_ref[t], sin64_ref[t]
            q0, q1 = _rope(q0, cos, sin, 16), _rope(q1, cos, sin, 16)
            k = _rope(k, cos, sin, 16)
        q0, q1 = q0 * (HEAD_DIM ** -0.5 * LOG2E), q1 * (HEAD_DIM ** -0.5 * LOG2E)
        v = za[t][:, 384:512]
        if not positioned:
            kb16, vb16 = k.astype(BF16), v.astype(BF16)
            q0, q1 = _pair_attention([q0, q1], [(lambda i: kb16, lambda i: vb16, False)], (0, 0))
        qa_ref[t] = jnp.concatenate([q0, q1], axis=-1).astype(qa_ref.dtype)
        ka_ref[t] = k.astype(ka_ref.dtype) if positioned else k.T
        va_ref[t] = v.astype(va_ref.dtype) if positioned else v.T

    zd = project(IN_COLS - 4 * LANES, IN_COLS)
    zc = project(1280, 2304)

    for t in tiles:
        q = zb[t][:, 0:256] * (HEAD_DIM ** -0.5 * LOG2E)
        k, v = zb[t][:, 256:512], zb[t][:, 512:768]
        if not positioned:
            kb16, vb16 = k.astype(BF16), v.astype(BF16)
            q = jnp.concatenate(_pair_attention(
                [q[:, 0:LANES], q[:, LANES:]],
                [(lambda i, kb16=kb16: kb16[:, i * LANES:(i + 1) * LANES],
                  lambda i, vb16=vb16: vb16[:, i * LANES:(i + 1) * LANES], False)], (0, 1)), axis=-1)
        qb_ref[t] = q.astype(qb_ref.dtype)
        kb_ref[t] = k.astype(kb_ref.dtype) if positioned else k.T
        vb_ref[t] = v.astype(vb_ref.dtype) if positioned else v.T

    lead = IN_TAIL_SKIP + 2 * DN_GATES
    shifted = []
    for t in tiles:
        rolled = [pltpu.roll(zd[t][:, j * LANES:(j + 1) * LANES], LANES - lead, 1) for j in range(4)]
        keep = lane < LANES - lead
        shifted.append([jnp.where(keep, rolled[j], rolled[(j + 1) % 4]) for j in range(4)])

    cq = jnp.concatenate([_rms_full(jnp.concatenate(shifted[t][0:2], axis=-1), mqg_ref[...]).astype(BF16)
                          for t in tiles], axis=0)
    qm = _bdot(cq, wq_ref[...])
    for t in tiles:
        zc_ref[t] = zc[t]
        zm_ref[t] = zd[t][:, 0:LANES]
        ckv = _rms_full(shifted[t][2], mkg_ref[...])
        ckv_ref[t] = ckv.astype(ckv_ref.dtype)
        kr = jnp.where(lane < MLA_ROPE, shifted[t][3], 0.0)
        q = qm[t * tm:(t + 1) * tm]
        if positioned:
            kr = _rope(kr, coskr_ref[t], sinkr_ref[t], 8)
            cm, sm = cosm_ref[t], sinm_ref[t]
            q = jnp.concatenate([_rope(q[:, i * LANES:(i + 1) * LANES], cm, sm, 8)
                                 for i in range(MLA_HEADS)], axis=-1)
        krr_ref[t] = kr.astype(krr_ref.dtype)
        q = q * (MLA_SCALE * LOG2E)
        if not positioned:
            c16 = ckv.astype(BF16)
            k16 = _bdot(jnp.concatenate([c16, kr.astype(BF16)], axis=-1), wk_ref[...]).astype(BF16)
            v16 = _bdot(c16, wv_ref[...]).astype(BF16)
            q = jnp.concatenate(_mla_attention(
                [q[:, i * LANES:(i + 1) * LANES].astype(BF16) for i in range(MLA_HEADS)],
                lambda i, k16=k16: k16[:, i * LANES:(i + 1) * LANES],
                lambda i, v16=v16: v16[:, i * LANES:(i + 1) * LANES]), axis=-1)
        qd_ref[t] = q.astype(qd_ref.dtype)


def _inproj(x, mods, per_batch_mods, lw, layer, ropes):
    b, l, _ = x.shape
    positioned = ropes is not None
    tm, ts = ROW_TILE, (LAT_TILES_PER_STEP if positioned else TILES_PER_STEP)
    n_tiles = b * l // tm
    tiles_per_seq = l // tm
    row = lambda w: pl.BlockSpec((ts, tm, w), lambda i: (i, 0, 0))
    const = lambda shape: pl.BlockSpec((None,) + shape, lambda i: (layer,) + (0,) * len(shape))
    assert not per_batch_mods or tiles_per_seq % ts == 0
    mod_spec = pl.BlockSpec((None, None, MOD_CHUNKS, D_MODEL),
                            (lambda i: (layer, 1 + i * ts // tiles_per_seq, 0, 0)) if per_batch_mods
                            else (lambda i: (layer, 0, 0, 0)))
    in_specs = [row(D_MODEL), mod_spec, const((1, D_MODEL)), const((IN_COLS, D_MODEL)),
                const((1, 256)), const((1, 128)), const((1, MLA_Q_LORA)),
                const((MLA_Q_LORA, 4 * LANES)), const((1, MLA_KV_LORA))]
    args = [x.reshape(n_tiles, tm, D_MODEL), mods, lw["norm1_g"], lw["w_in"], lw["qn_g"], lw["kn_g"],
            lw["mla_qn_g"], lw["wq"], lw["mla_kvn_g"]]
    if positioned:
        steps_per_seq = tiles_per_seq // ts
        in_specs += [pl.BlockSpec((ts, tm, LANES), lambda i: (i % steps_per_seq, 0, 0))] * 6
        args += [r.reshape(tiles_per_seq, tm, LANES) for r in ropes]
        widths, dtypes = _IN_OUT_WIDTHS, _IN_OUT_DTYPES_LAT
    else:
        assert tiles_per_seq == 1
        in_specs += [const((2 * LANES, 4 * LANES)), const((LANES, 2 * LANES))]
        args += [lw["wk"], lw["wv"]]
        widths, dtypes = _IN_OUT_WIDTHS_CTX, _IN_OUT_DTYPES_CTX
    flipped = () if positioned else (1, 2, 4, 5)
    shapes = [(n_tiles, w, tm) if i in flipped else (n_tiles, tm, w) for i, w in enumerate(widths)]
    outs = pl.pallas_call(
        functools.partial(_inproj_kernel, positioned),
        grid=(n_tiles // ts,),
        in_specs=in_specs,
        out_specs=[pl.BlockSpec((ts,) + shp[1:], lambda i: (i, 0, 0)) for shp in shapes],
        out_shape=[jax.ShapeDtypeStruct(shp, dt) for shp, dt in zip(shapes, dtypes)],
        compiler_params=_cparams(1),
        name="inproj_lat" if positioned else "inproj_attn_ctx",
    )(*args)
    return [o if i in flipped else o.reshape(b, l, o.shape[-1]) for i, o in enumerate(outs)]


def _attn_pair_kernel(qmap, q_ref, kc_ref, vc_ref, k_ref, v_ref, o_ref, kcbuf, vcbuf, kbuf, vbuf):
    @pl.when(pl.program_id(1) == 0)
    def _():
        kcbuf[...] = kc_ref[...].astype(BF16)
        vcbuf[...] = vc_ref[...].astype(BF16)
        kbuf[...] = k_ref[...].astype(BF16)
        vbuf[...] = v_ref[...].astype(BF16)

    rows = lambda buf: (lambda i: buf[i * LANES:(i + 1) * LANES, :])
    cols = lambda buf: (lambda i: buf[:, i * LANES:(i + 1) * LANES])
    tq = q_ref.shape[0]
    q_blocks = [q_ref[:, p * LANES:(p + 1) * LANES].astype(F32) for p in range(len(qmap))]
    assert len(set(qmap)) == 1
    stacked = _pair_attention([jnp.concatenate(q_blocks, axis=0)],
                              [(rows(kcbuf), rows(vcbuf), True), (cols(kbuf), cols(vbuf), False)], qmap[:1])[0]
    for p in range(len(qmap)):
        o_ref[:, p * LANES:(p + 1) * LANES] = stacked[p * tq:(p + 1) * tq].astype(o_ref.dtype)


def _attn_pair(q, k, v, qmap, kc_t, vc_t, layer, name):
    b, lq, wq = q.shape
    ls, wk = k.shape[1], k.shape[2]
    lc = kc_t.shape[3]
    tq = min(ATTN_Q_TILE // len(qmap), lq)
    cspec = pl.BlockSpec((None, None, wk, lc), lambda bi, i: (bi, layer, 0, 0))
    sspec = pl.BlockSpec((None, ls, wk), lambda bi, i: (bi, 0, 0))
    return pl.pallas_call(
        functools.partial(_attn_pair_kernel, qmap),
        grid=(b, lq // tq),
        in_specs=[pl.BlockSpec((None, tq, wq), lambda bi, i: (bi, i, 0)), cspec, cspec, sspec, sspec],
        out_specs=pl.BlockSpec((None, tq, wq), lambda bi, i: (bi, i, 0)),
        out_shape=jax.ShapeDtypeStruct((b, lq, wq), BF16),
        scratch_shapes=[pltpu.VMEM((wk, lc), BF16), pltpu.VMEM((wk, lc), BF16),
                        pltpu.VMEM((ls, wk), BF16), pltpu.VMEM((ls, wk), BF16)],
        compiler_params=_cparams(2),
        name=name,
    )(q, kc_t, vc_t, k, v)


def _mla_kernel(q_ref, ckvc_ref, krc_ref, ckv_ref, kr_ref, wk_ref, wv_ref, o_ref, kbuf, vbuf):
    @pl.when(pl.program_id(1) == 0)
    def _():
        def expand(c_ref, r_ref, r0, r1):
            c = c_ref[...].astype(BF16)
            ckr = jnp.concatenate([c, r_ref[...].astype(BF16)], axis=-1)
            kbuf[r0:r1, :] = _bdot(ckr, wk_ref[...]).astype(BF16)
            vbuf[r0:r1, :] = _bdot(c, wv_ref[...]).astype(BF16)
        off = ckvc_ref.shape[0]
        expand(ckvc_ref, krc_ref, 0, off)
        expand(ckv_ref, kr_ref, off, kbuf.shape[0])

    outs = _mla_attention([q_ref[:, hd * LANES:(hd + 1) * LANES].astype(BF16) for hd in range(MLA_HEADS)],
                          lambda i: kbuf[:, i * LANES:(i + 1) * LANES],
                          lambda i: vbuf[:, i * LANES:(i + 1) * LANES])
    for p, o in enumerate(outs):
        o_ref[:, p * LANES:(p + 1) * LANES] = o.astype(o_ref.dtype)


def _mla(q, ckv, kr, lw, layer, ckv_cache, kr_cache):
    b, lq, wq = q.shape
    ls, lc = ckv.shape[1], ckv_cache.shape[2]
    tq = min(ATTN_Q_TILE, lq)
    cspec = pl.BlockSpec((None, None, lc, LANES), lambda bi, i: (bi, layer, 0, 0))
    sspec = pl.BlockSpec((None, ls, LANES), lambda bi, i: (bi, 0, 0))
    return pl.pallas_call(
        _mla_kernel,
        grid=(b, lq // tq),
        in_specs=[pl.BlockSpec((None, tq, wq), lambda bi, i: (bi, i, 0)), cspec, cspec, sspec, sspec,
                  pl.BlockSpec((None, 2 * LANES, 4 * LANES), lambda bi, i: (layer, 0, 0)),
                  pl.BlockSpec((None, LANES, 2 * LANES), lambda bi, i: (layer, 0, 0))],
        out_specs=pl.BlockSpec((None, tq, 2 * LANES), lambda bi, i: (bi, i, 0)),
        out_shape=jax.ShapeDtypeStruct((b, lq, 2 * LANES), BF16),
        scratch_shapes=[pltpu.VMEM((lc + ls, 4 * LANES), BF16), pltpu.VMEM((lc + ls, 2 * LANES), BF16)],
        compiler_params=_cparams(2),
        name="mla_lat",
    )(q, ckv_cache, kr_cache, ckv, kr, lw["wk"], lw["wv"])


def _na_kernel(q_ref, k_ref, v_ref, kc_ref, vc_ref, bias_ref, o_ref):
    start = pl.multiple_of(pl.program_id(0) * 256, 256)
    kband = k_ref[pl.ds(start, NA_BAND), :].astype(BF16)
    vband = v_ref[pl.ds(start, NA_BAND), :].astype(BF16)
    kc = kc_ref[...].astype(BF16)
    vc = vc_ref[...].astype(BF16)
    tq = q_ref.shape[0]
    lo = _lane((tq, LANES)) < HEAD_DIM

    def scores(hd):
        p, half = divmod(hd, 2)
        sl = slice(p * LANES, (p + 1) * LANES)
        qm = jnp.where(lo if half == 0 else jnp.logical_not(lo), q_ref[:, sl].astype(F32), 0.0).astype(BF16)
        bias = jnp.concatenate([bias_ref[0, hd], bias_ref[1, hd]], axis=0)
        s_loc = lax.dot_general(qm, kband[:, sl], _NT, preferred_element_type=F32) + bias
        return [s_loc, _bdot(qm, kc[sl, :])]

    def attend(hd, es):
        sl = slice((hd // 2) * LANES, (hd // 2 + 1) * LANES)
        return (_bdot(es[0].astype(BF16), vband[:, sl])
                + lax.dot_general(es[1].astype(BF16), vc[sl, :], _NT, preferred_element_type=F32))

    outs = _pipelined_heads(NA_HEADS, scores, attend)
    for p in range(NA_HEADS // 2):
        o_ref[:, p * LANES:(p + 1) * LANES] = jnp.where(lo, outs[2 * p], outs[2 * p + 1]).astype(o_ref.dtype)


def _na_latent(q, k, v, kc_t, vc_t, layer, bias_blocks):
    b, n, w = q.shape
    lc = kc_t.shape[3]
    nq = n // (2 * Q_TILE)
    full = pl.BlockSpec((None, n, w), lambda j, bi: (bi, 0, 0))
    cspec = pl.BlockSpec((None, None, w, lc), lambda j, bi: (bi, layer, 0, 0))
    return pl.pallas_call(
        _na_kernel,
        grid=(nq, b),
        in_specs=[pl.BlockSpec((None, 2 * Q_TILE, w), lambda j, bi: (bi, j, 0)), full, full, cspec, cspec,
                  pl.BlockSpec((None, 2, NA_HEADS, Q_TILE, NA_BAND), lambda j, bi: (layer, j, 0, 0, 0))],
        out_specs=pl.BlockSpec((None, 2 * Q_TILE, w), lambda j, bi: (bi, j, 0)),
        out_shape=jax.ShapeDtypeStruct((b, n, w), BF16),
        compiler_params=_cparams(2),
        name="na_latent",
    )(q, k, v, kc_t, vc_t, bias_blocks)


NA_GRID_ROWS = 16
NA_BAND_ROW0 = (0, 0, 4, 4)


def _na_bias_kernel(b_ref, o_ref, tp_s):
    hd = pl.program_id(0)
    n_dr, n_dc = 2 * NA_KH - 1, 2 * NA_KW - 1
    shape = (GRID_W, LANES)
    c = lax.broadcasted_iota(jnp.int32, shape, 0)
    lane = _lane(shape)
    kc = lane & (GRID_W - 1)
    lo = lane < GRID_W
    diff = kc - c + (NA_KW - 1)
    c0 = jnp.clip(c - NA_KW // 2, 0, GRID_W - NA_KW)
    col_ok = (kc >= c0) & (kc < c0 + NA_KW)
    neg = jnp.full(shape, NEG_INF, F32)
    for dr0 in range(-1, n_dr):
        acc = neg
        for d in range(n_dc):
            v_lo = b_ref[hd * n_dr + dr0, d] if dr0 >= 0 else 0.0
            v_hi = b_ref[hd * n_dr + dr0 + 1, d] if dr0 + 1 < n_dr else 0.0
            acc = jnp.where(diff == d, jnp.where(lo, v_lo, v_hi), acc)
        tp_s[dr0 + 1] = jnp.where(col_ok, acc * LOG2E, NEG_INF)
    for j in range(NA_GRID_ROWS // 4):
        for ri in range(4):
            r = 4 * j + ri
            r0 = min(max(r - NA_KH // 2, 0), NA_GRID_ROWS - NA_KH)
            for kp in range(NA_BAND // LANES):
                kr = NA_BAND_ROW0[j] + 2 * kp
                ok_lo, ok_hi = r0 <= kr < r0 + NA_KH, r0 <= kr + 1 < r0 + NA_KH
                dr0 = kr - r + (NA_KH - 1)
                if ok_lo and ok_hi:
                    t = tp_s[dr0 + 1]
                elif ok_lo:
                    t = jnp.where(lo, tp_s[dr0 + 1], NEG_INF)
                elif ok_hi:
                    t = jnp.where(lo, NEG_INF, tp_s[dr0 + 1])
                else:
                    t = neg
                o_ref[j, ri * GRID_W:(ri + 1) * GRID_W, kp * LANES:(kp + 1) * LANES] = t


def _na_bias_blocks(bias):
    nq = NA_GRID_ROWS // 4
    return pl.pallas_call(
        _na_bias_kernel,
        grid=(DEPTH * NA_HEADS,),
        in_specs=[pl.BlockSpec(memory_space=pltpu.SMEM)],
        out_specs=pl.BlockSpec((None, nq, None, Q_TILE, NA_BAND),
                               lambda i: (i // NA_HEADS, 0, i % NA_HEADS, 0, 0)),
        out_shape=jax.ShapeDtypeStruct((DEPTH, nq, NA_HEADS, Q_TILE, NA_BAND), F32),
        scratch_shapes=[pltpu.VMEM((2 * NA_KH, GRID_W, LANES), F32)],
        compiler_params=_cparams(1),
        name="na_bias",
    )(bias.reshape(DEPTH * NA_HEADS * (2 * NA_KH - 1), 2 * NA_KW - 1))


def _widen(cols, n):
    blk = _lane((n, DN_HEADS * DN_DV)) >> 6
    return jnp.where(blk == 0, cols[0], jnp.where(blk == 1, cols[1], jnp.where(blk == 2, cols[2], cols[3])))


def _deltanet_kernel(seq, has_state, *refs):
    if has_state:
        zc_all, zm_all, s0_ref, cw_ref, alog_ref, dtb_ref, og_ref, o_ref = refs[:8]
    else:
        zc_all, zm_all, cw_ref, alog_ref, dtb_ref, og_ref, o_ref, sfin_ref = refs[:8]
    scratch = refs[8:]
    n_seqs = zc_all.shape[0]
    g_all, o_all, c_all, mp_all = scratch[4:8]
    n_chunks = seq // DN_CHUNK
    wide = DN_HEADS * DN_DV

    a_off, b_off = IN_TAIL_SKIP, IN_TAIL_SKIP + DN_GATES
    ri = lax.broadcasted_iota(jnp.int32, (DN_ROWS, DN_ROWS), 0)
    ci = lax.broadcasted_iota(jnp.int32, (DN_ROWS, DN_ROWS), 1)
    same = (ri >> 6) == (ci >> 6)
    tri = [jnp.tile(jnp.where(same & ((ci <= ri) if d == 0 else (ci >= ri)), 1.0, 0.0).astype(BF16), (1, 3))
           for d in range(2)]
    n_blocks = seq // DN_ROWS

    def terms3(x):
        hi = x.astype(BF16)
        rest = x - hi.astype(F32)
        mid = rest.astype(BF16)
        return hi, mid, (rest - mid.astype(F32)).astype(BF16)

    def split3(x):
        return jnp.concatenate(terms3(x), axis=0)

    def preprocess(views, rb, carry):
        zc_ref, zm_ref, q_s, k_s, v_s, b_s, g_s = views[:7]
        r0 = pl.multiple_of(rb * DN_ROWS, DN_ROWS)
        rows = pl.ds(r0, DN_ROWS)
        before = pl.ds(pl.multiple_of(jnp.maximum(r0 - 8, 0), 8), 8)
        after = pl.ds(pl.multiple_of(jnp.minimum(r0 + DN_ROWS, seq - 8), 8), 8)
        for part, dst in enumerate((q_s, k_s, v_s)):
            cs = slice(part * wide, (part + 1) * wide)
            head = jnp.where(rb > 0, zc_ref[before, cs], 0.0)
            tail = jnp.where(rb < n_blocks - 1, zc_ref[after, cs], 0.0)
            xe = jnp.concatenate([head, zc_ref[rows, cs], tail], axis=0)
            w = cw_ref[:, cs]
            y = (w[0:1] * xe[7:7 + DN_ROWS] + w[1:2] * xe[8:8 + DN_ROWS]
                 + w[2:3] * xe[9:9 + DN_ROWS] + w[3:4] * xe[10:10 + DN_ROWS])
            y = _silu(y)
            if part == 0:
                y = _head_l2(y) * (DN_DK ** -0.5)
            elif part == 1:
                y = _head_l2(y)
            dst[rows, :] = y
        zm = zm_ref[rows, :]
        xa = zm + dtb_ref[...]
        logd = -jnp.exp(alog_ref[...]) * (jnp.maximum(xa, 0.0) + jnp.log1p(jnp.exp(-jnp.abs(xa))))
        beta = 1.0 / (1.0 + jnp.exp(-zm))
        logd3 = split3(logd)
        for d in range(2):
            b_s[d, rows, :] = _widen([beta[:, b_off + 4 * d + hd:b_off + 4 * d + hd + 1]
                                      for hd in range(DN_HEADS)], DN_ROWS)
            g = _bdot(tri[d], logd3)
            g_s[d, rows, :] = _widen([g[:, a_off + 4 * d + hd:a_off + 4 * d + hd + 1]
                                      for hd in range(DN_HEADS)], DN_ROWS)
        return carry

    ii = lax.broadcasted_iota(jnp.int32, (DN_CHUNK, wide), 0)
    jj = _lane((DN_CHUNK, wide)) & (DN_CHUNK - 1)
    blk = _lane((DN_CHUNK, wide)) >> 6
    diag = ii == jj
    eye = jnp.where(diag, 1.0, 0.0)
    half_mask = [jnp.where((_lane((DN_CHUNK, LANES)) >> 6) == half, 1.0, 0.0).astype(BF16) for half in range(2)]
    zero_block = jnp.zeros((DN_CHUNK, LANES), BF16)

    def bd(z):
        zb = z.astype(BF16)
        rows = []
        for hd in range(DN_HEADS):
            col, half = divmod(hd, 2)
            kept = zb[:, col * LANES:(col + 1) * LANES] * half_mask[half]
            rows.append(jnp.concatenate([kept, zero_block] if col == 0 else [zero_block, kept], axis=1))
        return jnp.concatenate(rows, axis=0)

    def fold(gram):
        out = jnp.where(blk == 0, gram[0:DN_CHUNK], 0.0)
        for hd in range(1, DN_HEADS):
            out = out + jnp.where(blk == hd, gram[hd * DN_CHUNK:(hd + 1) * DN_CHUNK], 0.0)
        return out

    tri_masks = []
    for d in range(2):
        incl = (jj <= ii) if d == 0 else (jj >= ii)
        strict = (jj < ii) if d == 0 else (jj > ii)
        pair = [((ii >> (lvl + 1)) == (jj >> (lvl + 1)))
                & (((ii >> lvl) & 1) == (1 - d)) & (((jj >> lvl) & 1) == d) for lvl in range(6)]
        tri_masks.append((incl, strict, pair))

    def prepare(views, step, carry):
        q_s, k_s, v_s, b_s, g_s, o_s, c_s, mp_s = views[2:]
        chunks = [step * PREP_CHUNKS + i for i in range(PREP_CHUNKS)]
        rows = [pl.ds(pl.multiple_of(c * DN_CHUNK, DN_CHUNK), DN_CHUNK) for c in chunks]
        qkv = [(q_s[r, :], k_s[r, :], v_s[r, :]) for r in rows]
        inst = [(ci, d) for ci in range(PREP_CHUNKS) for d in range(2)]
        beta = {(ci, d): b_s[d, rows[ci], :] for ci, d in inst}
        kb = {(ci, d): qkv[ci][1] * beta[ci, d] for ci, d in inst}
        r = [lax.dot_general(jnp.concatenate([kb[ci, 0], kb[ci, 1], qkv[ci][0]], axis=0).astype(BF16),
                             bd(qkv[ci][1]), _NT, preferred_element_type=F32)
             for ci in range(PREP_CHUNKS)]
        g, a, qk, eg, t = {}, {}, {}, {}, {}
        for ci, d in inst:
            incl, strict, pair = tri_masks[d]
            g[ci, d] = g_s[d, rows[ci], :]
            g_row = jnp.sum(jnp.where(diag, g[ci, d], 0.0), axis=0, keepdims=True)
            dm = jnp.where(incl, jnp.exp(jnp.where(incl, g[ci, d] - g_row, 0.0)), 0.0)
            a[ci, d] = jnp.where(strict, r[ci][d * DN_CHUNK:(d + 1) * DN_CHUNK] * dm, 0.0)
            qk[ci, d] = r[ci][2 * DN_CHUNK:] * dm
            eg[ci, d] = jnp.exp(g[ci, d])
            t[ci, d] = eye - jnp.where(pair[0], a[ci, d], 0.0)
        for lvl in range(1, 6):
            te = {i: _bdot(t[i].astype(BF16), bd(jnp.where(tri_masks[i[1]][2][lvl], a[i], 0.0))) for i in inst}
            t = {i: t[i] - _bdot(te[i].astype(BF16), bd(t[i])) for i in inst}
        nb = {i: jnp.where(diag, 0.0, t[i]).astype(BF16) for i in inst}
        rhs_u = {(ci, d): qkv[ci][2] * beta[ci, d] for ci, d in inst}
        rhs_w = {i: kb[i] * eg[i] for i in inst}
        q2 = {i: qk[i] + _bdot(qk[i].astype(BF16), bd(nb[i])) for i in inst}
        both = {i: jnp.concatenate([nb[i], q2[i].astype(BF16)], axis=0) for i in inst}
        ru = {i: _bdot(both[i], bd(rhs_u[i])) for i in inst}
        rw = {i: _bdot(both[i], bd(rhs_w[i])) for i in inst}
        u = {i: rhs_u[i] + ru[i][0:DN_CHUNK] for i in inst}
        w = {i: rhs_w[i] + rw[i][0:DN_CHUNK] for i in inst}
        kd = {}
        for ci, d in inst:
            g_last = g[ci, d][DN_CHUNK - 1:DN_CHUNK] if d == 0 else g[ci, d][0:1]
            kd[ci, d] = (qkv[ci][1] * jnp.exp(g_last - g[ci, d])).astype(BF16)
        p = {(ci, d): qkv[ci][0] * eg[ci, d] - rw[ci, d][DN_CHUNK:] for ci, d in inst}
        o0 = {i: ru[i][DN_CHUNK:] for i in inst}
        m = {i: fold(lax.dot_general(kd[i], w[i].astype(BF16), _TN, preferred_element_type=F32)) for i in inst}
        cc = {i: fold(lax.dot_general(kd[i], u[i].astype(BF16), _TN, preferred_element_type=F32)) for i in inst}
        for ci, d in inst:
            mrow = pl.ds(pl.multiple_of(chunks[ci] * (2 * DN_CHUNK), 2 * DN_CHUNK), 2 * DN_CHUNK)
            c_s[d, rows[ci], :] = cc[ci, d]
            mp_s[d, mrow, :] = jnp.concatenate([m[ci, d], p[ci, d]], axis=0).astype(BF16)
        for ci in range(PREP_CHUNKS):
            o_s[rows[ci], :] = o0[ci, 0] + o0[ci, 1]
        return carry

    for sq in range(n_seqs):
        views = (zc_all.at[sq], zm_all.at[sq]) + tuple(ref.at[sq] for ref in scratch)
        lax.fori_loop(0, n_blocks, functools.partial(preprocess, views), 0)
        lax.fori_loop(0, n_chunks // PREP_CHUNKS, functools.partial(prepare, views), 0)

    chains = [(sq, d) for sq in range(n_seqs) for d in range(2)]

    def scan(i, states):
        new = []
        for (sq, d), state in zip(chains, states):
            c = i if d == 0 else n_chunks - 1 - i
            r0 = pl.multiple_of(c * DN_CHUNK, DN_CHUNK)
            rows = pl.ds(r0, DN_CHUNK)
            mrow = pl.ds(pl.multiple_of(c * (2 * DN_CHUNK), 2 * DN_CHUNK), 2 * DN_CHUNK)
            edge = pl.ds(pl.multiple_of(r0 + (DN_CHUNK - 8 if d == 0 else 0), 8), 8)
            g_edge = g_all[sq, d, edge, :]
            g_last = g_edge[7:8] if d == 0 else g_edge[0:1]
            res = _bdot(mp_all[sq, d, mrow, :], bd(state))
            o_all[sq, rows, :] = o_all[sq, rows, :] + res[DN_CHUNK:]
            new.append(state * jnp.exp(g_last) - res[0:DN_CHUNK] + c_all[sq, d, rows, :])
        return tuple(new)

    place = [jnp.where(diag & (blk == hd), 1.0, 0.0).astype(BF16) for hd in range(DN_HEADS)]

    def to_wide(heads):
        parts = [_bdot(term, place[hd]) for hd, x in enumerate(heads) for term in terms3(x)]
        return functools.reduce(lambda a, b: a + b, parts)

    def head_of(s, hd):
        parts = [lax.dot_general(term, place[hd], _NT, preferred_element_type=F32) for term in terms3(s)]
        return functools.reduce(lambda a, b: a + b, parts)

    if has_state:
        init = tuple(to_wide([s0_ref[sq, d, hd] for hd in range(DN_HEADS)]) for sq, d in chains)
    else:
        init = tuple(jnp.zeros((DN_DK, wide), F32) for _ in chains)
    fin = lax.fori_loop(0, n_chunks, scan, init)
    if not has_state:
        for (sq, d), state in zip(chains, fin):
            for hd in range(DN_HEADS):
                sfin_ref[sq, d, hd] = head_of(state, hd)

    for sq in range(n_seqs):
        gate = _silu(zc_all[sq, :, 3 * wide:4 * wide])
        o_ref[sq] = (_head_rms(o_all[sq], og_ref[...]) * gate).astype(o_ref.dtype)


def _deltanet(zc, zm, lw, layer, state=None):
    b, seq, _ = zc.shape
    wide = DN_HEADS * DN_DV
    has_state = state is not None
    ns = max(n for n in (1, 2, 4) if b % n == 0 and n * seq * DN_VMEM_BYTES_PER_ROW <= DN_VMEM_BUDGET)
    per_b = lambda w: pl.BlockSpec((ns, seq, w), lambda bi: (bi, 0, 0))
    const = lambda shape: pl.BlockSpec((None,) + shape, lambda bi: (layer,) + (0,) * len(shape))
    st_spec = pl.BlockSpec((ns, 2, DN_HEADS, DN_DK, DN_DV), lambda bi: (bi, 0, 0, 0, 0))
    in_specs = [per_b(4 * wide), per_b(LANES)]
    args = [zc, zm]
    if has_state:
        in_specs.append(pl.BlockSpec((ns, None, 2, DN_HEADS, DN_DK, DN_DV),
                                     lambda bi: (bi, layer, 0, 0, 0, 0)))
        args.append(state)
    in_specs += [const((DN_CONV, DN_QKV)), const((1, LANES)), const((1, LANES)), const((1, wide))]
    args += [lw["dn_conv_w"], lw["dn_alog_row"], lw["dn_dtb_row"], lw["dn_out_g"]]
    out_specs = [per_b(wide)]
    out_shape = [jax.ShapeDtypeStruct((b, seq, wide), BF16)]
    if not has_state:
        out_specs.append(st_spec)
        out_shape.append(jax.ShapeDtypeStruct((b, 2, DN_HEADS, DN_DK, DN_DV), F32))
    res = pl.pallas_call(
        functools.partial(_deltanet_kernel, seq, has_state),
        grid=(b // ns,),
        in_specs=in_specs,
        out_specs=out_specs,
        out_shape=out_shape,
        scratch_shapes=[pltpu.VMEM((ns, seq, wide), F32), pltpu.VMEM((ns, seq, wide), F32),
                        pltpu.VMEM((ns, seq, wide), F32), pltpu.VMEM((ns, 2, seq, wide), F32),
                        pltpu.VMEM((ns, 2, seq, wide), F32), pltpu.VMEM((ns, seq, wide), F32),
                        pltpu.VMEM((ns, 2, seq, wide), F32), pltpu.VMEM((ns, 2, 2 * seq, wide), BF16)],
        compiler_params=_cparams(1),
        name="deltanet_lat" if has_state else "deltanet_ctx",
    )(*args)
    return (res[0], None) if has_state else (res[0], res[1])


def _outffn_kernel(final, oa_ref, ob_ref, oc_ref, od_ref, x_ref, mod_ref, g2_ref, wo_ref, wg_ref,
                   wu_ref, wd_ref, fg_ref, y_ref):
    tiles = range(TILES_PER_STEP)
    m = mod_ref[...]
    o = [jnp.concatenate([oa_ref[t], ob_ref[t], oc_ref[t], od_ref[t]], axis=-1).astype(BF16) for t in tiles]
    x1 = [x_ref[t] + m[2:3] * _bdot(o[t], wo_ref[...]) for t in tiles]
    h = [(_rms_full(x1[t], g2_ref[...]) * (1.0 + m[4:5]) + m[3:4]).astype(BF16) for t in tiles]
    gate = [_bdot(h[t], wg_ref[...]) for t in tiles]
    up = [_bdot(h[t], wu_ref[...]) for t in tiles]
    act = [(_silu(gate[t]) * up[t]).astype(BF16) for t in tiles]
    x2 = [x1[t] + m[5:6] * _bdot(act[t], wd_ref[...]) for t in tiles]
    for t in tiles:
        y_ref[t] = _rms_full(x2[t], fg_ref[...]) if final else x2[t]


def _outffn(outs, x, mods, per_batch_mods, lw, layer, final_g, final):
    b, l, _ = x.shape
    tm, ts = ROW_TILE, TILES_PER_STEP
    n_tiles = b * l // tm
    tiles_per_seq = l // tm
    assert not per_batch_mods or tiles_per_seq % ts == 0
    row = lambda w: pl.BlockSpec((ts, tm, w), lambda i: (i, 0, 0))
    const = lambda shape: pl.BlockSpec((None,) + shape, lambda i: (layer,) + (0,) * len(shape),
                                       pipeline_mode=pl.Buffered(1))
    mod_spec = pl.BlockSpec((None, None, MOD_CHUNKS, D_MODEL),
                            (lambda i: (layer, 1 + i * ts // tiles_per_seq, 0, 0)) if per_batch_mods
                            else (lambda i: (layer, 0, 0, 0)))
    tiled = lambda a: a.reshape(n_tiles, tm, a.shape[-1])
    y = pl.pallas_call(
        functools.partial(_outffn_kernel, final),
        grid=(n_tiles // ts,),
        in_specs=[row(256), row(256), row(256), row(256), row(D_MODEL), mod_spec, const((1, D_MODEL)),
                  const((D_MODEL, D_MODEL)), const((D_MODEL, D_FF)), const((D_MODEL, D_FF)),
                  const((D_FF, D_MODEL)), pl.BlockSpec((1, D_MODEL), lambda i: (0, 0))],
        out_specs=row(D_MODEL),
        out_shape=jax.ShapeDtypeStruct((n_tiles, tm, D_MODEL), F32),
        compiler_params=_cparams(1),
        name="outffn",
    )(*[tiled(a) for a in outs], tiled(x), mods, lw["norm2_g"], lw["w_out"], lw["w_gate"], lw["w_up"],
      lw["w_down"], final_g)
    return y.reshape(b, l, D_MODEL)


def _rope_tables(n):
    t = np.arange(n)

    def axis(pos, half):
        inv = np.float32(ROPE_BASE) ** (-np.arange(half, dtype=np.float32) / np.float32(half))
        ang = pos.astype(np.float32)[:, None] * inv[None, :]
        c, s = np.cos(ang), np.sin(ang)
        return np.concatenate([c, c], -1), np.concatenate([-s, s], -1)

    cr, sr = axis(t // GRID_W, 16)
    cc, sc = axis(t % GRID_W, 16)
    cos64 = np.tile(np.concatenate([cr, cc], -1), (1, 2))
    sin64 = np.tile(np.concatenate([sr, sc], -1), (1, 2))
    cr, sr = axis(t // GRID_W, 8)
    cc, sc = axis(t % GRID_W, 8)
    cos32, sin32 = np.concatenate([cr, cc], -1), np.concatenate([sr, sc], -1)
    one, zero = np.ones((n, 1), np.float32), np.zeros((n, 1), np.float32)
    cosm = np.concatenate([np.tile(one, (1, 64)), cos32, np.tile(one, (1, 32))], -1)
    sinm = np.concatenate([np.tile(zero, (1, 64)), sin32, np.tile(zero, (1, 32))], -1)
    coskr = np.concatenate([cos32, np.tile(one, (1, 96))], -1)
    sinkr = np.concatenate([sin32, np.tile(zero, (1, 96))], -1)
    return tuple(jnp.asarray(a, F32) for a in (cos64, sin64, cosm, sinm, coskr, sinkr))


_QA_ORDER = ((0, 64), (128, 192), (64, 128), (192, 256))


def _stacked_weights(p):
    w_in = jnp.swapaxes(p["w_in"], 1, 2).astype(BF16)
    w_out = p["w_out"]
    w_out = jnp.concatenate([w_out[:, a:b] for a, b in _QA_ORDER + ((256, w_out.shape[1]),)], axis=1).astype(BF16)
    wq = p["mla_wq_up"].reshape(DEPTH, MLA_Q_LORA, MLA_HEADS, MLA_NOPE + MLA_ROPE)
    wq = jnp.pad(wq, ((0, 0), (0, 0), (0, 0), (0, LANES - MLA_NOPE - MLA_ROPE))).reshape(DEPTH, MLA_Q_LORA, 4 * LANES)
    wkv = p["mla_wkv_up"].reshape(DEPTH, MLA_KV_LORA, MLA_HEADS, MLA_NOPE + MLA_V)
    wk_top = jnp.pad(wkv[..., :MLA_NOPE], ((0, 0), (0, 0), (0, 0), (0, LANES - MLA_NOPE)))
    place = jnp.pad(jnp.eye(MLA_ROPE, dtype=F32), ((0, LANES - MLA_ROPE), (MLA_NOPE, LANES - MLA_NOPE - MLA_ROPE)))
    wk_bot = jnp.broadcast_to(place[None, :, None, :], (DEPTH, LANES, MLA_HEADS, LANES))
    wk = jnp.concatenate([wk_top, wk_bot], axis=1).reshape(DEPTH, 2 * LANES, 4 * LANES)
    wv = wkv[..., MLA_NOPE:].reshape(DEPTH, MLA_KV_LORA, MLA_HEADS * MLA_V)
    gate_row = lambda v: jnp.pad(v.reshape(DEPTH, 1, DN_GATES),
                                 ((0, 0), (0, 0), (IN_TAIL_SKIP, LANES - IN_TAIL_SKIP - DN_GATES)))
    row = lambda v: v[:, None, :]
    return {
        "norm1_g": row(p["norm1_g"]), "norm2_g": row(p["norm2_g"]),
        "w_in": w_in, "w_out": w_out,
        "qn_g": row(jnp.tile(p["gqa_qn_g"], (1, 4))), "kn_g": row(jnp.tile(p["gqa_kn_g"], (1, 2))),
        "mla_qn_g": row(p["mla_qn_g"]), "mla_kvn_g": row(p["mla_kvn_g"]),
        "wq": wq.astype(BF16), "wk": wk.astype(BF16), "wv": wv.astype(BF16),
        "dn_conv_w": p["dn_conv_w"], "dn_alog_row": gate_row(p["dn_a_log"]),
        "dn_dtb_row": gate_row(p["dn_dt_bias"]), "dn_out_g": row(jnp.tile(p["dn_out_g"], (1, DN_HEADS))),
        "w_gate": p["ffn_w_gate"].astype(BF16), "w_up": p["ffn_w_up"].astype(BF16),
        "w_down": p["ffn_w_down"].astype(BF16),
    }


def kernel(x_prompt, x_sample, cache_gqa_k, cache_gqa_v, cache_na_k, cache_na_v, state_dn,
           cache_mla_ckv, cache_mla_krope, c, c_ctx, norm1_g, norm2_g, w_mod, b_mod, w_in, w_out,
           gqa_qn_g, gqa_kn_g, na_bias, dn_conv_w, dn_a_log, dn_dt_bias, dn_out_g, mla_qn_g,
           mla_wq_up, mla_kvn_g, mla_wkv_up, ffn_w_gate, ffn_w_up, ffn_w_down, final_g):
    p = {"norm1_g": norm1_g, "norm2_g": norm2_g, "w_in": w_in, "w_out": w_out, "gqa_qn_g": gqa_qn_g,
         "gqa_kn_g": gqa_kn_g, "dn_conv_w": dn_conv_w, "dn_a_log": dn_a_log, "dn_dt_bias": dn_dt_bias,
         "dn_out_g": dn_out_g, "mla_qn_g": mla_qn_g, "mla_wq_up": mla_wq_up, "mla_kvn_g": mla_kvn_g,
         "mla_wkv_up": mla_wkv_up, "ffn_w_gate": ffn_w_gate, "ffn_w_up": ffn_w_up, "ffn_w_down": ffn_w_down}
    nb_ctx, seq_ctx, _ = x_prompt.shape
    nb_lat, seq_lat, _ = x_sample.shape
    past = cache_gqa_k.shape[2]
    fg = final_g[None]

    cond = jnp.concatenate([c_ctx[None], c, jnp.zeros((16 - 1 - nb_lat, D_MODEL), F32)], axis=0)
    mods = _modulation(cond, w_mod, b_mod).reshape(DEPTH, 16, MOD_CHUNKS, D_MODEL)
    lw = _stacked_weights(p)

    x = x_prompt
    ctx_out = []
    for l in range(DEPTH):
        o_a, ka, va, o_b, kb, vb, zc, o_d, ckv, zm, krr = _inproj(x, mods, False, lw, l, None)
        o_c, s_dn = _deltanet(zc, zm, lw, l)
        x = _outffn((o_a, o_b, o_c, o_d), x, mods, False, lw, l, fg, l == DEPTH - 1)
        ctx_out.append((ka, va, kb, vb, s_dn, ckv, krr[:, :, :MLA_ROPE]))
    y_prompt = x
    new = [jnp.stack([s[i] for s in ctx_out], axis=1) for i in range(7)]
    for i in range(4):
        t = new[i].reshape(nb_ctx, DEPTH, -1, HEAD_DIM, seq_ctx)
        new[i] = jnp.transpose(t, (0, 1, 4, 2, 3))

    ropes = _rope_tables(seq_lat)
    keys_t = lambda c: jnp.transpose(c, (0, 1, 3, 4, 2)).reshape(nb_lat, DEPTH, -1, past)
    ck_a, cv_a, ck_b, cv_b = keys_t(cache_gqa_k), keys_t(cache_gqa_v), keys_t(cache_na_k), keys_t(cache_na_v)
    c_kr = jnp.pad(cache_mla_krope, ((0, 0), (0, 0), (0, 0), (0, LANES - MLA_ROPE)))
    bias_blocks = _na_bias_blocks(na_bias)
    x = x_sample
    for l in range(DEPTH):
        qa, ka, va, qb, kb, vb, zc, qd, ckv, zm, krr = _inproj(x, mods, True, lw, l, ropes)
        o_a = _attn_pair(qa, ka, va, (0, 0), ck_a, cv_a, l, name="gqa_lat")
        o_b = _na_latent(qb, kb, vb, ck_b, cv_b, l, bias_blocks)
        o_c, _ = _deltanet(zc, zm, lw, l, state=state_dn)
        o_d = _mla(qd, ckv, krr, lw, l, cache_mla_ckv, c_kr)
        x = _outffn((o_a, o_b, o_c, o_d), x, mods, True, lw, l, fg, l == DEPTH - 1)
    y_sample = x

    return (y_prompt, y_sample, *new)
```
